```python
import math
import jax, jax.numpy as jnp
from jax import lax
import numpy as np

D_MODEL = 1024
BATCH = 32
SEQ = 256
DEPTH = 4
DEC_BATCH = 4
DEC_SEQ = 2048
PAST_LEN = 512

GRID_W = 64
ROPE_THETA = 10000.0
EPS = 1e-6
Q_BLOCK = 128
DN_HEADS = 4
DN_DK = 128
DN_DV = 128
DN_CONV = 3
DN_CHUNK = 64
GQA_HEADS = 8
GQA_KV = 2
GQA_HD = 64
MLA_HEADS = 8
MLA_Q_LORA = 256
MLA_KV_LORA = 128
MLA_NOPE = 64
MLA_ROPE = 32
MLA_V = 64
N_BRANCH = 3
BRANCH_W = 512
FF_DENSE = 2816
N_EXPERTS = 8
TOP_K = 2
FF_EXPERT = 3584
IN_SIZES = (DN_HEADS * DN_DK, DN_HEADS * DN_DK, DN_HEADS * DN_DV, DN_HEADS * DN_DV,
            2 * DN_HEADS, 2 * DN_HEADS,
            GQA_HEADS * GQA_HD, GQA_KV * GQA_HD, GQA_KV * GQA_HD,
            MLA_Q_LORA, MLA_KV_LORA, MLA_ROPE,
            N_BRANCH * D_MODEL)

kernel_name = "hybrid_dit_prefix_ctx_step"

F32 = jnp.float32


def rmsnorm(x, g):
    xf = x.astype(F32)
    y = xf * lax.rsqrt(jnp.mean(xf * xf, axis=-1, keepdims=True) + EPS)
    return (y * g.astype(F32)).astype(x.dtype)


def l2norm(x):
    return x * lax.rsqrt(jnp.sum(x * x, axis=-1, keepdims=True) + EPS)


def grid_rope(n_tokens, rot_dim):
    rows = n_tokens // GRID_W
    t = jnp.arange(rows * GRID_W)
    row = (t // GRID_W).astype(F32)
    col = (t % GRID_W).astype(F32)
    n_freq = rot_dim // 4
    inv = ROPE_THETA ** (-jnp.arange(n_freq, dtype=F32) / n_freq)
    ang = jnp.concatenate([row[:, None] * inv, col[:, None] * inv], axis=-1)
    return jnp.cos(ang), jnp.sin(ang)


def apply_rope(x, cos, sin):
    half = x.shape[-1] // 2
    shape = (x.shape[1],) + (1,) * (x.ndim - 3) + (half,)
    cos, sin = cos.reshape(shape), sin.reshape(shape)
    xf = x.astype(F32)
    x1, x2 = xf[..., :half], xf[..., half:]
    return jnp.concatenate([x1 * cos - x2 * sin, x2 * cos + x1 * sin], axis=-1).astype(x.dtype)


def attention(q, k, v):
    B, Tq, KV, G, dq = q.shape
    dv = v.shape[-1]
    nb = Tq // Q_BLOCK
    qb = jnp.moveaxis(q.reshape(B, nb, Q_BLOCK, KV, G, dq), 1, 0)

    def one_block(qi):
        s = jnp.einsum('bqkgd,bskd->bkgqs', qi, k).astype(F32)
        p = jax.nn.softmax(s, axis=-1).astype(v.dtype)
        return jnp.einsum('bkgqs,bskd->bqkgd', p, v)

    o = lax.map(one_block, qb)
    return jnp.moveaxis(o, 0, 1).reshape(B, Tq, KV * G * dv)


def short_conv(x, w):
    K = w.shape[0]
    pad = K // 2
    T = x.shape[1]
    xp = jnp.pad(x, ((0, 0), (pad, pad), (0, 0)))
    y = w[0] * xp[:, 0:T]
    for i in range(1, K):
        y = y + w[i] * xp[:, i:i + T]
    return jax.nn.silu(y)


def gated_delta_chunked(q, k, v, g, beta, s0):
    dtype = v.dtype
    B, T, H, dk = q.shape
    dv = v.shape[-1]
    C = DN_CHUNK
    N = T // C
    q = l2norm(q.astype(F32)) * (dk ** -0.5)
    k = l2norm(k.astype(F32))
    chunk = lambda t: t.reshape(B, N, C, H, -1).transpose(0, 3, 1, 2, 4)
    q, k, v = chunk(q), chunk(k), chunk(v.astype(F32))
    beta = chunk(beta.astype(F32)[..., None])[..., 0]
    g = jnp.cumsum(chunk(g.astype(F32)[..., None])[..., 0], axis=-1)
    mask_incl = jnp.tril(jnp.ones((C, C), bool))
    mask_strict = jnp.tril(jnp.ones((C, C), bool), -1)
    decay = jnp.exp(jnp.where(mask_incl, g[..., :, None] - g[..., None, :], -jnp.inf))
    kb = k * beta[..., None]
    A = jnp.where(mask_strict, jnp.einsum('bhncd,bhnsd->bhncs', kb, k) * decay, 0.0)
    rhs = jnp.concatenate([v * beta[..., None], kb * jnp.exp(g)[..., None]], axis=-1)
    sol = lax.linalg.triangular_solve(A + jnp.eye(C, dtype=F32), rhs, left_side=True,
                                      lower=True, unit_diagonal=True)
    u, w = sol[..., :dv], sol[..., dv:]
    a_intra = jnp.einsum('bhncd,bhnsd->bhncs', q, k) * decay
    xs = tuple(jnp.moveaxis(t, 2, 0) for t in (q, k, u, w, a_intra, g))

    def step(S, inp):
        qc, kc, uc, wc, ac, gc = inp
        v_new = uc - jnp.einsum('bhck,bhkv->bhcv', wc, S)
        o = (jnp.einsum('bhck,bhkv->bhcv', qc * jnp.exp(gc)[..., None], S)
             + jnp.einsum('bhcs,bhsv->bhcv', ac, v_new))
        g_last = gc[..., -1]
        S = (S * jnp.exp(g_last)[..., None, None]
             + jnp.einsum('bhck,bhcv->bhkv', kc * jnp.exp(g_last[..., None] - gc)[..., None], v_new))
        return S, o

    S, o = lax.scan(step, s0.astype(F32), xs)
    o = o.transpose(1, 0, 3, 2, 4).reshape(B, T, H, dv)
    return o.astype(dtype), S.astype(dtype)


def deltanet(q, k, v, gate, b_raw, a_raw, p, s0):
    B, T, _ = q.shape
    qkv = short_conv(jnp.concatenate([q, k, v], axis=-1), p["dn_conv_w"])
    q, k, v = jnp.split(qkv, 3, axis=-1)
    q = q.reshape(B, T, DN_HEADS, DN_DK)
    k = k.reshape(B, T, DN_HEADS, DN_DK)
    v = v.reshape(B, T, DN_HEADS, DN_DV)
    beta = jax.nn.sigmoid(b_raw.reshape(B, T, 2, DN_HEADS).astype(F32))
    g = -jnp.exp(p["dn_a_log"].astype(F32)) * jax.nn.softplus(
        a_raw.reshape(B, T, 2, DN_HEADS).astype(F32) + p["dn_dt_bias"].astype(F32))
    o_f, s_f = gated_delta_chunked(q, k, v, g[:, :, 0], beta[:, :, 0], s0[:, 0])
    rev = lambda t: jnp.flip(t, axis=1)
    o_b, s_b = gated_delta_chunked(rev(q), rev(k), rev(v), rev(g[:, :, 1]), rev(beta[:, :, 1]), s0[:, 1])
    o = o_f + rev(o_b)
    o = rmsnorm(o, p["dn_norm_g"]) * jax.nn.silu(gate.reshape(B, T, DN_HEADS, DN_DV))
    return o.reshape(B, T, DN_HEADS * DN_DV), jnp.stack([s_f, s_b], axis=1)


def token_mixers(h, p, rope, ctx):
    B, T, _ = h.shape
    offsets = np.cumsum(IN_SIZES)[:-1].tolist()
    (dq, dk_, dv_, dgate, dbeta, dalpha, gq, gk, gv, mcq, mckv, mkr, bgate) = jnp.split(
        h @ p["w_in"], offsets, axis=-1)

    s0 = jnp.zeros((B, 2, DN_HEADS, DN_DK, DN_DV), h.dtype) if ctx is None else ctx["state_delta"]
    o_dn, s_dn = deltanet(dq, dk_, dv_, dgate, dbeta, dalpha, p, s0)

    q = rmsnorm(gq.reshape(B, T, GQA_HEADS, GQA_HD), p["gqa_q_norm"])
    k = rmsnorm(gk.reshape(B, T, GQA_KV, GQA_HD), p["gqa_k_norm"])
    v = gv.reshape(B, T, GQA_KV, GQA_HD)
    k_ctx_new, v_ctx_new = k, v
    if rope is not None:
        cos, sin = rope["gqa"]
        q = apply_rope(q, cos, sin)
        k = jnp.concatenate([apply_rope(k, cos, sin), ctx["gqa_k"]], axis=1)
        v = jnp.concatenate([v, ctx["gqa_v"]], axis=1)
    q = q.reshape(B, T, GQA_KV, GQA_HEADS // GQA_KV, GQA_HD) * (GQA_HD ** -0.5)
    o_gqa = attention(q, k, v)

    cq = rmsnorm(mcq, p["mla_q_norm"])
    qf = (cq @ p["mla_w_uq"]).reshape(B, T, MLA_HEADS, MLA_NOPE + MLA_ROPE)
    q_nope, q_rope = qf[..., :MLA_NOPE], qf[..., MLA_NOPE:]
    ckv = rmsnorm(mckv, p["mla_kv_norm"])
    kr = mkr
    ckv_ctx_new, kr_ctx_new = ckv, kr
    if rope is not None:
        cos, sin = rope["mla"]
        q_rope = apply_rope(q_rope, cos, sin)
        ckv = jnp.concatenate([ckv, ctx["mla_ckv"]], axis=1)
        kr = jnp.concatenate([apply_rope(kr, cos, sin), ctx["mla_krope"]], axis=1)
    Tk = ckv.shape[1]
    kv = (ckv @ p["mla_w_ukv"]).reshape(B, Tk, MLA_HEADS, MLA_NOPE + MLA_V)
    k_m = jnp.concatenate([kv[..., :MLA_NOPE],
                           jnp.broadcast_to(kr[:, :, None, :], (B, Tk, MLA_HEADS, MLA_ROPE))], axis=-1)
    v_m = kv[..., MLA_NOPE:]
    q_m = jnp.concatenate([q_nope, q_rope], axis=-1)[:, :, :, None, :] * ((MLA_NOPE + MLA_ROPE) ** -0.5)
    o_mla = attention(q_m, k_m, v_m)

    gates = jax.nn.sigmoid(bgate.reshape(B, T, N_BRANCH, D_MODEL).astype(F32)).astype(h.dtype)
    branches = jnp.stack([o_dn, o_gqa, o_mla], axis=2)
    proj = jnp.einsum('btnc,ncd->btnd', branches, p["w_branch"])
    out = jnp.einsum('btnd,btnd->btd', gates, proj) @ p["w_out"]
    if ctx is None:
        return out, (k_ctx_new, v_ctx_new, ckv_ctx_new, kr_ctx_new, s_dn)
    return out, None


def swiglu(x, w1, w3, w2):
    return (jax.nn.silu(x @ w1) * (x @ w3)) @ w2


def moe(x, router_w, router_b, w1, w3, w2):
    logits = (x @ router_w + router_b).astype(F32)
    top_v, top_i = lax.top_k(logits, TOP_K)
    probs = jax.nn.softmax(top_v, axis=-1)
    combine = jnp.sum(jax.nn.one_hot(top_i, N_EXPERTS, dtype=F32) * probs[..., None], axis=-2).astype(x.dtype)
    y = jnp.zeros_like(x)
    for e in range(N_EXPERTS):
        y = y + combine[..., e:e + 1] * swiglu(x, w1[e], w3[e], w2[e])
    return y


def adaln(cond, w, b):
    m = (jax.nn.silu(cond) @ w + b)[:, None, :]
    return jnp.split(m, 6, axis=-1)


def trunk_layer(x, mods, p, rope, ctx):
    sh1, sc1, g1, sh2, sc2, g2 = mods
    h = rmsnorm(x, p["norm1_g"]) * (1.0 + sc1) + sh1
    mix, new_ctx = token_mixers(h, p, rope, ctx)
    x = x + g1 * mix
    h = rmsnorm(x, p["norm2_g"]) * (1.0 + sc2) + sh2
    if "router_w" in p:
        f = moe(h, p["router_w"], p["router_b"], p["moe_w1"], p["moe_w3"], p["moe_w2"])
    else:
        f = swiglu(h, p["ffd_w1"], p["ffd_w3"], p["ffd_w2"])
    return x + g2 * f, new_ctx


def setup_inputs(seed: int = 0) -> dict:
    key = jax.random.key(seed)
    ks = iter(jax.random.split(key, 48))
    nrm = lambda shape, scale: jax.random.normal(next(ks), shape, F32) * scale
    ND = (DEPTH + 1) // 2
    NM = DEPTH // 2
    n_in = sum(IN_SIZES)
    dt = jnp.exp(jax.random.uniform(next(ks), (DEPTH, 2, DN_HEADS), F32,
                                    minval=math.log(1e-3), maxval=math.log(1e-1)))
    a_init = jax.random.uniform(next(ks), (DEPTH, 2, DN_HEADS), F32, minval=1.0, maxval=16.0)
    return {
        "x_prompt": nrm((BATCH, SEQ, D_MODEL), 1.0),
        "x_sample": nrm((DEC_BATCH, DEC_SEQ, D_MODEL), 1.0),
        "c": nrm((DEC_BATCH, D_MODEL), 1.0),
        "cache_gqa_k": nrm((DEC_BATCH, DEPTH, PAST_LEN, GQA_KV, GQA_HD), 1.0),
        "cache_gqa_v": nrm((DEC_BATCH, DEPTH, PAST_LEN, GQA_KV, GQA_HD), 1.0),
        "cache_mla_ckv": nrm((DEC_BATCH, DEPTH, PAST_LEN, MLA_KV_LORA), 1.0),
        "cache_mla_krope": nrm((DEC_BATCH, DEPTH, PAST_LEN, MLA_ROPE), 1.0),
        "state_delta": nrm((DEC_BATCH, DEPTH, 2, DN_HEADS, DN_DK, DN_DV), 0.1),
        "c_ctx": nrm((D_MODEL,), 1.0),
        "w_mod": nrm((DEPTH, D_MODEL, 6 * D_MODEL), 0.5 * D_MODEL ** -0.5),
        "b_mod": nrm((DEPTH, 6 * D_MODEL), 0.01),
        "norm1_g": 1.0 + nrm((DEPTH, D_MODEL), 0.02),
        "norm2_g": 1.0 + nrm((DEPTH, D_MODEL), 0.02),
        "w_in": nrm((DEPTH, D_MODEL, n_in), D_MODEL ** -0.5),
        "dn_conv_w": nrm((DEPTH, DN_CONV, 3 * DN_HEADS * DN_DK), DN_CONV ** -0.5),
        "dn_a_log": jnp.log(a_init),
        "dn_dt_bias": dt + jnp.log(-jnp.expm1(-dt)),
        "dn_norm_g": 1.0 + nrm((DEPTH, DN_DV), 0.02),
        "gqa_q_norm": 1.0 + nrm((DEPTH, GQA_HD), 0.02),
        "gqa_k_norm": 1.0 + nrm((DEPTH, GQA_HD), 0.02),
        "mla_q_norm": 1.0 + nrm((DEPTH, MLA_Q_LORA), 0.02),
        "mla_kv_norm": 1.0 + nrm((DEPTH, MLA_KV_LORA), 0.02),
        "mla_w_uq": nrm((DEPTH, MLA_Q_LORA, MLA_HEADS * (MLA_NOPE + MLA_ROPE)), MLA_Q_LORA ** -0.5),
        "mla_w_ukv": nrm((DEPTH, MLA_KV_LORA, MLA_HEADS * (MLA_NOPE + MLA_V)), MLA_KV_LORA ** -0.5),
        "w_branch": nrm((DEPTH, N_BRANCH, BRANCH_W, D_MODEL), BRANCH_W ** -0.5),
        "w_out": nrm((DEPTH, D_MODEL, D_MODEL), D_MODEL ** -0.5),
        "ffd_w1": nrm((ND, D_MODEL, FF_DENSE), D_MODEL ** -0.5),
        "ffd_w3": nrm((ND, D_MODEL, FF_DENSE), D_MODEL ** -0.5),
        "ffd_w2": nrm((ND, FF_DENSE, D_MODEL), FF_DENSE ** -0.5),
        "router_w": nrm((NM, D_MODEL, N_EXPERTS), D_MODEL ** -0.5),
        "router_b": nrm((NM, N_EXPERTS), 0.01),
        "moe_w1": nrm((NM, N_EXPERTS, D_MODEL, FF_EXPERT), D_MODEL ** -0.5),
        "moe_w3": nrm((NM, N_EXPERTS, D_MODEL, FF_EXPERT), D_MODEL ** -0.5),
        "moe_w2": nrm((NM, N_EXPERTS, FF_EXPERT, D_MODEL), FF_EXPERT ** -0.5),
        "final_g": 1.0 + nrm((D_MODEL,), 0.02),
    }


def reference(x_prompt, x_sample, c, cache_gqa_k, cache_gqa_v, cache_mla_ckv, cache_mla_krope,
              state_delta, c_ctx, w_mod, b_mod, norm1_g, norm2_g, w_in, dn_conv_w, dn_a_log,
              dn_dt_bias, dn_norm_g, gqa_q_norm, gqa_k_norm, mla_q_norm, mla_kv_norm, mla_w_uq,
              mla_w_ukv, w_branch, w_out, ffd_w1, ffd_w3, ffd_w2, router_w, router_b, moe_w1,
              moe_w3, moe_w2, final_g):
    T_lat = x_sample.shape[1]
    rope = {"gqa": grid_rope(T_lat, GQA_HD), "mla": grid_rope(T_lat, MLA_ROPE)}
    xp, xs = x_prompt, x_sample
    new_k, new_v, new_ckv, new_kr, new_s = [], [], [], [], []
    for l in range(DEPTH):
        p = {
            "norm1_g": norm1_g[l], "norm2_g": norm2_g[l], "w_in": w_in[l],
            "dn_conv_w": dn_conv_w[l], "dn_a_log": dn_a_log[l], "dn_dt_bias": dn_dt_bias[l],
            "dn_norm_g": dn_norm_g[l], "gqa_q_norm": gqa_q_norm[l], "gqa_k_norm": gqa_k_norm[l],
            "mla_q_norm": mla_q_norm[l], "mla_kv_norm": mla_kv_norm[l], "mla_w_uq": mla_w_uq[l],
            "mla_w_ukv": mla_w_ukv[l], "w_branch": w_branch[l], "w_out": w_out[l],
        }
        j = l // 2
        if l % 2 == 0:
            p.update(ffd_w1=ffd_w1[j], ffd_w3=ffd_w3[j], ffd_w2=ffd_w2[j])
        else:
            p.update(router_w=router_w[j], router_b=router_b[j], moe_w1=moe_w1[j],
                     moe_w3=moe_w3[j], moe_w2=moe_w2[j])
        xp, (k_c, v_c, ckv_c, kr_c, s_c) = trunk_layer(
            xp, adaln(c_ctx[None, :], w_mod[l], b_mod[l]), p, None, None)
        new_k.append(k_c); new_v.append(v_c); new_ckv.append(ckv_c); new_kr.append(kr_c); new_s.append(s_c)
        ctx = {"gqa_k": cache_gqa_k[:, l], "gqa_v": cache_gqa_v[:, l], "mla_ckv": cache_mla_ckv[:, l],
               "mla_krope": cache_mla_krope[:, l], "state_delta": state_delta[:, l]}
        xs, _ = trunk_layer(xs, adaln(c, w_mod[l], b_mod[l]), p, rope, ctx)
    y_prompt = rmsnorm(xp, final_g)
    y_sample = rmsnorm(xs, final_g)
    return (y_prompt, y_sample, jnp.stack(new_k, axis=1), jnp.stack(new_v, axis=1),
            jnp.stack(new_ckv, axis=1), jnp.stack(new_kr, axis=1), jnp.stack(new_s, axis=1))
```

```python
import functools
import math

import jax
import jax.numpy as jnp
import numpy as np
from jax import lax
from jax.experimental import pallas as pl
from jax.experimental.pallas import tpu as pltpu

F32 = jnp.float32
BF16 = jnp.bfloat16

D = 1024
BATCH, SEQ = 32, 256
DEC_BATCH, DEC_SEQ = 4, 2048
DEPTH = 4
PAST = 512
GRID_W = 64
ROPE_THETA = 10000.0
EPS = 1e-6
DN_H, DN_DK, DN_DV, DN_C = 4, 128, 128, 64
GQA_H, GQA_KV, GQA_HD = 8, 2, 64
MLA_H, MLA_QL, MLA_KVL, MLA_NOPE, MLA_ROPE, MLA_V = 8, 256, 128, 64, 32, 64
FF_DENSE, N_EXP, FF_EXP = 2816, 8, 3584

RC = BATCH * SEQ
RL = DEC_BATCH * DEC_SEQ
R = RC + RL

C_BG, C_QKV, C_DG, C_GQ, C_GKV, C_MCQ, C_MCKV, C_SM = 0, 3072, 4608, 5120, 5632, 5888, 6144, 6272
NP = 6400

VMEM_LIMIT = 56 * 1024 * 1024


def _cp(n_grid):
    return pltpu.CompilerParams(dimension_semantics=("arbitrary",) * n_grid,
                                vmem_limit_bytes=VMEM_LIMIT)


def _dot(a, b):
    return jnp.dot(a, b, preferred_element_type=F32)


def _bdot(a, b):
    return jnp.dot(a.astype(BF16), b.astype(BF16), preferred_element_type=F32)


def _bdot_nt(a, b):
    return lax.dot_general(a.astype(BF16), b.astype(BF16), (((1,), (1,)), ((), ())),
                           preferred_element_type=F32)


def _bdot_tn(a, b):
    return lax.dot_general(a.astype(BF16), b.astype(BF16), (((0,), (0,)), ((), ())),
                           preferred_element_type=F32)


def _split2(a):
    hi = a.astype(BF16)
    lo = (a - hi.astype(F32)).astype(BF16)
    return hi, lo


def _split3(a):
    a1 = a.astype(BF16)
    r1 = a - a1.astype(F32)
    a2 = r1.astype(BF16)
    a3 = (r1 - a2.astype(F32)).astype(BF16)
    return a1, a2, a3


def _dot3(a, b):
    ah, al = _split2(a)
    bh, bl = _split2(b)
    return _dot(ah, bh) + (_dot(ah, bl) + _dot(al, bh))


def _mask_dot_r(mask_bf, g):
    g1, g2, g3 = _split3(g)
    return _dot(mask_bf, g1) + (_dot(mask_bf, g2) + _dot(mask_bf, g3))


def _mask_dot_l(g, mask_bf):
    g1, g2, g3 = _split3(g)
    return _dot(g1, mask_bf) + (_dot(g2, mask_bf) + _dot(g3, mask_bf))


def _sigmoid(x):
    return 1.0 / (1.0 + jnp.exp(-x))


def _silu(x):
    return x * _sigmoid(x)


def _softplus(x):
    return jnp.maximum(x, 0.0) + jnp.log1p(jnp.exp(-jnp.abs(x)))


def _mod_group(row0):
    return jnp.where(row0 < RC, 0, 1 + (row0 - RC) // DEC_SEQ)


MODS_TN = 1536


def _mods_body(c_ref, w_ref, b_ref, o_ref):
    s = _silu(c_ref[...])
    o_ref[0] = _bdot(s, w_ref[0]) + b_ref[0]


def _mods(cond8, w_mod, b_mod):
    nj = 6 * D // MODS_TN
    return pl.pallas_call(
        _mods_body,
        grid=(DEPTH, nj),
        in_specs=[pl.BlockSpec((8, D), lambda l, j: (0, 0)),
                  pl.BlockSpec((1, D, MODS_TN), lambda l, j: (l, 0, j)),
                  pl.BlockSpec((1, 1, MODS_TN), lambda l, j: (l, 0, j))],
        out_specs=pl.BlockSpec((1, 8, MODS_TN), lambda l, j: (l, 0, j)),
        out_shape=jax.ShapeDtypeStruct((DEPTH, 8, 6 * D), F32),
        compiler_params=_cp(2),
        name="mods",
    )(cond8, w_mod, b_mod.reshape(DEPTH, 1, 6 * D))


IN_TM, IN_TN = 1024, 1280


def _modnorm(x, g, shift, scale):
    ms = jnp.mean(x * x, axis=-1, keepdims=True)
    y = x * lax.rsqrt(ms + EPS) * g
    return y * (1.0 + scale) + shift


def _inproj_body(x_ref, m_ref, g_ref, w_ref, o_ref, h_scr):
    @pl.when(pl.program_id(1) == 0)
    def _():
        h = _modnorm(x_ref[...], g_ref[...], m_ref[0, 0:1, :], m_ref[0, 1:2, :])
        h_scr[...] = h.astype(BF16)

    o_ref[...] = _dot(h_scr[...], w_ref[...])


def _inproj(x, mods_l, g, w_bf):
    return pl.pallas_call(
        _inproj_body,
        grid=(R // IN_TM, NP // IN_TN),
        in_specs=[pl.BlockSpec((IN_TM, D), lambda i, j: (i, 0)),
                  pl.BlockSpec((1, 6, D), lambda i, j: (_mod_group(i * IN_TM), 0, 0)),
                  pl.BlockSpec((1, D), lambda i, j: (0, 0)),
                  pl.BlockSpec((D, IN_TN), lambda i, j: (0, j))],
        out_specs=pl.BlockSpec((IN_TM, IN_TN), lambda i, j: (i, j)),
        out_shape=jax.ShapeDtypeStruct((R, NP), F32),
        scratch_shapes=[pltpu.VMEM((IN_TM, D), BF16)],
        compiler_params=_cp(2),
        name="inproj",
    )(x, mods_l, g, w_bf)


PREP_TM = 256


def _prep_body(x_ref, xp_ref, xn_ref, sm_ref, smt_ref, cw_ref, al_ref, dt_ref, alt_ref, dtt_ref,
               qkv_o, bg_o, gr_o):
    i = pl.program_id(0)
    n_ctx = RC // PREP_TM
    per_seq = DEC_SEQ // PREP_TM
    is_ctx = i < n_ctx
    j = (i - n_ctx) % per_seq
    first = jnp.logical_or(is_ctx, j == 0)
    last = jnp.logical_or(is_ctx, j == per_seq - 1)

    x = x_ref[...]
    prev_row = jnp.where(first, 0.0, xp_ref[7:8, :])
    next_row = jnp.where(last, 0.0, xn_ref[0:1, :])
    row = lax.broadcasted_iota(jnp.int32, (PREP_TM, 1), 0)
    xm = jnp.where(row == 0, prev_row, pltpu.roll(x, 1, axis=0))
    xq = jnp.where(row == PREP_TM - 1, next_row, pltpu.roll(x, PREP_TM - 1, axis=0))
    w = cw_ref[...]
    y = _silu(w[0:1] * xm + w[1:2] * x + w[2:3] * xq)

    for h in range(DN_H):
        qh = y[:, h * DN_DK:(h + 1) * DN_DK]
        qn = qh * lax.rsqrt(jnp.sum(qh * qh, axis=-1, keepdims=True) + EPS) * (DN_DK ** -0.5)
        qkv_o[:, h * DN_DK:(h + 1) * DN_DK] = qn
        kh = y[:, 512 + h * DN_DK:512 + (h + 1) * DN_DK]
        kn = kh * lax.rsqrt(jnp.sum(kh * kh, axis=-1, keepdims=True) + EPS)
        qkv_o[:, 512 + h * DN_DK:512 + (h + 1) * DN_DK] = kn
    qkv_o[:, 1024:1536] = y[:, 1024:1536]

    r = lax.broadcasted_iota(jnp.int32, (PREP_TM, PREP_TM), 0)
    c = lax.broadcasted_iota(jnp.int32, (PREP_TM, PREP_TM), 1)
    same = (r // DN_C) == (c // DN_C)
    low = jnp.where(jnp.logical_and(same, r >= c), 1.0, 0.0).astype(BF16)
    upp = jnp.where(jnp.logical_and(same, r <= c), 1.0, 0.0).astype(BF16)

    sm = sm_ref[...]
    beta = _sigmoid(sm)
    g = -jnp.exp(al_ref[...]) * _softplus(sm + dt_ref[...])
    gc_f = _mask_dot_r(low, g)
    gc_b = _mask_dot_r(upp, g)
    lane = lax.broadcasted_iota(jnp.int32, (1, 128), 1)
    bg_o[...] = jnp.where(lane < 8, beta, jnp.where(lane < 12, gc_f, jnp.where(lane < 16, gc_b, 0.0)))

    gt = -jnp.exp(alt_ref[...]) * _softplus(smt_ref[...] + dtt_ref[...])
    gct_f = _mask_dot_l(gt, upp)
    gct_b = _mask_dot_l(gt, low)
    sub = lax.broadcasted_iota(jnp.int32, (16, 1), 0)
    gct = jnp.where(sub < 12, gct_f, gct_b)
    for k in range(PREP_TM // DN_C):
        gr_o[k] = gct[8:16, k * DN_C:(k + 1) * DN_C]


def _dn_prep(proj, sm_t, conv_w, al128, dt128, al_t, dt_t):
    nb8 = R // 8
    qb = C_QKV // 1536
    return pl.pallas_call(
        _prep_body,
        grid=(R // PREP_TM,),
        in_specs=[pl.BlockSpec((PREP_TM, 1536), lambda i: (i, qb)),
                  pl.BlockSpec((8, 1536), lambda i: (jnp.maximum(i * (PREP_TM // 8) - 1, 0), qb)),
                  pl.BlockSpec((8, 1536), lambda i: (jnp.minimum((i + 1) * (PREP_TM // 8), nb8 - 1), qb)),
                  pl.BlockSpec((PREP_TM, 128), lambda i: (i, C_SM // 128)),
                  pl.BlockSpec((16, PREP_TM), lambda i: (0, i)),
                  pl.BlockSpec((3, 1536), lambda i: (0, 0)),
                  pl.BlockSpec((1, 128), lambda i: (0, 0)),
                  pl.BlockSpec((1, 128), lambda i: (0, 0)),
                  pl.BlockSpec((16, 1), lambda i: (0, 0)),
                  pl.BlockSpec((16, 1), lambda i: (0, 0))],
        out_specs=[pl.BlockSpec((PREP_TM, 1536), lambda i: (i, 0)),
                   pl.BlockSpec((PREP_TM, 128), lambda i: (i, 0)),
                   pl.BlockSpec((PREP_TM // DN_C, 8, DN_C), lambda i: (i, 0, 0))],
        out_shape=[jax.ShapeDtypeStruct((R, 1536), F32),
                   jax.ShapeDtypeStruct((R, 128), F32),
                   jax.ShapeDtypeStruct((R // DN_C, 8, DN_C), F32)],
        compiler_params=_cp(1),
        name="dn_prep",
    )(proj, proj, proj, proj, sm_t, conv_w, al128, dt128, al_t, dt_t)


def _unit_tri_inverse(a):
    n = a.shape[0]
    r = lax.broadcasted_iota(jnp.int32, (n, n), 0)
    c = lax.broadcasted_iota(jnp.int32, (n, n), 1)
    t = jnp.where(r == c, 1.0, 0.0) - a
    p = a
    for _ in range(5):
        p = _dot3(p, p)
        t = t + _dot3(t, p)
    return t


def _scan_body(T, has_s0, want_s, *refs):
    n_chunks = T // DN_C
    it = iter(refs)
    x_ref, bg_ref, gr_ref = next(it), next(it), next(it)
    s0_ref = next(it) if has_s0 else None
    o_ref = next(it)
    so_ref = next(it) if want_s else None
    s_scr = next(it)

    for d in range(2):
        for h in range(DN_H):
            s_scr[d, h] = s0_ref[0, d, h] if has_s0 else jnp.zeros((DN_DK, DN_DV), F32)
    o_ref[...] = jnp.zeros((T, DN_H * DN_DV), F32)

    r = lax.broadcasted_iota(jnp.int32, (DN_C, DN_C), 0)
    c = lax.broadcasted_iota(jnp.int32, (DN_C, DN_C), 1)

    def step(n, carry):
        for d in range(2):
            ci = n if d == 0 else n_chunks - 1 - n
            s = pl.multiple_of(ci * DN_C, DN_C)
            incl = (r >= c) if d == 0 else (r <= c)
            strict = (r > c) if d == 0 else (r < c)
            bg = bg_ref[pl.ds(s, DN_C), :]
            grow_all = gr_ref[ci]
            for h in range(DN_H):
                idx = d * DN_H + h
                q = x_ref[pl.ds(s, DN_C), h * DN_DK:(h + 1) * DN_DK]
                k = x_ref[pl.ds(s, DN_C), 512 + h * DN_DK:512 + (h + 1) * DN_DK]
                v = x_ref[pl.ds(s, DN_C), 1024 + h * DN_DV:1024 + (h + 1) * DN_DV]
                beta = bg[:, idx:idx + 1]
                gcol = bg[:, 8 + idx:9 + idx]
                grow = grow_all[idx:idx + 1, :]
                decay = jnp.exp(jnp.where(incl, gcol - grow, -1e30))
                kb = k * beta
                a = jnp.where(strict, _dot3_nt(kb, k) * decay, 0.0)
                tinv = _unit_tri_inverse(a)
                egc = jnp.exp(gcol)
                u = _dot3(tinv, v * beta)
                w = _dot3(tinv, kb * egc)
                a_intra = _bdot_nt(q, k) * decay
                st = s_scr[d, h]
                v_new = u - _bdot(w, st)
                o = _bdot(q * egc, st) + _bdot(a_intra, v_new)
                glast = gcol[DN_C - 1:DN_C, :] if d == 0 else gcol[0:1, :]
                kdec = k * jnp.exp(glast - gcol)
                s_scr[d, h] = st * jnp.exp(glast) + _bdot_tn(kdec, v_new)
                o_ref[pl.ds(s, DN_C), h * DN_DV:(h + 1) * DN_DV] += o
        return carry

    lax.fori_loop(0, n_chunks, step, 0)
    if want_s:
        for d in range(2):
            for h in range(DN_H):
                so_ref[0, d, h] = s_scr[d, h]


def _dot3_nt(a, b):
    ah, al = _split2(a)
    bh, bl = _split2(b)
    dn = (((1,), (1,)), ((), ()))
    f = lambda x, y: lax.dot_general(x, y, dn, preferred_element_type=F32)
    return f(ah, bh) + (f(ah, bl) + f(al, bh))


def _dn_scan(qkvn, bg, gr, s0, T, n_seq, row_off, want_s):
    has_s0 = s0 is not None
    nc = T // DN_C
    bo = row_off // T
    in_specs = [pl.BlockSpec((T, 1536), lambda i: (bo + i, 0)),
                pl.BlockSpec((T, 128), lambda i: (bo + i, 0)),
                pl.BlockSpec((nc, 8, DN_C), lambda i: (bo + i, 0, 0))]
    args = [qkvn, bg, gr]
    if has_s0:
        in_specs.append(pl.BlockSpec((1, 2, DN_H, DN_DK, DN_DV), lambda i: (i, 0, 0, 0, 0)))
        args.append(s0)
    out_specs = [pl.BlockSpec((T, DN_H * DN_DV), lambda i: (i, 0))]
    out_shape = [jax.ShapeDtypeStruct((n_seq * T, DN_H * DN_DV), F32)]
    if want_s:
        out_specs.append(pl.BlockSpec((1, 2, DN_H, DN_DK, DN_DV), lambda i: (i, 0, 0, 0, 0)))
        out_shape.append(jax.ShapeDtypeStruct((n_seq, 2, DN_H, DN_DK, DN_DV), F32))
    return pl.pallas_call(
        functools.partial(_scan_body, T, has_s0, want_s),
        grid=(n_seq,),
        in_specs=in_specs,
        out_specs=out_specs,
        out_shape=out_shape,
        scratch_shapes=[pltpu.VMEM((2, DN_H, DN_DK, DN_DV), F32)],
        compiler_params=_cp(1),
        name="dn_scan_%d" % T,
    )(*args)


def _group_mean_matrix(width, group):
    r = lax.broadcasted_iota(jnp.int32, (width, width), 0)
    c = lax.broadcasted_iota(jnp.int32, (width, width), 1)
    return jnp.where((r // group) == (c // group), 1.0 / group, 0.0).astype(BF16)


def _group_rmsnorm(x, w, group):
    m = _group_mean_matrix(x.shape[-1], group)
    hi, lo = _split2(x * x)
    ms = _dot(hi, m) + _dot(lo, m)
    return x * lax.rsqrt(ms + EPS) * w


def _rope(x, cos, sin_signed, group):
    width = x.shape[-1]
    half = group // 2
    lane = lax.broadcasted_iota(jnp.int32, (1, width), 1)
    swapped = jnp.where((lane % group) < half,
                        pltpu.roll(x, width - half, axis=1), pltpu.roll(x, half, axis=1))
    return x * cos + swapped * sin_signed


def _softmax_pv(s, v_bf):
    m = jnp.max(s, axis=-1, keepdims=True)
    p = jnp.exp(s - m)
    l = jnp.sum(p, axis=-1, keepdims=True)
    return _dot(p.astype(BF16), v_bf) / l


GQA_G = GQA_H // GQA_KV


def _gqa_body(T, TQ, latent, *refs):
    it = iter(refs)
    q_ref, kv_ref, qw_ref, kw_ref = next(it), next(it), next(it), next(it)
    if latent:
        cq_ref, sq_ref, ck_ref, sk_ref, kc_ref, vc_ref = (next(it) for _ in range(6))
    o_ref = next(it)
    kn_ref = None if latent else next(it)
    k_scr, v_scr = next(it), next(it)

    @pl.when(pl.program_id(1) == 0)
    def _():
        kv = kv_ref[...]
        k = _group_rmsnorm(kv[:, 0:128], kw_ref[...], GQA_HD)
        v = kv[:, 128:256]
        if latent:
            k = _rope(k, ck_ref[...], sk_ref[...], GQA_HD)
        else:
            kn_ref[...] = k
        for g in range(GQA_KV):
            k_scr[g, 0:T, :] = k[:, g * GQA_HD:(g + 1) * GQA_HD].astype(BF16)
            v_scr[g, 0:T, :] = v[:, g * GQA_HD:(g + 1) * GQA_HD].astype(BF16)
            if latent:
                k_scr[g, T:T + PAST, :] = kc_ref[0, :, g * GQA_HD:(g + 1) * GQA_HD].astype(BF16)
                v_scr[g, T:T + PAST, :] = vc_ref[0, :, g * GQA_HD:(g + 1) * GQA_HD].astype(BF16)

    q = _group_rmsnorm(q_ref[...], qw_ref[...], GQA_HD)
    if latent:
        q = _rope(q, cq_ref[...], sq_ref[...], GQA_HD)
    q = q * (GQA_HD ** -0.5)
    for g in range(GQA_KV):
        qg = jnp.concatenate(
            [q[:, (g * GQA_G + j) * GQA_HD:(g * GQA_G + j + 1) * GQA_HD] for j in range(GQA_G)], axis=0)
        s = _bdot_nt(qg, k_scr[g])
        o = _softmax_pv(s, v_scr[g])
        for j in range(GQA_G):
            hh = g * GQA_G + j
            o_ref[:, hh * GQA_HD:(hh + 1) * GQA_HD] = o[j * TQ:(j + 1) * TQ, :]


def _gqa(proj, qw, kw, T, TQ, n_seq, row_off, rope=None, cache=None):
    latent = rope is not None
    tk = T + (PAST if latent else 0)
    nq = T // TQ
    qo = row_off // TQ
    so = row_off // T
    in_specs = [pl.BlockSpec((TQ, 512), lambda i, j: (qo + i * nq + j, C_GQ // 512)),
                pl.BlockSpec((T, 256), lambda i, j: (so + i, C_GKV // 256)),
                pl.BlockSpec((1, 512), lambda i, j: (0, 0)),
                pl.BlockSpec((1, 128), lambda i, j: (0, 0))]
    args = [proj, proj, qw, kw]
    if latent:
        cq, sq, ck, sk = rope
        kc, vc = cache
        in_specs += [pl.BlockSpec((TQ, 512), lambda i, j: (j, 0)),
                     pl.BlockSpec((TQ, 512), lambda i, j: (j, 0)),
                     pl.BlockSpec((T, 128), lambda i, j: (0, 0)),
                     pl.BlockSpec((T, 128), lambda i, j: (0, 0)),
                     pl.BlockSpec((1, PAST, 128), lambda i, j: (i, 0, 0)),
                     pl.BlockSpec((1, PAST, 128), lambda i, j: (i, 0, 0))]
        args += [cq, sq, ck, sk, kc, vc]
    out_specs = [pl.BlockSpec((TQ, 512), lambda i, j: (i * nq + j, 0))]
    out_shape = [jax.ShapeDtypeStruct((n_seq * T, 512), F32)]
    if not latent:
        out_specs.append(pl.BlockSpec((T, 128), lambda i, j: (i, 0)))
        out_shape.append(jax.ShapeDtypeStruct((n_seq * T, 128), F32))
    return pl.pallas_call(
        functools.partial(_gqa_body, T, TQ, latent),
        grid=(n_seq, nq),
        in_specs=in_specs,
        out_specs=out_specs,
        out_shape=out_shape,
        scratch_shapes=[pltpu.VMEM((GQA_KV, tk, GQA_HD), BF16), pltpu.VMEM((GQA_KV, tk, GQA_HD), BF16)],
        compiler_params=_cp(2),
        name="gqa_%d" % T,
    )(*args)


MLA_DQ = MLA_NOPE + MLA_ROPE
MLA_KV_ROWS = 512


def _mla_body(T, TQ, latent, *refs):
    it = iter(refs)
    cq_ref, ckv_ref, sm_ref, wq_ref, wkv_ref, qw_ref, kvw_ref = (next(it) for _ in range(7))
    if latent:
        cosq_ref, sinq_ref, cosk_ref, sink_ref, cc_ref, kc_ref = (next(it) for _ in range(6))
    o_ref = next(it)
    cn_ref = None if latent else next(it)
    k_scr, v_scr = next(it), next(it)
    tk = T + (PAST if latent else 0)

    @pl.when(pl.program_id(1) == 0)
    def _():
        x = ckv_ref[...]
        ms = jnp.mean(x * x, axis=-1, keepdims=True)
        ckv = x * lax.rsqrt(ms + EPS) * kvw_ref[...]
        sm = sm_ref[...]
        if latent:
            sm = _rope_small(sm, cosk_ref[...], sink_ref[...])
        else:
            cn_ref[...] = ckv
        kr = sm[:, 16:16 + MLA_ROPE]

        def put(rows0, ckv_rows, kr_rows):
            n = ckv_rows.shape[0]
            kv = _bdot(ckv_rows, wkv_ref[...])
            krb = kr_rows.astype(BF16)
            for h in range(MLA_H):
                k_scr[h, rows0:rows0 + n, 0:MLA_NOPE] = kv[:, h * MLA_NOPE:(h + 1) * MLA_NOPE].astype(BF16)
                k_scr[h, rows0:rows0 + n, MLA_NOPE:MLA_DQ] = krb
                v_scr[h, rows0:rows0 + n, :] = kv[:, 512 + h * MLA_V:512 + (h + 1) * MLA_V].astype(BF16)

        for r0 in range(0, T, MLA_KV_ROWS):
            r1 = min(r0 + MLA_KV_ROWS, T)
            put(r0, ckv[r0:r1], kr[r0:r1])
        if latent:
            put(T, cc_ref[0], kc_ref[0])

    x = cq_ref[...]
    ms = jnp.mean(x * x, axis=-1, keepdims=True)
    cq = x * lax.rsqrt(ms + EPS) * qw_ref[...]
    qf = _bdot(cq, wq_ref[...])
    qn = qf[:, 0:512]
    qr = qf[:, 512:768]
    if latent:
        qr = _rope(qr, cosq_ref[...], sinq_ref[...], MLA_ROPE)
    scale = MLA_DQ ** -0.5
    for h in range(MLA_H):
        qh = jnp.concatenate([qn[:, h * MLA_NOPE:(h + 1) * MLA_NOPE],
                              qr[:, h * MLA_ROPE:(h + 1) * MLA_ROPE]], axis=1) * scale
        s = _bdot_nt(qh, k_scr[h])
        o_ref[:, h * MLA_V:(h + 1) * MLA_V] = _softmax_pv(s, v_scr[h])


def _rope_small(sm, cos, sin_signed):
    lane = lax.broadcasted_iota(jnp.int32, (1, 128), 1)
    half = MLA_ROPE // 2
    swapped = jnp.where(lane < 16 + half, pltpu.roll(sm, 128 - half, axis=1), pltpu.roll(sm, half, axis=1))
    return sm * cos + swapped * sin_signed


def _mla(proj, wq, wkv, qw, kvw, T, TQ, n_seq, row_off, rope=None, cache=None):
    latent = rope is not None
    tk = T + (PAST if latent else 0)
    nq = T // TQ
    qo = row_off // TQ
    so = row_off // T
    in_specs = [pl.BlockSpec((TQ, 256), lambda i, j: (qo + i * nq + j, C_MCQ // 256)),
                pl.BlockSpec((T, 128), lambda i, j: (so + i, C_MCKV // 128)),
                pl.BlockSpec((T, 128), lambda i, j: (so + i, C_SM // 128)),
                pl.BlockSpec((MLA_QL, 768), lambda i, j: (0, 0)),
                pl.BlockSpec((MLA_KVL, 1024), lambda i, j: (0, 0)),
                pl.BlockSpec((1, 256), lambda i, j: (0, 0)),
                pl.BlockSpec((1, 128), lambda i, j: (0, 0))]
    args = [proj, proj, proj, wq, wkv, qw, kvw]
    if latent:
        cosq, sinq, cosk, sink = rope
        cc, kc = cache
        in_specs += [pl.BlockSpec((TQ, 256), lambda i, j: (j, 0)),
                     pl.BlockSpec((TQ, 256), lambda i, j: (j, 0)),
                     pl.BlockSpec((T, 128), lambda i, j: (0, 0)),
                     pl.BlockSpec((T, 128), lambda i, j: (0, 0)),
                     pl.BlockSpec((1, PAST, 128), lambda i, j: (i, 0, 0)),
                     pl.BlockSpec((1, PAST, MLA_ROPE), lambda i, j: (i, 0, 0))]
        args += [cosq, sinq, cosk, sink, cc, kc]
    out_specs = [pl.BlockSpec((TQ, 512), lambda i, j: (i * nq + j, 0))]
    out_shape = [jax.ShapeDtypeStruct((n_seq * T, 512), F32)]
    if not latent:
        out_specs.append(pl.BlockSpec((T, 128), lambda i, j: (i, 0)))
        out_shape.append(jax.ShapeDtypeStruct((n_seq * T, 128), F32))
    return pl.pallas_call(
        functools.partial(_mla_body, T, TQ, latent),
        grid=(n_seq, nq),
        in_specs=in_specs,
        out_specs=out_specs,
        out_shape=out_shape,
        scratch_shapes=[pltpu.VMEM((MLA_H, tk, MLA_DQ), BF16), pltpu.VMEM((MLA_H, tk, MLA_V), BF16)],
        compiler_params=_cp(2),
        name="mla_%d" % T,
    )(*args)


MG_TM = 512


def _merge_body(x_ref, m_ref, odn_ref, dg_ref, og_ref, om_ref, bg_ref, ng_ref, wb_ref, wo_ref, o_ref):
    odn = odn_ref[...]
    dg = dg_ref[...]
    ng = ng_ref[...]
    parts = []
    for h in range(DN_H):
        oh = odn[:, h * DN_DV:(h + 1) * DN_DV]
        ms = jnp.mean(oh * oh, axis=-1, keepdims=True)
        parts.append(oh * lax.rsqrt(ms + EPS) * ng * _silu(dg[:, h * DN_DV:(h + 1) * DN_DV]))
    br0 = jnp.concatenate(parts, axis=1)
    merged = _sigmoid(bg_ref[:, 0:D]) * _bdot(br0, wb_ref[0])
    merged = merged + _sigmoid(bg_ref[:, D:2 * D]) * _bdot(og_ref[...], wb_ref[1])
    merged = merged + _sigmoid(bg_ref[:, 2 * D:3 * D]) * _bdot(om_ref[...], wb_ref[2])
    out = _bdot(merged, wo_ref[...])
    o_ref[...] = x_ref[...] + m_ref[0, 2:3, :] * out


def _merge(x, mods_l, odn, proj, ogqa, omla, ng, wb, wo):
    row = lambda i: (i, 0)
    return pl.pallas_call(
        _merge_body,
        grid=(R // MG_TM,),
        in_specs=[pl.BlockSpec((MG_TM, D), row),
                  pl.BlockSpec((1, 6, D), lambda i: (_mod_group(i * MG_TM), 0, 0)),
                  pl.BlockSpec((MG_TM, 512), row),
                  pl.BlockSpec((MG_TM, 512), lambda i: (i, C_DG // 512)),
                  pl.BlockSpec((MG_TM, 512), row),
                  pl.BlockSpec((MG_TM, 512), row),
                  pl.BlockSpec((MG_TM, 3 * D), lambda i: (i, C_BG // (3 * D))),
                  pl.BlockSpec((1, DN_DV), lambda i: (0, 0)),
                  pl.BlockSpec((3, 512, D), lambda i: (0, 0, 0)),
                  pl.BlockSpec((D, D), lambda i: (0, 0))],
        out_specs=pl.BlockSpec((MG_TM, D), row),
        out_shape=jax.ShapeDtypeStruct((R, D), F32),
        compiler_params=_cp(1),
        name="merge",
    )(x, mods_l, odn, proj, ogqa, omla, proj, ng, wb, wo)


FF_TM, FF_TF = 1024, 1408


def _ffn_body(x_ref, m_ref, g_ref, w1_ref, w3_ref, w2_ref, o_ref, h_scr, acc_scr):
    f = pl.program_id(1)

    @pl.when(f == 0)
    def _():
        h = _modnorm(x_ref[...], g_ref[...], m_ref[0, 3:4, :], m_ref[0, 4:5, :])
        h_scr[...] = h.astype(BF16)
        acc_scr[...] = jnp.zeros_like(acc_scr)

    h = h_scr[...]
    a = _silu(_dot(h, w1_ref[...])) * _dot(h, w3_ref[...])
    acc_scr[...] += _dot(a.astype(BF16), w2_ref[...])

    @pl.when(f == pl.num_programs(1) - 1)
    def _():
        o_ref[...] = x_ref[...] + m_ref[0, 5:6, :] * acc_scr[...]


def _ffn(x, mods_l, g, w1, w3, w2):
    return pl.pallas_call(
        _ffn_body,
        grid=(R // FF_TM, FF_DENSE // FF_TF),
        in_specs=[pl.BlockSpec((FF_TM, D), lambda i, f: (i, 0)),
                  pl.BlockSpec((1, 6, D), lambda i, f: (_mod_group(i * FF_TM), 0, 0)),
                  pl.BlockSpec((1, D), lambda i, f: (0, 0)),
                  pl.BlockSpec((D, FF_TF), lambda i, f: (0, f)),
                  pl.BlockSpec((D, FF_TF), lambda i, f: (0, f)),
                  pl.BlockSpec((FF_TF, D), lambda i, f: (f, 0))],
        out_specs=pl.BlockSpec((FF_TM, D), lambda i, f: (i, 0)),
        out_shape=jax.ShapeDtypeStruct((R, D), F32),
        scratch_shapes=[pltpu.VMEM((FF_TM, D), BF16), pltpu.VMEM((FF_TM, D), F32)],
        compiler_params=_cp(2),
        name="ffn",
    )(x, mods_l, g, w1, w3, w2)


MOE_TM, MOE_TF = 512, 1792


def _top2_combine(logits):
    lane = lax.broadcasted_iota(jnp.int32, logits.shape, 1)
    m1 = jnp.max(logits, axis=-1, keepdims=True)
    i1 = jnp.min(jnp.where(logits == m1, lane, 128), axis=-1, keepdims=True)
    sel1 = lane == i1
    rest = jnp.where(sel1, -jnp.inf, logits)
    m2 = jnp.max(rest, axis=-1, keepdims=True)
    i2 = jnp.min(jnp.where(rest == m2, lane, 128), axis=-1, keepdims=True)
    sel2 = lane == i2
    e2 = jnp.exp(m2 - m1)
    p1 = 1.0 / (1.0 + e2)
    p2 = e2 / (1.0 + e2)
    return jnp.where(sel1, p1, 0.0) + jnp.where(sel2, p2, 0.0)


def _moe_body(x_ref, m_ref, g_ref, rw_ref, rb_ref, w1_ref, w3_ref, w2_ref, o_ref, h_scr, comb_scr, acc_scr):
    e = pl.program_id(1)
    f = pl.program_id(2)

    @pl.when(jnp.logical_and(e == 0, f == 0))
    def _():
        h = _modnorm(x_ref[...], g_ref[...], m_ref[0, 3:4, :], m_ref[0, 4:5, :])
        h_scr[...] = h.astype(BF16)
        logits = _dot3(h, rw_ref[...]) + rb_ref[...]
        comb_scr[...] = _top2_combine(logits)
        acc_scr[...] = jnp.zeros_like(acc_scr)

    h = h_scr[...]
    a = _silu(_dot(h, w1_ref[0])) * _dot(h, w3_ref[0])
    lane = lax.broadcasted_iota(jnp.int32, (1, 128), 1)
    ce = jnp.sum(jnp.where(lane == e, comb_scr[...], 0.0), axis=-1, keepdims=True)
    acc_scr[...] += ce * _dot(a.astype(BF16), w2_ref[0])

    @pl.when(jnp.logical_and(e == pl.num_programs(1) - 1, f == pl.num_programs(2) - 1))
    def _():
        o_ref[...] = x_ref[...] + m_ref[0, 5:6, :] * acc_scr[...]


def _moe(x, mods_l, g, rw128, rb128, w1, w3, w2):
    return pl.pallas_call(
        _moe_body,
        grid=(R // MOE_TM, N_EXP, FF_EXP // MOE_TF),
        in_specs=[pl.BlockSpec((MOE_TM, D), lambda i, e, f: (i, 0)),
                  pl.BlockSpec((1, 6, D), lambda i, e, f: (_mod_group(i * MOE_TM), 0, 0)),
                  pl.BlockSpec((1, D), lambda i, e, f: (0, 0)),
                  pl.BlockSpec((D, 128), lambda i, e, f: (0, 0)),
                  pl.BlockSpec((1, 128), lambda i, e, f: (0, 0)),
                  pl.BlockSpec((1, D, MOE_TF), lambda i, e, f: (e, 0, f)),
                  pl.BlockSpec((1, D, MOE_TF), lambda i, e, f: (e, 0, f)),
                  pl.BlockSpec((1, MOE_TF, D), lambda i, e, f: (e, f, 0))],
        out_specs=pl.BlockSpec((MOE_TM, D), lambda i, e, f: (i, 0)),
        out_shape=jax.ShapeDtypeStruct((R, D), F32),
        scratch_shapes=[pltpu.VMEM((MOE_TM, D), BF16), pltpu.VMEM((MOE_TM, 128), F32),
                        pltpu.VMEM((MOE_TM, D), F32)],
        compiler_params=_cp(3),
        name="moe",
    )(x, mods_l, g, rw128, rb128, w1, w3, w2)


FN_TM = 1024


def _final_body(x_ref, g_ref, o_ref):
    x = x_ref[...]
    ms = jnp.mean(x * x, axis=-1, keepdims=True)
    o_ref[...] = x * lax.rsqrt(ms + EPS) * g_ref[...]


def _final_norm(x, g):
    return pl.pallas_call(
        _final_body,
        grid=(R // FN_TM,),
        in_specs=[pl.BlockSpec((FN_TM, D), lambda i: (i, 0)), pl.BlockSpec((1, D), lambda i: (0, 0))],
        out_specs=pl.BlockSpec((FN_TM, D), lambda i: (i, 0)),
        out_shape=jax.ShapeDtypeStruct((R, D), F32),
        compiler_params=_cp(1),
        name="final_norm",
    )(x, g)


def _rope_tables(n_tokens, rot_dim):
    t = np.arange(n_tokens)
    row = (t // GRID_W).astype(np.float32)
    col = (t % GRID_W).astype(np.float32)
    n_freq = rot_dim // 4
    inv = (ROPE_THETA ** (-jnp.arange(n_freq, dtype=F32) / n_freq))
    ang = jnp.concatenate([jnp.asarray(row)[:, None] * inv, jnp.asarray(col)[:, None] * inv], axis=-1)
    cos, sin = jnp.cos(ang), jnp.sin(ang)
    return jnp.concatenate([cos, cos], axis=-1), jnp.concatenate([-sin, sin], axis=-1)


def _permute_w_in(w):
    return jnp.concatenate(
        [w[:, 3248:6320], w[:, 0:1536], w[:, 1536:2048], w[:, 2064:2576], w[:, 2576:2832],
         w[:, 2832:3088], w[:, 3088:3216], w[:, 2048:2064], w[:, 3216:3248],
         jnp.zeros((D, NP - 6320), w.dtype)], axis=1).astype(BF16)


def kernel(x_prompt, x_sample, c, cache_gqa_k, cache_gqa_v, cache_mla_ckv, cache_mla_krope, state_delta, c_ctx, w_mod, b_mod, norm1_g, norm2_g, w_in, dn_conv_w, dn_a_log, dn_dt_bias, dn_norm_g, gqa_q_norm, gqa_k_norm, mla_q_norm, mla_kv_norm, mla_w_uq, mla_w_ukv, w_branch, w_out, ffd_w1, ffd_w3, ffd_w2, router_w, router_b, moe_w1, moe_w3, moe_w2, final_g):
    x = jnp.concatenate([x_prompt.reshape(RC, D), x_sample.reshape(RL, D)], axis=0)
    cond8 = jnp.concatenate([c_ctx[None, :], c, jnp.zeros((3, D), F32)], axis=0)
    mods = _mods(cond8, w_mod, b_mod).reshape(DEPTH, 8, 6, D)

    cg, sg = _rope_tables(DEC_SEQ, GQA_HD)
    gqa_rope = (jnp.tile(cg, (1, GQA_H)), jnp.tile(sg, (1, GQA_H)),
                jnp.tile(cg, (1, GQA_KV)), jnp.tile(sg, (1, GQA_KV)))
    cm, sm_ = _rope_tables(DEC_SEQ, MLA_ROPE)
    padk = lambda t: jnp.pad(t, ((0, 0), (16, 128 - 16 - MLA_ROPE)))
    mla_rope = (jnp.tile(cm, (1, MLA_H)), jnp.tile(sm_, (1, MLA_H)),
                jnp.pad(cm, ((0, 0), (16, 128 - 16 - MLA_ROPE)), constant_values=1.0), padk(sm_))

    new_k, new_v, new_ckv, new_kr, new_s = [], [], [], [], []
    for l in range(DEPTH):
        proj = _inproj(x, mods[l], norm1_g[l][None, :], _permute_w_in(w_in[l]))

        pad128 = lambda v: jnp.pad(v.reshape(1, 8), ((0, 0), (8, 112)))
        padt = lambda v: jnp.pad(v.reshape(8, 1), ((8, 0), (0, 0)))
        sm_t = proj[:, C_SM:C_SM + 16].T
        qkvn, bg, gr = _dn_prep(proj, sm_t, dn_conv_w[l], pad128(dn_a_log[l]), pad128(dn_dt_bias[l]),
                                padt(dn_a_log[l]), padt(dn_dt_bias[l]))
        o_dn_c, s_c = _dn_scan(qkvn, bg, gr, None, SEQ, BATCH, 0, True)
        (o_dn_l,) = _dn_scan(qkvn, bg, gr, state_delta[:, l], DEC_SEQ, DEC_BATCH, RC, False)
        o_dn = jnp.concatenate([o_dn_c, o_dn_l], axis=0)

        qw = jnp.tile(gqa_q_norm[l][None, :], (1, GQA_H))
        kw = jnp.tile(gqa_k_norm[l][None, :], (1, GQA_KV))
        o_g_c, kn_c = _gqa(proj, qw, kw, SEQ, SEQ, BATCH, 0)
        (o_g_l,) = _gqa(proj, qw, kw, DEC_SEQ, 128, DEC_BATCH, RC, rope=gqa_rope,
                        cache=(cache_gqa_k[:, l].reshape(DEC_BATCH, PAST, 128),
                               cache_gqa_v[:, l].reshape(DEC_BATCH, PAST, 128)))
        o_gqa = jnp.concatenate([o_g_c, o_g_l], axis=0)

        wq = mla_w_uq[l].reshape(MLA_QL, MLA_H, MLA_DQ)
        wq = jnp.concatenate([wq[:, :, :MLA_NOPE].reshape(MLA_QL, -1), wq[:, :, MLA_NOPE:].reshape(MLA_QL, -1)],
                             axis=1).astype(BF16)
        wkv = mla_w_ukv[l].reshape(MLA_KVL, MLA_H, MLA_NOPE + MLA_V)
        wkv = jnp.concatenate([wkv[:, :, :MLA_NOPE].reshape(MLA_KVL, -1), wkv[:, :, MLA_NOPE:].reshape(MLA_KVL, -1)],
                              axis=1).astype(BF16)
        mqw, mkvw = mla_q_norm[l][None, :], mla_kv_norm[l][None, :]
        o_m_c, ckv_c = _mla(proj, wq, wkv, mqw, mkvw, SEQ, SEQ, BATCH, 0)
        (o_m_l,) = _mla(proj, wq, wkv, mqw, mkvw, DEC_SEQ, 256, DEC_BATCH, RC, rope=mla_rope,
                        cache=(cache_mla_ckv[:, l], cache_mla_krope[:, l]))
        o_mla = jnp.concatenate([o_m_c, o_m_l], axis=0)

        x = _merge(x, mods[l], o_dn, proj, o_gqa, o_mla, dn_norm_g[l][None, :],
                   w_branch[l].astype(BF16), w_out[l].astype(BF16))

        j = l // 2
        if l % 2 == 0:
            x = _ffn(x, mods[l], norm2_g[l][None, :], ffd_w1[j].astype(BF16), ffd_w3[j].astype(BF16),
                     ffd_w2[j].astype(BF16))
        else:
            rw128 = jnp.pad(router_w[j], ((0, 0), (0, 128 - N_EXP)))
            rb128 = jnp.pad(router_b[j][None, :], ((0, 0), (0, 128 - N_EXP)), constant_values=-jnp.inf)
            x = _moe(x, mods[l], norm2_g[l][None, :], rw128, rb128, moe_w1[j].astype(BF16),
                     moe_w3[j].astype(BF16), moe_w2[j].astype(BF16))

        new_k.append(kn_c.reshape(BATCH, SEQ, GQA_KV, GQA_HD))
        new_v.append(proj[:RC, C_GKV + 128:C_GKV + 256].reshape(BATCH, SEQ, GQA_KV, GQA_HD))
        new_ckv.append(ckv_c.reshape(BATCH, SEQ, MLA_KVL))
        new_kr.append(proj[:RC, C_SM + 16:C_SM + 16 + MLA_ROPE].reshape(BATCH, SEQ, MLA_ROPE))
        new_s.append(s_c)

    y = _final_norm(x, final_g[None, :])
    return (y[:RC].reshape(BATCH, SEQ, D), y[RC:].reshape(DEC_BATCH, DEC_SEQ, D),
            jnp.stack(new_k, axis=1), jnp.stack(new_v, axis=1), jnp.stack(new_ckv, axis=1),
            jnp.stack(new_kr, axis=1), jnp.stack(new_s, axis=1))
```

```python
import functools
import math

import jax
import jax.numpy as jnp
import numpy as np
from jax import lax
from jax.experimental import pallas as pl
from jax.experimental.pallas import tpu as pltpu

F32 = jnp.float32
BF16 = jnp.bfloat16

D = 1024
BATCH, SEQ = 32, 256
DEC_BATCH, DEC_SEQ = 4, 2048
DEPTH = 4
PAST = 512
GRID_W = 64
ROPE_THETA = 10000.0
EPS = 1e-6
DN_H, DN_DK, DN_DV, DN_C = 4, 128, 128, 64
GQA_H, GQA_KV, GQA_HD = 8, 2, 64
MLA_H, MLA_QL, MLA_KVL, MLA_NOPE, MLA_ROPE, MLA_V = 8, 256, 128, 64, 32, 64
FF_DENSE, N_EXP, FF_EXP = 2816, 8, 3584

RC = BATCH * SEQ
RL = DEC_BATCH * DEC_SEQ
R = RC + RL

C_BG, C_QKV, C_DG, C_GQ, C_GKV, C_MCQ, C_MCKV, C_SM = 0, 3072, 4608, 5120, 5632, 5888, 6144, 6272
NP = 6400

VMEM_LIMIT = 56 * 1024 * 1024


def _cp(n_grid):
    return pltpu.CompilerParams(dimension_semantics=("arbitrary",) * n_grid,
                                vmem_limit_bytes=VMEM_LIMIT)


def _dot(a, b):
    return jnp.dot(a, b, preferred_element_type=F32)


def _bdot(a, b):
    return jnp.dot(a.astype(BF16), b.astype(BF16), preferred_element_type=F32)


def _bdot_nt(a, b):
    return lax.dot_general(a.astype(BF16), b.astype(BF16), (((1,), (1,)), ((), ())),
                           preferred_element_type=F32)


def _bdot_tn(a, b):
    return lax.dot_general(a.astype(BF16), b.astype(BF16), (((0,), (0,)), ((), ())),
                           preferred_element_type=F32)


def _split2(a):
    hi = a.astype(BF16)
    lo = (a - hi.astype(F32)).astype(BF16)
    return hi, lo


def _split3(a):
    a1 = a.astype(BF16)
    r1 = a - a1.astype(F32)
    a2 = r1.astype(BF16)
    a3 = (r1 - a2.astype(F32)).astype(BF16)
    return a1, a2, a3


def _dot3(a, b):
    ah, al = _split2(a)
    bh, bl = _split2(b)
    return _dot(ah, bh) + (_dot(ah, bl) + _dot(al, bh))


def _mask_dot_r(mask_bf, g):
    g1, g2, g3 = _split3(g)
    return _dot(mask_bf, g1) + (_dot(mask_bf, g2) + _dot(mask_bf, g3))


def _mask_dot_l(g, mask_bf):
    g1, g2, g3 = _split3(g)
    return _dot(g1, mask_bf) + (_dot(g2, mask_bf) + _dot(g3, mask_bf))


def _sigmoid(x):
    return 1.0 / (1.0 + jnp.exp(-x))


def _silu(x):
    return x * _sigmoid(x)


def _softplus(x):
    return jnp.maximum(x, 0.0) + jnp.log1p(jnp.exp(-jnp.abs(x)))


def _mod_group(row0):
    return jnp.where(row0 < RC, 0, 1 + (row0 - RC) // DEC_SEQ)


MODS_TN = 1536


def _mods_body(c_ref, w_ref, b_ref, o_ref):
    s = _silu(c_ref[...])
    o_ref[0] = _bdot(s, w_ref[0]) + b_ref[0]


def _mods(cond8, w_mod, b_mod):
    nj = 6 * D // MODS_TN
    return pl.pallas_call(
        _mods_body,
        grid=(DEPTH, nj),
        in_specs=[pl.BlockSpec((8, D), lambda l, j: (0, 0)),
                  pl.BlockSpec((1, D, MODS_TN), lambda l, j: (l, 0, j)),
                  pl.BlockSpec((1, 1, MODS_TN), lambda l, j: (l, 0, j))],
        out_specs=pl.BlockSpec((1, 8, MODS_TN), lambda l, j: (l, 0, j)),
        out_shape=jax.ShapeDtypeStruct((DEPTH, 8, 6 * D), F32),
        compiler_params=_cp(2),
        name="mods",
    )(cond8, w_mod, b_mod.reshape(DEPTH, 1, 6 * D))


IN_TM, IN_TN = 1024, 1280


def _modnorm(x, g, shift, scale):
    ms = jnp.mean(x * x, axis=-1, keepdims=True)
    y = x * lax.rsqrt(ms + EPS) * g
    return y * (1.0 + scale) + shift


def _inproj_body(x_ref, m_ref, g_ref, w_ref, o_ref, h_scr):
    @pl.when(pl.program_id(1) == 0)
    def _():
        h = _modnorm(x_ref[...], g_ref[...], m_ref[0, 0:1, :], m_ref[0, 1:2, :])
        h_scr[...] = h.astype(BF16)

    o_ref[...] = _dot(h_scr[...], w_ref[...])


def _inproj(x, mods_l, g, w_bf):
    return pl.pallas_call(
        _inproj_body,
        grid=(R // IN_TM, NP // IN_TN),
        in_specs=[pl.BlockSpec((IN_TM, D), lambda i, j: (i, 0)),
                  pl.BlockSpec((1, 6, D), lambda i, j: (_mod_group(i * IN_TM), 0, 0)),
                  pl.BlockSpec((1, D), lambda i, j: (0, 0)),
                  pl.BlockSpec((D, IN_TN), lambda i, j: (0, j))],
        out_specs=pl.BlockSpec((IN_TM, IN_TN), lambda i, j: (i, j)),
        out_shape=jax.ShapeDtypeStruct((R, NP), F32),
        scratch_shapes=[pltpu.VMEM((IN_TM, D), BF16)],
        compiler_params=_cp(2),
        name="inproj",
    )(x, mods_l, g, w_bf)


PREP_TM = 256
PREP_NC = PREP_TM // DN_C
PREP_LOCKSTEP = 4


def _prep_body(x_ref, xp_ref, xn_ref, sm_ref, smt_ref, cw_ref, al_ref, dt_ref, alt_ref, dtt_ref,
               u_o, w_o, qg_o, kd_o, ai_o, bg_o, qkv_o, gr_o):
    i = pl.program_id(0)
    n_ctx = RC // PREP_TM
    per_seq = DEC_SEQ // PREP_TM
    is_ctx = i < n_ctx
    j = (i - n_ctx) % per_seq
    first = jnp.logical_or(is_ctx, j == 0)
    last = jnp.logical_or(is_ctx, j == per_seq - 1)

    x = x_ref[...]
    prev_row = jnp.where(first, 0.0, xp_ref[7:8, :])
    next_row = jnp.where(last, 0.0, xn_ref[0:1, :])
    row = lax.broadcasted_iota(jnp.int32, (PREP_TM, 1), 0)
    xm = jnp.where(row == 0, prev_row, pltpu.roll(x, 1, axis=0))
    xq = jnp.where(row == PREP_TM - 1, next_row, pltpu.roll(x, PREP_TM - 1, axis=0))
    w = cw_ref[...]
    y = _silu(w[0:1] * xm + w[1:2] * x + w[2:3] * xq)

    for h in range(DN_H):
        qh = y[:, h * DN_DK:(h + 1) * DN_DK]
        qn = qh * lax.rsqrt(jnp.sum(qh * qh, axis=-1, keepdims=True) + EPS) * (DN_DK ** -0.5)
        qkv_o[:, h * DN_DK:(h + 1) * DN_DK] = qn
        kh = y[:, 512 + h * DN_DK:512 + (h + 1) * DN_DK]
        kn = kh * lax.rsqrt(jnp.sum(kh * kh, axis=-1, keepdims=True) + EPS)
        qkv_o[:, 512 + h * DN_DK:512 + (h + 1) * DN_DK] = kn
    qkv_o[:, 1024:1536] = y[:, 1024:1536]

    r = lax.broadcasted_iota(jnp.int32, (PREP_TM, PREP_TM), 0)
    c = lax.broadcasted_iota(jnp.int32, (PREP_TM, PREP_TM), 1)
    same = (r // DN_C) == (c // DN_C)
    low = jnp.where(jnp.logical_and(same, r >= c), 1.0, 0.0).astype(BF16)
    upp = jnp.where(jnp.logical_and(same, r <= c), 1.0, 0.0).astype(BF16)

    sm = sm_ref[...]
    beta = _sigmoid(sm)
    g = -jnp.exp(al_ref[...]) * _softplus(sm + dt_ref[...])
    gc_f = _mask_dot_r(low, g)
    gc_b = _mask_dot_r(upp, g)
    lane = lax.broadcasted_iota(jnp.int32, (1, 128), 1)
    bg_o[...] = jnp.where(lane < 8, beta, jnp.where(lane < 12, gc_f, jnp.where(lane < 16, gc_b, 0.0)))

    gt = -jnp.exp(alt_ref[...]) * _softplus(smt_ref[...] + dtt_ref[...])
    gct_f = _mask_dot_l(gt, upp)
    gct_b = _mask_dot_l(gt, low)
    sub = lax.broadcasted_iota(jnp.int32, (16, 1), 0)
    gct = jnp.where(sub < 12, gct_f, gct_b)
    for k in range(PREP_NC):
        gr_o[k] = gct[8:16, k * DN_C:(k + 1) * DN_C]

    rr = lax.broadcasted_iota(jnp.int32, (DN_C, DN_C), 0)
    cc = lax.broadcasted_iota(jnp.int32, (DN_C, DN_C), 1)

    def chunk_group(gi, carry):
        prob = [(cj, d, h) for cj in range(PREP_LOCKSTEP) for d in range(2) for h in range(DN_H)]
        ci = [gi * PREP_LOCKSTEP + cj for cj in range(PREP_LOCKSTEP)]
        rows_c = [pl.ds(pl.multiple_of(c_ * DN_C, DN_C), DN_C) for c_ in ci]
        bgc = [bg_o[r_, :] for r_ in rows_c]
        grow_all = [gr_o[c_] for c_ in ci]
        rows = [rows_c[cj] for cj, d, h in prob]
        q = [qkv_o[rows_c[cj], h * DN_DK:(h + 1) * DN_DK] for cj, d, h in prob]
        k = [qkv_o[rows_c[cj], 512 + h * DN_DK:512 + (h + 1) * DN_DK] for cj, d, h in prob]
        v = [qkv_o[rows_c[cj], 1024 + h * DN_DV:1024 + (h + 1) * DN_DV] for cj, d, h in prob]
        beta = [bgc[cj][:, d * DN_H + h:d * DN_H + h + 1] for cj, d, h in prob]
        gcol = [bgc[cj][:, 8 + d * DN_H + h:9 + d * DN_H + h] for cj, d, h in prob]
        grow = [grow_all[cj][d * DN_H + h:d * DN_H + h + 1, :] for cj, d, h in prob]
        incl = [(rr >= cc) if d == 0 else (rr <= cc) for cj, d, h in prob]
        strict = [(rr > cc) if d == 0 else (rr < cc) for cj, d, h in prob]
        n = len(prob)
        decay = [jnp.exp(jnp.where(incl[i], gcol[i] - grow[i], -1e30)) for i in range(n)]
        kb = [k[i] * beta[i] for i in range(n)]
        kk = [_bdot_nt(kb[i], k[i]) for i in range(n)]
        qk = [_bdot_nt(q[i], k[i]) for i in range(n)]
        a = [jnp.where(strict[i], kk[i] * decay[i], 0.0) for i in range(n)]
        blk = lambda b: (rr // b) == (cc // b)
        eye = jnp.where(rr == cc, 1.0, 0.0)
        p = [jnp.where(blk(8), a[i], 0.0) for i in range(n)]
        t = [eye - p[i] for i in range(n)]
        for _ in range(2):
            p = [_bdot(p[i], p[i]) for i in range(n)]
            t = [t[i] + _bdot(t[i], p[i]) for i in range(n)]
        for b in (16, 32, 64):
            m = jnp.logical_and(blk(b), jnp.logical_not(blk(b // 2)))
            tl = [_bdot(t[i], jnp.where(m, a[i], 0.0)) for i in range(n)]
            t = [t[i] - _bdot(tl[i], t[i]) for i in range(n)]
        egc = [jnp.exp(gcol[i]) for i in range(n)]
        glast = [gcol[i][DN_C - 1:DN_C, :] if prob[i][1] == 0 else gcol[i][0:1, :] for i in range(n)]
        uw = [_bdot(t[i], jnp.concatenate([v[i] * beta[i], kb[i] * egc[i]], axis=1)) for i in range(n)]
        for i, (cj, d, h) in enumerate(prob):
            u_o[d, rows[i], h * DN_DV:(h + 1) * DN_DV] = uw[i][:, 0:DN_DV]
            w_o[d, rows[i], h * DN_DK:(h + 1) * DN_DK] = uw[i][:, DN_DV:DN_DV + DN_DK].astype(BF16)
            qg_o[d, rows[i], h * DN_DK:(h + 1) * DN_DK] = (q[i] * egc[i]).astype(BF16)
            kd_o[d, rows[i], h * DN_DK:(h + 1) * DN_DK] = (k[i] * jnp.exp(glast[i] - gcol[i])).astype(BF16)
            ai_o[d, rows[i], h * DN_C:(h + 1) * DN_C] = (qk[i] * decay[i]).astype(BF16)
        return carry

    lax.fori_loop(0, PREP_NC // PREP_LOCKSTEP, chunk_group, 0)


def _dn_prep(proj, sm_t, conv_w, al128, dt128, al_t, dt_t):
    nb8 = R // 8
    qb = C_QKV // 1536
    wide = lambda dt: jax.ShapeDtypeStruct((2, R, DN_H * DN_DK), dt)
    wide_spec = pl.BlockSpec((2, PREP_TM, DN_H * DN_DK), lambda i: (0, i, 0))
    return pl.pallas_call(
        _prep_body,
        grid=(R // PREP_TM,),
        in_specs=[pl.BlockSpec((PREP_TM, 1536), lambda i: (i, qb)),
                  pl.BlockSpec((8, 1536), lambda i: (jnp.maximum(i * (PREP_TM // 8) - 1, 0), qb)),
                  pl.BlockSpec((8, 1536), lambda i: (jnp.minimum((i + 1) * (PREP_TM // 8), nb8 - 1), qb)),
                  pl.BlockSpec((PREP_TM, 128), lambda i: (i, C_SM // 128)),
                  pl.BlockSpec((16, PREP_TM), lambda i: (0, i)),
                  pl.BlockSpec((3, 1536), lambda i: (0, 0)),
                  pl.BlockSpec((1, 128), lambda i: (0, 0)),
                  pl.BlockSpec((1, 128), lambda i: (0, 0)),
                  pl.BlockSpec((16, 1), lambda i: (0, 0)),
                  pl.BlockSpec((16, 1), lambda i: (0, 0))],
        out_specs=[wide_spec, wide_spec, wide_spec, wide_spec,
                   pl.BlockSpec((2, PREP_TM, DN_H * DN_C), lambda i: (0, i, 0)),
                   pl.BlockSpec((PREP_TM, 128), lambda i: (i, 0))],
        out_shape=[wide(F32), wide(BF16), wide(BF16), wide(BF16),
                   jax.ShapeDtypeStruct((2, R, DN_H * DN_C), BF16),
                   jax.ShapeDtypeStruct((R, 128), F32)],
        scratch_shapes=[pltpu.VMEM((PREP_TM, 1536), F32), pltpu.VMEM((PREP_NC, 8, DN_C), F32)],
        compiler_params=_cp(1),
        name="dn_prep",
    )(proj, proj, proj, proj, sm_t, conv_w, al128, dt128, al_t, dt_t)


SCAN_CTX = RC // PREP_TM
SCAN_PER = DEC_SEQ // PREP_TM
SCAN_NSEQ = BATCH + DEC_BATCH


def _scan_bwd_block(i):
    j = i - SCAN_CTX
    return jnp.where(i < SCAN_CTX, i, SCAN_CTX + (j // SCAN_PER) * SCAN_PER + (SCAN_PER - 1 - j % SCAN_PER))


def _scan_body(uf, wf, qf, kf, af, bf, ub, wb, qb, kb_, ab, bb, s0_ref, of_ref, ob_ref, so_ref, s_scr):
    i = pl.program_id(0)
    is_ctx = i < SCAN_CTX
    j = (i - SCAN_CTX) % SCAN_PER
    first = jnp.logical_or(is_ctx, j == 0)
    last = jnp.logical_or(is_ctx, j == SCAN_PER - 1)

    @pl.when(first)
    def _():
        for d in range(2):
            for h in range(DN_H):
                s_scr[d, h] = jnp.where(is_ctx, 0.0, s0_ref[0, d, h])

    dirs = ((uf, wf, qf, kf, af, bf, of_ref), (ub, wb, qb, kb_, ab, bb, ob_ref))

    prob = [(d, h) for d in range(2) for h in range(DN_H)]
    tn = (((0,), (0,)), ((), ()))

    def step(n, carry):
        rows = [pl.ds(pl.multiple_of((n if d == 0 else PREP_NC - 1 - n) * DN_C, DN_C), DN_C) for d in range(2)]
        bgc = [dirs[d][5][rows[d], :] for d in range(2)]
        cols = [slice(h * DN_DK, (h + 1) * DN_DK) for d, h in prob]
        st = [s_scr[d, h] for d, h in prob]
        stb = [x.astype(BF16) for x in st]
        ws = [_dot(dirs[d][1][0, rows[d], cols[i]], stb[i]) for i, (d, h) in enumerate(prob)]
        qs = [_dot(dirs[d][2][0, rows[d], cols[i]], stb[i]) for i, (d, h) in enumerate(prob)]
        vb = [(dirs[d][0][0, rows[d], cols[i]] - ws[i]).astype(BF16) for i, (d, h) in enumerate(prob)]
        av = [_dot(dirs[d][4][0, rows[d], h * DN_C:(h + 1) * DN_C], vb[i]) for i, (d, h) in enumerate(prob)]
        kv = [lax.dot_general(dirs[d][3][0, rows[d], cols[i]], vb[i], tn, preferred_element_type=F32)
              for i, (d, h) in enumerate(prob)]
        for i, (d, h) in enumerate(prob):
            gcol = bgc[d][:, 8 + d * DN_H + h:9 + d * DN_H + h]
            glast = gcol[DN_C - 1:DN_C, :] if d == 0 else gcol[0:1, :]
            s_scr[d, h] = st[i] * jnp.exp(glast) + kv[i]
            dirs[d][6][rows[d], cols[i]] = qs[i] + av[i]
        return carry

    lax.fori_loop(0, PREP_NC, step, 0)

    @pl.when(last)
    def _():
        for d in range(2):
            for h in range(DN_H):
                so_ref[0, d, h] = s_scr[d, h]


def _dn_scan(u, w, qg, kd, ai, bg, s0):
    fwd3 = lambda i: (0, i, 0)
    bwd3 = lambda i: (1, _scan_bwd_block(i), 0)
    wide = lambda m: pl.BlockSpec((1, PREP_TM, DN_H * DN_DK), m)
    narrow = lambda m: pl.BlockSpec((1, PREP_TM, DN_H * DN_C), m)
    state_spec = lambda m: pl.BlockSpec((1, 2, DN_H, DN_DK, DN_DV), m)
    seq_of = lambda i: jnp.where(i < SCAN_CTX, i, SCAN_CTX + (i - SCAN_CTX) // SCAN_PER)
    return pl.pallas_call(
        _scan_body,
        grid=(R // PREP_TM,),
        in_specs=[wide(fwd3), wide(fwd3), wide(fwd3), wide(fwd3), narrow(fwd3),
                  pl.BlockSpec((PREP_TM, 128), lambda i: (i, 0)),
                  wide(bwd3), wide(bwd3), wide(bwd3), wide(bwd3), narrow(bwd3),
                  pl.BlockSpec((PREP_TM, 128), lambda i: (_scan_bwd_block(i), 0)),
                  state_spec(lambda i: (jnp.maximum(i - SCAN_CTX, 0) // SCAN_PER, 0, 0, 0, 0))],
        out_specs=[pl.BlockSpec((PREP_TM, DN_H * DN_DV), lambda i: (i, 0)),
                   pl.BlockSpec((PREP_TM, DN_H * DN_DV), lambda i: (_scan_bwd_block(i), 0)),
                   state_spec(lambda i: (seq_of(i), 0, 0, 0, 0))],
        out_shape=[jax.ShapeDtypeStruct((R, DN_H * DN_DV), F32),
                   jax.ShapeDtypeStruct((R, DN_H * DN_DV), F32),
                   jax.ShapeDtypeStruct((SCAN_NSEQ, 2, DN_H, DN_DK, DN_DV), F32)],
        scratch_shapes=[pltpu.VMEM((2, DN_H, DN_DK, DN_DV), F32)],
        compiler_params=_cp(1),
        name="dn_scan",
    )(u, w, qg, kd, ai, bg, u, w, qg, kd, ai, bg, s0)


def _group_mean_matrix(width, group):
    r = lax.broadcasted_iota(jnp.int32, (width, width), 0)
    c = lax.broadcasted_iota(jnp.int32, (width, width), 1)
    return jnp.where((r // group) == (c // group), 1.0 / group, 0.0).astype(BF16)


def _group_rmsnorm(x, w, group):
    m = _group_mean_matrix(x.shape[-1], group)
    hi, lo = _split2(x * x)
    ms = _dot(hi, m) + _dot(lo, m)
    return x * lax.rsqrt(ms + EPS) * w


def _rope(x, cos, sin_signed, group):
    width = x.shape[-1]
    half = group // 2
    lane = lax.broadcasted_iota(jnp.int32, (1, width), 1)
    swapped = jnp.where((lane % group) < half,
                        pltpu.roll(x, width - half, axis=1), pltpu.roll(x, half, axis=1))
    return x * cos + swapped * sin_signed


def _softmax_pv(s, v_bf):
    m = jnp.max(s, axis=-1, keepdims=True)
    p = jnp.exp(s - m)
    l = jnp.sum(p, axis=-1, keepdims=True)
    return _dot(p.astype(BF16), v_bf) / l


GQA_G = GQA_H // GQA_KV


def _gqa_body(T, TQ, latent, *refs):
    it = iter(refs)
    q_ref, kv_ref, qw_ref, kw_ref = next(it), next(it), next(it), next(it)
    if latent:
        cq_ref, sq_ref, ck_ref, sk_ref, kc_ref, vc_ref = (next(it) for _ in range(6))
    o_ref = next(it)
    kn_ref = None if latent else next(it)
    k_scr, v_scr = next(it), next(it)

    @pl.when(pl.program_id(1) == 0)
    def _():
        kv = kv_ref[...]
        k = _group_rmsnorm(kv[:, 0:128], kw_ref[...], GQA_HD)
        v = kv[:, 128:256]
        if latent:
            k = _rope(k, ck_ref[...], sk_ref[...], GQA_HD)
        else:
            kn_ref[...] = k
        for g in range(GQA_KV):
            k_scr[g, 0:T, :] = k[:, g * GQA_HD:(g + 1) * GQA_HD].astype(BF16)
            v_scr[g, 0:T, :] = v[:, g * GQA_HD:(g + 1) * GQA_HD].astype(BF16)
            if latent:
                k_scr[g, T:T + PAST, :] = kc_ref[0, :, g * GQA_HD:(g + 1) * GQA_HD].astype(BF16)
                v_scr[g, T:T + PAST, :] = vc_ref[0, :, g * GQA_HD:(g + 1) * GQA_HD].astype(BF16)

    q = _group_rmsnorm(q_ref[...], qw_ref[...], GQA_HD)
    if latent:
        q = _rope(q, cq_ref[...], sq_ref[...], GQA_HD)
    q = q * (GQA_HD ** -0.5)
    for g in range(GQA_KV):
        qg = jnp.concatenate(
            [q[:, (g * GQA_G + j) * GQA_HD:(g * GQA_G + j + 1) * GQA_HD] for j in range(GQA_G)], axis=0)
        s = _bdot_nt(qg, k_scr[g])
        o = _softmax_pv(s, v_scr[g])
        for j in range(GQA_G):
            hh = g * GQA_G + j
            o_ref[:, hh * GQA_HD:(hh + 1) * GQA_HD] = o[j * TQ:(j + 1) * TQ, :]


def _gqa(proj, qw, kw, T, TQ, n_seq, row_off, rope=None, cache=None):
    latent = rope is not None
    tk = T + (PAST if latent else 0)
    nq = T // TQ
    qo = row_off // TQ
    so = row_off // T
    in_specs = [pl.BlockSpec((TQ, 512), lambda i, j: (qo + i * nq + j, C_GQ // 512)),
                pl.BlockSpec((T, 256), lambda i, j: (so + i, C_GKV // 256)),
                pl.BlockSpec((1, 512), lambda i, j: (0, 0)),
                pl.BlockSpec((1, 128), lambda i, j: (0, 0))]
    args = [proj, proj, qw, kw]
    if latent:
        cq, sq, ck, sk = rope
        kc, vc = cache
        in_specs += [pl.BlockSpec((TQ, 512), lambda i, j: (j, 0)),
                     pl.BlockSpec((TQ, 512), lambda i, j: (j, 0)),
                     pl.BlockSpec((T, 128), lambda i, j: (0, 0)),
                     pl.BlockSpec((T, 128), lambda i, j: (0, 0)),
                     pl.BlockSpec((1, PAST, 128), lambda i, j: (i, 0, 0)),
                     pl.BlockSpec((1, PAST, 128), lambda i, j: (i, 0, 0))]
        args += [cq, sq, ck, sk, kc, vc]
    out_specs = [pl.BlockSpec((TQ, 512), lambda i, j: (i * nq + j, 0))]
    out_shape = [jax.ShapeDtypeStruct((n_seq * T, 512), F32)]
    if not latent:
        out_specs.append(pl.BlockSpec((T, 128), lambda i, j: (i, 0)))
        out_shape.append(jax.ShapeDtypeStruct((n_seq * T, 128), F32))
    return pl.pallas_call(
        functools.partial(_gqa_body, T, TQ, latent),
        grid=(n_seq, nq),
        in_specs=in_specs,
        out_specs=out_specs,
        out_shape=out_shape,
        scratch_shapes=[pltpu.VMEM((GQA_KV, tk, GQA_HD), BF16), pltpu.VMEM((GQA_KV, tk, GQA_HD), BF16)],
        compiler_params=_cp(2),
        name="gqa_%d" % T,
    )(*args)


MLA_DQ = MLA_NOPE + MLA_ROPE
MLA_KV_ROWS = 512


def _mla_body(T, TQ, latent, *refs):
    it = iter(refs)
    cq_ref, ckv_ref, sm_ref, wq_ref, wkv_ref, qw_ref, kvw_ref = (next(it) for _ in range(7))
    if latent:
        cosq_ref, sinq_ref, cosk_ref, sink_ref, cc_ref, kc_ref = (next(it) for _ in range(6))
    o_ref = next(it)
    cn_ref = None if latent else next(it)
    k_scr, v_scr = next(it), next(it)
    tk = T + (PAST if latent else 0)

    @pl.when(pl.program_id(1) == 0)
    def _():
        x = ckv_ref[...]
        ms = jnp.mean(x * x, axis=-1, keepdims=True)
        ckv = x * lax.rsqrt(ms + EPS) * kvw_ref[...]
        sm = sm_ref[...]
        if latent:
            sm = _rope_small(sm, cosk_ref[...], sink_ref[...])
        else:
            cn_ref[...] = ckv
        kr = sm[:, 16:16 + MLA_ROPE]

        def put(rows0, ckv_rows, kr_rows):
            n = ckv_rows.shape[0]
            kv = _bdot(ckv_rows, wkv_ref[...])
            krb = kr_rows.astype(BF16)
            for h in range(MLA_H):
                k_scr[h, rows0:rows0 + n, 0:MLA_NOPE] = kv[:, h * MLA_NOPE:(h + 1) * MLA_NOPE].astype(BF16)
                k_scr[h, rows0:rows0 + n, MLA_NOPE:MLA_DQ] = krb
                v_scr[h, rows0:rows0 + n, :] = kv[:, 512 + h * MLA_V:512 + (h + 1) * MLA_V].astype(BF16)

        for r0 in range(0, T, MLA_KV_ROWS):
            r1 = min(r0 + MLA_KV_ROWS, T)
            put(r0, ckv[r0:r1], kr[r0:r1])
        if latent:
            put(T, cc_ref[0], kc_ref[0])

    x = cq_ref[...]
    ms = jnp.mean(x * x, axis=-1, keepdims=True)
    cq = x * lax.rsqrt(ms + EPS) * qw_ref[...]
    qf = _bdot(cq, wq_ref[...])
    qn = qf[:, 0:512]
    qr = qf[:, 512:768]
    if latent:
        qr = _rope(qr, cosq_ref[...], sinq_ref[...], MLA_ROPE)
    scale = MLA_DQ ** -0.5
    for h in range(MLA_H):
        qh = jnp.concatenate([qn[:, h * MLA_NOPE:(h + 1) * MLA_NOPE],
                              qr[:, h * MLA_ROPE:(h + 1) * MLA_ROPE]], axis=1) * scale
        s = _bdot_nt(qh, k_scr[h])
        o_ref[:, h * MLA_V:(h + 1) * MLA_V] = _softmax_pv(s, v_scr[h])


def _rope_small(sm, cos, sin_signed):
    lane = lax.broadcasted_iota(jnp.int32, (1, 128), 1)
    half = MLA_ROPE // 2
    swapped = jnp.where(lane < 16 + half, pltpu.roll(sm, 128 - half, axis=1), pltpu.roll(sm, half, axis=1))
    return sm * cos + swapped * sin_signed


def _mla(proj, wq, wkv, qw, kvw, T, TQ, n_seq, row_off, rope=None, cache=None):
    latent = rope is not None
    tk = T + (PAST if latent else 0)
    nq = T // TQ
    qo = row_off // TQ
    so = row_off // T
    in_specs = [pl.BlockSpec((TQ, 256), lambda i, j: (qo + i * nq + j, C_MCQ // 256)),
                pl.BlockSpec((T, 128), lambda i, j: (so + i, C_MCKV // 128)),
                pl.BlockSpec((T, 128), lambda i, j: (so + i, C_SM // 128)),
                pl.BlockSpec((MLA_QL, 768), lambda i, j: (0, 0)),
                pl.BlockSpec((MLA_KVL, 1024), lambda i, j: (0, 0)),
                pl.BlockSpec((1, 256), lambda i, j: (0, 0)),
                pl.BlockSpec((1, 128), lambda i, j: (0, 0))]
    args = [proj, proj, proj, wq, wkv, qw, kvw]
    if latent:
        cosq, sinq, cosk, sink = rope
        cc, kc = cache
        in_specs += [pl.BlockSpec((TQ, 256), lambda i, j: (j, 0)),
                     pl.BlockSpec((TQ, 256), lambda i, j: (j, 0)),
                     pl.BlockSpec((T, 128), lambda i, j: (0, 0)),
                     pl.BlockSpec((T, 128), lambda i, j: (0, 0)),
                     pl.BlockSpec((1, PAST, 128), lambda i, j: (i, 0, 0)),
                     pl.BlockSpec((1, PAST, MLA_ROPE), lambda i, j: (i, 0, 0))]
        args += [cosq, sinq, cosk, sink, cc, kc]
    out_specs = [pl.BlockSpec((TQ, 512), lambda i, j: (i * nq + j, 0))]
    out_shape = [jax.ShapeDtypeStruct((n_seq * T, 512), F32)]
    if not latent:
        out_specs.append(pl.BlockSpec((T, 128), lambda i, j: (i, 0)))
        out_shape.append(jax.ShapeDtypeStruct((n_seq * T, 128), F32))
    return pl.pallas_call(
        functools.partial(_mla_body, T, TQ, latent),
        grid=(n_seq, nq),
        in_specs=in_specs,
        out_specs=out_specs,
        out_shape=out_shape,
        scratch_shapes=[pltpu.VMEM((MLA_H, tk, MLA_DQ), BF16), pltpu.VMEM((MLA_H, tk, MLA_V), BF16)],
        compiler_params=_cp(2),
        name="mla_%d" % T,
    )(*args)


MG_TM = 512


def _merge_body(x_ref, m_ref, of_ref, ob_ref, dg_ref, og_ref, om_ref, bg_ref, ng_ref, wb_ref, wo_ref, o_ref):
    odn = of_ref[...] + ob_ref[...]
    dg = dg_ref[...]
    ng = ng_ref[...]
    parts = []
    for h in range(DN_H):
        oh = odn[:, h * DN_DV:(h + 1) * DN_DV]
        ms = jnp.mean(oh * oh, axis=-1, keepdims=True)
        parts.append(oh * lax.rsqrt(ms + EPS) * ng * _silu(dg[:, h * DN_DV:(h + 1) * DN_DV]))
    br0 = jnp.concatenate(parts, axis=1)
    merged = _sigmoid(bg_ref[:, 0:D]) * _bdot(br0, wb_ref[0])
    merged = merged + _sigmoid(bg_ref[:, D:2 * D]) * _bdot(og_ref[...], wb_ref[1])
    merged = merged + _sigmoid(bg_ref[:, 2 * D:3 * D]) * _bdot(om_ref[...], wb_ref[2])
    out = _bdot(merged, wo_ref[...])
    o_ref[...] = x_ref[...] + m_ref[0, 2:3, :] * out


def _merge(x, mods_l, o_f, o_b, proj, ogqa, omla, ng, wb, wo):
    row = lambda i: (i, 0)
    return pl.pallas_call(
        _merge_body,
        grid=(R // MG_TM,),
        in_specs=[pl.BlockSpec((MG_TM, D), row),
                  pl.BlockSpec((1, 6, D), lambda i: (_mod_group(i * MG_TM), 0, 0)),
                  pl.BlockSpec((MG_TM, 512), row),
                  pl.BlockSpec((MG_TM, 512), row),
                  pl.BlockSpec((MG_TM, 512), lambda i: (i, C_DG // 512)),
                  pl.BlockSpec((MG_TM, 512), row),
                  pl.BlockSpec((MG_TM, 512), row),
                  pl.BlockSpec((MG_TM, 3 * D), lambda i: (i, C_BG // (3 * D))),
                  pl.BlockSpec((1, DN_DV), lambda i: (0, 0)),
                  pl.BlockSpec((3, 512, D), lambda i: (0, 0, 0)),
                  pl.BlockSpec((D, D), lambda i: (0, 0))],
        out_specs=pl.BlockSpec((MG_TM, D), row),
        out_shape=jax.ShapeDtypeStruct((R, D), F32),
        compiler_params=_cp(1),
        name="merge",
    )(x, mods_l, o_f, o_b, proj, ogqa, omla, proj, ng, wb, wo)


FF_TM, FF_TF = 1024, 1408


def _ffn_body(x_ref, m_ref, g_ref, w1_ref, w3_ref, w2_ref, o_ref, h_scr, acc_scr):
    f = pl.program_id(1)

    @pl.when(f == 0)
    def _():
        h = _modnorm(x_ref[...], g_ref[...], m_ref[0, 3:4, :], m_ref[0, 4:5, :])
        h_scr[...] = h.astype(BF16)
        acc_scr[...] = jnp.zeros_like(acc_scr)

    h = h_scr[...]
    a = _silu(_dot(h, w1_ref[...])) * _dot(h, w3_ref[...])
    acc_scr[...] += _dot(a.astype(BF16), w2_ref[...])

    @pl.when(f == pl.num_programs(1) - 1)
    def _():
        o_ref[...] = x_ref[...] + m_ref[0, 5:6, :] * acc_scr[...]


def _ffn(x, mods_l, g, w1, w3, w2):
    return pl.pallas_call(
        _ffn_body,
        grid=(R // FF_TM, FF_DENSE // FF_TF),
        in_specs=[pl.BlockSpec((FF_TM, D), lambda i, f: (i, 0)),
                  pl.BlockSpec((1, 6, D), lambda i, f: (_mod_group(i * FF_TM), 0, 0)),
                  pl.BlockSpec((1, D), lambda i, f: (0, 0)),
                  pl.BlockSpec((D, FF_TF), lambda i, f: (0, f)),
                  pl.BlockSpec((D, FF_TF), lambda i, f: (0, f)),
                  pl.BlockSpec((FF_TF, D), lambda i, f: (f, 0))],
        out_specs=pl.BlockSpec((FF_TM, D), lambda i, f: (i, 0)),
        out_shape=jax.ShapeDtypeStruct((R, D), F32),
        scratch_shapes=[pltpu.VMEM((FF_TM, D), BF16), pltpu.VMEM((FF_TM, D), F32)],
        compiler_params=_cp(2),
        name="ffn",
    )(x, mods_l, g, w1, w3, w2)


MOE_TM, MOE_TF = 512, 1792


def _top2_combine(logits):
    lane = lax.broadcasted_iota(jnp.int32, logits.shape, 1)
    m1 = jnp.max(logits, axis=-1, keepdims=True)
    i1 = jnp.min(jnp.where(logits == m1, lane, 128), axis=-1, keepdims=True)
    sel1 = lane == i1
    rest = jnp.where(sel1, -jnp.inf, logits)
    m2 = jnp.max(rest, axis=-1, keepdims=True)
    i2 = jnp.min(jnp.where(rest == m2, lane, 128), axis=-1, keepdims=True)
    sel2 = lane == i2
    e2 = jnp.exp(m2 - m1)
    p1 = 1.0 / (1.0 + e2)
    p2 = e2 / (1.0 + e2)
    return jnp.where(sel1, p1, 0.0) + jnp.where(sel2, p2, 0.0)


def _moe_body(x_ref, m_ref, g_ref, rw_ref, rb_ref, w1_ref, w3_ref, w2_ref, o_ref, h_scr, comb_scr, acc_scr):
    e = pl.program_id(1)
    f = pl.program_id(2)

    @pl.when(jnp.logical_and(e == 0, f == 0))
    def _():
        h = _modnorm(x_ref[...], g_ref[...], m_ref[0, 3:4, :], m_ref[0, 4:5, :])
        h_scr[...] = h.astype(BF16)
        logits = _dot3(h, rw_ref[...]) + rb_ref[...]
        comb_scr[...] = _top2_combine(logits)
        acc_scr[...] = jnp.zeros_like(acc_scr)

    h = h_scr[...]
    a = _silu(_dot(h, w1_ref[0])) * _dot(h, w3_ref[0])
    lane = lax.broadcasted_iota(jnp.int32, (1, 128), 1)
    ce = jnp.sum(jnp.where(lane == e, comb_scr[...], 0.0), axis=-1, keepdims=True)
    acc_scr[...] += ce * _dot(a.astype(BF16), w2_ref[0])

    @pl.when(jnp.logical_and(e == pl.num_programs(1) - 1, f == pl.num_programs(2) - 1))
    def _():
        o_ref[...] = x_ref[...] + m_ref[0, 5:6, :] * acc_scr[...]


def _moe(x, mods_l, g, rw128, rb128, w1, w3, w2):
    return pl.pallas_call(
        _moe_body,
        grid=(R // MOE_TM, N_EXP, FF_EXP // MOE_TF),
        in_specs=[pl.BlockSpec((MOE_TM, D), lambda i, e, f: (i, 0)),
                  pl.BlockSpec((1, 6, D), lambda i, e, f: (_mod_group(i * MOE_TM), 0, 0)),
                  pl.BlockSpec((1, D), lambda i, e, f: (0, 0)),
                  pl.BlockSpec((D, 128), lambda i, e, f: (0, 0)),
                  pl.BlockSpec((1, 128), lambda i, e, f: (0, 0)),
                  pl.BlockSpec((1, D, MOE_TF), lambda i, e, f: (e, 0, f)),
                  pl.BlockSpec((1, D, MOE_TF), lambda i, e, f: (e, 0, f)),
                  pl.BlockSpec((1, MOE_TF, D), lambda i, e, f: (e, f, 0))],
        out_specs=pl.BlockSpec((MOE_TM, D), lambda i, e, f: (i, 0)),
        out_shape=jax.ShapeDtypeStruct((R, D), F32),
        scratch_shapes=[pltpu.VMEM((MOE_TM, D), BF16), pltpu.VMEM((MOE_TM, 128), F32),
                        pltpu.VMEM((MOE_TM, D), F32)],
        compiler_params=_cp(3),
        name="moe",
    )(x, mods_l, g, rw128, rb128, w1, w3, w2)


FN_TM = 1024


def _final_body(x_ref, g_ref, o_ref):
    x = x_ref[...]
    ms = jnp.mean(x * x, axis=-1, keepdims=True)
    o_ref[...] = x * lax.rsqrt(ms + EPS) * g_ref[...]


def _final_norm(x, g):
    return pl.pallas_call(
        _final_body,
        grid=(R // FN_TM,),
        in_specs=[pl.BlockSpec((FN_TM, D), lambda i: (i, 0)), pl.BlockSpec((1, D), lambda i: (0, 0))],
        out_specs=pl.BlockSpec((FN_TM, D), lambda i: (i, 0)),
        out_shape=jax.ShapeDtypeStruct((R, D), F32),
        compiler_params=_cp(1),
        name="final_norm",
    )(x, g)


def _rope_tables(n_tokens, rot_dim):
    t = np.arange(n_tokens)
    row = (t // GRID_W).astype(np.float32)
    col = (t % GRID_W).astype(np.float32)
    n_freq = rot_dim // 4
    inv = (ROPE_THETA ** (-jnp.arange(n_freq, dtype=F32) / n_freq))
    ang = jnp.concatenate([jnp.asarray(row)[:, None] * inv, jnp.asarray(col)[:, None] * inv], axis=-1)
    cos, sin = jnp.cos(ang), jnp.sin(ang)
    return jnp.concatenate([cos, cos], axis=-1), jnp.concatenate([-sin, sin], axis=-1)


def _permute_w_in(w):
    return jnp.concatenate(
        [w[:, 3248:6320], w[:, 0:1536], w[:, 1536:2048], w[:, 2064:2576], w[:, 2576:2832],
         w[:, 2832:3088], w[:, 3088:3216], w[:, 2048:2064], w[:, 3216:3248],
         jnp.zeros((D, NP - 6320), w.dtype)], axis=1).astype(BF16)


def kernel(x_prompt, x_sample, c, cache_gqa_k, cache_gqa_v, cache_mla_ckv, cache_mla_krope, state_delta, c_ctx, w_mod, b_mod, norm1_g, norm2_g, w_in, dn_conv_w, dn_a_log, dn_dt_bias, dn_norm_g, gqa_q_norm, gqa_k_norm, mla_q_norm, mla_kv_norm, mla_w_uq, mla_w_ukv, w_branch, w_out, ffd_w1, ffd_w3, ffd_w2, router_w, router_b, moe_w1, moe_w3, moe_w2, final_g):
    x = jnp.concatenate([x_prompt.reshape(RC, D), x_sample.reshape(RL, D)], axis=0)
    cond8 = jnp.concatenate([c_ctx[None, :], c, jnp.zeros((3, D), F32)], axis=0)
    mods = _mods(cond8, w_mod, b_mod).reshape(DEPTH, 8, 6, D)

    cg, sg = _rope_tables(DEC_SEQ, GQA_HD)
    gqa_rope = (jnp.tile(cg, (1, GQA_H)), jnp.tile(sg, (1, GQA_H)),
                jnp.tile(cg, (1, GQA_KV)), jnp.tile(sg, (1, GQA_KV)))
    cm, sm_ = _rope_tables(DEC_SEQ, MLA_ROPE)
    padk = lambda t: jnp.pad(t, ((0, 0), (16, 128 - 16 - MLA_ROPE)))
    mla_rope = (jnp.tile(cm, (1, MLA_H)), jnp.tile(sm_, (1, MLA_H)),
                jnp.pad(cm, ((0, 0), (16, 128 - 16 - MLA_ROPE)), constant_values=1.0), padk(sm_))

    new_k, new_v, new_ckv, new_kr, new_s = [], [], [], [], []
    for l in range(DEPTH):
        proj = _inproj(x, mods[l], norm1_g[l][None, :], _permute_w_in(w_in[l]))

        pad128 = lambda v: jnp.pad(v.reshape(1, 8), ((0, 0), (8, 112)))
        padt = lambda v: jnp.pad(v.reshape(8, 1), ((8, 0), (0, 0)))
        sm_t = proj[:, C_SM:C_SM + 16].T
        dn_u, dn_w, dn_qg, dn_kd, dn_ai, dn_bg = _dn_prep(
            proj, sm_t, dn_conv_w[l], pad128(dn_a_log[l]), pad128(dn_dt_bias[l]),
            padt(dn_a_log[l]), padt(dn_dt_bias[l]))
        o_dn_f, o_dn_b, s_all = _dn_scan(dn_u, dn_w, dn_qg, dn_kd, dn_ai, dn_bg, state_delta[:, l])
        s_c = s_all[:BATCH]

        qw = jnp.tile(gqa_q_norm[l][None, :], (1, GQA_H))
        kw = jnp.tile(gqa_k_norm[l][None, :], (1, GQA_KV))
        o_g_c, kn_c = _gqa(proj, qw, kw, SEQ, SEQ, BATCH, 0)
        (o_g_l,) = _gqa(proj, qw, kw, DEC_SEQ, 128, DEC_BATCH, RC, rope=gqa_rope,
                        cache=(cache_gqa_k[:, l].reshape(DEC_BATCH, PAST, 128),
                               cache_gqa_v[:, l].reshape(DEC_BATCH, PAST, 128)))
        o_gqa = jnp.concatenate([o_g_c, o_g_l], axis=0)

        wq = mla_w_uq[l].reshape(MLA_QL, MLA_H, MLA_DQ)
        wq = jnp.concatenate([wq[:, :, :MLA_NOPE].reshape(MLA_QL, -1), wq[:, :, MLA_NOPE:].reshape(MLA_QL, -1)],
                             axis=1).astype(BF16)
        wkv = mla_w_ukv[l].reshape(MLA_KVL, MLA_H, MLA_NOPE + MLA_V)
        wkv = jnp.concatenate([wkv[:, :, :MLA_NOPE].reshape(MLA_KVL, -1), wkv[:, :, MLA_NOPE:].reshape(MLA_KVL, -1)],
                              axis=1).astype(BF16)
        mqw, mkvw = mla_q_norm[l][None, :], mla_kv_norm[l][None, :]
        o_m_c, ckv_c = _mla(proj, wq, wkv, mqw, mkvw, SEQ, SEQ, BATCH, 0)
        (o_m_l,) = _mla(proj, wq, wkv, mqw, mkvw, DEC_SEQ, 256, DEC_BATCH, RC, rope=mla_rope,
                        cache=(cache_mla_ckv[:, l], cache_mla_krope[:, l]))
        o_mla = jnp.concatenate([o_m_c, o_m_l], axis=0)

        x = _merge(x, mods[l], o_dn_f, o_dn_b, proj, o_gqa, o_mla, dn_norm_g[l][None, :],
                   w_branch[l].astype(BF16), w_out[l].astype(BF16))

        j = l // 2
        if l % 2 == 0:
            x = _ffn(x, mods[l], norm2_g[l][None, :], ffd_w1[j].astype(BF16), ffd_w3[j].astype(BF16),
                     ffd_w2[j].astype(BF16))
        else:
            rw128 = jnp.pad(router_w[j], ((0, 0), (0, 128 - N_EXP)))
            rb128 = jnp.pad(router_b[j][None, :], ((0, 0), (0, 128 - N_EXP)), constant_values=-jnp.inf)
            x = _moe(x, mods[l], norm2_g[l][None, :], rw128, rb128, moe_w1[j].astype(BF16),
                     moe_w3[j].astype(BF16), moe_w2[j].astype(BF16))

        new_k.append(kn_c.reshape(BATCH, SEQ, GQA_KV, GQA_HD))
        new_v.append(proj[:RC, C_GKV + 128:C_GKV + 256].reshape(BATCH, SEQ, GQA_KV, GQA_HD))
        new_ckv.append(ckv_c.reshape(BATCH, SEQ, MLA_KVL))
        new_kr.append(proj[:RC, C_SM + 16:C_SM + 16 + MLA_ROPE].reshape(BATCH, SEQ, MLA_ROPE))
        new_s.append(s_c)

    y = _final_norm(x, final_g[None, :])
    return (y[:RC].reshape(BATCH, SEQ, D), y[RC:].reshape(DEC_BATCH, DEC_SEQ, D),
            jnp.stack(new_k, axis=1), jnp.stack(new_v, axis=1), jnp.stack(new_ckv, axis=1),
            jnp.stack(new_kr, axis=1), jnp.stack(new_s, axis=1))
```

```python
import functools
import math

import jax
import jax.numpy as jnp
import numpy as np
from jax import lax
from jax.experimental import pallas as pl
from jax.experimental.pallas import tpu as pltpu

F32 = jnp.float32
BF16 = jnp.bfloat16

D = 1024
BATCH, SEQ = 32, 256
DEC_BATCH, DEC_SEQ = 4, 2048
DEPTH = 4
PAST = 512
GRID_W = 64
ROPE_THETA = 10000.0
EPS = 1e-6
DN_H, DN_DK, DN_DV, DN_C = 4, 128, 128, 64
GQA_H, GQA_KV, GQA_HD = 8, 2, 64
MLA_H, MLA_QL, MLA_KVL, MLA_NOPE, MLA_ROPE, MLA_V = 8, 256, 128, 64, 32, 64
FF_DENSE, N_EXP, FF_EXP = 2816, 8, 3584

RC = BATCH * SEQ
RL = DEC_BATCH * DEC_SEQ
R = RC + RL

C_BG, C_QKV, C_DG, C_GQ, C_GKV, C_MCQ, C_MCKV, C_SM = 0, 3072, 4608, 5120, 5632, 5888, 6144, 6272
NP = 6400

VMEM_LIMIT = 56 * 1024 * 1024


def _cp(n_grid):
    return pltpu.CompilerParams(dimension_semantics=("arbitrary",) * n_grid,
                                vmem_limit_bytes=VMEM_LIMIT)


def _dot(a, b):
    return jnp.dot(a, b, preferred_element_type=F32)


def _bdot(a, b):
    return jnp.dot(a.astype(BF16), b.astype(BF16), preferred_element_type=F32)


def _bdot_nt(a, b):
    return lax.dot_general(a.astype(BF16), b.astype(BF16), (((1,), (1,)), ((), ())),
                           preferred_element_type=F32)


def _bdot_tn(a, b):
    return lax.dot_general(a.astype(BF16), b.astype(BF16), (((0,), (0,)), ((), ())),
                           preferred_element_type=F32)


def _split2(a):
    hi = a.astype(BF16)
    lo = (a - hi.astype(F32)).astype(BF16)
    return hi, lo


def _split3(a):
    a1 = a.astype(BF16)
    r1 = a - a1.astype(F32)
    a2 = r1.astype(BF16)
    a3 = (r1 - a2.astype(F32)).astype(BF16)
    return a1, a2, a3


def _dot3(a, b):
    ah, al = _split2(a)
    bh, bl = _split2(b)
    return _dot(ah, bh) + (_dot(ah, bl) + _dot(al, bh))


def _mask_dot_r(mask_bf, g):
    g1, g2, g3 = _split3(g)
    return _dot(mask_bf, g1) + (_dot(mask_bf, g2) + _dot(mask_bf, g3))


def _mask_dot_l(g, mask_bf):
    g1, g2, g3 = _split3(g)
    return _dot(g1, mask_bf) + (_dot(g2, mask_bf) + _dot(g3, mask_bf))


def _sigmoid(x):
    return 1.0 / (1.0 + jnp.exp(-x))


def _silu(x):
    return x * _sigmoid(x)


def _softplus(x):
    return jnp.maximum(x, 0.0) + jnp.log1p(jnp.exp(-jnp.abs(x)))


def _mod_group(row0):
    return jnp.where(row0 < RC, 0, 1 + (row0 - RC) // DEC_SEQ)


MODS_TN = 1536


def _mods_body(c_ref, w_ref, b_ref, o_ref):
    s = _silu(c_ref[...])
    o_ref[0] = _bdot(s, w_ref[0]) + b_ref[0]


def _mods(cond8, w_mod, b_mod):
    nj = 6 * D // MODS_TN
    return pl.pallas_call(
        _mods_body,
        grid=(DEPTH, nj),
        in_specs=[pl.BlockSpec((8, D), lambda l, j: (0, 0)),
                  pl.BlockSpec((1, D, MODS_TN), lambda l, j: (l, 0, j)),
                  pl.BlockSpec((1, 1, MODS_TN), lambda l, j: (l, 0, j))],
        out_specs=pl.BlockSpec((1, 8, MODS_TN), lambda l, j: (l, 0, j)),
        out_shape=jax.ShapeDtypeStruct((DEPTH, 8, 6 * D), F32),
        compiler_params=_cp(2),
        name="mods",
    )(cond8, w_mod, b_mod.reshape(DEPTH, 1, 6 * D))


IN_TM, IN_TN = 1024, 1280


def _modnorm(x, g, shift, scale):
    ms = jnp.mean(x * x, axis=-1, keepdims=True)
    y = x * lax.rsqrt(ms + EPS) * g
    return y * (1.0 + scale) + shift


def _inproj_body(x_ref, m_ref, g_ref, w_ref, o_ref, h_scr):
    @pl.when(pl.program_id(1) == 0)
    def _():
        h = _modnorm(x_ref[...], g_ref[...], m_ref[0, 0:1, :], m_ref[0, 1:2, :])
        h_scr[...] = h.astype(BF16)

    o_ref[...] = _dot(h_scr[...], w_ref[...])


def _inproj(x, mods_l, g, w_bf):
    return pl.pallas_call(
        _inproj_body,
        grid=(R // IN_TM, NP // IN_TN),
        in_specs=[pl.BlockSpec((IN_TM, D), lambda i, j: (i, 0)),
                  pl.BlockSpec((1, 6, D), lambda i, j: (_mod_group(i * IN_TM), 0, 0)),
                  pl.BlockSpec((1, D), lambda i, j: (0, 0)),
                  pl.BlockSpec((D, IN_TN), lambda i, j: (0, j))],
        out_specs=pl.BlockSpec((IN_TM, IN_TN), lambda i, j: (i, j)),
        out_shape=jax.ShapeDtypeStruct((R, NP), F32),
        scratch_shapes=[pltpu.VMEM((IN_TM, D), BF16)],
        compiler_params=_cp(2),
        name="inproj",
    )(x, mods_l, g, w_bf)


PREP_TM = 256
PREP_NC = PREP_TM // DN_C
PREP_LOCKSTEP = 4


def _prep_body(x_ref, xp_ref, xn_ref, sm_ref, smt_ref, cw_ref, al_ref, dt_ref, alt_ref, dtt_ref,
               u_o, w_o, qg_o, kd_o, ai_o, bg_o, qkv_o, gr_o):
    i = pl.program_id(0)
    n_ctx = RC // PREP_TM
    per_seq = DEC_SEQ // PREP_TM
    is_ctx = i < n_ctx
    j = (i - n_ctx) % per_seq
    first = jnp.logical_or(is_ctx, j == 0)
    last = jnp.logical_or(is_ctx, j == per_seq - 1)

    x = x_ref[...]
    prev_row = jnp.where(first, 0.0, xp_ref[7:8, :])
    next_row = jnp.where(last, 0.0, xn_ref[0:1, :])
    row = lax.broadcasted_iota(jnp.int32, (PREP_TM, 1), 0)
    xm = jnp.where(row == 0, prev_row, pltpu.roll(x, 1, axis=0))
    xq = jnp.where(row == PREP_TM - 1, next_row, pltpu.roll(x, PREP_TM - 1, axis=0))
    w = cw_ref[...]
    y = _silu(w[0:1] * xm + w[1:2] * x + w[2:3] * xq)

    for h in range(DN_H):
        qh = y[:, h * DN_DK:(h + 1) * DN_DK]
        qn = qh * lax.rsqrt(jnp.sum(qh * qh, axis=-1, keepdims=True) + EPS) * (DN_DK ** -0.5)
        qkv_o[:, h * DN_DK:(h + 1) * DN_DK] = qn
        kh = y[:, 512 + h * DN_DK:512 + (h + 1) * DN_DK]
        kn = kh * lax.rsqrt(jnp.sum(kh * kh, axis=-1, keepdims=True) + EPS)
        qkv_o[:, 512 + h * DN_DK:512 + (h + 1) * DN_DK] = kn
    qkv_o[:, 1024:1536] = y[:, 1024:1536]

    r = lax.broadcasted_iota(jnp.int32, (PREP_TM, PREP_TM), 0)
    c = lax.broadcasted_iota(jnp.int32, (PREP_TM, PREP_TM), 1)
    same = (r // DN_C) == (c // DN_C)
    low = jnp.where(jnp.logical_and(same, r >= c), 1.0, 0.0).astype(BF16)
    upp = jnp.where(jnp.logical_and(same, r <= c), 1.0, 0.0).astype(BF16)

    sm = sm_ref[...]
    beta = _sigmoid(sm)
    g = -jnp.exp(al_ref[...]) * _softplus(sm + dt_ref[...])
    gc_f = _mask_dot_r(low, g)
    gc_b = _mask_dot_r(upp, g)
    lane = lax.broadcasted_iota(jnp.int32, (1, 128), 1)
    bg_o[...] = jnp.where(lane < 8, beta, jnp.where(lane < 12, gc_f, jnp.where(lane < 16, gc_b, 0.0)))

    gt = -jnp.exp(alt_ref[...]) * _softplus(smt_ref[...] + dtt_ref[...])
    gct_f = _mask_dot_l(gt, upp)
    gct_b = _mask_dot_l(gt, low)
    sub = lax.broadcasted_iota(jnp.int32, (16, 1), 0)
    gct = jnp.where(sub < 12, gct_f, gct_b)
    for k in range(PREP_NC):
        gr_o[k] = gct[8:16, k * DN_C:(k + 1) * DN_C]

    rr = lax.broadcasted_iota(jnp.int32, (DN_C, DN_C), 0)
    cc = lax.broadcasted_iota(jnp.int32, (DN_C, DN_C), 1)

    def chunk_group(gi, carry):
        prob = [(cj, d, h) for cj in range(PREP_LOCKSTEP) for d in range(2) for h in range(DN_H)]
        ci = [gi * PREP_LOCKSTEP + cj for cj in range(PREP_LOCKSTEP)]
        rows_c = [pl.ds(pl.multiple_of(c_ * DN_C, DN_C), DN_C) for c_ in ci]
        bgc = [bg_o[r_, :] for r_ in rows_c]
        grow_all = [gr_o[c_] for c_ in ci]
        rows = [rows_c[cj] for cj, d, h in prob]
        q = [qkv_o[rows_c[cj], h * DN_DK:(h + 1) * DN_DK] for cj, d, h in prob]
        k = [qkv_o[rows_c[cj], 512 + h * DN_DK:512 + (h + 1) * DN_DK] for cj, d, h in prob]
        v = [qkv_o[rows_c[cj], 1024 + h * DN_DV:1024 + (h + 1) * DN_DV] for cj, d, h in prob]
        beta = [bgc[cj][:, d * DN_H + h:d * DN_H + h + 1] for cj, d, h in prob]
        gcol = [bgc[cj][:, 8 + d * DN_H + h:9 + d * DN_H + h] for cj, d, h in prob]
        grow = [grow_all[cj][d * DN_H + h:d * DN_H + h + 1, :] for cj, d, h in prob]
        incl = [(rr >= cc) if d == 0 else (rr <= cc) for cj, d, h in prob]
        strict = [(rr > cc) if d == 0 else (rr < cc) for cj, d, h in prob]
        n = len(prob)
        decay = [jnp.exp(jnp.where(incl[i], gcol[i] - grow[i], -1e30)) for i in range(n)]
        kb = [k[i] * beta[i] for i in range(n)]
        kk = [_bdot_nt(kb[i], k[i]) for i in range(n)]
        qk = [_bdot_nt(q[i], k[i]) for i in range(n)]
        a = [jnp.where(strict[i], kk[i] * decay[i], 0.0) for i in range(n)]
        blk = lambda b: (rr // b) == (cc // b)
        eye = jnp.where(rr == cc, 1.0, 0.0)
        p = [jnp.where(blk(8), a[i], 0.0) for i in range(n)]
        t = [eye - p[i] for i in range(n)]
        for _ in range(2):
            p = [_bdot(p[i], p[i]) for i in range(n)]
            t = [t[i] + _bdot(t[i], p[i]) for i in range(n)]
        for b in (16, 32, 64):
            m = jnp.logical_and(blk(b), jnp.logical_not(blk(b // 2)))
            tl = [_bdot(t[i], jnp.where(m, a[i], 0.0)) for i in range(n)]
            t = [t[i] - _bdot(tl[i], t[i]) for i in range(n)]
        egc = [jnp.exp(gcol[i]) for i in range(n)]
        glast = [gcol[i][DN_C - 1:DN_C, :] if prob[i][1] == 0 else gcol[i][0:1, :] for i in range(n)]
        uw = [_bdot(t[i], jnp.concatenate([v[i] * beta[i], kb[i] * egc[i]], axis=1)) for i in range(n)]
        for i, (cj, d, h) in enumerate(prob):
            u_o[d, rows[i], h * DN_DV:(h + 1) * DN_DV] = uw[i][:, 0:DN_DV]
            w_o[d, rows[i], h * DN_DK:(h + 1) * DN_DK] = uw[i][:, DN_DV:DN_DV + DN_DK].astype(BF16)
            qg_o[d, rows[i], h * DN_DK:(h + 1) * DN_DK] = (q[i] * egc[i]).astype(BF16)
            kd_o[d, rows[i], h * DN_DK:(h + 1) * DN_DK] = (k[i] * jnp.exp(glast[i] - gcol[i])).astype(BF16)
            ai_o[d, rows[i], h * DN_C:(h + 1) * DN_C] = (qk[i] * decay[i]).astype(BF16)
        return carry

    lax.fori_loop(0, PREP_NC // PREP_LOCKSTEP, chunk_group, 0)


def _dn_prep(proj, sm_t, conv_w, al128, dt128, al_t, dt_t):
    nb8 = R // 8
    qb = C_QKV // 1536
    wide = lambda dt: jax.ShapeDtypeStruct((2, R, DN_H * DN_DK), dt)
    wide_spec = pl.BlockSpec((2, PREP_TM, DN_H * DN_DK), lambda i: (0, i, 0))
    return pl.pallas_call(
        _prep_body,
        grid=(R // PREP_TM,),
        in_specs=[pl.BlockSpec((PREP_TM, 1536), lambda i: (i, qb)),
                  pl.BlockSpec((8, 1536), lambda i: (jnp.maximum(i * (PREP_TM // 8) - 1, 0), qb)),
                  pl.BlockSpec((8, 1536), lambda i: (jnp.minimum((i + 1) * (PREP_TM // 8), nb8 - 1), qb)),
                  pl.BlockSpec((PREP_TM, 128), lambda i: (i, C_SM // 128)),
                  pl.BlockSpec((16, PREP_TM), lambda i: (0, i)),
                  pl.BlockSpec((3, 1536), lambda i: (0, 0)),
                  pl.BlockSpec((1, 128), lambda i: (0, 0)),
                  pl.BlockSpec((1, 128), lambda i: (0, 0)),
                  pl.BlockSpec((16, 1), lambda i: (0, 0)),
                  pl.BlockSpec((16, 1), lambda i: (0, 0))],
        out_specs=[wide_spec, wide_spec, wide_spec, wide_spec,
                   pl.BlockSpec((2, PREP_TM, DN_H * DN_C), lambda i: (0, i, 0)),
                   pl.BlockSpec((PREP_TM, 128), lambda i: (i, 0))],
        out_shape=[wide(F32), wide(BF16), wide(BF16), wide(BF16),
                   jax.ShapeDtypeStruct((2, R, DN_H * DN_C), BF16),
                   jax.ShapeDtypeStruct((R, 128), F32)],
        scratch_shapes=[pltpu.VMEM((PREP_TM, 1536), F32), pltpu.VMEM((PREP_NC, 8, DN_C), F32)],
        compiler_params=_cp(1),
        name="dn_prep",
    )(proj, proj, proj, proj, sm_t, conv_w, al128, dt128, al_t, dt_t)


SCAN_CTX = RC // PREP_TM
SCAN_PER = DEC_SEQ // PREP_TM
SCAN_NSEQ = BATCH + DEC_BATCH


def _scan_bwd_block(i):
    j = i - SCAN_CTX
    return jnp.where(i < SCAN_CTX, i, SCAN_CTX + (j // SCAN_PER) * SCAN_PER + (SCAN_PER - 1 - j % SCAN_PER))


def _scan_body(uf, wf, qf, kf, af, bf, ub, wb, qb, kb_, ab, bb, s0_ref, of_ref, ob_ref, so_ref, s_scr):
    i = pl.program_id(0)
    is_ctx = i < SCAN_CTX
    j = (i - SCAN_CTX) % SCAN_PER
    first = jnp.logical_or(is_ctx, j == 0)
    last = jnp.logical_or(is_ctx, j == SCAN_PER - 1)

    @pl.when(first)
    def _():
        for d in range(2):
            for h in range(DN_H):
                s_scr[d, h] = jnp.where(is_ctx, 0.0, s0_ref[0, d, h])

    dirs = ((uf, wf, qf, kf, af, bf, of_ref), (ub, wb, qb, kb_, ab, bb, ob_ref))

    prob = [(d, h) for d in range(2) for h in range(DN_H)]
    tn = (((0,), (0,)), ((), ()))

    def step(n, carry):
        rows = [pl.ds(pl.multiple_of((n if d == 0 else PREP_NC - 1 - n) * DN_C, DN_C), DN_C) for d in range(2)]
        bgc = [dirs[d][5][rows[d], :] for d in range(2)]
        cols = [slice(h * DN_DK, (h + 1) * DN_DK) for d, h in prob]
        st = [s_scr[d, h] for d, h in prob]
        stb = [x.astype(BF16) for x in st]
        ws = [_dot(dirs[d][1][0, rows[d], cols[i]], stb[i]) for i, (d, h) in enumerate(prob)]
        qs = [_dot(dirs[d][2][0, rows[d], cols[i]], stb[i]) for i, (d, h) in enumerate(prob)]
        vb = [(dirs[d][0][0, rows[d], cols[i]] - ws[i]).astype(BF16) for i, (d, h) in enumerate(prob)]
        av = [_dot(dirs[d][4][0, rows[d], h * DN_C:(h + 1) * DN_C], vb[i]) for i, (d, h) in enumerate(prob)]
        kv = [lax.dot_general(dirs[d][3][0, rows[d], cols[i]], vb[i], tn, preferred_element_type=F32)
              for i, (d, h) in enumerate(prob)]
        for i, (d, h) in enumerate(prob):
            gcol = bgc[d][:, 8 + d * DN_H + h:9 + d * DN_H + h]
            glast = gcol[DN_C - 1:DN_C, :] if d == 0 else gcol[0:1, :]
            s_scr[d, h] = st[i] * jnp.exp(glast) + kv[i]
            dirs[d][6][rows[d], cols[i]] = qs[i] + av[i]
        return carry

    lax.fori_loop(0, PREP_NC, step, 0)

    @pl.when(last)
    def _():
        for d in range(2):
            for h in range(DN_H):
                so_ref[0, d, h] = s_scr[d, h]


def _dn_scan(u, w, qg, kd, ai, bg, s0):
    fwd3 = lambda i: (0, i, 0)
    bwd3 = lambda i: (1, _scan_bwd_block(i), 0)
    wide = lambda m: pl.BlockSpec((1, PREP_TM, DN_H * DN_DK), m)
    narrow = lambda m: pl.BlockSpec((1, PREP_TM, DN_H * DN_C), m)
    state_spec = lambda m: pl.BlockSpec((1, 2, DN_H, DN_DK, DN_DV), m)
    seq_of = lambda i: jnp.where(i < SCAN_CTX, i, SCAN_CTX + (i - SCAN_CTX) // SCAN_PER)
    return pl.pallas_call(
        _scan_body,
        grid=(R // PREP_TM,),
        in_specs=[wide(fwd3), wide(fwd3), wide(fwd3), wide(fwd3), narrow(fwd3),
                  pl.BlockSpec((PREP_TM, 128), lambda i: (i, 0)),
                  wide(bwd3), wide(bwd3), wide(bwd3), wide(bwd3), narrow(bwd3),
                  pl.BlockSpec((PREP_TM, 128), lambda i: (_scan_bwd_block(i), 0)),
                  state_spec(lambda i: (jnp.maximum(i - SCAN_CTX, 0) // SCAN_PER, 0, 0, 0, 0))],
        out_specs=[pl.BlockSpec((PREP_TM, DN_H * DN_DV), lambda i: (i, 0)),
                   pl.BlockSpec((PREP_TM, DN_H * DN_DV), lambda i: (_scan_bwd_block(i), 0)),
                   state_spec(lambda i: (seq_of(i), 0, 0, 0, 0))],
        out_shape=[jax.ShapeDtypeStruct((R, DN_H * DN_DV), F32),
                   jax.ShapeDtypeStruct((R, DN_H * DN_DV), F32),
                   jax.ShapeDtypeStruct((SCAN_NSEQ, 2, DN_H, DN_DK, DN_DV), F32)],
        scratch_shapes=[pltpu.VMEM((2, DN_H, DN_DK, DN_DV), F32)],
        compiler_params=_cp(1),
        name="dn_scan",
    )(u, w, qg, kd, ai, bg, u, w, qg, kd, ai, bg, s0)


def _group_mean_matrix(width, group):
    r = lax.broadcasted_iota(jnp.int32, (width, width), 0)
    c = lax.broadcasted_iota(jnp.int32, (width, width), 1)
    return jnp.where((r // group) == (c // group), 1.0 / group, 0.0).astype(BF16)


def _group_rmsnorm(x, w, group):
    m = _group_mean_matrix(x.shape[-1], group)
    hi, lo = _split2(x * x)
    ms = _dot(hi, m) + _dot(lo, m)
    return x * lax.rsqrt(ms + EPS) * w


def _rope(x, cos, sin_signed, group):
    width = x.shape[-1]
    half = group // 2
    lane = lax.broadcasted_iota(jnp.int32, (1, width), 1)
    swapped = jnp.where((lane % group) < half,
                        pltpu.roll(x, width - half, axis=1), pltpu.roll(x, half, axis=1))
    return x * cos + swapped * sin_signed


def _softmax_pv(s, v_bf):
    m = jnp.max(s, axis=-1, keepdims=True)
    p = jnp.exp(s - m)
    l = jnp.sum(p, axis=-1, keepdims=True)
    return _dot(p.astype(BF16), v_bf) / l


GQA_G = GQA_H // GQA_KV


def _gqa_body(T, TQ, latent, *refs):
    it = iter(refs)
    q_ref, kv_ref, qw_ref, kw_ref = next(it), next(it), next(it), next(it)
    if latent:
        cq_ref, sq_ref, ck_ref, sk_ref, kc_ref, vc_ref = (next(it) for _ in range(6))
    o_ref = next(it)
    kn_ref = None if latent else next(it)
    k_scr, v_scr = next(it), next(it)

    @pl.when(pl.program_id(1) == 0)
    def _():
        kv = kv_ref[...]
        k = _group_rmsnorm(kv[:, 0:128], kw_ref[...], GQA_HD)
        v = kv[:, 128:256]
        if latent:
            k = _rope(k, ck_ref[...], sk_ref[...], GQA_HD)
        else:
            kn_ref[...] = k
        for g in range(GQA_KV):
            k_scr[g, 0:T, :] = k[:, g * GQA_HD:(g + 1) * GQA_HD].astype(BF16)
            v_scr[g, 0:T, :] = v[:, g * GQA_HD:(g + 1) * GQA_HD].astype(BF16)
            if latent:
                k_scr[g, T:T + PAST, :] = kc_ref[0, :, g * GQA_HD:(g + 1) * GQA_HD].astype(BF16)
                v_scr[g, T:T + PAST, :] = vc_ref[0, :, g * GQA_HD:(g + 1) * GQA_HD].astype(BF16)

    q = _group_rmsnorm(q_ref[...], qw_ref[...], GQA_HD)
    if latent:
        q = _rope(q, cq_ref[...], sq_ref[...], GQA_HD)
    q = q * (GQA_HD ** -0.5)
    for g in range(GQA_KV):
        qg = jnp.concatenate(
            [q[:, (g * GQA_G + j) * GQA_HD:(g * GQA_G + j + 1) * GQA_HD] for j in range(GQA_G)], axis=0)
        s = _bdot_nt(qg, k_scr[g])
        o = _softmax_pv(s, v_scr[g])
        for j in range(GQA_G):
            hh = g * GQA_G + j
            o_ref[:, hh * GQA_HD:(hh + 1) * GQA_HD] = o[j * TQ:(j + 1) * TQ, :]


def _gqa(proj, qw, kw, T, TQ, n_seq, row_off, rope=None, cache=None):
    latent = rope is not None
    tk = T + (PAST if latent else 0)
    nq = T // TQ
    qo = row_off // TQ
    so = row_off // T
    in_specs = [pl.BlockSpec((TQ, 512), lambda i, j: (qo + i * nq + j, C_GQ // 512)),
                pl.BlockSpec((T, 256), lambda i, j: (so + i, C_GKV // 256)),
                pl.BlockSpec((1, 512), lambda i, j: (0, 0)),
                pl.BlockSpec((1, 128), lambda i, j: (0, 0))]
    args = [proj, proj, qw, kw]
    if latent:
        cq, sq, ck, sk = rope
        kc, vc = cache
        in_specs += [pl.BlockSpec((TQ, 512), lambda i, j: (j, 0)),
                     pl.BlockSpec((TQ, 512), lambda i, j: (j, 0)),
                     pl.BlockSpec((T, 128), lambda i, j: (0, 0)),
                     pl.BlockSpec((T, 128), lambda i, j: (0, 0)),
                     pl.BlockSpec((1, PAST, 128), lambda i, j: (i, 0, 0)),
                     pl.BlockSpec((1, PAST, 128), lambda i, j: (i, 0, 0))]
        args += [cq, sq, ck, sk, kc, vc]
    out_specs = [pl.BlockSpec((TQ, 512), lambda i, j: (i * nq + j, 0))]
    out_shape = [jax.ShapeDtypeStruct((n_seq * T, 512), F32)]
    if not latent:
        out_specs.append(pl.BlockSpec((T, 128), lambda i, j: (i, 0)))
        out_shape.append(jax.ShapeDtypeStruct((n_seq * T, 128), F32))
    return pl.pallas_call(
        functools.partial(_gqa_body, T, TQ, latent),
        grid=(n_seq, nq),
        in_specs=in_specs,
        out_specs=out_specs,
        out_shape=out_shape,
        scratch_shapes=[pltpu.VMEM((GQA_KV, tk, GQA_HD), BF16), pltpu.VMEM((GQA_KV, tk, GQA_HD), BF16)],
        compiler_params=_cp(2),
        name="gqa_%d" % T,
    )(*args)


MLA_DQ = MLA_NOPE + MLA_ROPE
MLA_KV_ROWS = 512


def _mla_body(T, TQ, latent, *refs):
    it = iter(refs)
    cq_ref, ckv_ref, sm_ref, wq_ref, wkv_ref, qw_ref, kvw_ref = (next(it) for _ in range(7))
    if latent:
        cosq_ref, sinq_ref, cosk_ref, sink_ref, cc_ref, kc_ref = (next(it) for _ in range(6))
    o_ref = next(it)
    cn_ref = None if latent else next(it)
    k_scr, v_scr = next(it), next(it)
    tk = T + (PAST if latent else 0)

    @pl.when(pl.program_id(1) == 0)
    def _():
        x = ckv_ref[...]
        ms = jnp.mean(x * x, axis=-1, keepdims=True)
        ckv = x * lax.rsqrt(ms + EPS) * kvw_ref[...]
        sm = sm_ref[...]
        if latent:
            sm = _rope_small(sm, cosk_ref[...], sink_ref[...])
        else:
            cn_ref[...] = ckv
        kr = sm[:, 16:16 + MLA_ROPE]

        def put(rows0, ckv_rows, kr_rows):
            n = ckv_rows.shape[0]
            kv = _bdot(ckv_rows, wkv_ref[...])
            krb = kr_rows.astype(BF16)
            for h in range(MLA_H):
                k_scr[h, rows0:rows0 + n, 0:MLA_NOPE] = kv[:, h * MLA_NOPE:(h + 1) * MLA_NOPE].astype(BF16)
                k_scr[h, rows0:rows0 + n, MLA_NOPE:MLA_DQ] = krb
                v_scr[h, rows0:rows0 + n, :] = kv[:, 512 + h * MLA_V:512 + (h + 1) * MLA_V].astype(BF16)

        for r0 in range(0, T, MLA_KV_ROWS):
            r1 = min(r0 + MLA_KV_ROWS, T)
            put(r0, ckv[r0:r1], kr[r0:r1])
        if latent:
            put(T, cc_ref[0], kc_ref[0])

    x = cq_ref[...]
    ms = jnp.mean(x * x, axis=-1, keepdims=True)
    cq = x * lax.rsqrt(ms + EPS) * qw_ref[...]
    qf = _bdot(cq, wq_ref[...])
    qn = qf[:, 0:512]
    qr = qf[:, 512:768]
    if latent:
        qr = _rope(qr, cosq_ref[...], sinq_ref[...], MLA_ROPE)
    scale = MLA_DQ ** -0.5
    for h in range(MLA_H):
        qh = jnp.concatenate([qn[:, h * MLA_NOPE:(h + 1) * MLA_NOPE],
                              qr[:, h * MLA_ROPE:(h + 1) * MLA_ROPE]], axis=1) * scale
        s = _bdot_nt(qh, k_scr[h])
        o_ref[:, h * MLA_V:(h + 1) * MLA_V] = _softmax_pv(s, v_scr[h])


def _rope_small(sm, cos, sin_signed):
    lane = lax.broadcasted_iota(jnp.int32, (1, 128), 1)
    half = MLA_ROPE // 2
    swapped = jnp.where(lane < 16 + half, pltpu.roll(sm, 128 - half, axis=1), pltpu.roll(sm, half, axis=1))
    return sm * cos + swapped * sin_signed


def _mla(proj, wq, wkv, qw, kvw, T, TQ, n_seq, row_off, rope=None, cache=None):
    latent = rope is not None
    tk = T + (PAST if latent else 0)
    nq = T // TQ
    qo = row_off // TQ
    so = row_off // T
    in_specs = [pl.BlockSpec((TQ, 256), lambda i, j: (qo + i * nq + j, C_MCQ // 256)),
                pl.BlockSpec((T, 128), lambda i, j: (so + i, C_MCKV // 128)),
                pl.BlockSpec((T, 128), lambda i, j: (so + i, C_SM // 128)),
                pl.BlockSpec((MLA_QL, 768), lambda i, j: (0, 0)),
                pl.BlockSpec((MLA_KVL, 1024), lambda i, j: (0, 0)),
                pl.BlockSpec((1, 256), lambda i, j: (0, 0)),
                pl.BlockSpec((1, 128), lambda i, j: (0, 0))]
    args = [proj, proj, proj, wq, wkv, qw, kvw]
    if latent:
        cosq, sinq, cosk, sink = rope
        cc, kc = cache
        in_specs += [pl.BlockSpec((TQ, 256), lambda i, j: (j, 0)),
                     pl.BlockSpec((TQ, 256), lambda i, j: (j, 0)),
                     pl.BlockSpec((T, 128), lambda i, j: (0, 0)),
                     pl.BlockSpec((T, 128), lambda i, j: (0, 0)),
                     pl.BlockSpec((1, PAST, 128), lambda i, j: (i, 0, 0)),
                     pl.BlockSpec((1, PAST, MLA_ROPE), lambda i, j: (i, 0, 0))]
        args += [cosq, sinq, cosk, sink, cc, kc]
    out_specs = [pl.BlockSpec((TQ, 512), lambda i, j: (i * nq + j, 0))]
    out_shape = [jax.ShapeDtypeStruct((n_seq * T, 512), F32)]
    if not latent:
        out_specs.append(pl.BlockSpec((T, 128), lambda i, j: (i, 0)))
        out_shape.append(jax.ShapeDtypeStruct((n_seq * T, 128), F32))
    return pl.pallas_call(
        functools.partial(_mla_body, T, TQ, latent),
        grid=(n_seq, nq),
        in_specs=in_specs,
        out_specs=out_specs,
        out_shape=out_shape,
        scratch_shapes=[pltpu.VMEM((MLA_H, tk, MLA_DQ), BF16), pltpu.VMEM((MLA_H, tk, MLA_V), BF16)],
        compiler_params=_cp(2),
        name="mla_%d" % T,
    )(*args)


MG_TM = 512


def _merge_body(x_ref, m_ref, of_ref, ob_ref, dg_ref, og_ref, om_ref, bg_ref, ng_ref, wb_ref, wo_ref, o_ref):
    odn = of_ref[...] + ob_ref[...]
    dg = dg_ref[...]
    ng = ng_ref[...]
    parts = []
    for h in range(DN_H):
        oh = odn[:, h * DN_DV:(h + 1) * DN_DV]
        ms = jnp.mean(oh * oh, axis=-1, keepdims=True)
        parts.append(oh * lax.rsqrt(ms + EPS) * ng * _silu(dg[:, h * DN_DV:(h + 1) * DN_DV]))
    br0 = jnp.concatenate(parts, axis=1)
    merged = _sigmoid(bg_ref[:, 0:D]) * _bdot(br0, wb_ref[0])
    merged = merged + _sigmoid(bg_ref[:, D:2 * D]) * _bdot(og_ref[...], wb_ref[1])
    merged = merged + _sigmoid(bg_ref[:, 2 * D:3 * D]) * _bdot(om_ref[...], wb_ref[2])
    out = _bdot(merged, wo_ref[...])
    o_ref[...] = x_ref[...] + m_ref[0, 2:3, :] * out


def _merge(x, mods_l, o_f, o_b, proj, ogqa, omla, ng, wb, wo):
    row = lambda i: (i, 0)
    return pl.pallas_call(
        _merge_body,
        grid=(R // MG_TM,),
        in_specs=[pl.BlockSpec((MG_TM, D), row),
                  pl.BlockSpec((1, 6, D), lambda i: (_mod_group(i * MG_TM), 0, 0)),
                  pl.BlockSpec((MG_TM, 512), row),
                  pl.BlockSpec((MG_TM, 512), row),
                  pl.BlockSpec((MG_TM, 512), lambda i: (i, C_DG // 512)),
                  pl.BlockSpec((MG_TM, 512), row),
                  pl.BlockSpec((MG_TM, 512), row),
                  pl.BlockSpec((MG_TM, 3 * D), lambda i: (i, C_BG // (3 * D))),
                  pl.BlockSpec((1, DN_DV), lambda i: (0, 0)),
                  pl.BlockSpec((3, 512, D), lambda i: (0, 0, 0)),
                  pl.BlockSpec((D, D), lambda i: (0, 0))],
        out_specs=pl.BlockSpec((MG_TM, D), row),
        out_shape=jax.ShapeDtypeStruct((R, D), F32),
        compiler_params=_cp(1),
        name="merge",
    )(x, mods_l, o_f, o_b, proj, ogqa, omla, proj, ng, wb, wo)


FF_TM, FF_TF = 1024, 1408


def _ffn_body(x_ref, m_ref, g_ref, w1_ref, w3_ref, w2_ref, o_ref, h_scr, acc_scr):
    f = pl.program_id(1)

    @pl.when(f == 0)
    def _():
        h = _modnorm(x_ref[...], g_ref[...], m_ref[0, 3:4, :], m_ref[0, 4:5, :])
        h_scr[...] = h.astype(BF16)
        acc_scr[...] = jnp.zeros_like(acc_scr)

    h = h_scr[...]
    a = _silu(_dot(h, w1_ref[...])) * _dot(h, w3_ref[...])
    acc_scr[...] += _dot(a.astype(BF16), w2_ref[...])

    @pl.when(f == pl.num_programs(1) - 1)
    def _():
        o_ref[...] = x_ref[...] + m_ref[0, 5:6, :] * acc_scr[...]


def _ffn(x, mods_l, g, w1, w3, w2):
    return pl.pallas_call(
        _ffn_body,
        grid=(R // FF_TM, FF_DENSE // FF_TF),
        in_specs=[pl.BlockSpec((FF_TM, D), lambda i, f: (i, 0)),
                  pl.BlockSpec((1, 6, D), lambda i, f: (_mod_group(i * FF_TM), 0, 0)),
                  pl.BlockSpec((1, D), lambda i, f: (0, 0)),
                  pl.BlockSpec((D, FF_TF), lambda i, f: (0, f)),
                  pl.BlockSpec((D, FF_TF), lambda i, f: (0, f)),
                  pl.BlockSpec((FF_TF, D), lambda i, f: (f, 0))],
        out_specs=pl.BlockSpec((FF_TM, D), lambda i, f: (i, 0)),
        out_shape=jax.ShapeDtypeStruct((R, D), F32),
        scratch_shapes=[pltpu.VMEM((FF_TM, D), BF16), pltpu.VMEM((FF_TM, D), F32)],
        compiler_params=_cp(2),
        name="ffn",
    )(x, mods_l, g, w1, w3, w2)


RT_TM = 512
MOE_TG, MOE_TF = 512, 1792
MOE_NT = 2 * R // MOE_TG + N_EXP
MOE_ROWS = MOE_NT * MOE_TG
CB_TM = 256


def _router_body(x_ref, m_ref, g_ref, rw_ref, rb_ref, hn_ref, route_ref, cnt_ref, base_scr):
    @pl.when(pl.program_id(0) == 0)
    def _():
        base_scr[...] = jnp.zeros_like(base_scr)

    h = _modnorm(x_ref[...], g_ref[...], m_ref[0, 3:4, :], m_ref[0, 4:5, :])
    hn_ref[...] = h
    logits = _dot3(h, rw_ref[...]) + rb_ref[...]
    lane = lax.broadcasted_iota(jnp.int32, logits.shape, 1)
    m1 = jnp.max(logits, axis=-1, keepdims=True)
    i1 = jnp.min(jnp.where(logits == m1, lane, 128), axis=-1, keepdims=True)
    sel1 = lane == i1
    rest = jnp.where(sel1, -jnp.inf, logits)
    m2 = jnp.max(rest, axis=-1, keepdims=True)
    i2 = jnp.min(jnp.where(rest == m2, lane, 128), axis=-1, keepdims=True)
    sel2 = lane == i2
    e2 = jnp.exp(m2 - m1)
    p1 = 1.0 / (1.0 + e2)
    p2 = e2 / (1.0 + e2)

    cnt = jnp.where(jnp.logical_or(sel1, sel2), 1.0, 0.0)
    r = lax.broadcasted_iota(jnp.int32, (RT_TM, RT_TM), 0)
    c = lax.broadcasted_iota(jnp.int32, (RT_TM, RT_TM), 1)
    before = jnp.where(r > c, 1.0, 0.0).astype(BF16)
    seen = base_scr[...] + _dot(before, cnt.astype(BF16))
    rank1 = jnp.sum(jnp.where(sel1, seen, 0.0), axis=-1, keepdims=True)
    rank2 = jnp.sum(jnp.where(sel2, seen, 0.0), axis=-1, keepdims=True)
    vals = (i1.astype(F32), i2.astype(F32), rank1, rank2, p1, p2)
    route = jnp.zeros(logits.shape, F32)
    for k, val in enumerate(vals):
        route = jnp.where(lane == k, val, route)
    route_ref[...] = route
    base_scr[...] += jnp.sum(cnt, axis=0, keepdims=True)
    cnt_ref[...] = base_scr[...]


def _router(x, mods_l, g, rw128, rb128):
    return pl.pallas_call(
        _router_body,
        grid=(R // RT_TM,),
        in_specs=[pl.BlockSpec((RT_TM, D), lambda i: (i, 0)),
                  pl.BlockSpec((1, 6, D), lambda i: (_mod_group(i * RT_TM), 0, 0)),
                  pl.BlockSpec((1, D), lambda i: (0, 0)),
                  pl.BlockSpec((D, 128), lambda i: (0, 0)),
                  pl.BlockSpec((1, 128), lambda i: (0, 0))],
        out_specs=[pl.BlockSpec((RT_TM, D), lambda i: (i, 0)),
                   pl.BlockSpec((RT_TM, 128), lambda i: (i, 0)),
                   pl.BlockSpec((1, 128), lambda i: (0, 0))],
        out_shape=[jax.ShapeDtypeStruct((R, D), F32),
                   jax.ShapeDtypeStruct((R, 128), F32),
                   jax.ShapeDtypeStruct((1, 128), F32)],
        scratch_shapes=[pltpu.VMEM((1, 128), F32)],
        compiler_params=_cp(1),
        name="moe_router",
    )(x, mods_l, g, rw128, rb128)


def _gather_rows(idx_ref, n, src_hbm, dst, sem):
    def issue(i, carry):
        pltpu.make_async_copy(src_hbm.at[pl.ds(idx_ref[0, 0, i], 1)], dst.at[pl.ds(i, 1)], sem).start()
        return carry

    lax.fori_loop(0, n, issue, 0, unroll=8)
    pltpu.make_async_copy(src_hbm.at[pl.ds(0, n)], dst, sem).wait()


def _experts_body(te_ref, tv_ref, src_ref, hn_hbm, w1_ref, w3_ref, w2_ref, ys_ref, xg_scr, xb_scr, acc_scr, sem):
    t = pl.program_id(0)
    f = pl.program_id(1)
    valid = tv_ref[t] > 0

    @pl.when(jnp.logical_and(valid, f == 0))
    def _():
        _gather_rows(src_ref, MOE_TG, hn_hbm, xg_scr, sem)
        xb_scr[...] = xg_scr[...].astype(BF16)

    @pl.when(valid)
    def _():
        xb = xb_scr[...]
        a = _silu(_dot(xb, w1_ref[0])) * _dot(xb, w3_ref[0])
        y = _dot(a.astype(BF16), w2_ref[0])

        @pl.when(f == 0)
        def _():
            acc_scr[...] = y

        @pl.when(f > 0)
        def _():
            acc_scr[...] += y

    last = f == pl.num_programs(1) - 1

    @pl.when(jnp.logical_and(last, valid))
    def _():
        ys_ref[...] = acc_scr[...]

    @pl.when(jnp.logical_and(last, jnp.logical_not(valid)))
    def _():
        ys_ref[...] = jnp.zeros_like(ys_ref)


def _experts(tile_e, tile_v, src, hn, w1, w3, w2):
    grid_spec = pltpu.PrefetchScalarGridSpec(
        num_scalar_prefetch=2,
        grid=(MOE_NT, FF_EXP // MOE_TF),
        in_specs=[pl.BlockSpec((1, 1, MOE_TG), lambda t, f, te, tv: (t, 0, 0), memory_space=pltpu.SMEM),
                  pl.BlockSpec(memory_space=pl.ANY),
                  pl.BlockSpec((1, D, MOE_TF), lambda t, f, te, tv: (te[t], 0, f)),
                  pl.BlockSpec((1, D, MOE_TF), lambda t, f, te, tv: (te[t], 0, f)),
                  pl.BlockSpec((1, MOE_TF, D), lambda t, f, te, tv: (te[t], f, 0))],
        out_specs=pl.BlockSpec((MOE_TG, D), lambda t, f, te, tv: (t, 0)),
        scratch_shapes=[pltpu.VMEM((MOE_TG, D), F32), pltpu.VMEM((MOE_TG, D), BF16),
                        pltpu.VMEM((MOE_TG, D), F32), pltpu.SemaphoreType.DMA(())],
    )
    return pl.pallas_call(
        _experts_body,
        grid_spec=grid_spec,
        out_shape=jax.ShapeDtypeStruct((MOE_ROWS, D), F32),
        compiler_params=_cp(2),
        name="moe_experts",
    )(tile_e, tile_v, src, hn, w1, w3, w2)


def _combine_body(pos_ref, x_ref, m_ref, route_ref, ys_hbm, o_ref, buf, sem):
    _gather_rows(pos_ref, 2 * CB_TM, ys_hbm, buf, sem)
    route = route_ref[...]
    y = route[:, 4:5] * buf[0:CB_TM, :] + route[:, 5:6] * buf[CB_TM:2 * CB_TM, :]
    o_ref[...] = x_ref[...] + m_ref[0, 5:6, :] * y


def _combine(pos, x, mods_l, route, ys):
    return pl.pallas_call(
        _combine_body,
        grid=(R // CB_TM,),
        in_specs=[pl.BlockSpec((1, 1, 2 * CB_TM), lambda i: (i, 0, 0), memory_space=pltpu.SMEM),
                  pl.BlockSpec((CB_TM, D), lambda i: (i, 0)),
                  pl.BlockSpec((1, 6, D), lambda i: (_mod_group(i * CB_TM), 0, 0)),
                  pl.BlockSpec((CB_TM, 128), lambda i: (i, 0)),
                  pl.BlockSpec(memory_space=pl.ANY)],
        out_specs=pl.BlockSpec((CB_TM, D), lambda i: (i, 0)),
        out_shape=jax.ShapeDtypeStruct((R, D), F32),
        scratch_shapes=[pltpu.VMEM((2 * CB_TM, D), F32), pltpu.SemaphoreType.DMA(())],
        compiler_params=_cp(1),
        name="moe_combine",
    )(pos, x, mods_l, route, ys)


def _moe(x, mods_l, g, rw128, rb128, w1, w3, w2):
    hn, route, cnt = _router(x, mods_l, g, rw128, rb128)
    eid = route[:, 0:2].astype(jnp.int32)
    rank = route[:, 2:4].astype(jnp.int32)
    counts = cnt[0, :N_EXP].astype(jnp.int32)
    gsize = (counts + MOE_TG - 1) // MOE_TG * MOE_TG
    gend = jnp.cumsum(gsize)
    pos = (gend - gsize)[eid] + rank
    tile_start = jnp.arange(MOE_NT, dtype=jnp.int32) * MOE_TG
    tile_e = jnp.minimum(jnp.sum(tile_start[:, None] >= gend[None, :], axis=1), N_EXP - 1).astype(jnp.int32)
    tile_v = (tile_start < gend[-1]).astype(jnp.int32)
    tok = jnp.broadcast_to(jnp.arange(R, dtype=jnp.int32)[:, None], (R, 2))
    src = jnp.zeros((MOE_ROWS,), jnp.int32).at[pos.reshape(-1)].set(tok.reshape(-1))
    ys = _experts(tile_e, tile_v, src.reshape(MOE_NT, 1, MOE_TG), hn, w1, w3, w2)
    pos_t = pos.reshape(R // CB_TM, CB_TM, 2).transpose(0, 2, 1).reshape(R // CB_TM, 1, 2 * CB_TM)
    return _combine(pos_t, x, mods_l, route, ys)


FN_TM = 1024


def _final_body(x_ref, g_ref, o_ref):
    x = x_ref[...]
    ms = jnp.mean(x * x, axis=-1, keepdims=True)
    o_ref[...] = x * lax.rsqrt(ms + EPS) * g_ref[...]


def _final_norm(x, g):
    return pl.pallas_call(
        _final_body,
        grid=(R // FN_TM,),
        in_specs=[pl.BlockSpec((FN_TM, D), lambda i: (i, 0)), pl.BlockSpec((1, D), lambda i: (0, 0))],
        out_specs=pl.BlockSpec((FN_TM, D), lambda i: (i, 0)),
        out_shape=jax.ShapeDtypeStruct((R, D), F32),
        compiler_params=_cp(1),
        name="final_norm",
    )(x, g)


def _rope_tables(n_tokens, rot_dim):
    t = np.arange(n_tokens)
    row = (t // GRID_W).astype(np.float32)
    col = (t % GRID_W).astype(np.float32)
    n_freq = rot_dim // 4
    inv = (ROPE_THETA ** (-jnp.arange(n_freq, dtype=F32) / n_freq))
    ang = jnp.concatenate([jnp.asarray(row)[:, None] * inv, jnp.asarray(col)[:, None] * inv], axis=-1)
    cos, sin = jnp.cos(ang), jnp.sin(ang)
    return jnp.concatenate([cos, cos], axis=-1), jnp.concatenate([-sin, sin], axis=-1)


def _permute_w_in(w):
    return jnp.concatenate(
        [w[:, 3248:6320], w[:, 0:1536], w[:, 1536:2048], w[:, 2064:2576], w[:, 2576:2832],
         w[:, 2832:3088], w[:, 3088:3216], w[:, 2048:2064], w[:, 3216:3248],
         jnp.zeros((D, NP - 6320), w.dtype)], axis=1).astype(BF16)


def kernel(x_prompt, x_sample, c, cache_gqa_k, cache_gqa_v, cache_mla_ckv, cache_mla_krope, state_delta, c_ctx, w_mod, b_mod, norm1_g, norm2_g, w_in, dn_conv_w, dn_a_log, dn_dt_bias, dn_norm_g, gqa_q_norm, gqa_k_norm, mla_q_norm, mla_kv_norm, mla_w_uq, mla_w_ukv, w_branch, w_out, ffd_w1, ffd_w3, ffd_w2, router_w, router_b, moe_w1, moe_w3, moe_w2, final_g):
    x = jnp.concatenate([x_prompt.reshape(RC, D), x_sample.reshape(RL, D)], axis=0)
    cond8 = jnp.concatenate([c_ctx[None, :], c, jnp.zeros((3, D), F32)], axis=0)
    mods = _mods(cond8, w_mod, b_mod).reshape(DEPTH, 8, 6, D)

    cg, sg = _rope_tables(DEC_SEQ, GQA_HD)
    gqa_rope = (jnp.tile(cg, (1, GQA_H)), jnp.tile(sg, (1, GQA_H)),
                jnp.tile(cg, (1, GQA_KV)), jnp.tile(sg, (1, GQA_KV)))
    cm, sm_ = _rope_tables(DEC_SEQ, MLA_ROPE)
    padk = lambda t: jnp.pad(t, ((0, 0), (16, 128 - 16 - MLA_ROPE)))
    mla_rope = (jnp.tile(cm, (1, MLA_H)), jnp.tile(sm_, (1, MLA_H)),
                jnp.pad(cm, ((0, 0), (16, 128 - 16 - MLA_ROPE)), constant_values=1.0), padk(sm_))

    new_k, new_v, new_ckv, new_kr, new_s = [], [], [], [], []
    for l in range(DEPTH):
        proj = _inproj(x, mods[l], norm1_g[l][None, :], _permute_w_in(w_in[l]))

        pad128 = lambda v: jnp.pad(v.reshape(1, 8), ((0, 0), (8, 112)))
        padt = lambda v: jnp.pad(v.reshape(8, 1), ((8, 0), (0, 0)))
        sm_t = proj[:, C_SM:C_SM + 16].T
        dn_u, dn_w, dn_qg, dn_kd, dn_ai, dn_bg = _dn_prep(
            proj, sm_t, dn_conv_w[l], pad128(dn_a_log[l]), pad128(dn_dt_bias[l]),
            padt(dn_a_log[l]), padt(dn_dt_bias[l]))
        o_dn_f, o_dn_b, s_all = _dn_scan(dn_u, dn_w, dn_qg, dn_kd, dn_ai, dn_bg, state_delta[:, l])
        s_c = s_all[:BATCH]

        qw = jnp.tile(gqa_q_norm[l][None, :], (1, GQA_H))
        kw = jnp.tile(gqa_k_norm[l][None, :], (1, GQA_KV))
        o_g_c, kn_c = _gqa(proj, qw, kw, SEQ, SEQ, BATCH, 0)
        (o_g_l,) = _gqa(proj, qw, kw, DEC_SEQ, 128, DEC_BATCH, RC, rope=gqa_rope,
                        cache=(cache_gqa_k[:, l].reshape(DEC_BATCH, PAST, 128),
                               cache_gqa_v[:, l].reshape(DEC_BATCH, PAST, 128)))
        o_gqa = jnp.concatenate([o_g_c, o_g_l], axis=0)

        wq = mla_w_uq[l].reshape(MLA_QL, MLA_H, MLA_DQ)
        wq = jnp.concatenate([wq[:, :, :MLA_NOPE].reshape(MLA_QL, -1), wq[:, :, MLA_NOPE:].reshape(MLA_QL, -1)],
                             axis=1).astype(BF16)
        wkv = mla_w_ukv[l].reshape(MLA_KVL, MLA_H, MLA_NOPE + MLA_V)
        wkv = jnp.concatenate([wkv[:, :, :MLA_NOPE].reshape(MLA_KVL, -1), wkv[:, :, MLA_NOPE:].reshape(MLA_KVL, -1)],
                              axis=1).astype(BF16)
        mqw, mkvw = mla_q_norm[l][None, :], mla_kv_norm[l][None, :]
        o_m_c, ckv_c = _mla(proj, wq, wkv, mqw, mkvw, SEQ, SEQ, BATCH, 0)
        (o_m_l,) = _mla(proj, wq, wkv, mqw, mkvw, DEC_SEQ, 256, DEC_BATCH, RC, rope=mla_rope,
                        cache=(cache_mla_ckv[:, l], cache_mla_krope[:, l]))
        o_mla = jnp.concatenate([o_m_c, o_m_l], axis=0)

        x = _merge(x, mods[l], o_dn_f, o_dn_b, proj, o_gqa, o_mla, dn_norm_g[l][None, :],
                   w_branch[l].astype(BF16), w_out[l].astype(BF16))

        j = l // 2
        if l % 2 == 0:
            x = _ffn(x, mods[l], norm2_g[l][None, :], ffd_w1[j].astype(BF16), ffd_w3[j].astype(BF16),
                     ffd_w2[j].astype(BF16))
        else:
            rw128 = jnp.pad(router_w[j], ((0, 0), (0, 128 - N_EXP)))
            rb128 = jnp.pad(router_b[j][None, :], ((0, 0), (0, 128 - N_EXP)), constant_values=-jnp.inf)
            x = _moe(x, mods[l], norm2_g[l][None, :], rw128, rb128, moe_w1[j].astype(BF16),
                     moe_w3[j].astype(BF16), moe_w2[j].astype(BF16))

        new_k.append(kn_c.reshape(BATCH, SEQ, GQA_KV, GQA_HD))
        new_v.append(proj[:RC, C_GKV + 128:C_GKV + 256].reshape(BATCH, SEQ, GQA_KV, GQA_HD))
        new_ckv.append(ckv_c.reshape(BATCH, SEQ, MLA_KVL))
        new_kr.append(proj[:RC, C_SM + 16:C_SM + 16 + MLA_ROPE].reshape(BATCH, SEQ, MLA_ROPE))
        new_s.append(s_c)

    y = _final_norm(x, final_g[None, :])
    return (y[:RC].reshape(BATCH, SEQ, D), y[RC:].reshape(DEC_BATCH, DEC_SEQ, D),
            jnp.stack(new_k, axis=1), jnp.stack(new_v, axis=1), jnp.stack(new_ckv, axis=1),
            jnp.stack(new_kr, axis=1), jnp.stack(new_s, axis=1))
```

```python
import functools
import math

import jax
import jax.numpy as jnp
import numpy as np
from jax import lax
from jax.experimental import pallas as pl
from jax.experimental.pallas import tpu as pltpu

F32 = jnp.float32
BF16 = jnp.bfloat16

D = 1024
BATCH, SEQ = 32, 256
DEC_BATCH, DEC_SEQ = 4, 2048
DEPTH = 4
PAST = 512
GRID_W = 64
ROPE_THETA = 10000.0
EPS = 1e-6
DN_H, DN_DK, DN_DV, DN_C = 4, 128, 128, 64
GQA_H, GQA_KV, GQA_HD = 8, 2, 64
MLA_H, MLA_QL, MLA_KVL, MLA_NOPE, MLA_ROPE, MLA_V = 8, 256, 128, 64, 32, 64
FF_DENSE, N_EXP, FF_EXP = 2816, 8, 3584

RC = BATCH * SEQ
RL = DEC_BATCH * DEC_SEQ
R = RC + RL

C_BG, C_QKV, C_DG, C_GQ, C_GKV, C_MCQ, C_MCKV, C_SM = 0, 3072, 4608, 5120, 5632, 5888, 6144, 6272
NP = 6400

VMEM_LIMIT = 56 * 1024 * 1024


def _cp(n_grid):
    return pltpu.CompilerParams(dimension_semantics=("arbitrary",) * n_grid,
                                vmem_limit_bytes=VMEM_LIMIT)


def _dot(a, b):
    return jnp.dot(a, b, preferred_element_type=F32)


def _bdot(a, b):
    return jnp.dot(a.astype(BF16), b.astype(BF16), preferred_element_type=F32)


def _bdot_nt(a, b):
    return lax.dot_general(a.astype(BF16), b.astype(BF16), (((1,), (1,)), ((), ())),
                           preferred_element_type=F32)


def _bdot_tn(a, b):
    return lax.dot_general(a.astype(BF16), b.astype(BF16), (((0,), (0,)), ((), ())),
                           preferred_element_type=F32)


def _split2(a):
    hi = a.astype(BF16)
    lo = (a - hi.astype(F32)).astype(BF16)
    return hi, lo


def _split3(a):
    a1 = a.astype(BF16)
    r1 = a - a1.astype(F32)
    a2 = r1.astype(BF16)
    a3 = (r1 - a2.astype(F32)).astype(BF16)
    return a1, a2, a3


def _dot3(a, b):
    ah, al = _split2(a)
    bh, bl = _split2(b)
    return _dot(ah, bh) + (_dot(ah, bl) + _dot(al, bh))


def _mask_dot_r(mask_bf, g):
    g1, g2, g3 = _split3(g)
    return _dot(mask_bf, g1) + (_dot(mask_bf, g2) + _dot(mask_bf, g3))


def _mask_dot_l(g, mask_bf):
    g1, g2, g3 = _split3(g)
    return _dot(g1, mask_bf) + (_dot(g2, mask_bf) + _dot(g3, mask_bf))


def _sigmoid(x):
    return 1.0 / (1.0 + jnp.exp(-x))


def _silu(x):
    return x * _sigmoid(x)


def _softplus(x):
    return jnp.maximum(x, 0.0) + jnp.log1p(jnp.exp(-jnp.abs(x)))


def _mod_group(row0):
    return jnp.where(row0 < RC, 0, 1 + (row0 - RC) // DEC_SEQ)


MODS_TN = 1536


def _mods_body(c_ref, w_ref, b_ref, o_ref):
    s = _silu(c_ref[...])
    o_ref[0] = _bdot(s, w_ref[0]) + b_ref[0]


def _mods(cond8, w_mod, b_mod):
    nj = 6 * D // MODS_TN
    return pl.pallas_call(
        _mods_body,
        grid=(DEPTH, nj),
        in_specs=[pl.BlockSpec((8, D), lambda l, j: (0, 0)),
                  pl.BlockSpec((1, D, MODS_TN), lambda l, j: (l, 0, j)),
                  pl.BlockSpec((1, 1, MODS_TN), lambda l, j: (l, 0, j))],
        out_specs=pl.BlockSpec((1, 8, MODS_TN), lambda l, j: (l, 0, j)),
        out_shape=jax.ShapeDtypeStruct((DEPTH, 8, 6 * D), F32),
        compiler_params=_cp(2),
        name="mods",
    )(cond8, w_mod, b_mod.reshape(DEPTH, 1, 6 * D))


IN_TM, IN_TN = 1024, 1280


def _modnorm(x, g, shift, scale):
    ms = jnp.mean(x * x, axis=-1, keepdims=True)
    y = x * lax.rsqrt(ms + EPS) * g
    return y * (1.0 + scale) + shift


def _inproj_body(x_ref, m_ref, g_ref, w_ref, o_ref, h_scr):
    @pl.when(pl.program_id(1) == 0)
    def _():
        h = _modnorm(x_ref[...], g_ref[...], m_ref[0, 0:1, :], m_ref[0, 1:2, :])
        h_scr[...] = h.astype(BF16)

    o_ref[...] = _dot(h_scr[...], w_ref[...])


def _inproj(x, mods_l, g, w_bf):
    return pl.pallas_call(
        _inproj_body,
        grid=(R // IN_TM, NP // IN_TN),
        in_specs=[pl.BlockSpec((IN_TM, D), lambda i, j: (i, 0)),
                  pl.BlockSpec((1, 6, D), lambda i, j: (_mod_group(i * IN_TM), 0, 0)),
                  pl.BlockSpec((1, D), lambda i, j: (0, 0)),
                  pl.BlockSpec((D, IN_TN), lambda i, j: (0, j))],
        out_specs=pl.BlockSpec((IN_TM, IN_TN), lambda i, j: (i, j)),
        out_shape=jax.ShapeDtypeStruct((R, NP), F32),
        scratch_shapes=[pltpu.VMEM((IN_TM, D), BF16)],
        compiler_params=_cp(2),
        name="inproj",
    )(x, mods_l, g, w_bf)


PREP_TM = 256
PREP_NC = PREP_TM // DN_C
PREP_LOCKSTEP = 4


def _prep_body(x_ref, xp_ref, xn_ref, sm_ref, smt_ref, cw_ref, al_ref, dt_ref, alt_ref, dtt_ref,
               u_o, w_o, qg_o, kd_o, ai_o, bg_o, qkv_o, gr_o):
    i = pl.program_id(0)
    n_ctx = RC // PREP_TM
    per_seq = DEC_SEQ // PREP_TM
    is_ctx = i < n_ctx
    j = (i - n_ctx) % per_seq
    first = jnp.logical_or(is_ctx, j == 0)
    last = jnp.logical_or(is_ctx, j == per_seq - 1)

    x = x_ref[...]
    prev_row = jnp.where(first, 0.0, xp_ref[7:8, :])
    next_row = jnp.where(last, 0.0, xn_ref[0:1, :])
    row = lax.broadcasted_iota(jnp.int32, (PREP_TM, 1), 0)
    xm = jnp.where(row == 0, prev_row, pltpu.roll(x, 1, axis=0))
    xq = jnp.where(row == PREP_TM - 1, next_row, pltpu.roll(x, PREP_TM - 1, axis=0))
    w = cw_ref[...]
    y = _silu(w[0:1] * xm + w[1:2] * x + w[2:3] * xq)

    for h in range(DN_H):
        qh = y[:, h * DN_DK:(h + 1) * DN_DK]
        qn = qh * lax.rsqrt(jnp.sum(qh * qh, axis=-1, keepdims=True) + EPS) * (DN_DK ** -0.5)
        qkv_o[:, h * DN_DK:(h + 1) * DN_DK] = qn
        kh = y[:, 512 + h * DN_DK:512 + (h + 1) * DN_DK]
        kn = kh * lax.rsqrt(jnp.sum(kh * kh, axis=-1, keepdims=True) + EPS)
        qkv_o[:, 512 + h * DN_DK:512 + (h + 1) * DN_DK] = kn
    qkv_o[:, 1024:1536] = y[:, 1024:1536]

    r = lax.broadcasted_iota(jnp.int32, (PREP_TM, PREP_TM), 0)
    c = lax.broadcasted_iota(jnp.int32, (PREP_TM, PREP_TM), 1)
    same = (r // DN_C) == (c // DN_C)
    low = jnp.where(jnp.logical_and(same, r >= c), 1.0, 0.0).astype(BF16)
    upp = jnp.where(jnp.logical_and(same, r <= c), 1.0, 0.0).astype(BF16)

    sm = sm_ref[...]
    beta = _sigmoid(sm)
    g = -jnp.exp(al_ref[...]) * _softplus(sm + dt_ref[...])
    gc_f = _mask_dot_r(low, g)
    gc_b = _mask_dot_r(upp, g)
    lane = lax.broadcasted_iota(jnp.int32, (1, 128), 1)
    bg_o[...] = jnp.where(lane < 8, beta, jnp.where(lane < 12, gc_f, jnp.where(lane < 16, gc_b, 0.0)))

    gt = -jnp.exp(alt_ref[...]) * _softplus(smt_ref[...] + dtt_ref[...])
    gct_f = _mask_dot_l(gt, upp)
    gct_b = _mask_dot_l(gt, low)
    sub = lax.broadcasted_iota(jnp.int32, (16, 1), 0)
    gct = jnp.where(sub < 12, gct_f, gct_b)
    for k in range(PREP_NC):
        gr_o[k] = gct[8:16, k * DN_C:(k + 1) * DN_C]

    rr = lax.broadcasted_iota(jnp.int32, (DN_C, DN_C), 0)
    cc = lax.broadcasted_iota(jnp.int32, (DN_C, DN_C), 1)

    def chunk_group(gi, carry):
        prob = [(cj, d, h) for cj in range(PREP_LOCKSTEP) for d in range(2) for h in range(DN_H)]
        ci = [gi * PREP_LOCKSTEP + cj for cj in range(PREP_LOCKSTEP)]
        rows_c = [pl.ds(pl.multiple_of(c_ * DN_C, DN_C), DN_C) for c_ in ci]
        bgc = [bg_o[r_, :] for r_ in rows_c]
        grow_all = [gr_o[c_] for c_ in ci]
        rows = [rows_c[cj] for cj, d, h in prob]
        q = [qkv_o[rows_c[cj], h * DN_DK:(h + 1) * DN_DK] for cj, d, h in prob]
        k = [qkv_o[rows_c[cj], 512 + h * DN_DK:512 + (h + 1) * DN_DK] for cj, d, h in prob]
        v = [qkv_o[rows_c[cj], 1024 + h * DN_DV:1024 + (h + 1) * DN_DV] for cj, d, h in prob]
        beta = [bgc[cj][:, d * DN_H + h:d * DN_H + h + 1] for cj, d, h in prob]
        gcol = [bgc[cj][:, 8 + d * DN_H + h:9 + d * DN_H + h] for cj, d, h in prob]
        grow = [grow_all[cj][d * DN_H + h:d * DN_H + h + 1, :] for cj, d, h in prob]
        incl = [(rr >= cc) if d == 0 else (rr <= cc) for cj, d, h in prob]
        strict = [(rr > cc) if d == 0 else (rr < cc) for cj, d, h in prob]
        n = len(prob)
        decay = [jnp.exp(jnp.where(incl[i], gcol[i] - grow[i], -1e30)) for i in range(n)]
        kb = [k[i] * beta[i] for i in range(n)]
        kk = [_bdot_nt(kb[i], k[i]) for i in range(n)]
        qk = [_bdot_nt(q[i], k[i]) for i in range(n)]
        a = [jnp.where(strict[i], kk[i] * decay[i], 0.0) for i in range(n)]
        blk = lambda b: (rr // b) == (cc // b)
        eye = jnp.where(rr == cc, 1.0, 0.0)
        p = [jnp.where(blk(8), a[i], 0.0) for i in range(n)]
        t = [eye - p[i] for i in range(n)]
        for _ in range(2):
            p = [_bdot(p[i], p[i]) for i in range(n)]
            t = [t[i] + _bdot(t[i], p[i]) for i in range(n)]
        for b in (16, 32, 64):
            m = jnp.logical_and(blk(b), jnp.logical_not(blk(b // 2)))
            tl = [_bdot(t[i], jnp.where(m, a[i], 0.0)) for i in range(n)]
            t = [t[i] - _bdot(tl[i], t[i]) for i in range(n)]
        egc = [jnp.exp(gcol[i]) for i in range(n)]
        glast = [gcol[i][DN_C - 1:DN_C, :] if prob[i][1] == 0 else gcol[i][0:1, :] for i in range(n)]
        uw = [_bdot(t[i], jnp.concatenate([v[i] * beta[i], kb[i] * egc[i]], axis=1)) for i in range(n)]
        for i, (cj, d, h) in enumerate(prob):
            u_o[d, rows[i], h * DN_DV:(h + 1) * DN_DV] = uw[i][:, 0:DN_DV]
            w_o[d, rows[i], h * DN_DK:(h + 1) * DN_DK] = uw[i][:, DN_DV:DN_DV + DN_DK].astype(BF16)
            qg_o[d, rows[i], h * DN_DK:(h + 1) * DN_DK] = (q[i] * egc[i]).astype(BF16)
            kd_o[d, rows[i], h * DN_DK:(h + 1) * DN_DK] = (k[i] * jnp.exp(glast[i] - gcol[i])).astype(BF16)
            ai_o[d, rows[i], h * DN_C:(h + 1) * DN_C] = (qk[i] * decay[i]).astype(BF16)
        return carry

    lax.fori_loop(0, PREP_NC // PREP_LOCKSTEP, chunk_group, 0)


def _dn_prep(proj, sm_t, conv_w, al128, dt128, al_t, dt_t):
    nb8 = R // 8
    qb = C_QKV // 1536
    wide = lambda dt: jax.ShapeDtypeStruct((2, R, DN_H * DN_DK), dt)
    wide_spec = pl.BlockSpec((2, PREP_TM, DN_H * DN_DK), lambda i: (0, i, 0))
    return pl.pallas_call(
        _prep_body,
        grid=(R // PREP_TM,),
        in_specs=[pl.BlockSpec((PREP_TM, 1536), lambda i: (i, qb)),
                  pl.BlockSpec((8, 1536), lambda i: (jnp.maximum(i * (PREP_TM // 8) - 1, 0), qb)),
                  pl.BlockSpec((8, 1536), lambda i: (jnp.minimum((i + 1) * (PREP_TM // 8), nb8 - 1), qb)),
                  pl.BlockSpec((PREP_TM, 128), lambda i: (i, C_SM // 128)),
                  pl.BlockSpec((16, PREP_TM), lambda i: (0, i)),
                  pl.BlockSpec((3, 1536), lambda i: (0, 0)),
                  pl.BlockSpec((1, 128), lambda i: (0, 0)),
                  pl.BlockSpec((1, 128), lambda i: (0, 0)),
                  pl.BlockSpec((16, 1), lambda i: (0, 0)),
                  pl.BlockSpec((16, 1), lambda i: (0, 0))],
        out_specs=[wide_spec, wide_spec, wide_spec, wide_spec,
                   pl.BlockSpec((2, PREP_TM, DN_H * DN_C), lambda i: (0, i, 0)),
                   pl.BlockSpec((PREP_TM, 128), lambda i: (i, 0))],
        out_shape=[wide(F32), wide(BF16), wide(BF16), wide(BF16),
                   jax.ShapeDtypeStruct((2, R, DN_H * DN_C), BF16),
                   jax.ShapeDtypeStruct((R, 128), F32)],
        scratch_shapes=[pltpu.VMEM((PREP_TM, 1536), F32), pltpu.VMEM((PREP_NC, 8, DN_C), F32)],
        compiler_params=_cp(1),
        name="dn_prep",
    )(proj, proj, proj, proj, sm_t, conv_w, al128, dt128, al_t, dt_t)


SCAN_CTX = RC // PREP_TM
SCAN_PER = DEC_SEQ // PREP_TM
SCAN_NSEQ = BATCH + DEC_BATCH


def _scan_bwd_block(i):
    j = i - SCAN_CTX
    return jnp.where(i < SCAN_CTX, i, SCAN_CTX + (j // SCAN_PER) * SCAN_PER + (SCAN_PER - 1 - j % SCAN_PER))


def _scan_body(uf, wf, qf, kf, af, bf, ub, wb, qb, kb_, ab, bb, s0_ref, of_ref, ob_ref, so_ref, s_scr):
    i = pl.program_id(0)
    is_ctx = i < SCAN_CTX
    j = (i - SCAN_CTX) % SCAN_PER
    first = jnp.logical_or(is_ctx, j == 0)
    last = jnp.logical_or(is_ctx, j == SCAN_PER - 1)

    @pl.when(first)
    def _():
        for d in range(2):
            for h in range(DN_H):
                s_scr[d, h] = jnp.where(is_ctx, 0.0, s0_ref[0, d, h])

    dirs = ((uf, wf, qf, kf, af, bf, of_ref), (ub, wb, qb, kb_, ab, bb, ob_ref))

    prob = [(d, h) for d in range(2) for h in range(DN_H)]
    tn = (((0,), (0,)), ((), ()))

    def step(n, carry):
        rows = [pl.ds(pl.multiple_of((n if d == 0 else PREP_NC - 1 - n) * DN_C, DN_C), DN_C) for d in range(2)]
        bgc = [dirs[d][5][rows[d], :] for d in range(2)]
        cols = [slice(h * DN_DK, (h + 1) * DN_DK) for d, h in prob]
        st = [s_scr[d, h] for d, h in prob]
        stb = [x.astype(BF16) for x in st]
        ws = [_dot(dirs[d][1][0, rows[d], cols[i]], stb[i]) for i, (d, h) in enumerate(prob)]
        qs = [_dot(dirs[d][2][0, rows[d], cols[i]], stb[i]) for i, (d, h) in enumerate(prob)]
        vb = [(dirs[d][0][0, rows[d], cols[i]] - ws[i]).astype(BF16) for i, (d, h) in enumerate(prob)]
        av = [_dot(dirs[d][4][0, rows[d], h * DN_C:(h + 1) * DN_C], vb[i]) for i, (d, h) in enumerate(prob)]
        kv = [lax.dot_general(dirs[d][3][0, rows[d], cols[i]], vb[i], tn, preferred_element_type=F32)
              for i, (d, h) in enumerate(prob)]
        for i, (d, h) in enumerate(prob):
            gcol = bgc[d][:, 8 + d * DN_H + h:9 + d * DN_H + h]
            glast = gcol[DN_C - 1:DN_C, :] if d == 0 else gcol[0:1, :]
            s_scr[d, h] = st[i] * jnp.exp(glast) + kv[i]
            dirs[d][6][rows[d], cols[i]] = qs[i] + av[i]
        return carry

    lax.fori_loop(0, PREP_NC, step, 0)

    @pl.when(last)
    def _():
        for d in range(2):
            for h in range(DN_H):
                so_ref[0, d, h] = s_scr[d, h]


def _dn_scan(u, w, qg, kd, ai, bg, s0):
    fwd3 = lambda i: (0, i, 0)
    bwd3 = lambda i: (1, _scan_bwd_block(i), 0)
    wide = lambda m: pl.BlockSpec((1, PREP_TM, DN_H * DN_DK), m)
    narrow = lambda m: pl.BlockSpec((1, PREP_TM, DN_H * DN_C), m)
    state_spec = lambda m: pl.BlockSpec((1, 2, DN_H, DN_DK, DN_DV), m)
    seq_of = lambda i: jnp.where(i < SCAN_CTX, i, SCAN_CTX + (i - SCAN_CTX) // SCAN_PER)
    return pl.pallas_call(
        _scan_body,
        grid=(R // PREP_TM,),
        in_specs=[wide(fwd3), wide(fwd3), wide(fwd3), wide(fwd3), narrow(fwd3),
                  pl.BlockSpec((PREP_TM, 128), lambda i: (i, 0)),
                  wide(bwd3), wide(bwd3), wide(bwd3), wide(bwd3), narrow(bwd3),
                  pl.BlockSpec((PREP_TM, 128), lambda i: (_scan_bwd_block(i), 0)),
                  state_spec(lambda i: (jnp.maximum(i - SCAN_CTX, 0) // SCAN_PER, 0, 0, 0, 0))],
        out_specs=[pl.BlockSpec((PREP_TM, DN_H * DN_DV), lambda i: (i, 0)),
                   pl.BlockSpec((PREP_TM, DN_H * DN_DV), lambda i: (_scan_bwd_block(i), 0)),
                   state_spec(lambda i: (seq_of(i), 0, 0, 0, 0))],
        out_shape=[jax.ShapeDtypeStruct((R, DN_H * DN_DV), F32),
                   jax.ShapeDtypeStruct((R, DN_H * DN_DV), F32),
                   jax.ShapeDtypeStruct((SCAN_NSEQ, 2, DN_H, DN_DK, DN_DV), F32)],
        scratch_shapes=[pltpu.VMEM((2, DN_H, DN_DK, DN_DV), F32)],
        compiler_params=_cp(1),
        name="dn_scan",
    )(u, w, qg, kd, ai, bg, u, w, qg, kd, ai, bg, s0)


def _group_mean_matrix(width, group):
    r = lax.broadcasted_iota(jnp.int32, (width, width), 0)
    c = lax.broadcasted_iota(jnp.int32, (width, width), 1)
    return jnp.where((r // group) == (c // group), 1.0 / group, 0.0).astype(BF16)


def _group_rmsnorm(x, w, group):
    m = _group_mean_matrix(x.shape[-1], group)
    hi, lo = _split2(x * x)
    ms = _dot(hi, m) + _dot(lo, m)
    return x * lax.rsqrt(ms + EPS) * w


def _rope(x, cos, sin_signed, group):
    width = x.shape[-1]
    half = group // 2
    lane = lax.broadcasted_iota(jnp.int32, (1, width), 1)
    swapped = jnp.where((lane % group) < half,
                        pltpu.roll(x, width - half, axis=1), pltpu.roll(x, half, axis=1))
    return x * cos + swapped * sin_signed


def _softmax_pv(s, v_bf):
    m = jnp.max(s, axis=-1, keepdims=True)
    p = jnp.exp(s - m)
    l = jnp.sum(p, axis=-1, keepdims=True)
    return _dot(p.astype(BF16), v_bf) / l


GQA_G = GQA_H // GQA_KV


def _gqa_body(T, TQ, latent, *refs):
    it = iter(refs)
    q_ref, kv_ref, qw_ref, kw_ref = next(it), next(it), next(it), next(it)
    if latent:
        cq_ref, sq_ref, ck_ref, sk_ref, kc_ref, vc_ref = (next(it) for _ in range(6))
    o_ref = next(it)
    kn_ref = None if latent else next(it)
    k_scr, v_scr = next(it), next(it)

    @pl.when(pl.program_id(1) == 0)
    def _():
        kv = kv_ref[...]
        k = _group_rmsnorm(kv[:, 0:128], kw_ref[...], GQA_HD)
        v = kv[:, 128:256]
        if latent:
            k = _rope(k, ck_ref[...], sk_ref[...], GQA_HD)
        else:
            kn_ref[...] = k
        for g in range(GQA_KV):
            k_scr[g, 0:T, :] = k[:, g * GQA_HD:(g + 1) * GQA_HD].astype(BF16)
            v_scr[g, 0:T, :] = v[:, g * GQA_HD:(g + 1) * GQA_HD].astype(BF16)
            if latent:
                k_scr[g, T:T + PAST, :] = kc_ref[0, :, g * GQA_HD:(g + 1) * GQA_HD].astype(BF16)
                v_scr[g, T:T + PAST, :] = vc_ref[0, :, g * GQA_HD:(g + 1) * GQA_HD].astype(BF16)

    q = _group_rmsnorm(q_ref[...], qw_ref[...], GQA_HD)
    if latent:
        q = _rope(q, cq_ref[...], sq_ref[...], GQA_HD)
    q = q * (GQA_HD ** -0.5)
    for g in range(GQA_KV):
        qg = jnp.concatenate(
            [q[:, (g * GQA_G + j) * GQA_HD:(g * GQA_G + j + 1) * GQA_HD] for j in range(GQA_G)], axis=0)
        s = _bdot_nt(qg, k_scr[g])
        o = _softmax_pv(s, v_scr[g])
        for j in range(GQA_G):
            hh = g * GQA_G + j
            o_ref[:, hh * GQA_HD:(hh + 1) * GQA_HD] = o[j * TQ:(j + 1) * TQ, :]


def _gqa(proj, qw, kw, T, TQ, n_seq, row_off, rope=None, cache=None):
    latent = rope is not None
    tk = T + (PAST if latent else 0)
    nq = T // TQ
    qo = row_off // TQ
    so = row_off // T
    in_specs = [pl.BlockSpec((TQ, 512), lambda i, j: (qo + i * nq + j, C_GQ // 512)),
                pl.BlockSpec((T, 256), lambda i, j: (so + i, C_GKV // 256)),
                pl.BlockSpec((1, 512), lambda i, j: (0, 0)),
                pl.BlockSpec((1, 128), lambda i, j: (0, 0))]
    args = [proj, proj, qw, kw]
    if latent:
        cq, sq, ck, sk = rope
        kc, vc = cache
        in_specs += [pl.BlockSpec((TQ, 512), lambda i, j: (j, 0)),
                     pl.BlockSpec((TQ, 512), lambda i, j: (j, 0)),
                     pl.BlockSpec((T, 128), lambda i, j: (0, 0)),
                     pl.BlockSpec((T, 128), lambda i, j: (0, 0)),
                     pl.BlockSpec((1, PAST, 128), lambda i, j: (i, 0, 0)),
                     pl.BlockSpec((1, PAST, 128), lambda i, j: (i, 0, 0))]
        args += [cq, sq, ck, sk, kc, vc]
    out_specs = [pl.BlockSpec((TQ, 512), lambda i, j: (i * nq + j, 0))]
    out_shape = [jax.ShapeDtypeStruct((n_seq * T, 512), F32)]
    if not latent:
        out_specs.append(pl.BlockSpec((T, 128), lambda i, j: (i, 0)))
        out_shape.append(jax.ShapeDtypeStruct((n_seq * T, 128), F32))
    return pl.pallas_call(
        functools.partial(_gqa_body, T, TQ, latent),
        grid=(n_seq, nq),
        in_specs=in_specs,
        out_specs=out_specs,
        out_shape=out_shape,
        scratch_shapes=[pltpu.VMEM((GQA_KV, tk, GQA_HD), BF16), pltpu.VMEM((GQA_KV, tk, GQA_HD), BF16)],
        compiler_params=_cp(2),
        name="gqa_%d" % T,
    )(*args)


MLA_DQ = MLA_NOPE + MLA_ROPE
MLA_KV_ROWS = 512


def _mla_body(T, TQ, latent, *refs):
    it = iter(refs)
    cq_ref, ckv_ref, sm_ref, wq_ref, wkv_ref, qw_ref, kvw_ref = (next(it) for _ in range(7))
    if latent:
        cosq_ref, sinq_ref, cosk_ref, sink_ref, cc_ref, kc_ref = (next(it) for _ in range(6))
    o_ref = next(it)
    cn_ref = None if latent else next(it)
    k_scr, v_scr = next(it), next(it)
    tk = T + (PAST if latent else 0)

    @pl.when(pl.program_id(1) == 0)
    def _():
        x = ckv_ref[...]
        ms = jnp.mean(x * x, axis=-1, keepdims=True)
        ckv = x * lax.rsqrt(ms + EPS) * kvw_ref[...]
        sm = sm_ref[...]
        if latent:
            sm = _rope_small(sm, cosk_ref[...], sink_ref[...])
        else:
            cn_ref[...] = ckv
        kr = sm[:, 16:16 + MLA_ROPE]

        def put(rows0, ckv_rows, kr_rows):
            n = ckv_rows.shape[0]
            kv = _bdot(ckv_rows, wkv_ref[...])
            krb = kr_rows.astype(BF16)
            for h in range(MLA_H):
                k_scr[h, rows0:rows0 + n, 0:MLA_NOPE] = kv[:, h * MLA_NOPE:(h + 1) * MLA_NOPE].astype(BF16)
                k_scr[h, rows0:rows0 + n, MLA_NOPE:MLA_DQ] = krb
                v_scr[h, rows0:rows0 + n, :] = kv[:, 512 + h * MLA_V:512 + (h + 1) * MLA_V].astype(BF16)

        for r0 in range(0, T, MLA_KV_ROWS):
            r1 = min(r0 + MLA_KV_ROWS, T)
            put(r0, ckv[r0:r1], kr[r0:r1])
        if latent:
            put(T, cc_ref[0], kc_ref[0])

    x = cq_ref[...]
    ms = jnp.mean(x * x, axis=-1, keepdims=True)
    cq = x * lax.rsqrt(ms + EPS) * qw_ref[...]
    qf = _bdot(cq, wq_ref[...])
    qn = qf[:, 0:512]
    qr = qf[:, 512:768]
    if latent:
        qr = _rope(qr, cosq_ref[...], sinq_ref[...], MLA_ROPE)
    scale = MLA_DQ ** -0.5
    for h in range(MLA_H):
        qh = jnp.concatenate([qn[:, h * MLA_NOPE:(h + 1) * MLA_NOPE],
                              qr[:, h * MLA_ROPE:(h + 1) * MLA_ROPE]], axis=1) * scale
        s = _bdot_nt(qh, k_scr[h])
        o_ref[:, h * MLA_V:(h + 1) * MLA_V] = _softmax_pv(s, v_scr[h])


def _rope_small(sm, cos, sin_signed):
    lane = lax.broadcasted_iota(jnp.int32, (1, 128), 1)
    half = MLA_ROPE // 2
    swapped = jnp.where(lane < 16 + half, pltpu.roll(sm, 128 - half, axis=1), pltpu.roll(sm, half, axis=1))
    return sm * cos + swapped * sin_signed


def _mla(proj, wq, wkv, qw, kvw, T, TQ, n_seq, row_off, rope=None, cache=None):
    latent = rope is not None
    tk = T + (PAST if latent else 0)
    nq = T // TQ
    qo = row_off // TQ
    so = row_off // T
    in_specs = [pl.BlockSpec((TQ, 256), lambda i, j: (qo + i * nq + j, C_MCQ // 256)),
                pl.BlockSpec((T, 128), lambda i, j: (so + i, C_MCKV // 128)),
                pl.BlockSpec((T, 128), lambda i, j: (so + i, C_SM // 128)),
                pl.BlockSpec((MLA_QL, 768), lambda i, j: (0, 0)),
                pl.BlockSpec((MLA_KVL, 1024), lambda i, j: (0, 0)),
                pl.BlockSpec((1, 256), lambda i, j: (0, 0)),
                pl.BlockSpec((1, 128), lambda i, j: (0, 0))]
    args = [proj, proj, proj, wq, wkv, qw, kvw]
    if latent:
        cosq, sinq, cosk, sink = rope
        cc, kc = cache
        in_specs += [pl.BlockSpec((TQ, 256), lambda i, j: (j, 0)),
                     pl.BlockSpec((TQ, 256), lambda i, j: (j, 0)),
                     pl.BlockSpec((T, 128), lambda i, j: (0, 0)),
                     pl.BlockSpec((T, 128), lambda i, j: (0, 0)),
                     pl.BlockSpec((1, PAST, 128), lambda i, j: (i, 0, 0)),
                     pl.BlockSpec((1, PAST, MLA_ROPE), lambda i, j: (i, 0, 0))]
        args += [cosq, sinq, cosk, sink, cc, kc]
    out_specs = [pl.BlockSpec((TQ, 512), lambda i, j: (i * nq + j, 0))]
    out_shape = [jax.ShapeDtypeStruct((n_seq * T, 512), F32)]
    if not latent:
        out_specs.append(pl.BlockSpec((T, 128), lambda i, j: (i, 0)))
        out_shape.append(jax.ShapeDtypeStruct((n_seq * T, 128), F32))
    return pl.pallas_call(
        functools.partial(_mla_body, T, TQ, latent),
        grid=(n_seq, nq),
        in_specs=in_specs,
        out_specs=out_specs,
        out_shape=out_shape,
        scratch_shapes=[pltpu.VMEM((MLA_H, tk, MLA_DQ), BF16), pltpu.VMEM((MLA_H, tk, MLA_V), BF16)],
        compiler_params=_cp(2),
        name="mla_%d" % T,
    )(*args)


MG_TM = 512


def _merge_body(x_ref, m_ref, of_ref, ob_ref, dg_ref, ogc_ref, ogl_ref, omc_ref, oml_ref, bg_ref, ng_ref,
                wb_ref, wo_ref, o_ref):
    is_ctx = pl.program_id(0) < RC // MG_TM
    og = jnp.where(is_ctx, ogc_ref[...], ogl_ref[...])
    om = jnp.where(is_ctx, omc_ref[...], oml_ref[...])
    odn = of_ref[...] + ob_ref[...]
    dg = dg_ref[...]
    ng = ng_ref[...]
    parts = []
    for h in range(DN_H):
        oh = odn[:, h * DN_DV:(h + 1) * DN_DV]
        ms = jnp.mean(oh * oh, axis=-1, keepdims=True)
        parts.append(oh * lax.rsqrt(ms + EPS) * ng * _silu(dg[:, h * DN_DV:(h + 1) * DN_DV]))
    br0 = jnp.concatenate(parts, axis=1)
    merged = _sigmoid(bg_ref[:, 0:D]) * _bdot(br0, wb_ref[0])
    merged = merged + _sigmoid(bg_ref[:, D:2 * D]) * _bdot(og, wb_ref[1])
    merged = merged + _sigmoid(bg_ref[:, 2 * D:3 * D]) * _bdot(om, wb_ref[2])
    out = _bdot(merged, wo_ref[...])
    o_ref[...] = x_ref[...] + m_ref[0, 2:3, :] * out


def _merge(x, mods_l, o_f, o_b, proj, og_c, og_l, om_c, om_l, ng, wb, wo):
    row = lambda i: (i, 0)
    n_ctx = RC // MG_TM
    ctx_row = lambda i: (jnp.minimum(i, n_ctx - 1), 0)
    lat_row = lambda i: (jnp.maximum(i - n_ctx, 0), 0)
    return pl.pallas_call(
        _merge_body,
        grid=(R // MG_TM,),
        in_specs=[pl.BlockSpec((MG_TM, D), row),
                  pl.BlockSpec((1, 6, D), lambda i: (_mod_group(i * MG_TM), 0, 0)),
                  pl.BlockSpec((MG_TM, 512), row),
                  pl.BlockSpec((MG_TM, 512), row),
                  pl.BlockSpec((MG_TM, 512), lambda i: (i, C_DG // 512)),
                  pl.BlockSpec((MG_TM, 512), ctx_row),
                  pl.BlockSpec((MG_TM, 512), lat_row),
                  pl.BlockSpec((MG_TM, 512), ctx_row),
                  pl.BlockSpec((MG_TM, 512), lat_row),
                  pl.BlockSpec((MG_TM, 3 * D), lambda i: (i, C_BG // (3 * D))),
                  pl.BlockSpec((1, DN_DV), lambda i: (0, 0)),
                  pl.BlockSpec((3, 512, D), lambda i: (0, 0, 0)),
                  pl.BlockSpec((D, D), lambda i: (0, 0))],
        out_specs=pl.BlockSpec((MG_TM, D), row),
        out_shape=jax.ShapeDtypeStruct((R, D), F32),
        compiler_params=_cp(1),
        name="merge",
    )(x, mods_l, o_f, o_b, proj, og_c, og_l, om_c, om_l, proj, ng, wb, wo)


FF_TM = 512
FF_CHUNKS = ((0, 1536), (1536, FF_DENSE))


def _ffn_body(x_ref, m_ref, g_ref, w1_ref, w3_ref, w2_ref, o_ref):
    x = x_ref[...]
    h = _modnorm(x, g_ref[...], m_ref[0, 3:4, :], m_ref[0, 4:5, :]).astype(BF16)
    y = None
    for c0, c1 in FF_CHUNKS:
        a = _silu(_dot(h, w1_ref[:, c0:c1])) * _dot(h, w3_ref[:, c0:c1])
        yc = _dot(a.astype(BF16), w2_ref[c0:c1, :])
        y = yc if y is None else y + yc
    o_ref[...] = x + m_ref[0, 5:6, :] * y


def _ffn(x, mods_l, g, w1, w3, w2):
    once = pl.Buffered(1)
    return pl.pallas_call(
        _ffn_body,
        grid=(R // FF_TM,),
        in_specs=[pl.BlockSpec((FF_TM, D), lambda i: (i, 0)),
                  pl.BlockSpec((1, 6, D), lambda i: (_mod_group(i * FF_TM), 0, 0)),
                  pl.BlockSpec((1, D), lambda i: (0, 0)),
                  pl.BlockSpec((D, FF_DENSE), lambda i: (0, 0), pipeline_mode=once),
                  pl.BlockSpec((D, FF_DENSE), lambda i: (0, 0), pipeline_mode=once),
                  pl.BlockSpec((FF_DENSE, D), lambda i: (0, 0), pipeline_mode=once)],
        out_specs=pl.BlockSpec((FF_TM, D), lambda i: (i, 0)),
        out_shape=jax.ShapeDtypeStruct((R, D), F32),
        compiler_params=_cp(1),
        name="ffn",
    )(x, mods_l, g, w1, w3, w2)


RT_TM = 512
MOE_TG, MOE_TF = 512, 1792
MOE_NT = 2 * R // MOE_TG + N_EXP
MOE_ROWS = MOE_NT * MOE_TG
CB_TM = 256


def _router_body(x_ref, m_ref, g_ref, rw_ref, rb_ref, hn_ref, route_ref, cnt_ref, base_scr):
    @pl.when(pl.program_id(0) == 0)
    def _():
        base_scr[...] = jnp.zeros_like(base_scr)

    h = _modnorm(x_ref[...], g_ref[...], m_ref[0, 3:4, :], m_ref[0, 4:5, :])
    hn_ref[...] = h
    logits = _dot3(h, rw_ref[...]) + rb_ref[...]
    lane = lax.broadcasted_iota(jnp.int32, logits.shape, 1)
    m1 = jnp.max(logits, axis=-1, keepdims=True)
    i1 = jnp.min(jnp.where(logits == m1, lane, 128), axis=-1, keepdims=True)
    sel1 = lane == i1
    rest = jnp.where(sel1, -jnp.inf, logits)
    m2 = jnp.max(rest, axis=-1, keepdims=True)
    i2 = jnp.min(jnp.where(rest == m2, lane, 128), axis=-1, keepdims=True)
    sel2 = lane == i2
    e2 = jnp.exp(m2 - m1)
    p1 = 1.0 / (1.0 + e2)
    p2 = e2 / (1.0 + e2)

    cnt = jnp.where(jnp.logical_or(sel1, sel2), 1.0, 0.0)
    r = lax.broadcasted_iota(jnp.int32, (RT_TM, RT_TM), 0)
    c = lax.broadcasted_iota(jnp.int32, (RT_TM, RT_TM), 1)
    before = jnp.where(r > c, 1.0, 0.0).astype(BF16)
    seen = base_scr[...] + _dot(before, cnt.astype(BF16))
    rank1 = jnp.sum(jnp.where(sel1, seen, 0.0), axis=-1, keepdims=True)
    rank2 = jnp.sum(jnp.where(sel2, seen, 0.0), axis=-1, keepdims=True)
    vals = (i1.astype(F32), i2.astype(F32), rank1, rank2, p1, p2)
    route = jnp.zeros(logits.shape, F32)
    for k, val in enumerate(vals):
        route = jnp.where(lane == k, val, route)
    route_ref[...] = route
    base_scr[...] += jnp.sum(cnt, axis=0, keepdims=True)
    cnt_ref[...] = base_scr[...]


def _router(x, mods_l, g, rw128, rb128):
    return pl.pallas_call(
        _router_body,
        grid=(R // RT_TM,),
        in_specs=[pl.BlockSpec((RT_TM, D), lambda i: (i, 0)),
                  pl.BlockSpec((1, 6, D), lambda i: (_mod_group(i * RT_TM), 0, 0)),
                  pl.BlockSpec((1, D), lambda i: (0, 0)),
                  pl.BlockSpec((D, 128), lambda i: (0, 0)),
                  pl.BlockSpec((1, 128), lambda i: (0, 0))],
        out_specs=[pl.BlockSpec((RT_TM, D), lambda i: (i, 0)),
                   pl.BlockSpec((RT_TM, 128), lambda i: (i, 0)),
                   pl.BlockSpec((1, 128), lambda i: (0, 0))],
        out_shape=[jax.ShapeDtypeStruct((R, D), F32),
                   jax.ShapeDtypeStruct((R, 128), F32),
                   jax.ShapeDtypeStruct((1, 128), F32)],
        scratch_shapes=[pltpu.VMEM((1, 128), F32)],
        compiler_params=_cp(1),
        name="moe_router",
    )(x, mods_l, g, rw128, rb128)


def _rows_copy(src_hbm, dst, sem, n):
    return pltpu.make_async_copy(src_hbm.at[pl.ds(0, n)], dst, sem)


def _start_row_gather(idx_ref, n, src_hbm, dst, sem):
    def issue(i, carry):
        pltpu.make_async_copy(src_hbm.at[pl.ds(idx_ref[0, 0, i], 1)], dst.at[pl.ds(i, 1)], sem).start()
        return carry

    lax.fori_loop(0, n, issue, 0, unroll=8)


def _gather_rows(idx_ref, n, src_hbm, dst, sem):
    _start_row_gather(idx_ref, n, src_hbm, dst, sem)
    _rows_copy(src_hbm, dst, sem, n).wait()


def _experts_body(te_ref, tv_ref, src_ref, nsrc_ref, hn_hbm, w1_ref, w3_ref, w2_ref, ys_ref, xg_scr, sems):
    t = pl.program_id(0)
    slot = t % 2
    valid = tv_ref[t] > 0
    nxt = jnp.minimum(t + 1, MOE_NT - 1)
    next_valid = jnp.logical_and(t + 1 < MOE_NT, tv_ref[nxt] > 0)

    @pl.when(jnp.logical_and(t == 0, valid))
    def _():
        _start_row_gather(src_ref, MOE_TG, hn_hbm, xg_scr.at[0], sems.at[0])

    @pl.when(next_valid)
    def _():
        _start_row_gather(nsrc_ref, MOE_TG, hn_hbm, xg_scr.at[1 - slot], sems.at[1 - slot])

    @pl.when(valid)
    def _():
        _rows_copy(hn_hbm, xg_scr.at[slot], sems.at[slot], MOE_TG).wait()
        xb = xg_scr[slot].astype(BF16)
        y = None
        for c in range(FF_EXP // MOE_TF):
            cs = slice(c * MOE_TF, (c + 1) * MOE_TF)
            a = _silu(_dot(xb, w1_ref[0, :, cs])) * _dot(xb, w3_ref[0, :, cs])
            yc = _dot(a.astype(BF16), w2_ref[0, cs, :])
            y = yc if y is None else y + yc
        ys_ref[...] = y

    @pl.when(jnp.logical_not(valid))
    def _():
        ys_ref[...] = jnp.zeros_like(ys_ref)


def _experts(tile_e, tile_v, src, hn, w1, w3, w2):
    once = pl.Buffered(1)
    grid_spec = pltpu.PrefetchScalarGridSpec(
        num_scalar_prefetch=2,
        grid=(MOE_NT,),
        in_specs=[pl.BlockSpec((1, 1, MOE_TG), lambda t, te, tv: (t, 0, 0), memory_space=pltpu.SMEM),
                  pl.BlockSpec((1, 1, MOE_TG), lambda t, te, tv: (jnp.minimum(t + 1, MOE_NT - 1), 0, 0),
                               memory_space=pltpu.SMEM),
                  pl.BlockSpec(memory_space=pl.ANY),
                  pl.BlockSpec((1, D, FF_EXP), lambda t, te, tv: (te[t], 0, 0), pipeline_mode=once),
                  pl.BlockSpec((1, D, FF_EXP), lambda t, te, tv: (te[t], 0, 0), pipeline_mode=once),
                  pl.BlockSpec((1, FF_EXP, D), lambda t, te, tv: (te[t], 0, 0), pipeline_mode=once)],
        out_specs=pl.BlockSpec((MOE_TG, D), lambda t, te, tv: (t, 0)),
        scratch_shapes=[pltpu.VMEM((2, MOE_TG, D), F32), pltpu.SemaphoreType.DMA((2,))],
    )
    return pl.pallas_call(
        _experts_body,
        grid_spec=grid_spec,
        out_shape=jax.ShapeDtypeStruct((MOE_ROWS, D), F32),
        compiler_params=_cp(1),
        name="moe_experts",
    )(tile_e, tile_v, src, src, hn, w1, w3, w2)


def _combine_body(pos_ref, x_ref, m_ref, route_ref, ys_hbm, o_ref, buf, sem):
    _gather_rows(pos_ref, 2 * CB_TM, ys_hbm, buf, sem)
    route = route_ref[...]
    y = route[:, 4:5] * buf[0:CB_TM, :] + route[:, 5:6] * buf[CB_TM:2 * CB_TM, :]
    o_ref[...] = x_ref[...] + m_ref[0, 5:6, :] * y


def _combine(pos, x, mods_l, route, ys):
    return pl.pallas_call(
        _combine_body,
        grid=(R // CB_TM,),
        in_specs=[pl.BlockSpec((1, 1, 2 * CB_TM), lambda i: (i, 0, 0), memory_space=pltpu.SMEM),
                  pl.BlockSpec((CB_TM, D), lambda i: (i, 0)),
                  pl.BlockSpec((1, 6, D), lambda i: (_mod_group(i * CB_TM), 0, 0)),
                  pl.BlockSpec((CB_TM, 128), lambda i: (i, 0)),
                  pl.BlockSpec(memory_space=pl.ANY)],
        out_specs=pl.BlockSpec((CB_TM, D), lambda i: (i, 0)),
        out_shape=jax.ShapeDtypeStruct((R, D), F32),
        scratch_shapes=[pltpu.VMEM((2 * CB_TM, D), F32), pltpu.SemaphoreType.DMA(())],
        compiler_params=_cp(1),
        name="moe_combine",
    )(pos, x, mods_l, route, ys)


def _moe(x, mods_l, g, rw128, rb128, w1, w3, w2):
    hn, route, cnt = _router(x, mods_l, g, rw128, rb128)
    eid = route[:, 0:2].astype(jnp.int32)
    rank = route[:, 2:4].astype(jnp.int32)
    counts = cnt[0, :N_EXP].astype(jnp.int32)
    gsize = (counts + MOE_TG - 1) // MOE_TG * MOE_TG
    gend = jnp.cumsum(gsize)
    pos = (gend - gsize)[eid] + rank
    tile_start = jnp.arange(MOE_NT, dtype=jnp.int32) * MOE_TG
    tile_e = jnp.minimum(jnp.sum(tile_start[:, None] >= gend[None, :], axis=1), N_EXP - 1).astype(jnp.int32)
    tile_v = (tile_start < gend[-1]).astype(jnp.int32)
    tok = jnp.broadcast_to(jnp.arange(R, dtype=jnp.int32)[:, None], (R, 2))
    src = jnp.zeros((MOE_ROWS,), jnp.int32).at[pos.reshape(-1)].set(tok.reshape(-1))
    ys = _experts(tile_e, tile_v, src.reshape(MOE_NT, 1, MOE_TG), hn, w1, w3, w2)
    pos_t = pos.reshape(R // CB_TM, CB_TM, 2).transpose(0, 2, 1).reshape(R // CB_TM, 1, 2 * CB_TM)
    return _combine(pos_t, x, mods_l, route, ys)


FN_TM = 1024


def _final_body(x_ref, g_ref, o_ref):
    x = x_ref[...]
    ms = jnp.mean(x * x, axis=-1, keepdims=True)
    o_ref[...] = x * lax.rsqrt(ms + EPS) * g_ref[...]


def _final_norm(x, g, row_off, n_rows):
    bo = row_off // FN_TM
    return pl.pallas_call(
        _final_body,
        grid=(n_rows // FN_TM,),
        in_specs=[pl.BlockSpec((FN_TM, D), lambda i: (bo + i, 0)), pl.BlockSpec((1, D), lambda i: (0, 0))],
        out_specs=pl.BlockSpec((FN_TM, D), lambda i: (i, 0)),
        out_shape=jax.ShapeDtypeStruct((n_rows, D), F32),
        compiler_params=_cp(1),
        name="final_norm",
    )(x, g)


def _rope_tables(n_tokens, rot_dim):
    t = np.arange(n_tokens)
    row = (t // GRID_W).astype(np.float32)
    col = (t % GRID_W).astype(np.float32)
    n_freq = rot_dim // 4
    inv = (ROPE_THETA ** (-jnp.arange(n_freq, dtype=F32) / n_freq))
    ang = jnp.concatenate([jnp.asarray(row)[:, None] * inv, jnp.asarray(col)[:, None] * inv], axis=-1)
    cos, sin = jnp.cos(ang), jnp.sin(ang)
    return jnp.concatenate([cos, cos], axis=-1), jnp.concatenate([-sin, sin], axis=-1)


def _permute_w_in(w):
    return jnp.concatenate(
        [w[:, 3248:6320], w[:, 0:2048], w[:, 2064:3216], w[:, 2048:2064], w[:, 3216:3248],
         jnp.zeros((D, NP - 6320), w.dtype)], axis=1).astype(BF16)


def kernel(x_prompt, x_sample, c, cache_gqa_k, cache_gqa_v, cache_mla_ckv, cache_mla_krope, state_delta, c_ctx, w_mod, b_mod, norm1_g, norm2_g, w_in, dn_conv_w, dn_a_log, dn_dt_bias, dn_norm_g, gqa_q_norm, gqa_k_norm, mla_q_norm, mla_kv_norm, mla_w_uq, mla_w_ukv, w_branch, w_out, ffd_w1, ffd_w3, ffd_w2, router_w, router_b, moe_w1, moe_w3, moe_w2, final_g):
    x = jnp.concatenate([x_prompt.reshape(RC, D), x_sample.reshape(RL, D)], axis=0)
    cond8 = jnp.concatenate([c_ctx[None, :], c, jnp.zeros((3, D), F32)], axis=0)
    mods = _mods(cond8, w_mod, b_mod).reshape(DEPTH, 8, 6, D)

    cg, sg = _rope_tables(DEC_SEQ, GQA_HD)
    gqa_rope = (jnp.tile(cg, (1, GQA_H)), jnp.tile(sg, (1, GQA_H)),
                jnp.tile(cg, (1, GQA_KV)), jnp.tile(sg, (1, GQA_KV)))
    cm, sm_ = _rope_tables(DEC_SEQ, MLA_ROPE)
    padk = lambda t: jnp.pad(t, ((0, 0), (16, 128 - 16 - MLA_ROPE)))
    mla_rope = (jnp.tile(cm, (1, MLA_H)), jnp.tile(sm_, (1, MLA_H)),
                jnp.pad(cm, ((0, 0), (16, 128 - 16 - MLA_ROPE)), constant_values=1.0), padk(sm_))

    new_k, new_v, new_ckv, new_kr, new_s = [], [], [], [], []
    for l in range(DEPTH):
        proj = _inproj(x, mods[l], norm1_g[l][None, :], _permute_w_in(w_in[l]))

        pad128 = lambda v: jnp.pad(v.reshape(1, 8), ((0, 0), (8, 112)))
        padt = lambda v: jnp.pad(v.reshape(8, 1), ((8, 0), (0, 0)))
        sm_t = proj[:, C_SM:C_SM + 16].T
        dn_u, dn_w, dn_qg, dn_kd, dn_ai, dn_bg = _dn_prep(
            proj, sm_t, dn_conv_w[l], pad128(dn_a_log[l]), pad128(dn_dt_bias[l]),
            padt(dn_a_log[l]), padt(dn_dt_bias[l]))
        o_dn_f, o_dn_b, s_all = _dn_scan(dn_u, dn_w, dn_qg, dn_kd, dn_ai, dn_bg, state_delta[:, l])
        s_c = s_all[:BATCH]

        qw = jnp.tile(gqa_q_norm[l][None, :], (1, GQA_H))
        kw = jnp.tile(gqa_k_norm[l][None, :], (1, GQA_KV))
        o_g_c, kn_c = _gqa(proj, qw, kw, SEQ, SEQ, BATCH, 0)
        (o_g_l,) = _gqa(proj, qw, kw, DEC_SEQ, 128, DEC_BATCH, RC, rope=gqa_rope,
                        cache=(cache_gqa_k[:, l].reshape(DEC_BATCH, PAST, 128),
                               cache_gqa_v[:, l].reshape(DEC_BATCH, PAST, 128)))

        wq = mla_w_uq[l].reshape(MLA_QL, MLA_H, MLA_DQ)
        wq = jnp.concatenate([wq[:, :, :MLA_NOPE].reshape(MLA_QL, -1), wq[:, :, MLA_NOPE:].reshape(MLA_QL, -1)],
                             axis=1).astype(BF16)
        wkv = mla_w_ukv[l].reshape(MLA_KVL, MLA_H, MLA_NOPE + MLA_V)
        wkv = jnp.concatenate([wkv[:, :, :MLA_NOPE].reshape(MLA_KVL, -1), wkv[:, :, MLA_NOPE:].reshape(MLA_KVL, -1)],
                              axis=1).astype(BF16)
        mqw, mkvw = mla_q_norm[l][None, :], mla_kv_norm[l][None, :]
        o_m_c, ckv_c = _mla(proj, wq, wkv, mqw, mkvw, SEQ, SEQ, BATCH, 0)
        (o_m_l,) = _mla(proj, wq, wkv, mqw, mkvw, DEC_SEQ, 256, DEC_BATCH, RC, rope=mla_rope,
                        cache=(cache_mla_ckv[:, l], cache_mla_krope[:, l]))
        x = _merge(x, mods[l], o_dn_f, o_dn_b, proj, o_g_c, o_g_l, o_m_c, o_m_l, dn_norm_g[l][None, :],
                   w_branch[l].astype(BF16), w_out[l].astype(BF16))

        j = l // 2
        if l % 2 == 0:
            x = _ffn(x, mods[l], norm2_g[l][None, :], ffd_w1[j].astype(BF16), ffd_w3[j].astype(BF16),
                     ffd_w2[j].astype(BF16))
        else:
            rw128 = jnp.pad(router_w[j], ((0, 0), (0, 128 - N_EXP)))
            rb128 = jnp.pad(router_b[j][None, :], ((0, 0), (0, 128 - N_EXP)), constant_values=-jnp.inf)
            x = _moe(x, mods[l], norm2_g[l][None, :], rw128, rb128, moe_w1[j].astype(BF16),
                     moe_w3[j].astype(BF16), moe_w2[j].astype(BF16))

        new_k.append(kn_c.reshape(BATCH, SEQ, GQA_KV, GQA_HD))
        new_v.append(proj[:RC, C_GKV + 128:C_GKV + 256].reshape(BATCH, SEQ, GQA_KV, GQA_HD))
        new_ckv.append(ckv_c.reshape(BATCH, SEQ, MLA_KVL))
        new_kr.append(proj[:RC, C_SM + 16:C_SM + 16 + MLA_ROPE].reshape(BATCH, SEQ, MLA_ROPE))
        new_s.append(s_c)

    y_c = _final_norm(x, final_g[None, :], 0, RC)
    y_l = _final_norm(x, final_g[None, :], RC, RL)
    return (y_c.reshape(BATCH, SEQ, D), y_l.reshape(DEC_BATCH, DEC_SEQ, D),
            jnp.stack(new_k, axis=1), jnp.stack(new_v, axis=1), jnp.stack(new_ckv, axis=1),
            jnp.stack(new_kr, axis=1), jnp.stack(new_s, axis=1))
```

```python
import functools
import math

import jax
import jax.numpy as jnp
import numpy as np
from jax import lax
from jax.experimental import pallas as pl
from jax.experimental.pallas import tpu as pltpu

F32 = jnp.float32
BF16 = jnp.bfloat16

D = 1024
BATCH, SEQ = 32, 256
DEC_BATCH, DEC_SEQ = 4, 2048
DEPTH = 4
PAST = 512
GRID_W = 64
ROPE_THETA = 10000.0
EPS = 1e-6
DN_H, DN_DK, DN_DV, DN_C = 4, 128, 128, 64
GQA_H, GQA_KV, GQA_HD = 8, 2, 64
MLA_H, MLA_QL, MLA_KVL, MLA_NOPE, MLA_ROPE, MLA_V = 8, 256, 128, 64, 32, 64
FF_DENSE, N_EXP, FF_EXP = 2816, 8, 3584

RC = BATCH * SEQ
RL = DEC_BATCH * DEC_SEQ
R = RC + RL

N_BG = 3 * D
C_QKV, C_DG, C_GQ, C_GKV, C_MCQ, C_MCKV, C_SM = 0, 1536, 2048, 2560, 2816, 3072, 3200
N_PROJ = 3328
NP = N_BG + N_PROJ

VMEM_LIMIT = 56 * 1024 * 1024


def _cp(n_grid):
    return pltpu.CompilerParams(dimension_semantics=("arbitrary",) * n_grid,
                                vmem_limit_bytes=VMEM_LIMIT)


def _dot(a, b):
    return jnp.dot(a, b, preferred_element_type=F32)


def _bdot(a, b):
    return jnp.dot(a.astype(BF16), b.astype(BF16), preferred_element_type=F32)


def _bdot_nt(a, b):
    return lax.dot_general(a.astype(BF16), b.astype(BF16), (((1,), (1,)), ((), ())),
                           preferred_element_type=F32)


def _bdot_tn(a, b):
    return lax.dot_general(a.astype(BF16), b.astype(BF16), (((0,), (0,)), ((), ())),
                           preferred_element_type=F32)


def _split2(a):
    hi = a.astype(BF16)
    lo = (a - hi.astype(F32)).astype(BF16)
    return hi, lo


def _split3(a):
    a1 = a.astype(BF16)
    r1 = a - a1.astype(F32)
    a2 = r1.astype(BF16)
    a3 = (r1 - a2.astype(F32)).astype(BF16)
    return a1, a2, a3


def _dot3(a, b):
    ah, al = _split2(a)
    bh, bl = _split2(b)
    return _dot(ah, bh) + (_dot(ah, bl) + _dot(al, bh))


def _mask_dot_r(mask_bf, g):
    g1, g2, g3 = _split3(g)
    return _dot(mask_bf, g1) + (_dot(mask_bf, g2) + _dot(mask_bf, g3))


def _mask_dot_l(g, mask_bf):
    g1, g2, g3 = _split3(g)
    return _dot(g1, mask_bf) + (_dot(g2, mask_bf) + _dot(g3, mask_bf))


def _sigmoid(x):
    return 1.0 / (1.0 + jnp.exp(-x))


def _silu(x):
    return x * _sigmoid(x)


def _softplus(x):
    return jnp.maximum(x, 0.0) + jnp.log1p(jnp.exp(-jnp.abs(x)))


def _mod_group(row0):
    return jnp.where(row0 < RC, 0, 1 + (row0 - RC) // DEC_SEQ)


MODS_TN = 1536


def _mods_body(c_ref, w_ref, b_ref, o_ref):
    s = _silu(c_ref[...])
    o_ref[0] = _bdot(s, w_ref[0]) + b_ref[0]


def _mods(cond8, w_mod, b_mod):
    nj = 6 * D // MODS_TN
    return pl.pallas_call(
        _mods_body,
        grid=(DEPTH, nj),
        in_specs=[pl.BlockSpec((8, D), lambda l, j: (0, 0)),
                  pl.BlockSpec((1, D, MODS_TN), lambda l, j: (l, 0, j)),
                  pl.BlockSpec((1, 1, MODS_TN), lambda l, j: (l, 0, j))],
        out_specs=pl.BlockSpec((1, 8, MODS_TN), lambda l, j: (l, 0, j)),
        out_shape=jax.ShapeDtypeStruct((DEPTH, 8, 6 * D), F32),
        compiler_params=_cp(2),
        name="mods",
    )(cond8, w_mod, b_mod.reshape(DEPTH, 1, 6 * D))


IN_TM = 512
IN_CHUNK = 1536


def _modnorm(x, g, shift, scale):
    ms = jnp.mean(x * x, axis=-1, keepdims=True)
    y = x * lax.rsqrt(ms + EPS) * g
    return y * (1.0 + scale) + shift


def _inproj_body(x_ref, m_ref, g_ref, w_ref, bg_ref, o_ref):
    h = _modnorm(x_ref[...], g_ref[...], m_ref[0, 0:1, :], m_ref[0, 1:2, :]).astype(BF16)
    for c0 in range(0, N_BG, IN_CHUNK):
        bg_ref[:, c0:c0 + IN_CHUNK] = _dot(h, w_ref[:, c0:c0 + IN_CHUNK]).astype(BF16)
    for c0 in range(0, N_PROJ, IN_CHUNK):
        c1 = min(c0 + IN_CHUNK, N_PROJ)
        o_ref[:, c0:c1] = _dot(h, w_ref[:, N_BG + c0:N_BG + c1])


def _inproj(x, mods_l, g, w_bf):
    return pl.pallas_call(
        _inproj_body,
        grid=(R // IN_TM,),
        in_specs=[pl.BlockSpec((IN_TM, D), lambda i: (i, 0)),
                  pl.BlockSpec((1, 6, D), lambda i: (_mod_group(i * IN_TM), 0, 0)),
                  pl.BlockSpec((1, D), lambda i: (0, 0)),
                  pl.BlockSpec((D, NP), lambda i: (0, 0), pipeline_mode=pl.Buffered(1))],
        out_specs=[pl.BlockSpec((IN_TM, N_BG), lambda i: (i, 0)),
                   pl.BlockSpec((IN_TM, N_PROJ), lambda i: (i, 0))],
        out_shape=[jax.ShapeDtypeStruct((R, N_BG), BF16), jax.ShapeDtypeStruct((R, N_PROJ), F32)],
        compiler_params=_cp(1),
        name="inproj",
    )(x, mods_l, g, w_bf)


PREP_TM = 256
PREP_NC = PREP_TM // DN_C
PREP_LOCKSTEP = 4


def _prep_body(x_ref, xp_ref, xn_ref, sm_ref, smt_ref, cw_ref, al_ref, dt_ref, alt_ref, dtt_ref,
               u_o, w_o, qg_o, kd_o, ai_o, bg_o, qkv_o, gr_o):
    i = pl.program_id(0)
    n_ctx = RC // PREP_TM
    per_seq = DEC_SEQ // PREP_TM
    is_ctx = i < n_ctx
    j = (i - n_ctx) % per_seq
    first = jnp.logical_or(is_ctx, j == 0)
    last = jnp.logical_or(is_ctx, j == per_seq - 1)

    x = x_ref[...]
    prev_row = jnp.where(first, 0.0, xp_ref[7:8, :])
    next_row = jnp.where(last, 0.0, xn_ref[0:1, :])
    row = lax.broadcasted_iota(jnp.int32, (PREP_TM, 1), 0)
    xm = jnp.where(row == 0, prev_row, pltpu.roll(x, 1, axis=0))
    xq = jnp.where(row == PREP_TM - 1, next_row, pltpu.roll(x, PREP_TM - 1, axis=0))
    w = cw_ref[...]
    y = _silu(w[0:1] * xm + w[1:2] * x + w[2:3] * xq)

    for h in range(DN_H):
        qh = y[:, h * DN_DK:(h + 1) * DN_DK]
        qn = qh * lax.rsqrt(jnp.sum(qh * qh, axis=-1, keepdims=True) + EPS) * (DN_DK ** -0.5)
        qkv_o[:, h * DN_DK:(h + 1) * DN_DK] = qn
        kh = y[:, 512 + h * DN_DK:512 + (h + 1) * DN_DK]
        kn = kh * lax.rsqrt(jnp.sum(kh * kh, axis=-1, keepdims=True) + EPS)
        qkv_o[:, 512 + h * DN_DK:512 + (h + 1) * DN_DK] = kn
    qkv_o[:, 1024:1536] = y[:, 1024:1536]

    r = lax.broadcasted_iota(jnp.int32, (PREP_TM, PREP_TM), 0)
    c = lax.broadcasted_iota(jnp.int32, (PREP_TM, PREP_TM), 1)
    same = (r // DN_C) == (c // DN_C)
    low = jnp.where(jnp.logical_and(same, r >= c), 1.0, 0.0).astype(BF16)
    upp = jnp.where(jnp.logical_and(same, r <= c), 1.0, 0.0).astype(BF16)

    sm = sm_ref[...]
    beta = _sigmoid(sm)
    g = -jnp.exp(al_ref[...]) * _softplus(sm + dt_ref[...])
    gc_f = _mask_dot_r(low, g)
    gc_b = _mask_dot_r(upp, g)
    lane = lax.broadcasted_iota(jnp.int32, (1, 128), 1)
    bg_o[...] = jnp.where(lane < 8, beta, jnp.where(lane < 12, gc_f, jnp.where(lane < 16, gc_b, 0.0)))

    gt = -jnp.exp(alt_ref[...]) * _softplus(smt_ref[...] + dtt_ref[...])
    gct_f = _mask_dot_l(gt, upp)
    gct_b = _mask_dot_l(gt, low)
    sub = lax.broadcasted_iota(jnp.int32, (16, 1), 0)
    gct = jnp.where(sub < 12, gct_f, gct_b)
    for k in range(PREP_NC):
        gr_o[k] = gct[8:16, k * DN_C:(k + 1) * DN_C]

    rr = lax.broadcasted_iota(jnp.int32, (DN_C, DN_C), 0)
    cc = lax.broadcasted_iota(jnp.int32, (DN_C, DN_C), 1)

    def chunk_group(gi, carry):
        prob = [(cj, d, h) for cj in range(PREP_LOCKSTEP) for d in range(2) for h in range(DN_H)]
        ci = [gi * PREP_LOCKSTEP + cj for cj in range(PREP_LOCKSTEP)]
        rows_c = [pl.ds(pl.multiple_of(c_ * DN_C, DN_C), DN_C) for c_ in ci]
        bgc = [bg_o[r_, :] for r_ in rows_c]
        grow_all = [gr_o[c_] for c_ in ci]
        rows = [rows_c[cj] for cj, d, h in prob]
        q = [qkv_o[rows_c[cj], h * DN_DK:(h + 1) * DN_DK] for cj, d, h in prob]
        k = [qkv_o[rows_c[cj], 512 + h * DN_DK:512 + (h + 1) * DN_DK] for cj, d, h in prob]
        v = [qkv_o[rows_c[cj], 1024 + h * DN_DV:1024 + (h + 1) * DN_DV] for cj, d, h in prob]
        beta = [bgc[cj][:, d * DN_H + h:d * DN_H + h + 1] for cj, d, h in prob]
        gcol = [bgc[cj][:, 8 + d * DN_H + h:9 + d * DN_H + h] for cj, d, h in prob]
        grow = [grow_all[cj][d * DN_H + h:d * DN_H + h + 1, :] for cj, d, h in prob]
        incl = [(rr >= cc) if d == 0 else (rr <= cc) for cj, d, h in prob]
        strict = [(rr > cc) if d == 0 else (rr < cc) for cj, d, h in prob]
        n = len(prob)
        decay = [jnp.exp(jnp.where(incl[i], gcol[i] - grow[i], -1e30)) for i in range(n)]
        kb = [k[i] * beta[i] for i in range(n)]
        kk = [_bdot_nt(kb[i], k[i]) for i in range(n)]
        qk = [_bdot_nt(q[i], k[i]) for i in range(n)]
        a = [jnp.where(strict[i], kk[i] * decay[i], 0.0) for i in range(n)]
        blk = lambda b: (rr // b) == (cc // b)
        eye = jnp.where(rr == cc, 1.0, 0.0)
        p = [jnp.where(blk(8), a[i], 0.0) for i in range(n)]
        t = [eye - p[i] for i in range(n)]
        for _ in range(2):
            p = [_bdot(p[i], p[i]) for i in range(n)]
            t = [t[i] + _bdot(t[i], p[i]) for i in range(n)]
        for b in (16, 32, 64):
            m = jnp.logical_and(blk(b), jnp.logical_not(blk(b // 2)))
            tl = [_bdot(t[i], jnp.where(m, a[i], 0.0)) for i in range(n)]
            t = [t[i] - _bdot(tl[i], t[i]) for i in range(n)]
        egc = [jnp.exp(gcol[i]) for i in range(n)]
        glast = [gcol[i][DN_C - 1:DN_C, :] if prob[i][1] == 0 else gcol[i][0:1, :] for i in range(n)]
        uw = [_bdot(t[i], jnp.concatenate([v[i] * beta[i], kb[i] * egc[i]], axis=1)) for i in range(n)]
        for i, (cj, d, h) in enumerate(prob):
            u_o[d, rows[i], h * DN_DV:(h + 1) * DN_DV] = uw[i][:, 0:DN_DV]
            w_o[d, rows[i], h * DN_DK:(h + 1) * DN_DK] = uw[i][:, DN_DV:DN_DV + DN_DK].astype(BF16)
            qg_o[d, rows[i], h * DN_DK:(h + 1) * DN_DK] = (q[i] * egc[i]).astype(BF16)
            kd_o[d, rows[i], h * DN_DK:(h + 1) * DN_DK] = (k[i] * jnp.exp(glast[i] - gcol[i])).astype(BF16)
            ai_o[d, rows[i], h * DN_C:(h + 1) * DN_C] = (qk[i] * decay[i]).astype(BF16)
        return carry

    lax.fori_loop(0, PREP_NC // PREP_LOCKSTEP, chunk_group, 0)


def _dn_prep(proj, sm_t, conv_w, al128, dt128, al_t, dt_t):
    nb8 = R // 8
    qb = C_QKV // 1536
    wide = lambda dt: jax.ShapeDtypeStruct((2, R, DN_H * DN_DK), dt)
    wide_spec = pl.BlockSpec((2, PREP_TM, DN_H * DN_DK), lambda i: (0, i, 0))
    return pl.pallas_call(
        _prep_body,
        grid=(R // PREP_TM,),
        in_specs=[pl.BlockSpec((PREP_TM, 1536), lambda i: (i, qb)),
                  pl.BlockSpec((8, 1536), lambda i: (jnp.maximum(i * (PREP_TM // 8) - 1, 0), qb)),
                  pl.BlockSpec((8, 1536), lambda i: (jnp.minimum((i + 1) * (PREP_TM // 8), nb8 - 1), qb)),
                  pl.BlockSpec((PREP_TM, 128), lambda i: (i, C_SM // 128)),
                  pl.BlockSpec((16, PREP_TM), lambda i: (0, i)),
                  pl.BlockSpec((3, 1536), lambda i: (0, 0)),
                  pl.BlockSpec((1, 128), lambda i: (0, 0)),
                  pl.BlockSpec((1, 128), lambda i: (0, 0)),
                  pl.BlockSpec((16, 1), lambda i: (0, 0)),
                  pl.BlockSpec((16, 1), lambda i: (0, 0))],
        out_specs=[wide_spec, wide_spec, wide_spec, wide_spec,
                   pl.BlockSpec((2, PREP_TM, DN_H * DN_C), lambda i: (0, i, 0)),
                   pl.BlockSpec((PREP_TM, 128), lambda i: (i, 0))],
        out_shape=[wide(F32), wide(BF16), wide(BF16), wide(BF16),
                   jax.ShapeDtypeStruct((2, R, DN_H * DN_C), BF16),
                   jax.ShapeDtypeStruct((R, 128), F32)],
        scratch_shapes=[pltpu.VMEM((PREP_TM, 1536), F32), pltpu.VMEM((PREP_NC, 8, DN_C), F32)],
        compiler_params=_cp(1),
        name="dn_prep",
    )(proj, proj, proj, proj, sm_t, conv_w, al128, dt128, al_t, dt_t)


SCAN_CTX = RC // PREP_TM
SCAN_PER = DEC_SEQ // PREP_TM
SCAN_NSEQ = BATCH + DEC_BATCH


def _scan_bwd_block(i):
    j = i - SCAN_CTX
    return jnp.where(i < SCAN_CTX, i, SCAN_CTX + (j // SCAN_PER) * SCAN_PER + (SCAN_PER - 1 - j % SCAN_PER))


def _scan_body(uf, wf, qf, kf, af, bf, ub, wb, qb, kb_, ab, bb, s0_ref, of_ref, ob_ref, so_ref, s_scr):
    i = pl.program_id(0)
    is_ctx = i < SCAN_CTX
    j = (i - SCAN_CTX) % SCAN_PER
    first = jnp.logical_or(is_ctx, j == 0)
    last = jnp.logical_or(is_ctx, j == SCAN_PER - 1)

    @pl.when(first)
    def _():
        for d in range(2):
            for h in range(DN_H):
                s_scr[d, h] = jnp.where(is_ctx, 0.0, s0_ref[0, d, h])

    dirs = ((uf, wf, qf, kf, af, bf, of_ref), (ub, wb, qb, kb_, ab, bb, ob_ref))

    prob = [(d, h) for d in range(2) for h in range(DN_H)]
    tn = (((0,), (0,)), ((), ()))

    def step(n, carry):
        rows = [pl.ds(pl.multiple_of((n if d == 0 else PREP_NC - 1 - n) * DN_C, DN_C), DN_C) for d in range(2)]
        bgc = [dirs[d][5][rows[d], :] for d in range(2)]
        cols = [slice(h * DN_DK, (h + 1) * DN_DK) for d, h in prob]
        st = [s_scr[d, h] for d, h in prob]
        stb = [x.astype(BF16) for x in st]
        ws = [_dot(dirs[d][1][0, rows[d], cols[i]], stb[i]) for i, (d, h) in enumerate(prob)]
        qs = [_dot(dirs[d][2][0, rows[d], cols[i]], stb[i]) for i, (d, h) in enumerate(prob)]
        vb = [(dirs[d][0][0, rows[d], cols[i]] - ws[i]).astype(BF16) for i, (d, h) in enumerate(prob)]
        av = [_dot(dirs[d][4][0, rows[d], h * DN_C:(h + 1) * DN_C], vb[i]) for i, (d, h) in enumerate(prob)]
        kv = [lax.dot_general(dirs[d][3][0, rows[d], cols[i]], vb[i], tn, preferred_element_type=F32)
              for i, (d, h) in enumerate(prob)]
        for i, (d, h) in enumerate(prob):
            gcol = bgc[d][:, 8 + d * DN_H + h:9 + d * DN_H + h]
            glast = gcol[DN_C - 1:DN_C, :] if d == 0 else gcol[0:1, :]
            s_scr[d, h] = st[i] * jnp.exp(glast) + kv[i]
            dirs[d][6][rows[d], cols[i]] = qs[i] + av[i]
        return carry

    lax.fori_loop(0, PREP_NC, step, 0)

    @pl.when(last)
    def _():
        for d in range(2):
            for h in range(DN_H):
                so_ref[0, d, h] = s_scr[d, h]


def _dn_scan(u, w, qg, kd, ai, bg, s0):
    fwd3 = lambda i: (0, i, 0)
    bwd3 = lambda i: (1, _scan_bwd_block(i), 0)
    wide = lambda m: pl.BlockSpec((1, PREP_TM, DN_H * DN_DK), m)
    narrow = lambda m: pl.BlockSpec((1, PREP_TM, DN_H * DN_C), m)
    state_spec = lambda m: pl.BlockSpec((1, 2, DN_H, DN_DK, DN_DV), m)
    seq_of = lambda i: jnp.where(i < SCAN_CTX, i, SCAN_CTX + (i - SCAN_CTX) // SCAN_PER)
    return pl.pallas_call(
        _scan_body,
        grid=(R // PREP_TM,),
        in_specs=[wide(fwd3), wide(fwd3), wide(fwd3), wide(fwd3), narrow(fwd3),
                  pl.BlockSpec((PREP_TM, 128), lambda i: (i, 0)),
                  wide(bwd3), wide(bwd3), wide(bwd3), wide(bwd3), narrow(bwd3),
                  pl.BlockSpec((PREP_TM, 128), lambda i: (_scan_bwd_block(i), 0)),
                  state_spec(lambda i: (jnp.maximum(i - SCAN_CTX, 0) // SCAN_PER, 0, 0, 0, 0))],
        out_specs=[pl.BlockSpec((PREP_TM, DN_H * DN_DV), lambda i: (i, 0)),
                   pl.BlockSpec((PREP_TM, DN_H * DN_DV), lambda i: (_scan_bwd_block(i), 0)),
                   state_spec(lambda i: (seq_of(i), 0, 0, 0, 0))],
        out_shape=[jax.ShapeDtypeStruct((R, DN_H * DN_DV), F32),
                   jax.ShapeDtypeStruct((R, DN_H * DN_DV), F32),
                   jax.ShapeDtypeStruct((SCAN_NSEQ, 2, DN_H, DN_DK, DN_DV), F32)],
        scratch_shapes=[pltpu.VMEM((2, DN_H, DN_DK, DN_DV), F32)],
        compiler_params=_cp(1),
        name="dn_scan",
    )(u, w, qg, kd, ai, bg, u, w, qg, kd, ai, bg, s0)


def _group_mean_matrix(width, group):
    r = lax.broadcasted_iota(jnp.int32, (width, width), 0)
    c = lax.broadcasted_iota(jnp.int32, (width, width), 1)
    return jnp.where((r // group) == (c // group), 1.0 / group, 0.0).astype(BF16)


def _group_rmsnorm(x, w, group):
    m = _group_mean_matrix(x.shape[-1], group)
    hi, lo = _split2(x * x)
    ms = _dot(hi, m) + _dot(lo, m)
    return x * lax.rsqrt(ms + EPS) * w


def _rope(x, cos, sin_signed, group):
    width = x.shape[-1]
    half = group // 2
    lane = lax.broadcasted_iota(jnp.int32, (1, width), 1)
    swapped = jnp.where((lane % group) < half,
                        pltpu.roll(x, width - half, axis=1), pltpu.roll(x, half, axis=1))
    return x * cos + swapped * sin_signed


def _attention_units(n, scores, values):
    outs = []
    s_next = scores(0)
    for u in range(n):
        s = s_next
        if u + 1 < n:
            s_next = scores(u + 1)
        m = jnp.max(s, axis=-1, keepdims=True)
        p = jnp.exp(s - m)
        l = jnp.sum(p, axis=-1, keepdims=True)
        outs.append(_dot(p.astype(BF16), values(u)) / l)
    return outs


GQA_G = GQA_H // GQA_KV


def _gqa_body(T, TQ, latent, *refs):
    it = iter(refs)
    q_ref, kv_ref, qw_ref, kw_ref = next(it), next(it), next(it), next(it)
    if latent:
        cq_ref, sq_ref, ck_ref, sk_ref, kc_ref, vc_ref = (next(it) for _ in range(6))
    o_ref = next(it)
    kn_ref = None if latent else next(it)
    k_scr, v_scr = next(it), next(it)

    @pl.when(pl.program_id(1) == 0)
    def _():
        kv = kv_ref[...]
        k = _group_rmsnorm(kv[:, 0:128], kw_ref[...], GQA_HD)
        v = kv[:, 128:256]
        if latent:
            k = _rope(k, ck_ref[...], sk_ref[...], GQA_HD)
        else:
            kn_ref[...] = k
        for g in range(GQA_KV):
            k_scr[g, 0:T, :] = k[:, g * GQA_HD:(g + 1) * GQA_HD].astype(BF16)
            v_scr[g, 0:T, :] = v[:, g * GQA_HD:(g + 1) * GQA_HD].astype(BF16)
            if latent:
                k_scr[g, T:T + PAST, :] = kc_ref[0, :, g * GQA_HD:(g + 1) * GQA_HD].astype(BF16)
                v_scr[g, T:T + PAST, :] = vc_ref[0, :, g * GQA_HD:(g + 1) * GQA_HD].astype(BF16)

    q = _group_rmsnorm(q_ref[...], qw_ref[...], GQA_HD)
    if latent:
        q = _rope(q, cq_ref[...], sq_ref[...], GQA_HD)
    q = q * (GQA_HD ** -0.5)
    head = lambda hh: q[:, hh * GQA_HD:(hh + 1) * GQA_HD]
    qu = [jnp.concatenate([head(2 * u), head(2 * u + 1)], axis=0) for u in range(GQA_H // 2)]
    grp = lambda u: (2 * u) // GQA_G
    outs = _attention_units(GQA_H // 2, lambda u: _bdot_nt(qu[u], k_scr[grp(u)]), lambda u: v_scr[grp(u)])
    for u, o in enumerate(outs):
        for j in range(2):
            hh = 2 * u + j
            o_ref[:, hh * GQA_HD:(hh + 1) * GQA_HD] = o[j * TQ:(j + 1) * TQ, :].astype(o_ref.dtype)


def _gqa(proj, qw, kw, T, TQ, n_seq, row_off, rope=None, cache=None):
    latent = rope is not None
    tk = T + (PAST if latent else 0)
    nq = T // TQ
    qo = row_off // TQ
    so = row_off // T
    in_specs = [pl.BlockSpec((TQ, 512), lambda i, j: (qo + i * nq + j, C_GQ // 512)),
                pl.BlockSpec((T, 256), lambda i, j: (so + i, C_GKV // 256)),
                pl.BlockSpec((1, 512), lambda i, j: (0, 0)),
                pl.BlockSpec((1, 128), lambda i, j: (0, 0))]
    args = [proj, proj, qw, kw]
    if latent:
        cq, sq, ck, sk = rope
        kc, vc = cache
        in_specs += [pl.BlockSpec((TQ, 512), lambda i, j: (j, 0)),
                     pl.BlockSpec((TQ, 512), lambda i, j: (j, 0)),
                     pl.BlockSpec((T, 128), lambda i, j: (0, 0)),
                     pl.BlockSpec((T, 128), lambda i, j: (0, 0)),
                     pl.BlockSpec((1, PAST, 128), lambda i, j: (i, 0, 0)),
                     pl.BlockSpec((1, PAST, 128), lambda i, j: (i, 0, 0))]
        args += [cq, sq, ck, sk, kc, vc]
    out_specs = [pl.BlockSpec((TQ, 512), lambda i, j: (i * nq + j, 0))]
    out_shape = [jax.ShapeDtypeStruct((n_seq * T, 512), BF16)]
    if not latent:
        out_specs.append(pl.BlockSpec((T, 128), lambda i, j: (i, 0)))
        out_shape.append(jax.ShapeDtypeStruct((n_seq * T, 128), F32))
    return pl.pallas_call(
        functools.partial(_gqa_body, T, TQ, latent),
        grid=(n_seq, nq),
        in_specs=in_specs,
        out_specs=out_specs,
        out_shape=out_shape,
        scratch_shapes=[pltpu.VMEM((GQA_KV, tk, GQA_HD), BF16), pltpu.VMEM((GQA_KV, tk, GQA_HD), BF16)],
        compiler_params=_cp(2),
        name="gqa_%d" % T,
    )(*args)


MLA_DQ = MLA_NOPE + MLA_ROPE
MLA_KV_ROWS = 512


def _mla_body(T, TQ, latent, *refs):
    it = iter(refs)
    cq_ref, ckv_ref, sm_ref, wq_ref, wkv_ref, qw_ref, kvw_ref = (next(it) for _ in range(7))
    if latent:
        cosq_ref, sinq_ref, cosk_ref, sink_ref, cc_ref, kc_ref = (next(it) for _ in range(6))
    o_ref = next(it)
    cn_ref = None if latent else next(it)
    k_scr, v_scr = next(it), next(it)
    tk = T + (PAST if latent else 0)

    @pl.when(pl.program_id(1) == 0)
    def _():
        x = ckv_ref[...]
        ms = jnp.mean(x * x, axis=-1, keepdims=True)
        ckv = x * lax.rsqrt(ms + EPS) * kvw_ref[...]
        sm = sm_ref[...]
        if latent:
            sm = _rope_small(sm, cosk_ref[...], sink_ref[...])
        else:
            cn_ref[...] = ckv
        kr = sm[:, 16:16 + MLA_ROPE]

        def put(rows0, ckv_rows, kr_rows):
            n = ckv_rows.shape[0]
            kv = _bdot(ckv_rows, wkv_ref[...])
            krb = kr_rows.astype(BF16)
            for h in range(MLA_H):
                k_scr[h, rows0:rows0 + n, 0:MLA_NOPE] = kv[:, h * MLA_NOPE:(h + 1) * MLA_NOPE].astype(BF16)
                k_scr[h, rows0:rows0 + n, MLA_NOPE:MLA_DQ] = krb
                v_scr[h, rows0:rows0 + n, :] = kv[:, 512 + h * MLA_V:512 + (h + 1) * MLA_V].astype(BF16)

        for r0 in range(0, T, MLA_KV_ROWS):
            r1 = min(r0 + MLA_KV_ROWS, T)
            put(r0, ckv[r0:r1], kr[r0:r1])
        if latent:
            put(T, cc_ref[0], kc_ref[0])

    x = cq_ref[...]
    ms = jnp.mean(x * x, axis=-1, keepdims=True)
    cq = x * lax.rsqrt(ms + EPS) * qw_ref[...]
    qf = _bdot(cq, wq_ref[...])
    qn = qf[:, 0:512]
    qr = qf[:, 512:768]
    if latent:
        qr = _rope(qr, cosq_ref[...], sinq_ref[...], MLA_ROPE)
    scale = MLA_DQ ** -0.5
    qh = [jnp.concatenate([qn[:, h * MLA_NOPE:(h + 1) * MLA_NOPE],
                           qr[:, h * MLA_ROPE:(h + 1) * MLA_ROPE]], axis=1) * scale for h in range(MLA_H)]
    outs = _attention_units(MLA_H, lambda h: _bdot_nt(qh[h], k_scr[h]), lambda h: v_scr[h])
    for h, o in enumerate(outs):
        o_ref[:, h * MLA_V:(h + 1) * MLA_V] = o.astype(o_ref.dtype)


def _rope_small(sm, cos, sin_signed):
    lane = lax.broadcasted_iota(jnp.int32, (1, 128), 1)
    half = MLA_ROPE // 2
    swapped = jnp.where(lane < 16 + half, pltpu.roll(sm, 128 - half, axis=1), pltpu.roll(sm, half, axis=1))
    return sm * cos + swapped * sin_signed


def _mla(proj, wq, wkv, qw, kvw, T, TQ, n_seq, row_off, rope=None, cache=None):
    latent = rope is not None
    tk = T + (PAST if latent else 0)
    nq = T // TQ
    qo = row_off // TQ
    so = row_off // T
    in_specs = [pl.BlockSpec((TQ, 256), lambda i, j: (qo + i * nq + j, C_MCQ // 256)),
                pl.BlockSpec((T, 128), lambda i, j: (so + i, C_MCKV // 128)),
                pl.BlockSpec((T, 128), lambda i, j: (so + i, C_SM // 128)),
                pl.BlockSpec((MLA_QL, 768), lambda i, j: (0, 0)),
                pl.BlockSpec((MLA_KVL, 1024), lambda i, j: (0, 0)),
                pl.BlockSpec((1, 256), lambda i, j: (0, 0)),
                pl.BlockSpec((1, 128), lambda i, j: (0, 0))]
    args = [proj, proj, proj, wq, wkv, qw, kvw]
    if latent:
        cosq, sinq, cosk, sink = rope
        cc, kc = cache
        in_specs += [pl.BlockSpec((TQ, 256), lambda i, j: (j, 0)),
                     pl.BlockSpec((TQ, 256), lambda i, j: (j, 0)),
                     pl.BlockSpec((T, 128), lambda i, j: (0, 0)),
                     pl.BlockSpec((T, 128), lambda i, j: (0, 0)),
                     pl.BlockSpec((1, PAST, 128), lambda i, j: (i, 0, 0)),
                     pl.BlockSpec((1, PAST, MLA_ROPE), lambda i, j: (i, 0, 0))]
        args += [cosq, sinq, cosk, sink, cc, kc]
    out_specs = [pl.BlockSpec((TQ, 512), lambda i, j: (i * nq + j, 0))]
    out_shape = [jax.ShapeDtypeStruct((n_seq * T, 512), BF16)]
    if not latent:
        out_specs.append(pl.BlockSpec((T, 128), lambda i, j: (i, 0)))
        out_shape.append(jax.ShapeDtypeStruct((n_seq * T, 128), F32))
    return pl.pallas_call(
        functools.partial(_mla_body, T, TQ, latent),
        grid=(n_seq, nq),
        in_specs=in_specs,
        out_specs=out_specs,
        out_shape=out_shape,
        scratch_shapes=[pltpu.VMEM((MLA_H, tk, MLA_DQ), BF16), pltpu.VMEM((MLA_H, tk, MLA_V), BF16)],
        compiler_params=_cp(2),
        name="mla_%d" % T,
    )(*args)


MG_TM = 512


def _merge_body(x_ref, m_ref, of_ref, ob_ref, dg_ref, ogc_ref, ogl_ref, omc_ref, oml_ref, bg_ref, ng_ref,
                wb_ref, wo_ref, o_ref):
    is_ctx = pl.program_id(0) < RC // MG_TM
    og = jnp.where(is_ctx, ogc_ref[...], ogl_ref[...])
    om = jnp.where(is_ctx, omc_ref[...], oml_ref[...])
    odn = of_ref[...] + ob_ref[...]
    dg = dg_ref[...]
    ng = ng_ref[...]
    parts = []
    for h in range(DN_H):
        oh = odn[:, h * DN_DV:(h + 1) * DN_DV]
        ms = jnp.mean(oh * oh, axis=-1, keepdims=True)
        parts.append(oh * lax.rsqrt(ms + EPS) * ng * _silu(dg[:, h * DN_DV:(h + 1) * DN_DV]))
    br0 = jnp.concatenate(parts, axis=1)
    gate = lambda n: _sigmoid(bg_ref[:, n * D:(n + 1) * D].astype(F32))
    merged = gate(0) * _bdot(br0, wb_ref[0])
    merged = merged + gate(1) * _bdot(og, wb_ref[1])
    merged = merged + gate(2) * _bdot(om, wb_ref[2])
    out = _bdot(merged, wo_ref[...])
    o_ref[...] = x_ref[...] + m_ref[0, 2:3, :] * out


def _merge(x, mods_l, o_f, o_b, proj, pbg, og_c, og_l, om_c, om_l, ng, wb, wo):
    row = lambda i: (i, 0)
    n_ctx = RC // MG_TM
    ctx_row = lambda i: (jnp.minimum(i, n_ctx - 1), 0)
    lat_row = lambda i: (jnp.maximum(i - n_ctx, 0), 0)
    return pl.pallas_call(
        _merge_body,
        grid=(R // MG_TM,),
        in_specs=[pl.BlockSpec((MG_TM, D), row),
                  pl.BlockSpec((1, 6, D), lambda i: (_mod_group(i * MG_TM), 0, 0)),
                  pl.BlockSpec((MG_TM, 512), row),
                  pl.BlockSpec((MG_TM, 512), row),
                  pl.BlockSpec((MG_TM, 512), lambda i: (i, C_DG // 512)),
                  pl.BlockSpec((MG_TM, 512), ctx_row),
                  pl.BlockSpec((MG_TM, 512), lat_row),
                  pl.BlockSpec((MG_TM, 512), ctx_row),
                  pl.BlockSpec((MG_TM, 512), lat_row),
                  pl.BlockSpec((MG_TM, N_BG), row),
                  pl.BlockSpec((1, DN_DV), lambda i: (0, 0)),
                  pl.BlockSpec((3, 512, D), lambda i: (0, 0, 0)),
                  pl.BlockSpec((D, D), lambda i: (0, 0))],
        out_specs=pl.BlockSpec((MG_TM, D), row),
        out_shape=jax.ShapeDtypeStruct((R, D), F32),
        compiler_params=_cp(1),
        name="merge",
    )(x, mods_l, o_f, o_b, proj, og_c, og_l, om_c, om_l, pbg, ng, wb, wo)


FF_TM = 512
FF_CHUNKS = ((0, 1536), (1536, FF_DENSE))


def _ffn_body(x_ref, m_ref, g_ref, w1_ref, w3_ref, w2_ref, o_ref):
    x = x_ref[...]
    h = _modnorm(x, g_ref[...], m_ref[0, 3:4, :], m_ref[0, 4:5, :]).astype(BF16)
    y = None
    for c0, c1 in FF_CHUNKS:
        a = _silu(_dot(h, w1_ref[:, c0:c1])) * _dot(h, w3_ref[:, c0:c1])
        yc = _dot(a.astype(BF16), w2_ref[c0:c1, :])
        y = yc if y is None else y + yc
    o_ref[...] = x + m_ref[0, 5:6, :] * y


def _ffn(x, mods_l, g, w1, w3, w2):
    once = pl.Buffered(1)
    return pl.pallas_call(
        _ffn_body,
        grid=(R // FF_TM,),
        in_specs=[pl.BlockSpec((FF_TM, D), lambda i: (i, 0)),
                  pl.BlockSpec((1, 6, D), lambda i: (_mod_group(i * FF_TM), 0, 0)),
                  pl.BlockSpec((1, D), lambda i: (0, 0)),
                  pl.BlockSpec((D, FF_DENSE), lambda i: (0, 0), pipeline_mode=once),
                  pl.BlockSpec((D, FF_DENSE), lambda i: (0, 0), pipeline_mode=once),
                  pl.BlockSpec((FF_DENSE, D), lambda i: (0, 0), pipeline_mode=once)],
        out_specs=pl.BlockSpec((FF_TM, D), lambda i: (i, 0)),
        out_shape=jax.ShapeDtypeStruct((R, D), F32),
        compiler_params=_cp(1),
        name="ffn",
    )(x, mods_l, g, w1, w3, w2)


RT_TM = 512
MOE_TG, MOE_TF = 512, 1792
MOE_NT = 2 * R // MOE_TG + N_EXP
MOE_ROWS = MOE_NT * MOE_TG
CB_TM = 256


def _router_body(x_ref, m_ref, g_ref, rw_ref, rb_ref, hn_ref, route_ref, cnt_ref, base_scr):
    @pl.when(pl.program_id(0) == 0)
    def _():
        base_scr[...] = jnp.zeros_like(base_scr)

    h = _modnorm(x_ref[...], g_ref[...], m_ref[0, 3:4, :], m_ref[0, 4:5, :])
    hn_ref[...] = h
    logits = _dot3(h, rw_ref[...]) + rb_ref[...]
    lane = lax.broadcasted_iota(jnp.int32, logits.shape, 1)
    m1 = jnp.max(logits, axis=-1, keepdims=True)
    i1 = jnp.min(jnp.where(logits == m1, lane, 128), axis=-1, keepdims=True)
    sel1 = lane == i1
    rest = jnp.where(sel1, -jnp.inf, logits)
    m2 = jnp.max(rest, axis=-1, keepdims=True)
    i2 = jnp.min(jnp.where(rest == m2, lane, 128), axis=-1, keepdims=True)
    sel2 = lane == i2
    e2 = jnp.exp(m2 - m1)
    p1 = 1.0 / (1.0 + e2)
    p2 = e2 / (1.0 + e2)

    cnt = jnp.where(jnp.logical_or(sel1, sel2), 1.0, 0.0)
    r = lax.broadcasted_iota(jnp.int32, (RT_TM, RT_TM), 0)
    c = lax.broadcasted_iota(jnp.int32, (RT_TM, RT_TM), 1)
    before = jnp.where(r > c, 1.0, 0.0).astype(BF16)
    seen = base_scr[...] + _dot(before, cnt.astype(BF16))
    rank1 = jnp.sum(jnp.where(sel1, seen, 0.0), axis=-1, keepdims=True)
    rank2 = jnp.sum(jnp.where(sel2, seen, 0.0), axis=-1, keepdims=True)
    vals = (i1.astype(F32), i2.astype(F32), rank1, rank2, p1, p2)
    route = jnp.zeros(logits.shape, F32)
    for k, val in enumerate(vals):
        route = jnp.where(lane == k, val, route)
    route_ref[...] = route
    base_scr[...] += jnp.sum(cnt, axis=0, keepdims=True)
    cnt_ref[...] = base_scr[...]


def _router(x, mods_l, g, rw128, rb128):
    return pl.pallas_call(
        _router_body,
        grid=(R // RT_TM,),
        in_specs=[pl.BlockSpec((RT_TM, D), lambda i: (i, 0)),
                  pl.BlockSpec((1, 6, D), lambda i: (_mod_group(i * RT_TM), 0, 0)),
                  pl.BlockSpec((1, D), lambda i: (0, 0)),
                  pl.BlockSpec((D, 128), lambda i: (0, 0)),
                  pl.BlockSpec((1, 128), lambda i: (0, 0))],
        out_specs=[pl.BlockSpec((RT_TM, D), lambda i: (i, 0)),
                   pl.BlockSpec((RT_TM, 128), lambda i: (i, 0)),
                   pl.BlockSpec((1, 128), lambda i: (0, 0))],
        out_shape=[jax.ShapeDtypeStruct((R, D), F32),
                   jax.ShapeDtypeStruct((R, 128), F32),
                   jax.ShapeDtypeStruct((1, 128), F32)],
        scratch_shapes=[pltpu.VMEM((1, 128), F32)],
        compiler_params=_cp(1),
        name="moe_router",
    )(x, mods_l, g, rw128, rb128)


def _rows_copy(src_hbm, dst, sem, n):
    return pltpu.make_async_copy(src_hbm.at[pl.ds(0, n)], dst, sem)


def _start_row_gather(idx_ref, n, src_hbm, dst, sem):
    def issue(i, carry):
        pltpu.make_async_copy(src_hbm.at[pl.ds(idx_ref[0, 0, i], 1)], dst.at[pl.ds(i, 1)], sem).start()
        return carry

    lax.fori_loop(0, n, issue, 0, unroll=8)


def _gather_rows(idx_ref, n, src_hbm, dst, sem):
    _start_row_gather(idx_ref, n, src_hbm, dst, sem)
    _rows_copy(src_hbm, dst, sem, n).wait()


def _experts_body(te_ref, tv_ref, src_ref, nsrc_ref, hn_hbm, w1_ref, w3_ref, w2_ref, ys_ref, xg_scr, sems):
    t = pl.program_id(0)
    slot = t % 2
    valid = tv_ref[t] > 0
    nxt = jnp.minimum(t + 1, MOE_NT - 1)
    next_valid = jnp.logical_and(t + 1 < MOE_NT, tv_ref[nxt] > 0)

    @pl.when(jnp.logical_and(t == 0, valid))
    def _():
        _start_row_gather(src_ref, MOE_TG, hn_hbm, xg_scr.at[0], sems.at[0])

    @pl.when(next_valid)
    def _():
        _start_row_gather(nsrc_ref, MOE_TG, hn_hbm, xg_scr.at[1 - slot], sems.at[1 - slot])

    @pl.when(valid)
    def _():
        _rows_copy(hn_hbm, xg_scr.at[slot], sems.at[slot], MOE_TG).wait()
        xb = xg_scr[slot].astype(BF16)
        y = None
        for c in range(FF_EXP // MOE_TF):
            cs = slice(c * MOE_TF, (c + 1) * MOE_TF)
            a = _silu(_dot(xb, w1_ref[0, :, cs])) * _dot(xb, w3_ref[0, :, cs])
            yc = _dot(a.astype(BF16), w2_ref[0, cs, :])
            y = yc if y is None else y + yc
        ys_ref[...] = y

    @pl.when(jnp.logical_not(valid))
    def _():
        ys_ref[...] = jnp.zeros_like(ys_ref)


def _experts(tile_e, tile_v, src, hn, w1, w3, w2):
    once = pl.Buffered(1)
    grid_spec = pltpu.PrefetchScalarGridSpec(
        num_scalar_prefetch=2,
        grid=(MOE_NT,),
        in_specs=[pl.BlockSpec((1, 1, MOE_TG), lambda t, te, tv: (t, 0, 0), memory_space=pltpu.SMEM),
                  pl.BlockSpec((1, 1, MOE_TG), lambda t, te, tv: (jnp.minimum(t + 1, MOE_NT - 1), 0, 0),
                               memory_space=pltpu.SMEM),
                  pl.BlockSpec(memory_space=pl.ANY),
                  pl.BlockSpec((1, D, FF_EXP), lambda t, te, tv: (te[t], 0, 0), pipeline_mode=once),
                  pl.BlockSpec((1, D, FF_EXP), lambda t, te, tv: (te[t], 0, 0), pipeline_mode=once),
                  pl.BlockSpec((1, FF_EXP, D), lambda t, te, tv: (te[t], 0, 0), pipeline_mode=once)],
        out_specs=pl.BlockSpec((MOE_TG, D), lambda t, te, tv: (t, 0)),
        scratch_shapes=[pltpu.VMEM((2, MOE_TG, D), F32), pltpu.SemaphoreType.DMA((2,))],
    )
    return pl.pallas_call(
        _experts_body,
        grid_spec=grid_spec,
        out_shape=jax.ShapeDtypeStruct((MOE_ROWS, D), F32),
        compiler_params=_cp(1),
        name="moe_experts",
    )(tile_e, tile_v, src, src, hn, w1, w3, w2)


def _combine_body(pos_ref, x_ref, m_ref, route_ref, ys_hbm, o_ref, buf, sem):
    _gather_rows(pos_ref, 2 * CB_TM, ys_hbm, buf, sem)
    route = route_ref[...]
    y = route[:, 4:5] * buf[0:CB_TM, :] + route[:, 5:6] * buf[CB_TM:2 * CB_TM, :]
    o_ref[...] = x_ref[...] + m_ref[0, 5:6, :] * y


def _combine(pos, x, mods_l, route, ys):
    return pl.pallas_call(
        _combine_body,
        grid=(R // CB_TM,),
        in_specs=[pl.BlockSpec((1, 1, 2 * CB_TM), lambda i: (i, 0, 0), memory_space=pltpu.SMEM),
                  pl.BlockSpec((CB_TM, D), lambda i: (i, 0)),
                  pl.BlockSpec((1, 6, D), lambda i: (_mod_group(i * CB_TM), 0, 0)),
                  pl.BlockSpec((CB_TM, 128), lambda i: (i, 0)),
                  pl.BlockSpec(memory_space=pl.ANY)],
        out_specs=pl.BlockSpec((CB_TM, D), lambda i: (i, 0)),
        out_shape=jax.ShapeDtypeStruct((R, D), F32),
        scratch_shapes=[pltpu.VMEM((2 * CB_TM, D), F32), pltpu.SemaphoreType.DMA(())],
        compiler_params=_cp(1),
        name="moe_combine",
    )(pos, x, mods_l, route, ys)


def _moe(x, mods_l, g, rw128, rb128, w1, w3, w2):
    hn, route, cnt = _router(x, mods_l, g, rw128, rb128)
    eid = route[:, 0:2].astype(jnp.int32)
    rank = route[:, 2:4].astype(jnp.int32)
    counts = cnt[0, :N_EXP].astype(jnp.int32)
    gsize = (counts + MOE_TG - 1) // MOE_TG * MOE_TG
    gend = jnp.cumsum(gsize)
    pos = (gend - gsize)[eid] + rank
    tile_start = jnp.arange(MOE_NT, dtype=jnp.int32) * MOE_TG
    tile_e = jnp.minimum(jnp.sum(tile_start[:, None] >= gend[None, :], axis=1), N_EXP - 1).astype(jnp.int32)
    tile_v = (tile_start < gend[-1]).astype(jnp.int32)
    tok = jnp.broadcast_to(jnp.arange(R, dtype=jnp.int32)[:, None], (R, 2))
    src = jnp.zeros((MOE_ROWS,), jnp.int32).at[pos.reshape(-1)].set(tok.reshape(-1))
    ys = _experts(tile_e, tile_v, src.reshape(MOE_NT, 1, MOE_TG), hn, w1, w3, w2)
    pos_t = pos.reshape(R // CB_TM, CB_TM, 2).transpose(0, 2, 1).reshape(R // CB_TM, 1, 2 * CB_TM)
    return _combine(pos_t, x, mods_l, route, ys)


FN_TM = 1024


def _final_body(x_ref, g_ref, o_ref):
    x = x_ref[...]
    ms = jnp.mean(x * x, axis=-1, keepdims=True)
    o_ref[...] = x * lax.rsqrt(ms + EPS) * g_ref[...]


def _final_norm(x, g, row_off, n_rows):
    bo = row_off // FN_TM
    return pl.pallas_call(
        _final_body,
        grid=(n_rows // FN_TM,),
        in_specs=[pl.BlockSpec((FN_TM, D), lambda i: (bo + i, 0)), pl.BlockSpec((1, D), lambda i: (0, 0))],
        out_specs=pl.BlockSpec((FN_TM, D), lambda i: (i, 0)),
        out_shape=jax.ShapeDtypeStruct((n_rows, D), F32),
        compiler_params=_cp(1),
        name="final_norm",
    )(x, g)


def _rope_tables(n_tokens, rot_dim):
    t = np.arange(n_tokens)
    row = (t // GRID_W).astype(np.float32)
    col = (t % GRID_W).astype(np.float32)
    n_freq = rot_dim // 4
    inv = (ROPE_THETA ** (-jnp.arange(n_freq, dtype=F32) / n_freq))
    ang = jnp.concatenate([jnp.asarray(row)[:, None] * inv, jnp.asarray(col)[:, None] * inv], axis=-1)
    cos, sin = jnp.cos(ang), jnp.sin(ang)
    return jnp.concatenate([cos, cos], axis=-1), jnp.concatenate([-sin, sin], axis=-1)


def _permute_w_in(w):
    return jnp.concatenate(
        [w[:, 3248:6320], w[:, 0:2048], w[:, 2064:3216], w[:, 2048:2064], w[:, 3216:3248],
         jnp.zeros((D, NP - 6320), w.dtype)], axis=1).astype(BF16)


def kernel(x_prompt, x_sample, c, cache_gqa_k, cache_gqa_v, cache_mla_ckv, cache_mla_krope, state_delta, c_ctx, w_mod, b_mod, norm1_g, norm2_g, w_in, dn_conv_w, dn_a_log, dn_dt_bias, dn_norm_g, gqa_q_norm, gqa_k_norm, mla_q_norm, mla_kv_norm, mla_w_uq, mla_w_ukv, w_branch, w_out, ffd_w1, ffd_w3, ffd_w2, router_w, router_b, moe_w1, moe_w3, moe_w2, final_g):
    x = jnp.concatenate([x_prompt.reshape(RC, D), x_sample.reshape(RL, D)], axis=0)
    cond8 = jnp.concatenate([c_ctx[None, :], c, jnp.zeros((3, D), F32)], axis=0)
    mods = _mods(cond8, w_mod, b_mod).reshape(DEPTH, 8, 6, D)

    cg, sg = _rope_tables(DEC_SEQ, GQA_HD)
    gqa_rope = (jnp.tile(cg, (1, GQA_H)), jnp.tile(sg, (1, GQA_H)),
                jnp.tile(cg, (1, GQA_KV)), jnp.tile(sg, (1, GQA_KV)))
    cm, sm_ = _rope_tables(DEC_SEQ, MLA_ROPE)
    padk = lambda t: jnp.pad(t, ((0, 0), (16, 128 - 16 - MLA_ROPE)))
    mla_rope = (jnp.tile(cm, (1, MLA_H)), jnp.tile(sm_, (1, MLA_H)),
                jnp.pad(cm, ((0, 0), (16, 128 - 16 - MLA_ROPE)), constant_values=1.0), padk(sm_))

    new_k, new_v, new_ckv, new_kr, new_s = [], [], [], [], []
    for l in range(DEPTH):
        pbg, proj = _inproj(x, mods[l], norm1_g[l][None, :], _permute_w_in(w_in[l]))

        pad128 = lambda v: jnp.pad(v.reshape(1, 8), ((0, 0), (8, 112)))
        padt = lambda v: jnp.pad(v.reshape(8, 1), ((8, 0), (0, 0)))
        sm_t = proj[:, C_SM:C_SM + 16].T
        dn_u, dn_w, dn_qg, dn_kd, dn_ai, dn_bg = _dn_prep(
            proj, sm_t, dn_conv_w[l], pad128(dn_a_log[l]), pad128(dn_dt_bias[l]),
            padt(dn_a_log[l]), padt(dn_dt_bias[l]))
        o_dn_f, o_dn_b, s_all = _dn_scan(dn_u, dn_w, dn_qg, dn_kd, dn_ai, dn_bg, state_delta[:, l])
        s_c = s_all[:BATCH]

        qw = jnp.tile(gqa_q_norm[l][None, :], (1, GQA_H))
        kw = jnp.tile(gqa_k_norm[l][None, :], (1, GQA_KV))
        o_g_c, kn_c = _gqa(proj, qw, kw, SEQ, SEQ, BATCH, 0)
        (o_g_l,) = _gqa(proj, qw, kw, DEC_SEQ, 256, DEC_BATCH, RC, rope=gqa_rope,
                        cache=(cache_gqa_k[:, l].reshape(DEC_BATCH, PAST, 128),
                               cache_gqa_v[:, l].reshape(DEC_BATCH, PAST, 128)))

        wq = mla_w_uq[l].reshape(MLA_QL, MLA_H, MLA_DQ)
        wq = jnp.concatenate([wq[:, :, :MLA_NOPE].reshape(MLA_QL, -1), wq[:, :, MLA_NOPE:].reshape(MLA_QL, -1)],
                             axis=1).astype(BF16)
        wkv = mla_w_ukv[l].reshape(MLA_KVL, MLA_H, MLA_NOPE + MLA_V)
        wkv = jnp.concatenate([wkv[:, :, :MLA_NOPE].reshape(MLA_KVL, -1), wkv[:, :, MLA_NOPE:].reshape(MLA_KVL, -1)],
                              axis=1).astype(BF16)
        mqw, mkvw = mla_q_norm[l][None, :], mla_kv_norm[l][None, :]
        o_m_c, ckv_c = _mla(proj, wq, wkv, mqw, mkvw, SEQ, SEQ, BATCH, 0)
        (o_m_l,) = _mla(proj, wq, wkv, mqw, mkvw, DEC_SEQ, 256, DEC_BATCH, RC, rope=mla_rope,
                        cache=(cache_mla_ckv[:, l], cache_mla_krope[:, l]))
        x = _merge(x, mods[l], o_dn_f, o_dn_b, proj, pbg, o_g_c, o_g_l, o_m_c, o_m_l, dn_norm_g[l][None, :],
                   w_branch[l].astype(BF16), w_out[l].astype(BF16))

        j = l // 2
        if l % 2 == 0:
            x = _ffn(x, mods[l], norm2_g[l][None, :], ffd_w1[j].astype(BF16), ffd_w3[j].astype(BF16),
                     ffd_w2[j].astype(BF16))
        else:
            rw128 = jnp.pad(router_w[j], ((0, 0), (0, 128 - N_EXP)))
            rb128 = jnp.pad(router_b[j][None, :], ((0, 0), (0, 128 - N_EXP)), constant_values=-jnp.inf)
            x = _moe(x, mods[l], norm2_g[l][None, :], rw128, rb128, moe_w1[j].astype(BF16),
                     moe_w3[j].astype(BF16), moe_w2[j].astype(BF16))

        new_k.append(kn_c.reshape(BATCH, SEQ, GQA_KV, GQA_HD))
        new_v.append(proj[:RC, C_GKV + 128:C_GKV + 256].reshape(BATCH, SEQ, GQA_KV, GQA_HD))
        new_ckv.append(ckv_c.reshape(BATCH, SEQ, MLA_KVL))
        new_kr.append(proj[:RC, C_SM + 16:C_SM + 16 + MLA_ROPE].reshape(BATCH, SEQ, MLA_ROPE))
        new_s.append(s_c)

    y_c = _final_norm(x, final_g[None, :], 0, RC)
    y_l = _final_norm(x, final_g[None, :], RC, RL)
    return (y_c.reshape(BATCH, SEQ, D), y_l.reshape(DEC_BATCH, DEC_SEQ, D),
            jnp.stack(new_k, axis=1), jnp.stack(new_v, axis=1), jnp.stack(new_ckv, axis=1),
            jnp.stack(new_kr, axis=1), jnp.stack(new_s, axis=1))
```

```python
import functools
import math

import jax
import jax.numpy as jnp
import numpy as np
from jax import lax
from jax.experimental import pallas as pl
from jax.experimental.pallas import tpu as pltpu

F32 = jnp.float32
BF16 = jnp.bfloat16

D = 1024
BATCH, SEQ = 32, 256
DEC_BATCH, DEC_SEQ = 4, 2048
DEPTH = 4
PAST = 512
GRID_W = 64
ROPE_THETA = 10000.0
EPS = 1e-6
DN_H, DN_DK, DN_DV, DN_C = 4, 128, 128, 64
GQA_H, GQA_KV, GQA_HD = 8, 2, 64
MLA_H, MLA_QL, MLA_KVL, MLA_NOPE, MLA_ROPE, MLA_V = 8, 256, 128, 64, 32, 64
FF_DENSE, N_EXP, FF_EXP = 2816, 8, 3584

RC = BATCH * SEQ
RL = DEC_BATCH * DEC_SEQ
R = RC + RL

N_BG = 3 * D
C_QKV, C_DG, C_GQ, C_GKV, C_MCQ, C_MCKV, C_SM = 0, 1536, 2048, 2560, 2816, 3072, 3200
N_PROJ = 3328
NP = N_BG + N_PROJ

VMEM_LIMIT = 56 * 1024 * 1024


def _cp(n_grid):
    return pltpu.CompilerParams(dimension_semantics=("arbitrary",) * n_grid,
                                vmem_limit_bytes=VMEM_LIMIT)


def _dot(a, b):
    return jnp.dot(a, b, preferred_element_type=F32)


def _bdot(a, b):
    return jnp.dot(a.astype(BF16), b.astype(BF16), preferred_element_type=F32)


def _bdot_nt(a, b):
    return lax.dot_general(a.astype(BF16), b.astype(BF16), (((1,), (1,)), ((), ())),
                           preferred_element_type=F32)


def _bdot_tn(a, b):
    return lax.dot_general(a.astype(BF16), b.astype(BF16), (((0,), (0,)), ((), ())),
                           preferred_element_type=F32)


def _split2(a):
    hi = a.astype(BF16)
    lo = (a - hi.astype(F32)).astype(BF16)
    return hi, lo


def _split3(a):
    a1 = a.astype(BF16)
    r1 = a - a1.astype(F32)
    a2 = r1.astype(BF16)
    a3 = (r1 - a2.astype(F32)).astype(BF16)
    return a1, a2, a3


def _dot3(a, b):
    ah, al = _split2(a)
    bh, bl = _split2(b)
    return _dot(ah, bh) + (_dot(ah, bl) + _dot(al, bh))


def _mask_dot_r(mask_bf, g):
    g1, g2, g3 = _split3(g)
    return _dot(mask_bf, g1) + (_dot(mask_bf, g2) + _dot(mask_bf, g3))


def _mask_dot_l(g, mask_bf):
    g1, g2, g3 = _split3(g)
    return _dot(g1, mask_bf) + (_dot(g2, mask_bf) + _dot(g3, mask_bf))


def _sigmoid(x):
    return 1.0 / (1.0 + jnp.exp(-x))


def _silu(x):
    return x * _sigmoid(x)


def _softplus(x):
    return jnp.maximum(x, 0.0) + jnp.log1p(jnp.exp(-jnp.abs(x)))


def _mod_group(row0):
    return jnp.where(row0 < RC, 0, 1 + (row0 - RC) // DEC_SEQ)


MODS_TN = 1536


def _mods_body(c_ref, w_ref, b_ref, o_ref):
    s = _silu(c_ref[...])
    o_ref[0] = _bdot(s, w_ref[0]) + b_ref[0]


def _mods(cond8, w_mod, b_mod):
    nj = 6 * D // MODS_TN
    return pl.pallas_call(
        _mods_body,
        grid=(DEPTH, nj),
        in_specs=[pl.BlockSpec((8, D), lambda l, j: (0, 0)),
                  pl.BlockSpec((1, D, MODS_TN), lambda l, j: (l, 0, j)),
                  pl.BlockSpec((1, 1, MODS_TN), lambda l, j: (l, 0, j))],
        out_specs=pl.BlockSpec((1, 8, MODS_TN), lambda l, j: (l, 0, j)),
        out_shape=jax.ShapeDtypeStruct((DEPTH, 8, 6 * D), F32),
        compiler_params=_cp(2),
        name="mods",
    )(cond8, w_mod, b_mod.reshape(DEPTH, 1, 6 * D))


IN_TM = 512
IN_CHUNK = 1536


def _modnorm(x, g, shift, scale):
    ms = jnp.mean(x * x, axis=-1, keepdims=True)
    y = x * lax.rsqrt(ms + EPS) * g
    return y * (1.0 + scale) + shift


W_IN_PIECES = ((3248, 6320), (0, 2048), (2064, 3216), (2048, 2064), (3216, 3248))
W_IN_COLS = 6320


def _inproj_body(x_ref, m_ref, g_ref, wsrc_ref, bg_ref, o_ref, w_ref):
    @pl.when(pl.program_id(0) == 0)
    def _():
        at = 0
        for a, b in W_IN_PIECES:
            w_ref[:, at:at + (b - a)] = wsrc_ref[:, a:b]
            at += b - a
        w_ref[:, at:NP] = jnp.zeros((D, NP - at), BF16)

    h = _modnorm(x_ref[...], g_ref[...], m_ref[0, 0:1, :], m_ref[0, 1:2, :]).astype(BF16)
    for c0 in range(0, N_BG, IN_CHUNK):
        bg_ref[:, c0:c0 + IN_CHUNK] = _dot(h, w_ref[:, c0:c0 + IN_CHUNK]).astype(BF16)
    for c0 in range(0, N_PROJ, IN_CHUNK):
        c1 = min(c0 + IN_CHUNK, N_PROJ)
        o_ref[:, c0:c1] = _dot(h, w_ref[:, N_BG + c0:N_BG + c1])


def _inproj(x, mods_l, g, w_bf):
    return pl.pallas_call(
        _inproj_body,
        grid=(R // IN_TM,),
        in_specs=[pl.BlockSpec((IN_TM, D), lambda i: (i, 0)),
                  pl.BlockSpec((1, 6, D), lambda i: (_mod_group(i * IN_TM), 0, 0)),
                  pl.BlockSpec((1, D), lambda i: (0, 0)),
                  pl.BlockSpec((D, W_IN_COLS), lambda i: (0, 0), pipeline_mode=pl.Buffered(1))],
        out_specs=[pl.BlockSpec((IN_TM, N_BG), lambda i: (i, 0)),
                   pl.BlockSpec((IN_TM, N_PROJ), lambda i: (i, 0))],
        out_shape=[jax.ShapeDtypeStruct((R, N_BG), BF16), jax.ShapeDtypeStruct((R, N_PROJ), F32)],
        scratch_shapes=[pltpu.VMEM((D, NP), BF16)],
        compiler_params=_cp(1),
        name="inproj",
    )(x, mods_l, g, w_bf)


PREP_TM = 256
PREP_NC = PREP_TM // DN_C
PREP_LOCKSTEP = 4


def _prep_body(x_ref, xp_ref, xn_ref, sm_ref, smt_ref, cw_ref, al_ref, dt_ref, alt_ref, dtt_ref,
               u_o, w_o, qg_o, kd_o, ai_o, bg_o, qkv_o, gr_o):
    i = pl.program_id(0)
    n_ctx = RC // PREP_TM
    per_seq = DEC_SEQ // PREP_TM
    is_ctx = i < n_ctx
    j = (i - n_ctx) % per_seq
    first = jnp.logical_or(is_ctx, j == 0)
    last = jnp.logical_or(is_ctx, j == per_seq - 1)

    x = x_ref[...]
    prev_row = jnp.where(first, 0.0, xp_ref[7:8, :])
    next_row = jnp.where(last, 0.0, xn_ref[0:1, :])
    row = lax.broadcasted_iota(jnp.int32, (PREP_TM, 1), 0)
    xm = jnp.where(row == 0, prev_row, pltpu.roll(x, 1, axis=0))
    xq = jnp.where(row == PREP_TM - 1, next_row, pltpu.roll(x, PREP_TM - 1, axis=0))
    w = cw_ref[...]
    y = _silu(w[0:1] * xm + w[1:2] * x + w[2:3] * xq)

    for h in range(DN_H):
        qh = y[:, h * DN_DK:(h + 1) * DN_DK]
        qn = qh * lax.rsqrt(jnp.sum(qh * qh, axis=-1, keepdims=True) + EPS) * (DN_DK ** -0.5)
        qkv_o[:, h * DN_DK:(h + 1) * DN_DK] = qn
        kh = y[:, 512 + h * DN_DK:512 + (h + 1) * DN_DK]
        kn = kh * lax.rsqrt(jnp.sum(kh * kh, axis=-1, keepdims=True) + EPS)
        qkv_o[:, 512 + h * DN_DK:512 + (h + 1) * DN_DK] = kn
    qkv_o[:, 1024:1536] = y[:, 1024:1536]

    r = lax.broadcasted_iota(jnp.int32, (PREP_TM, PREP_TM), 0)
    c = lax.broadcasted_iota(jnp.int32, (PREP_TM, PREP_TM), 1)
    same = (r // DN_C) == (c // DN_C)
    low = jnp.where(jnp.logical_and(same, r >= c), 1.0, 0.0).astype(BF16)
    upp = jnp.where(jnp.logical_and(same, r <= c), 1.0, 0.0).astype(BF16)

    sm = sm_ref[...]
    beta = _sigmoid(sm)
    g = -jnp.exp(al_ref[...]) * _softplus(sm + dt_ref[...])
    gc_f = _mask_dot_r(low, g)
    gc_b = _mask_dot_r(upp, g)
    lane = lax.broadcasted_iota(jnp.int32, (1, 128), 1)
    bg_o[...] = jnp.where(lane < 8, beta, jnp.where(lane < 12, gc_f, jnp.where(lane < 16, gc_b, 0.0)))

    gt = -jnp.exp(alt_ref[...]) * _softplus(smt_ref[...] + dtt_ref[...])
    gct_f = _mask_dot_l(gt, upp)
    gct_b = _mask_dot_l(gt, low)
    sub = lax.broadcasted_iota(jnp.int32, (16, 1), 0)
    gct = jnp.where(sub < 12, gct_f, gct_b)
    for k in range(PREP_NC):
        gr_o[k] = gct[8:16, k * DN_C:(k + 1) * DN_C]

    rr = lax.broadcasted_iota(jnp.int32, (DN_C, DN_C), 0)
    cc = lax.broadcasted_iota(jnp.int32, (DN_C, DN_C), 1)

    def chunk_group(gi, carry):
        prob = [(cj, d, h) for cj in range(PREP_LOCKSTEP) for d in range(2) for h in range(DN_H)]
        ci = [gi * PREP_LOCKSTEP + cj for cj in range(PREP_LOCKSTEP)]
        rows_c = [pl.ds(pl.multiple_of(c_ * DN_C, DN_C), DN_C) for c_ in ci]
        bgc = [bg_o[r_, :] for r_ in rows_c]
        grow_all = [gr_o[c_] for c_ in ci]
        rows = [rows_c[cj] for cj, d, h in prob]
        q = [qkv_o[rows_c[cj], h * DN_DK:(h + 1) * DN_DK] for cj, d, h in prob]
        k = [qkv_o[rows_c[cj], 512 + h * DN_DK:512 + (h + 1) * DN_DK] for cj, d, h in prob]
        v = [qkv_o[rows_c[cj], 1024 + h * DN_DV:1024 + (h + 1) * DN_DV] for cj, d, h in prob]
        beta = [bgc[cj][:, d * DN_H + h:d * DN_H + h + 1] for cj, d, h in prob]
        gcol = [bgc[cj][:, 8 + d * DN_H + h:9 + d * DN_H + h] for cj, d, h in prob]
        grow = [grow_all[cj][d * DN_H + h:d * DN_H + h + 1, :] for cj, d, h in prob]
        incl = [(rr >= cc) if d == 0 else (rr <= cc) for cj, d, h in prob]
        strict = [(rr > cc) if d == 0 else (rr < cc) for cj, d, h in prob]
        n = len(prob)
        decay = [jnp.exp(jnp.where(incl[i], gcol[i] - grow[i], -1e30)) for i in range(n)]
        kb = [k[i] * beta[i] for i in range(n)]
        kk = [_bdot_nt(kb[i], k[i]) for i in range(n)]
        qk = [_bdot_nt(q[i], k[i]) for i in range(n)]
        a = [jnp.where(strict[i], kk[i] * decay[i], 0.0) for i in range(n)]
        blk = lambda b: (rr // b) == (cc // b)
        eye = jnp.where(rr == cc, 1.0, 0.0)
        p = [jnp.where(blk(8), a[i], 0.0) for i in range(n)]
        t = [eye - p[i] for i in range(n)]
        for _ in range(2):
            p = [_bdot(p[i], p[i]) for i in range(n)]
            t = [t[i] + _bdot(t[i], p[i]) for i in range(n)]
        for b in (16, 32, 64):
            m = jnp.logical_and(blk(b), jnp.logical_not(blk(b // 2)))
            tl = [_bdot(t[i], jnp.where(m, a[i], 0.0)) for i in range(n)]
            t = [t[i] - _bdot(tl[i], t[i]) for i in range(n)]
        egc = [jnp.exp(gcol[i]) for i in range(n)]
        glast = [gcol[i][DN_C - 1:DN_C, :] if prob[i][1] == 0 else gcol[i][0:1, :] for i in range(n)]
        uw = [_bdot(t[i], jnp.concatenate([v[i] * beta[i], kb[i] * egc[i]], axis=1)) for i in range(n)]
        for i, (cj, d, h) in enumerate(prob):
            u_o[d, rows[i], h * DN_DV:(h + 1) * DN_DV] = uw[i][:, 0:DN_DV]
            w_o[d, rows[i], h * DN_DK:(h + 1) * DN_DK] = uw[i][:, DN_DV:DN_DV + DN_DK].astype(BF16)
            qg_o[d, rows[i], h * DN_DK:(h + 1) * DN_DK] = (q[i] * egc[i]).astype(BF16)
            kd_o[d, rows[i], h * DN_DK:(h + 1) * DN_DK] = (k[i] * jnp.exp(glast[i] - gcol[i])).astype(BF16)
            ai_o[d, rows[i], h * DN_C:(h + 1) * DN_C] = (qk[i] * decay[i]).astype(BF16)
        return carry

    lax.fori_loop(0, PREP_NC // PREP_LOCKSTEP, chunk_group, 0)


def _dn_prep(proj, sm_t, conv_w, al128, dt128, al_t, dt_t):
    nb8 = R // 8
    qb = C_QKV // 1536
    wide = lambda dt: jax.ShapeDtypeStruct((2, R, DN_H * DN_DK), dt)
    wide_spec = pl.BlockSpec((2, PREP_TM, DN_H * DN_DK), lambda i: (0, i, 0))
    return pl.pallas_call(
        _prep_body,
        grid=(R // PREP_TM,),
        in_specs=[pl.BlockSpec((PREP_TM, 1536), lambda i: (i, qb)),
                  pl.BlockSpec((8, 1536), lambda i: (jnp.maximum(i * (PREP_TM // 8) - 1, 0), qb)),
                  pl.BlockSpec((8, 1536), lambda i: (jnp.minimum((i + 1) * (PREP_TM // 8), nb8 - 1), qb)),
                  pl.BlockSpec((PREP_TM, 128), lambda i: (i, C_SM // 128)),
                  pl.BlockSpec((16, PREP_TM), lambda i: (0, i)),
                  pl.BlockSpec((3, 1536), lambda i: (0, 0)),
                  pl.BlockSpec((1, 128), lambda i: (0, 0)),
                  pl.BlockSpec((1, 128), lambda i: (0, 0)),
                  pl.BlockSpec((16, 1), lambda i: (0, 0)),
                  pl.BlockSpec((16, 1), lambda i: (0, 0))],
        out_specs=[wide_spec, wide_spec, wide_spec, wide_spec,
                   pl.BlockSpec((2, PREP_TM, DN_H * DN_C), lambda i: (0, i, 0)),
                   pl.BlockSpec((PREP_TM, 128), lambda i: (i, 0))],
        out_shape=[wide(F32), wide(BF16), wide(BF16), wide(BF16),
                   jax.ShapeDtypeStruct((2, R, DN_H * DN_C), BF16),
                   jax.ShapeDtypeStruct((R, 128), F32)],
        scratch_shapes=[pltpu.VMEM((PREP_TM, 1536), F32), pltpu.VMEM((PREP_NC, 8, DN_C), F32)],
        compiler_params=_cp(1),
        name="dn_prep",
    )(proj, proj, proj, proj, sm_t, conv_w, al128, dt128, al_t, dt_t)


SCAN_CTX = RC // PREP_TM
SCAN_PER = DEC_SEQ // PREP_TM
SCAN_NSEQ = BATCH + DEC_BATCH


def _scan_bwd_block(i):
    j = i - SCAN_CTX
    return jnp.where(i < SCAN_CTX, i, SCAN_CTX + (j // SCAN_PER) * SCAN_PER + (SCAN_PER - 1 - j % SCAN_PER))


def _scan_body(uf, wf, qf, kf, af, bf, ub, wb, qb, kb_, ab, bb, s0_ref, of_ref, ob_ref, so_ref, s_scr):
    i = pl.program_id(0)
    is_ctx = i < SCAN_CTX
    j = (i - SCAN_CTX) % SCAN_PER
    first = jnp.logical_or(is_ctx, j == 0)
    last = jnp.logical_or(is_ctx, j == SCAN_PER - 1)

    @pl.when(first)
    def _():
        for d in range(2):
            for h in range(DN_H):
                s_scr[d, h] = jnp.where(is_ctx, 0.0, s0_ref[0, d, h])

    dirs = ((uf, wf, qf, kf, af, bf, of_ref), (ub, wb, qb, kb_, ab, bb, ob_ref))

    prob = [(d, h) for d in range(2) for h in range(DN_H)]
    tn = (((0,), (0,)), ((), ()))

    def step(n, carry):
        rows = [pl.ds(pl.multiple_of((n if d == 0 else PREP_NC - 1 - n) * DN_C, DN_C), DN_C) for d in range(2)]
        bgc = [dirs[d][5][rows[d], :] for d in range(2)]
        cols = [slice(h * DN_DK, (h + 1) * DN_DK) for d, h in prob]
        st = [s_scr[d, h] for d, h in prob]
        stb = [x.astype(BF16) for x in st]
        ws = [_dot(dirs[d][1][0, rows[d], cols[i]], stb[i]) for i, (d, h) in enumerate(prob)]
        qs = [_dot(dirs[d][2][0, rows[d], cols[i]], stb[i]) for i, (d, h) in enumerate(prob)]
        vb = [(dirs[d][0][0, rows[d], cols[i]] - ws[i]).astype(BF16) for i, (d, h) in enumerate(prob)]
        av = [_dot(dirs[d][4][0, rows[d], h * DN_C:(h + 1) * DN_C], vb[i]) for i, (d, h) in enumerate(prob)]
        kv = [lax.dot_general(dirs[d][3][0, rows[d], cols[i]], vb[i], tn, preferred_element_type=F32)
              for i, (d, h) in enumerate(prob)]
        for i, (d, h) in enumerate(prob):
            gcol = bgc[d][:, 8 + d * DN_H + h:9 + d * DN_H + h]
            glast = gcol[DN_C - 1:DN_C, :] if d == 0 else gcol[0:1, :]
            s_scr[d, h] = st[i] * jnp.exp(glast) + kv[i]
            dirs[d][6][rows[d], cols[i]] = qs[i] + av[i]
        return carry

    lax.fori_loop(0, PREP_NC, step, 0)

    @pl.when(last)
    def _():
        for d in range(2):
            for h in range(DN_H):
                so_ref[0, d, h] = s_scr[d, h]


def _dn_scan(u, w, qg, kd, ai, bg, s0):
    fwd3 = lambda i: (0, i, 0)
    bwd3 = lambda i: (1, _scan_bwd_block(i), 0)
    wide = lambda m: pl.BlockSpec((1, PREP_TM, DN_H * DN_DK), m)
    narrow = lambda m: pl.BlockSpec((1, PREP_TM, DN_H * DN_C), m)
    state_spec = lambda m: pl.BlockSpec((1, 2, DN_H, DN_DK, DN_DV), m)
    seq_of = lambda i: jnp.where(i < SCAN_CTX, i, SCAN_CTX + (i - SCAN_CTX) // SCAN_PER)
    return pl.pallas_call(
        _scan_body,
        grid=(R // PREP_TM,),
        in_specs=[wide(fwd3), wide(fwd3), wide(fwd3), wide(fwd3), narrow(fwd3),
                  pl.BlockSpec((PREP_TM, 128), lambda i: (i, 0)),
                  wide(bwd3), wide(bwd3), wide(bwd3), wide(bwd3), narrow(bwd3),
                  pl.BlockSpec((PREP_TM, 128), lambda i: (_scan_bwd_block(i), 0)),
                  state_spec(lambda i: (jnp.maximum(i - SCAN_CTX, 0) // SCAN_PER, 0, 0, 0, 0))],
        out_specs=[pl.BlockSpec((PREP_TM, DN_H * DN_DV), lambda i: (i, 0)),
                   pl.BlockSpec((PREP_TM, DN_H * DN_DV), lambda i: (_scan_bwd_block(i), 0)),
                   state_spec(lambda i: (seq_of(i), 0, 0, 0, 0))],
        out_shape=[jax.ShapeDtypeStruct((R, DN_H * DN_DV), F32),
                   jax.ShapeDtypeStruct((R, DN_H * DN_DV), F32),
                   jax.ShapeDtypeStruct((SCAN_NSEQ, 2, DN_H, DN_DK, DN_DV), F32)],
        scratch_shapes=[pltpu.VMEM((2, DN_H, DN_DK, DN_DV), F32)],
        compiler_params=_cp(1),
        name="dn_scan",
    )(u, w, qg, kd, ai, bg, u, w, qg, kd, ai, bg, s0)


def _group_mean_matrix(width, group):
    r = lax.broadcasted_iota(jnp.int32, (width, width), 0)
    c = lax.broadcasted_iota(jnp.int32, (width, width), 1)
    return jnp.where((r // group) == (c // group), 1.0 / group, 0.0).astype(BF16)


def _group_rmsnorm(x, w, group):
    m = _group_mean_matrix(x.shape[-1], group)
    hi, lo = _split2(x * x)
    ms = _dot(hi, m) + _dot(lo, m)
    return x * lax.rsqrt(ms + EPS) * w


def _rope(x, cos, sin_signed, group):
    width = x.shape[-1]
    half = group // 2
    lane = lax.broadcasted_iota(jnp.int32, (1, width), 1)
    swapped = jnp.where((lane % group) < half,
                        pltpu.roll(x, width - half, axis=1), pltpu.roll(x, half, axis=1))
    return x * cos + swapped * sin_signed


def _attention_units(n, scores, values):
    outs = []
    s_next = scores(0)
    for u in range(n):
        s = s_next
        if u + 1 < n:
            s_next = scores(u + 1)
        m = jnp.max(s, axis=-1, keepdims=True)
        p = jnp.exp(s - m)
        l = jnp.sum(p, axis=-1, keepdims=True)
        outs.append(_dot(p.astype(BF16), values(u)) / l)
    return outs


GQA_G = GQA_H // GQA_KV


def _gqa_body(T, TQ, latent, *refs):
    it = iter(refs)
    q_ref, kv_ref, qw_ref, kw_ref = next(it), next(it), next(it), next(it)
    if latent:
        cq_ref, sq_ref, ck_ref, sk_ref, kc_ref, vc_ref = (next(it) for _ in range(6))
    o_ref = next(it)
    kn_ref = None if latent else next(it)
    k_scr, v_scr = next(it), next(it)

    @pl.when(pl.program_id(1) == 0)
    def _():
        kv = kv_ref[...]
        k = _group_rmsnorm(kv[:, 0:128], kw_ref[...], GQA_HD)
        v = kv[:, 128:256]
        if latent:
            k = _rope(k, ck_ref[...], sk_ref[...], GQA_HD)
        else:
            kn_ref[...] = k
        for g in range(GQA_KV):
            k_scr[g, 0:T, :] = k[:, g * GQA_HD:(g + 1) * GQA_HD].astype(BF16)
            v_scr[g, 0:T, :] = v[:, g * GQA_HD:(g + 1) * GQA_HD].astype(BF16)
            if latent:
                k_scr[g, T:T + PAST, :] = kc_ref[0, :, g * GQA_HD:(g + 1) * GQA_HD].astype(BF16)
                v_scr[g, T:T + PAST, :] = vc_ref[0, :, g * GQA_HD:(g + 1) * GQA_HD].astype(BF16)

    q = _group_rmsnorm(q_ref[...], qw_ref[...], GQA_HD)
    if latent:
        q = _rope(q, cq_ref[...], sq_ref[...], GQA_HD)
    q = q * (GQA_HD ** -0.5)
    head = lambda hh: q[:, hh * GQA_HD:(hh + 1) * GQA_HD]
    qu = [jnp.concatenate([head(2 * u), head(2 * u + 1)], axis=0) for u in range(GQA_H // 2)]
    grp = lambda u: (2 * u) // GQA_G
    outs = _attention_units(GQA_H // 2, lambda u: _bdot_nt(qu[u], k_scr[grp(u)]), lambda u: v_scr[grp(u)])
    for u, o in enumerate(outs):
        for j in range(2):
            hh = 2 * u + j
            o_ref[:, hh * GQA_HD:(hh + 1) * GQA_HD] = o[j * TQ:(j + 1) * TQ, :].astype(o_ref.dtype)


def _gqa(proj, qw, kw, T, TQ, n_seq, row_off, rope=None, cache=None):
    latent = rope is not None
    tk = T + (PAST if latent else 0)
    nq = T // TQ
    qo = row_off // TQ
    so = row_off // T
    in_specs = [pl.BlockSpec((TQ, 512), lambda i, j: (qo + i * nq + j, C_GQ // 512)),
                pl.BlockSpec((T, 256), lambda i, j: (so + i, C_GKV // 256)),
                pl.BlockSpec((1, 512), lambda i, j: (0, 0)),
                pl.BlockSpec((1, 128), lambda i, j: (0, 0))]
    args = [proj, proj, qw, kw]
    if latent:
        cq, sq, ck, sk = rope
        kc, vc = cache
        in_specs += [pl.BlockSpec((TQ, 512), lambda i, j: (j, 0)),
                     pl.BlockSpec((TQ, 512), lambda i, j: (j, 0)),
                     pl.BlockSpec((T, 128), lambda i, j: (0, 0)),
                     pl.BlockSpec((T, 128), lambda i, j: (0, 0)),
                     pl.BlockSpec((1, PAST, 128), lambda i, j: (i, 0, 0)),
                     pl.BlockSpec((1, PAST, 128), lambda i, j: (i, 0, 0))]
        args += [cq, sq, ck, sk, kc, vc]
    out_specs = [pl.BlockSpec((TQ, 512), lambda i, j: (i * nq + j, 0))]
    out_shape = [jax.ShapeDtypeStruct((n_seq * T, 512), BF16)]
    if not latent:
        out_specs.append(pl.BlockSpec((T, 128), lambda i, j: (i, 0)))
        out_shape.append(jax.ShapeDtypeStruct((n_seq * T, 128), F32))
    return pl.pallas_call(
        functools.partial(_gqa_body, T, TQ, latent),
        grid=(n_seq, nq),
        in_specs=in_specs,
        out_specs=out_specs,
        out_shape=out_shape,
        scratch_shapes=[pltpu.VMEM((GQA_KV, tk, GQA_HD), BF16), pltpu.VMEM((GQA_KV, tk, GQA_HD), BF16)],
        compiler_params=_cp(2),
        name="gqa_%d" % T,
    )(*args)


MLA_DQ = MLA_NOPE + MLA_ROPE
MLA_KV_ROWS = 512


def _mla_body(T, TQ, latent, *refs):
    it = iter(refs)
    cq_ref, ckv_ref, sm_ref, wq_ref, wkv_ref, qw_ref, kvw_ref = (next(it) for _ in range(7))
    if latent:
        cosq_ref, sinq_ref, cosk_ref, sink_ref, cc_ref, kc_ref = (next(it) for _ in range(6))
    o_ref = next(it)
    cn_ref = None if latent else next(it)
    k_scr, v_scr = next(it), next(it)
    tk = T + (PAST if latent else 0)

    @pl.when(pl.program_id(1) == 0)
    def _():
        x = ckv_ref[...]
        ms = jnp.mean(x * x, axis=-1, keepdims=True)
        ckv = x * lax.rsqrt(ms + EPS) * kvw_ref[...]
        sm = sm_ref[...]
        if latent:
            sm = _rope_small(sm, cosk_ref[...], sink_ref[...])
        else:
            cn_ref[...] = ckv
        kr = sm[:, 16:16 + MLA_ROPE]

        def put(rows0, ckv_rows, kr_rows):
            n = ckv_rows.shape[0]
            kv = _bdot(ckv_rows, wkv_ref[...])
            krb = kr_rows.astype(BF16)
            for h in range(MLA_H):
                k_scr[h, rows0:rows0 + n, 0:MLA_NOPE] = kv[:, h * MLA_NOPE:(h + 1) * MLA_NOPE].astype(BF16)
                k_scr[h, rows0:rows0 + n, MLA_NOPE:MLA_DQ] = krb
                v_scr[h, rows0:rows0 + n, :] = kv[:, 512 + h * MLA_V:512 + (h + 1) * MLA_V].astype(BF16)

        for r0 in range(0, T, MLA_KV_ROWS):
            r1 = min(r0 + MLA_KV_ROWS, T)
            put(r0, ckv[r0:r1], kr[r0:r1])
        if latent:
            put(T, cc_ref[0], kc_ref[0])

    x = cq_ref[...]
    ms = jnp.mean(x * x, axis=-1, keepdims=True)
    cq = x * lax.rsqrt(ms + EPS) * qw_ref[...]
    qf = _bdot(cq, wq_ref[...])
    qn = qf[:, 0:512]
    qr = qf[:, 512:768]
    if latent:
        qr = _rope(qr, cosq_ref[...], sinq_ref[...], MLA_ROPE)
    scale = MLA_DQ ** -0.5
    qh = [jnp.concatenate([qn[:, h * MLA_NOPE:(h + 1) * MLA_NOPE],
                           qr[:, h * MLA_ROPE:(h + 1) * MLA_ROPE]], axis=1) * scale for h in range(MLA_H)]
    outs = _attention_units(MLA_H, lambda h: _bdot_nt(qh[h], k_scr[h]), lambda h: v_scr[h])
    for h, o in enumerate(outs):
        o_ref[:, h * MLA_V:(h + 1) * MLA_V] = o.astype(o_ref.dtype)


def _rope_small(sm, cos, sin_signed):
    lane = lax.broadcasted_iota(jnp.int32, (1, 128), 1)
    half = MLA_ROPE // 2
    swapped = jnp.where(lane < 16 + half, pltpu.roll(sm, 128 - half, axis=1), pltpu.roll(sm, half, axis=1))
    return sm * cos + swapped * sin_signed


def _mla(proj, wq, wkv, qw, kvw, T, TQ, n_seq, row_off, rope=None, cache=None):
    latent = rope is not None
    tk = T + (PAST if latent else 0)
    nq = T // TQ
    qo = row_off // TQ
    so = row_off // T
    in_specs = [pl.BlockSpec((TQ, 256), lambda i, j: (qo + i * nq + j, C_MCQ // 256)),
                pl.BlockSpec((T, 128), lambda i, j: (so + i, C_MCKV // 128)),
                pl.BlockSpec((T, 128), lambda i, j: (so + i, C_SM // 128)),
                pl.BlockSpec((MLA_QL, 768), lambda i, j: (0, 0)),
                pl.BlockSpec((MLA_KVL, 1024), lambda i, j: (0, 0)),
                pl.BlockSpec((1, 256), lambda i, j: (0, 0)),
                pl.BlockSpec((1, 128), lambda i, j: (0, 0))]
    args = [proj, proj, proj, wq, wkv, qw, kvw]
    if latent:
        cosq, sinq, cosk, sink = rope
        cc, kc = cache
        in_specs += [pl.BlockSpec((TQ, 256), lambda i, j: (j, 0)),
                     pl.BlockSpec((TQ, 256), lambda i, j: (j, 0)),
                     pl.BlockSpec((T, 128), lambda i, j: (0, 0)),
                     pl.BlockSpec((T, 128), lambda i, j: (0, 0)),
                     pl.BlockSpec((1, PAST, 128), lambda i, j: (i, 0, 0)),
                     pl.BlockSpec((1, PAST, MLA_ROPE), lambda i, j: (i, 0, 0))]
        args += [cosq, sinq, cosk, sink, cc, kc]
    out_specs = [pl.BlockSpec((TQ, 512), lambda i, j: (i * nq + j, 0))]
    out_shape = [jax.ShapeDtypeStruct((n_seq * T, 512), BF16)]
    if not latent:
        out_specs.append(pl.BlockSpec((T, 128), lambda i, j: (i, 0)))
        out_shape.append(jax.ShapeDtypeStruct((n_seq * T, 128), F32))
    return pl.pallas_call(
        functools.partial(_mla_body, T, TQ, latent),
        grid=(n_seq, nq),
        in_specs=in_specs,
        out_specs=out_specs,
        out_shape=out_shape,
        scratch_shapes=[pltpu.VMEM((MLA_H, tk, MLA_DQ), BF16), pltpu.VMEM((MLA_H, tk, MLA_V), BF16)],
        compiler_params=_cp(2),
        name="mla_%d" % T,
    )(*args)


MG_TM = 512


def _merge_body(x_ref, m_ref, of_ref, ob_ref, dg_ref, ogc_ref, ogl_ref, omc_ref, oml_ref, bg_ref, ng_ref,
                wb_ref, wo_ref, o_ref):
    is_ctx = pl.program_id(0) < RC // MG_TM
    og = jnp.where(is_ctx, ogc_ref[...], ogl_ref[...])
    om = jnp.where(is_ctx, omc_ref[...], oml_ref[...])
    odn = of_ref[...] + ob_ref[...]
    dg = dg_ref[...]
    ng = ng_ref[...]
    parts = []
    for h in range(DN_H):
        oh = odn[:, h * DN_DV:(h + 1) * DN_DV]
        ms = jnp.mean(oh * oh, axis=-1, keepdims=True)
        parts.append(oh * lax.rsqrt(ms + EPS) * ng * _silu(dg[:, h * DN_DV:(h + 1) * DN_DV]))
    br0 = jnp.concatenate(parts, axis=1)
    gate = lambda n: _sigmoid(bg_ref[:, n * D:(n + 1) * D].astype(F32))
    merged = gate(0) * _bdot(br0, wb_ref[0])
    merged = merged + gate(1) * _bdot(og, wb_ref[1])
    merged = merged + gate(2) * _bdot(om, wb_ref[2])
    out = _bdot(merged, wo_ref[...])
    o_ref[...] = x_ref[...] + m_ref[0, 2:3, :] * out


def _merge(x, mods_l, o_f, o_b, proj, pbg, og_c, og_l, om_c, om_l, ng, wb, wo):
    row = lambda i: (i, 0)
    n_ctx = RC // MG_TM
    ctx_row = lambda i: (jnp.minimum(i, n_ctx - 1), 0)
    lat_row = lambda i: (jnp.maximum(i - n_ctx, 0), 0)
    return pl.pallas_call(
        _merge_body,
        grid=(R // MG_TM,),
        in_specs=[pl.BlockSpec((MG_TM, D), row),
                  pl.BlockSpec((1, 6, D), lambda i: (_mod_group(i * MG_TM), 0, 0)),
                  pl.BlockSpec((MG_TM, 512), row),
                  pl.BlockSpec((MG_TM, 512), row),
                  pl.BlockSpec((MG_TM, 512), lambda i: (i, C_DG // 512)),
                  pl.BlockSpec((MG_TM, 512), ctx_row),
                  pl.BlockSpec((MG_TM, 512), lat_row),
                  pl.BlockSpec((MG_TM, 512), ctx_row),
                  pl.BlockSpec((MG_TM, 512), lat_row),
                  pl.BlockSpec((MG_TM, N_BG), row),
                  pl.BlockSpec((1, DN_DV), lambda i: (0, 0)),
                  pl.BlockSpec((3, 512, D), lambda i: (0, 0, 0)),
                  pl.BlockSpec((D, D), lambda i: (0, 0))],
        out_specs=pl.BlockSpec((MG_TM, D), row),
        out_shape=jax.ShapeDtypeStruct((R, D), F32),
        compiler_params=_cp(1),
        name="merge",
    )(x, mods_l, o_f, o_b, proj, og_c, og_l, om_c, om_l, pbg, ng, wb, wo)


FF_TM = 512
FF_CHUNKS = ((0, 1536), (1536, FF_DENSE))


def _ffn_body(x_ref, m_ref, g_ref, w1_ref, w3_ref, w2_ref, o_ref):
    x = x_ref[...]
    h = _modnorm(x, g_ref[...], m_ref[0, 3:4, :], m_ref[0, 4:5, :]).astype(BF16)
    y = None
    for c0, c1 in FF_CHUNKS:
        a = _silu(_dot(h, w1_ref[:, c0:c1])) * _dot(h, w3_ref[:, c0:c1])
        yc = _dot(a.astype(BF16), w2_ref[c0:c1, :])
        y = yc if y is None else y + yc
    o_ref[...] = x + m_ref[0, 5:6, :] * y


def _ffn(x, mods_l, g, w1, w3, w2):
    once = pl.Buffered(1)
    return pl.pallas_call(
        _ffn_body,
        grid=(R // FF_TM,),
        in_specs=[pl.BlockSpec((FF_TM, D), lambda i: (i, 0)),
                  pl.BlockSpec((1, 6, D), lambda i: (_mod_group(i * FF_TM), 0, 0)),
                  pl.BlockSpec((1, D), lambda i: (0, 0)),
                  pl.BlockSpec((D, FF_DENSE), lambda i: (0, 0), pipeline_mode=once),
                  pl.BlockSpec((D, FF_DENSE), lambda i: (0, 0), pipeline_mode=once),
                  pl.BlockSpec((FF_DENSE, D), lambda i: (0, 0), pipeline_mode=once)],
        out_specs=pl.BlockSpec((FF_TM, D), lambda i: (i, 0)),
        out_shape=jax.ShapeDtypeStruct((R, D), F32),
        compiler_params=_cp(1),
        name="ffn",
    )(x, mods_l, g, w1, w3, w2)


RT_TM = 512
MOE_TG, MOE_TF = 512, 1792
MOE_NT = 2 * R // MOE_TG + N_EXP
MOE_ROWS = MOE_NT * MOE_TG
CB_TM = 256


def _router_body(x_ref, m_ref, g_ref, rw_ref, rb_ref, hn_ref, route_ref, cnt_ref, base_scr):
    @pl.when(pl.program_id(0) == 0)
    def _():
        base_scr[...] = jnp.zeros_like(base_scr)

    h = _modnorm(x_ref[...], g_ref[...], m_ref[0, 3:4, :], m_ref[0, 4:5, :])
    hn_ref[...] = h
    logits = _dot3(h, rw_ref[...]) + rb_ref[...]
    lane = lax.broadcasted_iota(jnp.int32, logits.shape, 1)
    m1 = jnp.max(logits, axis=-1, keepdims=True)
    i1 = jnp.min(jnp.where(logits == m1, lane, 128), axis=-1, keepdims=True)
    sel1 = lane == i1
    rest = jnp.where(sel1, -jnp.inf, logits)
    m2 = jnp.max(rest, axis=-1, keepdims=True)
    i2 = jnp.min(jnp.where(rest == m2, lane, 128), axis=-1, keepdims=True)
    sel2 = lane == i2
    e2 = jnp.exp(m2 - m1)
    p1 = 1.0 / (1.0 + e2)
    p2 = e2 / (1.0 + e2)

    cnt = jnp.where(jnp.logical_or(sel1, sel2), 1.0, 0.0)
    r = lax.broadcasted_iota(jnp.int32, (RT_TM, RT_TM), 0)
    c = lax.broadcasted_iota(jnp.int32, (RT_TM, RT_TM), 1)
    before = jnp.where(r > c, 1.0, 0.0).astype(BF16)
    seen = base_scr[...] + _dot(before, cnt.astype(BF16))
    rank1 = jnp.sum(jnp.where(sel1, seen, 0.0), axis=-1, keepdims=True)
    rank2 = jnp.sum(jnp.where(sel2, seen, 0.0), axis=-1, keepdims=True)
    vals = (i1.astype(F32), i2.astype(F32), rank1, rank2, p1, p2)
    route = jnp.zeros(logits.shape, F32)
    for k, val in enumerate(vals):
        route = jnp.where(lane == k, val, route)
    route_ref[...] = route
    base_scr[...] += jnp.sum(cnt, axis=0, keepdims=True)
    cnt_ref[...] = base_scr[...]


def _router(x, mods_l, g, rw128, rb128):
    return pl.pallas_call(
        _router_body,
        grid=(R // RT_TM,),
        in_specs=[pl.BlockSpec((RT_TM, D), lambda i: (i, 0)),
                  pl.BlockSpec((1, 6, D), lambda i: (_mod_group(i * RT_TM), 0, 0)),
                  pl.BlockSpec((1, D), lambda i: (0, 0)),
                  pl.BlockSpec((D, 128), lambda i: (0, 0)),
                  pl.BlockSpec((1, 128), lambda i: (0, 0))],
        out_specs=[pl.BlockSpec((RT_TM, D), lambda i: (i, 0)),
                   pl.BlockSpec((RT_TM, 128), lambda i: (i, 0)),
                   pl.BlockSpec((1, 128), lambda i: (0, 0))],
        out_shape=[jax.ShapeDtypeStruct((R, D), F32),
                   jax.ShapeDtypeStruct((R, 128), F32),
                   jax.ShapeDtypeStruct((1, 128), F32)],
        scratch_shapes=[pltpu.VMEM((1, 128), F32)],
        compiler_params=_cp(1),
        name="moe_router",
    )(x, mods_l, g, rw128, rb128)


def _rows_copy(src_hbm, dst, sem, n):
    return pltpu.make_async_copy(src_hbm.at[pl.ds(0, n)], dst, sem)


def _start_row_gather(idx_ref, n, src_hbm, dst, sem, inline=False, both_queues=False):
    def issue(i, priority):
        pltpu.make_async_copy(src_hbm.at[pl.ds(idx_ref[0, 0, i], 1)], dst.at[pl.ds(i, 1)], sem).start(
            priority=priority)

    if inline:
        for i in range(n):
            issue(i, i % 2 if both_queues else 0)
    else:
        def body(i, carry):
            issue(i, 0)
            return carry

        lax.fori_loop(0, n, body, 0, unroll=8)


def _gather_rows(idx_ref, n, src_hbm, dst, sem):
    _start_row_gather(idx_ref, n, src_hbm, dst, sem, inline=True, both_queues=True)
    _rows_copy(src_hbm, dst, sem, n).wait()


def _experts_body(te_ref, tv_ref, src_ref, nsrc_ref, hn_hbm, w1_ref, w3_ref, w2_ref, ys_ref, xg_scr, sems):
    t = pl.program_id(0)
    slot = t % 2
    valid = tv_ref[t] > 0
    requested = jnp.where(t == 0, valid, tv_ref[jnp.maximum(t - 1, 0)] > 0)

    @pl.when(jnp.logical_and(t == 0, valid))
    def _():
        _start_row_gather(src_ref, MOE_TG, hn_hbm, xg_scr.at[0], sems.at[0])

    @pl.when(requested)
    def _():
        _rows_copy(hn_hbm, xg_scr.at[slot], sems.at[slot], MOE_TG).wait()

    @pl.when(valid)
    def _():
        xb = xg_scr[slot].astype(BF16)
        _start_row_gather(nsrc_ref, MOE_TG, hn_hbm, xg_scr.at[1 - slot], sems.at[1 - slot], inline=True)
        y = None
        for c in range(FF_EXP // MOE_TF):
            cs = slice(c * MOE_TF, (c + 1) * MOE_TF)
            a = _silu(_dot(xb, w1_ref[0, :, cs])) * _dot(xb, w3_ref[0, :, cs])
            yc = _dot(a.astype(BF16), w2_ref[0, cs, :])
            y = yc if y is None else y + yc
        ys_ref[...] = y

    @pl.when(jnp.logical_not(valid))
    def _():
        ys_ref[...] = jnp.zeros_like(ys_ref)

    @pl.when(jnp.logical_and(t == MOE_NT - 1, valid))
    def _():
        _rows_copy(hn_hbm, xg_scr.at[1 - slot], sems.at[1 - slot], MOE_TG).wait()


def _experts(tile_e, tile_v, src, hn, w1, w3, w2):
    once = pl.Buffered(1)
    grid_spec = pltpu.PrefetchScalarGridSpec(
        num_scalar_prefetch=2,
        grid=(MOE_NT,),
        in_specs=[pl.BlockSpec((1, 1, MOE_TG), lambda t, te, tv: (t, 0, 0), memory_space=pltpu.SMEM),
                  pl.BlockSpec((1, 1, MOE_TG), lambda t, te, tv: (t + 1, 0, 0), memory_space=pltpu.SMEM),
                  pl.BlockSpec(memory_space=pl.ANY),
                  pl.BlockSpec((1, D, FF_EXP), lambda t, te, tv: (te[t], 0, 0), pipeline_mode=once),
                  pl.BlockSpec((1, D, FF_EXP), lambda t, te, tv: (te[t], 0, 0), pipeline_mode=once),
                  pl.BlockSpec((1, FF_EXP, D), lambda t, te, tv: (te[t], 0, 0), pipeline_mode=once)],
        out_specs=pl.BlockSpec((MOE_TG, D), lambda t, te, tv: (t, 0)),
        scratch_shapes=[pltpu.VMEM((2, MOE_TG, D), F32), pltpu.SemaphoreType.DMA((2,))],
    )
    return pl.pallas_call(
        _experts_body,
        grid_spec=grid_spec,
        out_shape=jax.ShapeDtypeStruct((MOE_ROWS, D), F32),
        compiler_params=_cp(1),
        name="moe_experts",
    )(tile_e, tile_v, src, src, hn, w1, w3, w2)


def _combine_body(pos_ref, x_ref, m_ref, route_ref, ys_hbm, o_ref, buf, sem):
    _gather_rows(pos_ref, 2 * CB_TM, ys_hbm, buf, sem)
    route = route_ref[...]
    y = route[:, 4:5] * buf[0:CB_TM, :] + route[:, 5:6] * buf[CB_TM:2 * CB_TM, :]
    o_ref[...] = x_ref[...] + m_ref[0, 5:6, :] * y


def _combine(pos, x, mods_l, route, ys):
    return pl.pallas_call(
        _combine_body,
        grid=(R // CB_TM,),
        in_specs=[pl.BlockSpec((1, 1, 2 * CB_TM), lambda i: (i, 0, 0), memory_space=pltpu.SMEM),
                  pl.BlockSpec((CB_TM, D), lambda i: (i, 0)),
                  pl.BlockSpec((1, 6, D), lambda i: (_mod_group(i * CB_TM), 0, 0)),
                  pl.BlockSpec((CB_TM, 128), lambda i: (i, 0)),
                  pl.BlockSpec(memory_space=pl.ANY)],
        out_specs=pl.BlockSpec((CB_TM, D), lambda i: (i, 0)),
        out_shape=jax.ShapeDtypeStruct((R, D), F32),
        scratch_shapes=[pltpu.VMEM((2 * CB_TM, D), F32), pltpu.SemaphoreType.DMA(())],
        compiler_params=_cp(1),
        name="moe_combine",
    )(pos, x, mods_l, route, ys)


def _moe(x, mods_l, g, rw128, rb128, w1, w3, w2):
    hn, route, cnt = _router(x, mods_l, g, rw128, rb128)
    eid = route[:, 0:2].astype(jnp.int32)
    rank = route[:, 2:4].astype(jnp.int32)
    counts = cnt[0, :N_EXP].astype(jnp.int32)
    gsize = (counts + MOE_TG - 1) // MOE_TG * MOE_TG
    gend = jnp.cumsum(gsize)
    pos = (gend - gsize)[eid] + rank
    tile_start = jnp.arange(MOE_NT, dtype=jnp.int32) * MOE_TG
    tile_e = jnp.minimum(jnp.sum(tile_start[:, None] >= gend[None, :], axis=1), N_EXP - 1).astype(jnp.int32)
    tile_v = (tile_start < gend[-1]).astype(jnp.int32)
    tok = jnp.broadcast_to(jnp.arange(R, dtype=jnp.int32)[:, None], (R, 2))
    src = jnp.zeros((MOE_ROWS + MOE_TG,), jnp.int32).at[pos.reshape(-1)].set(tok.reshape(-1))
    ys = _experts(tile_e, tile_v, src.reshape(MOE_NT + 1, 1, MOE_TG), hn, w1, w3, w2)
    pos_t = pos.reshape(R // CB_TM, CB_TM, 2).transpose(0, 2, 1).reshape(R // CB_TM, 1, 2 * CB_TM)
    return _combine(pos_t, x, mods_l, route, ys)


FN_TM = 1024


def _final_body(x_ref, g_ref, o_ref):
    x = x_ref[...]
    ms = jnp.mean(x * x, axis=-1, keepdims=True)
    o_ref[...] = x * lax.rsqrt(ms + EPS) * g_ref[...]


def _final_norm(x, g, row_off, n_rows):
    bo = row_off // FN_TM
    return pl.pallas_call(
        _final_body,
        grid=(n_rows // FN_TM,),
        in_specs=[pl.BlockSpec((FN_TM, D), lambda i: (bo + i, 0)), pl.BlockSpec((1, D), lambda i: (0, 0))],
        out_specs=pl.BlockSpec((FN_TM, D), lambda i: (i, 0)),
        out_shape=jax.ShapeDtypeStruct((n_rows, D), F32),
        compiler_params=_cp(1),
        name="final_norm",
    )(x, g)


def _rope_tables(n_tokens, rot_dim):
    t = np.arange(n_tokens)
    row = (t // GRID_W).astype(np.float32)
    col = (t % GRID_W).astype(np.float32)
    n_freq = rot_dim // 4
    inv = (ROPE_THETA ** (-jnp.arange(n_freq, dtype=F32) / n_freq))
    ang = jnp.concatenate([jnp.asarray(row)[:, None] * inv, jnp.asarray(col)[:, None] * inv], axis=-1)
    cos, sin = jnp.cos(ang), jnp.sin(ang)
    return jnp.concatenate([cos, cos], axis=-1), jnp.concatenate([-sin, sin], axis=-1)


def kernel(x_prompt, x_sample, c, cache_gqa_k, cache_gqa_v, cache_mla_ckv, cache_mla_krope, state_delta, c_ctx, w_mod, b_mod, norm1_g, norm2_g, w_in, dn_conv_w, dn_a_log, dn_dt_bias, dn_norm_g, gqa_q_norm, gqa_k_norm, mla_q_norm, mla_kv_norm, mla_w_uq, mla_w_ukv, w_branch, w_out, ffd_w1, ffd_w3, ffd_w2, router_w, router_b, moe_w1, moe_w3, moe_w2, final_g):
    x = jnp.concatenate([x_prompt.reshape(RC, D), x_sample.reshape(RL, D)], axis=0)
    cond8 = jnp.concatenate([c_ctx[None, :], c, jnp.zeros((3, D), F32)], axis=0)
    mods = _mods(cond8, w_mod, b_mod).reshape(DEPTH, 8, 6, D)

    cg, sg = _rope_tables(DEC_SEQ, GQA_HD)
    gqa_rope = (jnp.tile(cg, (1, GQA_H)), jnp.tile(sg, (1, GQA_H)),
                jnp.tile(cg, (1, GQA_KV)), jnp.tile(sg, (1, GQA_KV)))
    cm, sm_ = _rope_tables(DEC_SEQ, MLA_ROPE)
    padk = lambda t: jnp.pad(t, ((0, 0), (16, 128 - 16 - MLA_ROPE)))
    mla_rope = (jnp.tile(cm, (1, MLA_H)), jnp.tile(sm_, (1, MLA_H)),
                jnp.pad(cm, ((0, 0), (16, 128 - 16 - MLA_ROPE)), constant_values=1.0), padk(sm_))

    new_k, new_v, new_ckv, new_kr, new_s = [], [], [], [], []
    for l in range(DEPTH):
        pbg, proj = _inproj(x, mods[l], norm1_g[l][None, :], w_in[l].astype(BF16))

        pad128 = lambda v: jnp.pad(v.reshape(1, 8), ((0, 0), (8, 112)))
        padt = lambda v: jnp.pad(v.reshape(8, 1), ((8, 0), (0, 0)))
        sm_t = proj[:, C_SM:C_SM + 16].T
        dn_u, dn_w, dn_qg, dn_kd, dn_ai, dn_bg = _dn_prep(
            proj, sm_t, dn_conv_w[l], pad128(dn_a_log[l]), pad128(dn_dt_bias[l]),
            padt(dn_a_log[l]), padt(dn_dt_bias[l]))
        o_dn_f, o_dn_b, s_all = _dn_scan(dn_u, dn_w, dn_qg, dn_kd, dn_ai, dn_bg, state_delta[:, l])
        s_c = s_all[:BATCH]

        qw = jnp.tile(gqa_q_norm[l][None, :], (1, GQA_H))
        kw = jnp.tile(gqa_k_norm[l][None, :], (1, GQA_KV))
        o_g_c, kn_c = _gqa(proj, qw, kw, SEQ, SEQ, BATCH, 0)
        (o_g_l,) = _gqa(proj, qw, kw, DEC_SEQ, 256, DEC_BATCH, RC, rope=gqa_rope,
                        cache=(cache_gqa_k[:, l].reshape(DEC_BATCH, PAST, 128),
                               cache_gqa_v[:, l].reshape(DEC_BATCH, PAST, 128)))

        wq = mla_w_uq[l].reshape(MLA_QL, MLA_H, MLA_DQ)
        wq = jnp.concatenate([wq[:, :, :MLA_NOPE].reshape(MLA_QL, -1), wq[:, :, MLA_NOPE:].reshape(MLA_QL, -1)],
                             axis=1).astype(BF16)
        wkv = mla_w_ukv[l].reshape(MLA_KVL, MLA_H, MLA_NOPE + MLA_V)
        wkv = jnp.concatenate([wkv[:, :, :MLA_NOPE].reshape(MLA_KVL, -1), wkv[:, :, MLA_NOPE:].reshape(MLA_KVL, -1)],
                              axis=1).astype(BF16)
        mqw, mkvw = mla_q_norm[l][None, :], mla_kv_norm[l][None, :]
        o_m_c, ckv_c = _mla(proj, wq, wkv, mqw, mkvw, SEQ, SEQ, BATCH, 0)
        (o_m_l,) = _mla(proj, wq, wkv, mqw, mkvw, DEC_SEQ, 256, DEC_BATCH, RC, rope=mla_rope,
                        cache=(cache_mla_ckv[:, l], cache_mla_krope[:, l]))
        x = _merge(x, mods[l], o_dn_f, o_dn_b, proj, pbg, o_g_c, o_g_l, o_m_c, o_m_l, dn_norm_g[l][None, :],
                   w_branch[l].astype(BF16), w_out[l].astype(BF16))

        j = l // 2
        if l % 2 == 0:
            x = _ffn(x, mods[l], norm2_g[l][None, :], ffd_w1[j].astype(BF16), ffd_w3[j].astype(BF16),
                     ffd_w2[j].astype(BF16))
        else:
            rw128 = jnp.pad(router_w[j], ((0, 0), (0, 128 - N_EXP)))
            rb128 = jnp.pad(router_b[j][None, :], ((0, 0), (0, 128 - N_EXP)), constant_values=-jnp.inf)
            x = _moe(x, mods[l], norm2_g[l][None, :], rw128, rb128, moe_w1[j].astype(BF16),
                     moe_w3[j].astype(BF16), moe_w2[j].astype(BF16))

        new_k.append(kn_c.reshape(BATCH, SEQ, GQA_KV, GQA_HD))
        new_v.append(proj[:RC, C_GKV + 128:C_GKV + 256].reshape(BATCH, SEQ, GQA_KV, GQA_HD))
        new_ckv.append(ckv_c.reshape(BATCH, SEQ, MLA_KVL))
        new_kr.append(proj[:RC, C_SM + 16:C_SM + 16 + MLA_ROPE].reshape(BATCH, SEQ, MLA_ROPE))
        new_s.append(s_c)

    y_c = _final_norm(x, final_g[None, :], 0, RC)
    y_l = _final_norm(x, final_g[None, :], RC, RL)
    return (y_c.reshape(BATCH, SEQ, D), y_l.reshape(DEC_BATCH, DEC_SEQ, D),
            jnp.stack(new_k, axis=1), jnp.stack(new_v, axis=1), jnp.stack(new_ckv, axis=1),
            jnp.stack(new_kr, axis=1), jnp.stack(new_s, axis=1))
```

```python
import functools
import math

import jax
import jax.numpy as jnp
import numpy as np
from jax import lax
from jax.experimental import pallas as pl
from jax.experimental.pallas import tpu as pltpu

F32 = jnp.float32
BF16 = jnp.bfloat16

D = 1024
BATCH, SEQ = 32, 256
DEC_BATCH, DEC_SEQ = 4, 2048
DEPTH = 4
PAST = 512
GRID_W = 64
ROPE_THETA = 10000.0
EPS = 1e-6
DN_H, DN_DK, DN_DV, DN_C = 4, 128, 128, 64
GQA_H, GQA_KV, GQA_HD = 8, 2, 64
MLA_H, MLA_QL, MLA_KVL, MLA_NOPE, MLA_ROPE, MLA_V = 8, 256, 128, 64, 32, 64
FF_DENSE, N_EXP, FF_EXP = 2816, 8, 3584

RC = BATCH * SEQ
RL = DEC_BATCH * DEC_SEQ
R = RC + RL

N_BG = 3 * D
C_QKV, C_DG, C_GQ, C_GKV, C_MCQ, C_MCKV, C_SM = 0, 1536, 2048, 2560, 2816, 3072, 3200
N_PROJ = 3328
NP = N_BG + N_PROJ

VMEM_LIMIT = 56 * 1024 * 1024


def _cp(n_grid):
    return pltpu.CompilerParams(dimension_semantics=("arbitrary",) * n_grid,
                                vmem_limit_bytes=VMEM_LIMIT)


def _dot(a, b):
    return jnp.dot(a, b, preferred_element_type=F32)


def _bdot(a, b):
    return jnp.dot(a.astype(BF16), b.astype(BF16), preferred_element_type=F32)


def _bdot_nt(a, b):
    return lax.dot_general(a.astype(BF16), b.astype(BF16), (((1,), (1,)), ((), ())),
                           preferred_element_type=F32)


def _bdot_tn(a, b):
    return lax.dot_general(a.astype(BF16), b.astype(BF16), (((0,), (0,)), ((), ())),
                           preferred_element_type=F32)


def _split2(a):
    hi = a.astype(BF16)
    lo = (a - hi.astype(F32)).astype(BF16)
    return hi, lo


def _split3(a):
    a1 = a.astype(BF16)
    r1 = a - a1.astype(F32)
    a2 = r1.astype(BF16)
    a3 = (r1 - a2.astype(F32)).astype(BF16)
    return a1, a2, a3


def _dot3(a, b):
    ah, al = _split2(a)
    bh, bl = _split2(b)
    return _dot(ah, bh) + (_dot(ah, bl) + _dot(al, bh))


def _mask_dot_r(mask_bf, g):
    g1, g2, g3 = _split3(g)
    return _dot(mask_bf, g1) + (_dot(mask_bf, g2) + _dot(mask_bf, g3))


def _mask_dot_l(g, mask_bf):
    g1, g2, g3 = _split3(g)
    return _dot(g1, mask_bf) + (_dot(g2, mask_bf) + _dot(g3, mask_bf))


def _sigmoid(x):
    return 1.0 / (1.0 + jnp.exp(-x))


def _silu(x):
    return x * _sigmoid(x)


def _softplus(x):
    return jnp.maximum(x, 0.0) + jnp.log1p(jnp.exp(-jnp.abs(x)))


def _mod_group(row0):
    return jnp.where(row0 < RC, 0, 1 + (row0 - RC) // DEC_SEQ)


MODS_TN = 1536


def _mods_body(c_ref, w_ref, b_ref, o_ref):
    s = _silu(c_ref[...])
    o_ref[0] = _bdot(s, w_ref[0]) + b_ref[0]


def _mods(cond8, w_mod, b_mod):
    nj = 6 * D // MODS_TN
    return pl.pallas_call(
        _mods_body,
        grid=(DEPTH, nj),
        in_specs=[pl.BlockSpec((8, D), lambda l, j: (0, 0)),
                  pl.BlockSpec((1, D, MODS_TN), lambda l, j: (l, 0, j)),
                  pl.BlockSpec((1, 1, MODS_TN), lambda l, j: (l, 0, j))],
        out_specs=pl.BlockSpec((1, 8, MODS_TN), lambda l, j: (l, 0, j)),
        out_shape=jax.ShapeDtypeStruct((DEPTH, 8, 6 * D), F32),
        compiler_params=_cp(2),
        name="mods",
    )(cond8, w_mod, b_mod.reshape(DEPTH, 1, 6 * D))


IN_TM = 512
IN_CHUNK = 1536


def _modnorm(x, g, shift, scale):
    ms = jnp.mean(x * x, axis=-1, keepdims=True)
    y = x * lax.rsqrt(ms + EPS) * g
    return y * (1.0 + scale) + shift


W_IN_PIECES = ((3248, 6320), (0, 2048), (2064, 3216), (2048, 2064), (3216, 3248))
W_IN_COLS = 6320


def _inproj_body(x_ref, m_ref, g_ref, wsrc_ref, bg_ref, o_ref, w_ref):
    @pl.when(pl.program_id(0) == 0)
    def _():
        at = 0
        for a, b in W_IN_PIECES:
            w_ref[:, at:at + (b - a)] = wsrc_ref[0, :, a:b]
            at += b - a
        w_ref[:, at:NP] = jnp.zeros((D, NP - at), BF16)

    h = _modnorm(x_ref[...], g_ref[...], m_ref[0, 0:1, :], m_ref[0, 1:2, :]).astype(BF16)
    for c0 in range(0, N_BG, IN_CHUNK):
        bg_ref[:, c0:c0 + IN_CHUNK] = _dot(h, w_ref[:, c0:c0 + IN_CHUNK]).astype(BF16)
    for c0 in range(0, N_PROJ, IN_CHUNK):
        c1 = min(c0 + IN_CHUNK, N_PROJ)
        o_ref[:, c0:c1] = _dot(h, w_ref[:, N_BG + c0:N_BG + c1])


def _inproj(x, mods_l, g, w_bf, l):
    return pl.pallas_call(
        _inproj_body,
        grid=(R // IN_TM,),
        in_specs=[pl.BlockSpec((IN_TM, D), lambda i: (i, 0)),
                  pl.BlockSpec((1, 6, D), lambda i: (_mod_group(i * IN_TM), 0, 0)),
                  pl.BlockSpec((1, D), lambda i: (0, 0)),
                  pl.BlockSpec((1, D, W_IN_COLS), lambda i: (l, 0, 0), pipeline_mode=pl.Buffered(1))],
        out_specs=[pl.BlockSpec((IN_TM, N_BG), lambda i: (i, 0)),
                   pl.BlockSpec((IN_TM, N_PROJ), lambda i: (i, 0))],
        out_shape=[jax.ShapeDtypeStruct((R, N_BG), BF16), jax.ShapeDtypeStruct((R, N_PROJ), F32)],
        scratch_shapes=[pltpu.VMEM((D, NP), BF16)],
        compiler_params=_cp(1),
        name="inproj",
    )(x, mods_l, g, w_bf)


PREP_TM = 256
PREP_NC = PREP_TM // DN_C
PREP_LOCKSTEP = 4


def _prep_body(x_ref, xp_ref, xn_ref, sm_ref, smt_ref, cw_ref, al_ref, dt_ref, alt_ref, dtt_ref,
               u_o, w_o, qg_o, kd_o, ai_o, bg_o, qkv_o, gr_o):
    i = pl.program_id(0)
    n_ctx = RC // PREP_TM
    per_seq = DEC_SEQ // PREP_TM
    is_ctx = i < n_ctx
    j = (i - n_ctx) % per_seq
    first = jnp.logical_or(is_ctx, j == 0)
    last = jnp.logical_or(is_ctx, j == per_seq - 1)

    x = x_ref[...]
    prev_row = jnp.where(first, 0.0, xp_ref[7:8, :])
    next_row = jnp.where(last, 0.0, xn_ref[0:1, :])
    row = lax.broadcasted_iota(jnp.int32, (PREP_TM, 1), 0)
    xm = jnp.where(row == 0, prev_row, pltpu.roll(x, 1, axis=0))
    xq = jnp.where(row == PREP_TM - 1, next_row, pltpu.roll(x, PREP_TM - 1, axis=0))
    w = cw_ref[...]
    y = _silu(w[0:1] * xm + w[1:2] * x + w[2:3] * xq)

    for h in range(DN_H):
        qh = y[:, h * DN_DK:(h + 1) * DN_DK]
        qn = qh * lax.rsqrt(jnp.sum(qh * qh, axis=-1, keepdims=True) + EPS) * (DN_DK ** -0.5)
        qkv_o[:, h * DN_DK:(h + 1) * DN_DK] = qn
        kh = y[:, 512 + h * DN_DK:512 + (h + 1) * DN_DK]
        kn = kh * lax.rsqrt(jnp.sum(kh * kh, axis=-1, keepdims=True) + EPS)
        qkv_o[:, 512 + h * DN_DK:512 + (h + 1) * DN_DK] = kn
    qkv_o[:, 1024:1536] = y[:, 1024:1536]

    r = lax.broadcasted_iota(jnp.int32, (PREP_TM, PREP_TM), 0)
    c = lax.broadcasted_iota(jnp.int32, (PREP_TM, PREP_TM), 1)
    same = (r // DN_C) == (c // DN_C)
    low = jnp.where(jnp.logical_and(same, r >= c), 1.0, 0.0).astype(BF16)
    upp = jnp.where(jnp.logical_and(same, r <= c), 1.0, 0.0).astype(BF16)

    sm = sm_ref[...]
    beta = _sigmoid(sm)
    g = -jnp.exp(al_ref[...]) * _softplus(sm + dt_ref[...])
    gc_f = _mask_dot_r(low, g)
    gc_b = _mask_dot_r(upp, g)
    lane = lax.broadcasted_iota(jnp.int32, (1, 128), 1)
    bg_o[...] = jnp.where(lane < 8, beta, jnp.where(lane < 12, gc_f, jnp.where(lane < 16, gc_b, 0.0)))

    gt = -jnp.exp(alt_ref[...]) * _softplus(smt_ref[...] + dtt_ref[...])
    gct_f = _mask_dot_l(gt, upp)
    gct_b = _mask_dot_l(gt, low)
    sub = lax.broadcasted_iota(jnp.int32, (16, 1), 0)
    gct = jnp.where(sub < 12, gct_f, gct_b)
    for k in range(PREP_NC):
        gr_o[k] = gct[8:16, k * DN_C:(k + 1) * DN_C]

    rr = lax.broadcasted_iota(jnp.int32, (DN_C, DN_C), 0)
    cc = lax.broadcasted_iota(jnp.int32, (DN_C, DN_C), 1)

    def chunk_group(gi, carry):
        prob = [(cj, d, h) for cj in range(PREP_LOCKSTEP) for d in range(2) for h in range(DN_H)]
        ci = [gi * PREP_LOCKSTEP + cj for cj in range(PREP_LOCKSTEP)]
        rows_c = [pl.ds(pl.multiple_of(c_ * DN_C, DN_C), DN_C) for c_ in ci]
        bgc = [bg_o[r_, :] for r_ in rows_c]
        grow_all = [gr_o[c_] for c_ in ci]
        rows = [rows_c[cj] for cj, d, h in prob]
        q = [qkv_o[rows_c[cj], h * DN_DK:(h + 1) * DN_DK] for cj, d, h in prob]
        k = [qkv_o[rows_c[cj], 512 + h * DN_DK:512 + (h + 1) * DN_DK] for cj, d, h in prob]
        v = [qkv_o[rows_c[cj], 1024 + h * DN_DV:1024 + (h + 1) * DN_DV] for cj, d, h in prob]
        beta = [bgc[cj][:, d * DN_H + h:d * DN_H + h + 1] for cj, d, h in prob]
        gcol = [bgc[cj][:, 8 + d * DN_H + h:9 + d * DN_H + h] for cj, d, h in prob]
        grow = [grow_all[cj][d * DN_H + h:d * DN_H + h + 1, :] for cj, d, h in prob]
        incl = [(rr >= cc) if d == 0 else (rr <= cc) for cj, d, h in prob]
        strict = [(rr > cc) if d == 0 else (rr < cc) for cj, d, h in prob]
        n = len(prob)
        decay = [jnp.exp(jnp.where(incl[i], gcol[i] - grow[i], -1e30)) for i in range(n)]
        kb = [k[i] * beta[i] for i in range(n)]
        kk = [_bdot_nt(kb[i], k[i]) for i in range(n)]
        qk = [_bdot_nt(q[i], k[i]) for i in range(n)]
        a = [jnp.where(strict[i], kk[i] * decay[i], 0.0) for i in range(n)]
        blk = lambda b: (rr // b) == (cc // b)
        eye = jnp.where(rr == cc, 1.0, 0.0)
        p = [jnp.where(blk(8), a[i], 0.0) for i in range(n)]
        t = [eye - p[i] for i in range(n)]
        for _ in range(2):
            p = [_bdot(p[i], p[i]) for i in range(n)]
            t = [t[i] + _bdot(t[i], p[i]) for i in range(n)]
        for b in (16, 32, 64):
            m = jnp.logical_and(blk(b), jnp.logical_not(blk(b // 2)))
            tl = [_bdot(t[i], jnp.where(m, a[i], 0.0)) for i in range(n)]
            t = [t[i] - _bdot(tl[i], t[i]) for i in range(n)]
        egc = [jnp.exp(gcol[i]) for i in range(n)]
        glast = [gcol[i][DN_C - 1:DN_C, :] if prob[i][1] == 0 else gcol[i][0:1, :] for i in range(n)]
        uw = [_bdot(t[i], jnp.concatenate([v[i] * beta[i], kb[i] * egc[i]], axis=1)) for i in range(n)]
        for i, (cj, d, h) in enumerate(prob):
            u_o[d, rows[i], h * DN_DV:(h + 1) * DN_DV] = uw[i][:, 0:DN_DV]
            w_o[d, rows[i], h * DN_DK:(h + 1) * DN_DK] = uw[i][:, DN_DV:DN_DV + DN_DK].astype(BF16)
            qg_o[d, rows[i], h * DN_DK:(h + 1) * DN_DK] = (q[i] * egc[i]).astype(BF16)
            kd_o[d, rows[i], h * DN_DK:(h + 1) * DN_DK] = (k[i] * jnp.exp(glast[i] - gcol[i])).astype(BF16)
            ai_o[d, rows[i], h * DN_C:(h + 1) * DN_C] = (qk[i] * decay[i]).astype(BF16)
        return carry

    lax.fori_loop(0, PREP_NC // PREP_LOCKSTEP, chunk_group, 0)


def _dn_prep(proj, sm_t, conv_w, al128, dt128, al_t, dt_t):
    nb8 = R // 8
    qb = C_QKV // 1536
    wide = lambda dt: jax.ShapeDtypeStruct((2, R, DN_H * DN_DK), dt)
    wide_spec = pl.BlockSpec((2, PREP_TM, DN_H * DN_DK), lambda i: (0, i, 0))
    return pl.pallas_call(
        _prep_body,
        grid=(R // PREP_TM,),
        in_specs=[pl.BlockSpec((PREP_TM, 1536), lambda i: (i, qb)),
                  pl.BlockSpec((8, 1536), lambda i: (jnp.maximum(i * (PREP_TM // 8) - 1, 0), qb)),
                  pl.BlockSpec((8, 1536), lambda i: (jnp.minimum((i + 1) * (PREP_TM // 8), nb8 - 1), qb)),
                  pl.BlockSpec((PREP_TM, 128), lambda i: (i, C_SM // 128)),
                  pl.BlockSpec((16, PREP_TM), lambda i: (0, i)),
                  pl.BlockSpec((3, 1536), lambda i: (0, 0)),
                  pl.BlockSpec((1, 128), lambda i: (0, 0)),
                  pl.BlockSpec((1, 128), lambda i: (0, 0)),
                  pl.BlockSpec((16, 1), lambda i: (0, 0)),
                  pl.BlockSpec((16, 1), lambda i: (0, 0))],
        out_specs=[wide_spec, wide_spec, wide_spec, wide_spec,
                   pl.BlockSpec((2, PREP_TM, DN_H * DN_C), lambda i: (0, i, 0)),
                   pl.BlockSpec((PREP_TM, 128), lambda i: (i, 0))],
        out_shape=[wide(F32), wide(BF16), wide(BF16), wide(BF16),
                   jax.ShapeDtypeStruct((2, R, DN_H * DN_C), BF16),
                   jax.ShapeDtypeStruct((R, 128), F32)],
        scratch_shapes=[pltpu.VMEM((PREP_TM, 1536), F32), pltpu.VMEM((PREP_NC, 8, DN_C), F32)],
        compiler_params=_cp(1),
        name="dn_prep",
    )(proj, proj, proj, proj, sm_t, conv_w, al128, dt128, al_t, dt_t)


SCAN_CTX = RC // PREP_TM
SCAN_PER = DEC_SEQ // PREP_TM
SCAN_NSEQ = BATCH + DEC_BATCH


def _scan_bwd_block(i):
    j = i - SCAN_CTX
    return jnp.where(i < SCAN_CTX, i, SCAN_CTX + (j // SCAN_PER) * SCAN_PER + (SCAN_PER - 1 - j % SCAN_PER))


def _scan_body(uf, wf, qf, kf, af, bf, ub, wb, qb, kb_, ab, bb, s0_ref, of_ref, ob_ref, so_ref, s_scr):
    i = pl.program_id(0)
    is_ctx = i < SCAN_CTX
    j = (i - SCAN_CTX) % SCAN_PER
    first = jnp.logical_or(is_ctx, j == 0)
    last = jnp.logical_or(is_ctx, j == SCAN_PER - 1)

    @pl.when(first)
    def _():
        for d in range(2):
            for h in range(DN_H):
                s_scr[d, h] = jnp.where(is_ctx, 0.0, s0_ref[0, d, h])

    dirs = ((uf, wf, qf, kf, af, bf, of_ref), (ub, wb, qb, kb_, ab, bb, ob_ref))

    prob = [(d, h) for d in range(2) for h in range(DN_H)]
    tn = (((0,), (0,)), ((), ()))

    def step(n, carry):
        rows = [pl.ds(pl.multiple_of((n if d == 0 else PREP_NC - 1 - n) * DN_C, DN_C), DN_C) for d in range(2)]
        bgc = [dirs[d][5][rows[d], :] for d in range(2)]
        cols = [slice(h * DN_DK, (h + 1) * DN_DK) for d, h in prob]
        st = [s_scr[d, h] for d, h in prob]
        stb = [x.astype(BF16) for x in st]
        ws = [_dot(dirs[d][1][0, rows[d], cols[i]], stb[i]) for i, (d, h) in enumerate(prob)]
        qs = [_dot(dirs[d][2][0, rows[d], cols[i]], stb[i]) for i, (d, h) in enumerate(prob)]
        vb = [(dirs[d][0][0, rows[d], cols[i]] - ws[i]).astype(BF16) for i, (d, h) in enumerate(prob)]
        av = [_dot(dirs[d][4][0, rows[d], h * DN_C:(h + 1) * DN_C], vb[i]) for i, (d, h) in enumerate(prob)]
        kv = [lax.dot_general(dirs[d][3][0, rows[d], cols[i]], vb[i], tn, preferred_element_type=F32)
              for i, (d, h) in enumerate(prob)]
        for i, (d, h) in enumerate(prob):
            gcol = bgc[d][:, 8 + d * DN_H + h:9 + d * DN_H + h]
            glast = gcol[DN_C - 1:DN_C, :] if d == 0 else gcol[0:1, :]
            s_scr[d, h] = st[i] * jnp.exp(glast) + kv[i]
            dirs[d][6][rows[d], cols[i]] = qs[i] + av[i]
        return carry

    lax.fori_loop(0, PREP_NC, step, 0)

    @pl.when(last)
    def _():
        for d in range(2):
            for h in range(DN_H):
                so_ref[0, d, h] = s_scr[d, h]


def _dn_scan(u, w, qg, kd, ai, bg, s0):
    fwd3 = lambda i: (0, i, 0)
    bwd3 = lambda i: (1, _scan_bwd_block(i), 0)
    wide = lambda m: pl.BlockSpec((1, PREP_TM, DN_H * DN_DK), m)
    narrow = lambda m: pl.BlockSpec((1, PREP_TM, DN_H * DN_C), m)
    state_spec = lambda m: pl.BlockSpec((1, 2, DN_H, DN_DK, DN_DV), m)
    seq_of = lambda i: jnp.where(i < SCAN_CTX, i, SCAN_CTX + (i - SCAN_CTX) // SCAN_PER)
    return pl.pallas_call(
        _scan_body,
        grid=(R // PREP_TM,),
        in_specs=[wide(fwd3), wide(fwd3), wide(fwd3), wide(fwd3), narrow(fwd3),
                  pl.BlockSpec((PREP_TM, 128), lambda i: (i, 0)),
                  wide(bwd3), wide(bwd3), wide(bwd3), wide(bwd3), narrow(bwd3),
                  pl.BlockSpec((PREP_TM, 128), lambda i: (_scan_bwd_block(i), 0)),
                  state_spec(lambda i: (jnp.maximum(i - SCAN_CTX, 0) // SCAN_PER, 0, 0, 0, 0))],
        out_specs=[pl.BlockSpec((PREP_TM, DN_H * DN_DV), lambda i: (i, 0)),
                   pl.BlockSpec((PREP_TM, DN_H * DN_DV), lambda i: (_scan_bwd_block(i), 0)),
                   state_spec(lambda i: (seq_of(i), 0, 0, 0, 0))],
        out_shape=[jax.ShapeDtypeStruct((R, DN_H * DN_DV), F32),
                   jax.ShapeDtypeStruct((R, DN_H * DN_DV), F32),
                   jax.ShapeDtypeStruct((SCAN_NSEQ, 2, DN_H, DN_DK, DN_DV), F32)],
        scratch_shapes=[pltpu.VMEM((2, DN_H, DN_DK, DN_DV), F32)],
        compiler_params=_cp(1),
        name="dn_scan",
    )(u, w, qg, kd, ai, bg, u, w, qg, kd, ai, bg, s0)


def _group_mean_matrix(width, group):
    r = lax.broadcasted_iota(jnp.int32, (width, width), 0)
    c = lax.broadcasted_iota(jnp.int32, (width, width), 1)
    return jnp.where((r // group) == (c // group), 1.0 / group, 0.0).astype(BF16)


def _group_rmsnorm(x, w, group):
    m = _group_mean_matrix(x.shape[-1], group)
    hi, lo = _split2(x * x)
    ms = _dot(hi, m) + _dot(lo, m)
    return x * lax.rsqrt(ms + EPS) * w


def _rope(x, cos, sin_signed, group):
    width = x.shape[-1]
    half = group // 2
    lane = lax.broadcasted_iota(jnp.int32, (1, width), 1)
    swapped = jnp.where((lane % group) < half,
                        pltpu.roll(x, width - half, axis=1), pltpu.roll(x, half, axis=1))
    return x * cos + swapped * sin_signed


def _attention_units(n, scores, values):
    outs = []
    s_next = scores(0)
    for u in range(n):
        s = s_next
        if u + 1 < n:
            s_next = scores(u + 1)
        m = jnp.max(s, axis=-1, keepdims=True)
        p = jnp.exp(s - m)
        l = jnp.sum(p, axis=-1, keepdims=True)
        outs.append(_dot(p.astype(BF16), values(u)) / l)
    return outs


GQA_G = GQA_H // GQA_KV


def _gqa_body(T, TQ, latent, *refs):
    it = iter(refs)
    q_ref, kv_ref, qw_ref, kw_ref = next(it), next(it), next(it), next(it)
    if latent:
        cq_ref, sq_ref, ck_ref, sk_ref, kc_ref, vc_ref = (next(it) for _ in range(6))
    o_ref = next(it)
    kn_ref = None if latent else next(it)
    k_scr, v_scr = next(it), next(it)

    @pl.when(pl.program_id(1) == 0)
    def _():
        kv = kv_ref[...]
        k = _group_rmsnorm(kv[:, 0:128], kw_ref[...], GQA_HD)
        v = kv[:, 128:256]
        if latent:
            k = _rope(k, ck_ref[...], sk_ref[...], GQA_HD)
        else:
            kn_ref[...] = k
        for g in range(GQA_KV):
            k_scr[g, 0:T, :] = k[:, g * GQA_HD:(g + 1) * GQA_HD].astype(BF16)
            v_scr[g, 0:T, :] = v[:, g * GQA_HD:(g + 1) * GQA_HD].astype(BF16)
            if latent:
                k_scr[g, T:T + PAST, :] = kc_ref[0, :, g * GQA_HD:(g + 1) * GQA_HD].astype(BF16)
                v_scr[g, T:T + PAST, :] = vc_ref[0, :, g * GQA_HD:(g + 1) * GQA_HD].astype(BF16)

    q = _group_rmsnorm(q_ref[...], qw_ref[...], GQA_HD)
    if latent:
        q = _rope(q, cq_ref[...], sq_ref[...], GQA_HD)
    q = q * (GQA_HD ** -0.5)
    head = lambda hh: q[:, hh * GQA_HD:(hh + 1) * GQA_HD]
    qu = [jnp.concatenate([head(2 * u), head(2 * u + 1)], axis=0) for u in range(GQA_H // 2)]
    grp = lambda u: (2 * u) // GQA_G
    outs = _attention_units(GQA_H // 2, lambda u: _bdot_nt(qu[u], k_scr[grp(u)]), lambda u: v_scr[grp(u)])
    for u, o in enumerate(outs):
        for j in range(2):
            hh = 2 * u + j
            o_ref[:, hh * GQA_HD:(hh + 1) * GQA_HD] = o[j * TQ:(j + 1) * TQ, :].astype(o_ref.dtype)


def _gqa(proj, qw, kw, T, TQ, n_seq, row_off, rope=None, cache=None):
    latent = rope is not None
    tk = T + (PAST if latent else 0)
    nq = T // TQ
    qo = row_off // TQ
    so = row_off // T
    in_specs = [pl.BlockSpec((TQ, 512), lambda i, j: (qo + i * nq + j, C_GQ // 512)),
                pl.BlockSpec((T, 256), lambda i, j: (so + i, C_GKV // 256)),
                pl.BlockSpec((1, 512), lambda i, j: (0, 0)),
                pl.BlockSpec((1, 128), lambda i, j: (0, 0))]
    args = [proj, proj, qw, kw]
    if latent:
        cq, sq, ck, sk = rope
        kc, vc = cache
        in_specs += [pl.BlockSpec((TQ, 512), lambda i, j: (j, 0)),
                     pl.BlockSpec((TQ, 512), lambda i, j: (j, 0)),
                     pl.BlockSpec((T, 128), lambda i, j: (0, 0)),
                     pl.BlockSpec((T, 128), lambda i, j: (0, 0)),
                     pl.BlockSpec((1, PAST, 128), lambda i, j: (i, 0, 0)),
                     pl.BlockSpec((1, PAST, 128), lambda i, j: (i, 0, 0))]
        args += [cq, sq, ck, sk, kc, vc]
    out_specs = [pl.BlockSpec((TQ, 512), lambda i, j: (i * nq + j, 0))]
    out_shape = [jax.ShapeDtypeStruct((n_seq * T, 512), BF16)]
    if not latent:
        out_specs.append(pl.BlockSpec((T, 128), lambda i, j: (i, 0)))
        out_shape.append(jax.ShapeDtypeStruct((n_seq * T, 128), F32))
    return pl.pallas_call(
        functools.partial(_gqa_body, T, TQ, latent),
        grid=(n_seq, nq),
        in_specs=in_specs,
        out_specs=out_specs,
        out_shape=out_shape,
        scratch_shapes=[pltpu.VMEM((GQA_KV, tk, GQA_HD), BF16), pltpu.VMEM((GQA_KV, tk, GQA_HD), BF16)],
        compiler_params=_cp(2),
        name="gqa_%d" % T,
    )(*args)


MLA_DQ = MLA_NOPE + MLA_ROPE
MLA_KV_ROWS = 512


def _mla_body(T, TQ, latent, *refs):
    it = iter(refs)
    cq_ref, ckv_ref, sm_ref, wq_ref, wkv_ref, qw_ref, kvw_ref = (next(it) for _ in range(7))
    if latent:
        cosq_ref, sinq_ref, cosk_ref, sink_ref, cc_ref, kc_ref = (next(it) for _ in range(6))
    o_ref = next(it)
    cn_ref = None if latent else next(it)
    k_scr, v_scr = next(it), next(it)
    tk = T + (PAST if latent else 0)

    @pl.when(pl.program_id(1) == 0)
    def _():
        x = ckv_ref[...]
        ms = jnp.mean(x * x, axis=-1, keepdims=True)
        ckv = x * lax.rsqrt(ms + EPS) * kvw_ref[...]
        sm = sm_ref[...]
        if latent:
            sm = _rope_small(sm, cosk_ref[...], sink_ref[...])
        else:
            cn_ref[...] = ckv
        kr = sm[:, 16:16 + MLA_ROPE]

        def put(rows0, ckv_rows, kr_rows):
            n = ckv_rows.shape[0]
            kv = _bdot(ckv_rows, wkv_ref[...])
            krb = kr_rows.astype(BF16)
            for h in range(MLA_H):
                k_scr[h, rows0:rows0 + n, 0:MLA_NOPE] = kv[:, h * MLA_NOPE:(h + 1) * MLA_NOPE].astype(BF16)
                k_scr[h, rows0:rows0 + n, MLA_NOPE:MLA_DQ] = krb
                v_scr[h, rows0:rows0 + n, :] = kv[:, 512 + h * MLA_V:512 + (h + 1) * MLA_V].astype(BF16)

        for r0 in range(0, T, MLA_KV_ROWS):
            r1 = min(r0 + MLA_KV_ROWS, T)
            put(r0, ckv[r0:r1], kr[r0:r1])
        if latent:
            put(T, cc_ref[0], kc_ref[0])

    x = cq_ref[...]
    ms = jnp.mean(x * x, axis=-1, keepdims=True)
    cq = x * lax.rsqrt(ms + EPS) * qw_ref[...]
    qf = _bdot(cq, wq_ref[...])
    qn = qf[:, 0:512]
    qr = qf[:, 512:768]
    if latent:
        qr = _rope(qr, cosq_ref[...], sinq_ref[...], MLA_ROPE)
    scale = MLA_DQ ** -0.5
    qh = [jnp.concatenate([qn[:, h * MLA_NOPE:(h + 1) * MLA_NOPE],
                           qr[:, h * MLA_ROPE:(h + 1) * MLA_ROPE]], axis=1) * scale for h in range(MLA_H)]
    outs = _attention_units(MLA_H, lambda h: _bdot_nt(qh[h], k_scr[h]), lambda h: v_scr[h])
    for h, o in enumerate(outs):
        o_ref[:, h * MLA_V:(h + 1) * MLA_V] = o.astype(o_ref.dtype)


def _rope_small(sm, cos, sin_signed):
    lane = lax.broadcasted_iota(jnp.int32, (1, 128), 1)
    half = MLA_ROPE // 2
    swapped = jnp.where(lane < 16 + half, pltpu.roll(sm, 128 - half, axis=1), pltpu.roll(sm, half, axis=1))
    return sm * cos + swapped * sin_signed


def _mla(proj, wq, wkv, qw, kvw, T, TQ, n_seq, row_off, rope=None, cache=None):
    latent = rope is not None
    tk = T + (PAST if latent else 0)
    nq = T // TQ
    qo = row_off // TQ
    so = row_off // T
    in_specs = [pl.BlockSpec((TQ, 256), lambda i, j: (qo + i * nq + j, C_MCQ // 256)),
                pl.BlockSpec((T, 128), lambda i, j: (so + i, C_MCKV // 128)),
                pl.BlockSpec((T, 128), lambda i, j: (so + i, C_SM // 128)),
                pl.BlockSpec((MLA_QL, 768), lambda i, j: (0, 0)),
                pl.BlockSpec((MLA_KVL, 1024), lambda i, j: (0, 0)),
                pl.BlockSpec((1, 256), lambda i, j: (0, 0)),
                pl.BlockSpec((1, 128), lambda i, j: (0, 0))]
    args = [proj, proj, proj, wq, wkv, qw, kvw]
    if latent:
        cosq, sinq, cosk, sink = rope
        cc, kc = cache
        in_specs += [pl.BlockSpec((TQ, 256), lambda i, j: (j, 0)),
                     pl.BlockSpec((TQ, 256), lambda i, j: (j, 0)),
                     pl.BlockSpec((T, 128), lambda i, j: (0, 0)),
                     pl.BlockSpec((T, 128), lambda i, j: (0, 0)),
                     pl.BlockSpec((1, PAST, 128), lambda i, j: (i, 0, 0)),
                     pl.BlockSpec((1, PAST, MLA_ROPE), lambda i, j: (i, 0, 0))]
        args += [cosq, sinq, cosk, sink, cc, kc]
    out_specs = [pl.BlockSpec((TQ, 512), lambda i, j: (i * nq + j, 0))]
    out_shape = [jax.ShapeDtypeStruct((n_seq * T, 512), BF16)]
    if not latent:
        out_specs.append(pl.BlockSpec((T, 128), lambda i, j: (i, 0)))
        out_shape.append(jax.ShapeDtypeStruct((n_seq * T, 128), F32))
    return pl.pallas_call(
        functools.partial(_mla_body, T, TQ, latent),
        grid=(n_seq, nq),
        in_specs=in_specs,
        out_specs=out_specs,
        out_shape=out_shape,
        scratch_shapes=[pltpu.VMEM((MLA_H, tk, MLA_DQ), BF16), pltpu.VMEM((MLA_H, tk, MLA_V), BF16)],
        compiler_params=_cp(2),
        name="mla_%d" % T,
    )(*args)


MG_TM = 512


def _merge_body(x_ref, m_ref, of_ref, ob_ref, dg_ref, ogc_ref, ogl_ref, omc_ref, oml_ref, bg_ref, ng_ref,
                wb_ref, wo_ref, o_ref):
    is_ctx = pl.program_id(0) < RC // MG_TM
    og = jnp.where(is_ctx, ogc_ref[...], ogl_ref[...])
    om = jnp.where(is_ctx, omc_ref[...], oml_ref[...])
    odn = of_ref[...] + ob_ref[...]
    dg = dg_ref[...]
    ng = ng_ref[...]
    parts = []
    for h in range(DN_H):
        oh = odn[:, h * DN_DV:(h + 1) * DN_DV]
        ms = jnp.mean(oh * oh, axis=-1, keepdims=True)
        parts.append(oh * lax.rsqrt(ms + EPS) * ng * _silu(dg[:, h * DN_DV:(h + 1) * DN_DV]))
    br0 = jnp.concatenate(parts, axis=1)
    gate = lambda n: _sigmoid(bg_ref[:, n * D:(n + 1) * D].astype(F32))
    merged = gate(0) * _bdot(br0, wb_ref[0, 0])
    merged = merged + gate(1) * _bdot(og, wb_ref[0, 1])
    merged = merged + gate(2) * _bdot(om, wb_ref[0, 2])
    out = _bdot(merged, wo_ref[0])
    o_ref[...] = x_ref[...] + m_ref[0, 2:3, :] * out


def _merge(x, mods_l, o_f, o_b, proj, pbg, og_c, og_l, om_c, om_l, ng, wb, wo, l):
    row = lambda i: (i, 0)
    n_ctx = RC // MG_TM
    ctx_row = lambda i: (jnp.minimum(i, n_ctx - 1), 0)
    lat_row = lambda i: (jnp.maximum(i - n_ctx, 0), 0)
    return pl.pallas_call(
        _merge_body,
        grid=(R // MG_TM,),
        in_specs=[pl.BlockSpec((MG_TM, D), row),
                  pl.BlockSpec((1, 6, D), lambda i: (_mod_group(i * MG_TM), 0, 0)),
                  pl.BlockSpec((MG_TM, 512), row),
                  pl.BlockSpec((MG_TM, 512), row),
                  pl.BlockSpec((MG_TM, 512), lambda i: (i, C_DG // 512)),
                  pl.BlockSpec((MG_TM, 512), ctx_row),
                  pl.BlockSpec((MG_TM, 512), lat_row),
                  pl.BlockSpec((MG_TM, 512), ctx_row),
                  pl.BlockSpec((MG_TM, 512), lat_row),
                  pl.BlockSpec((MG_TM, N_BG), row),
                  pl.BlockSpec((1, DN_DV), lambda i: (0, 0)),
                  pl.BlockSpec((1, 3, 512, D), lambda i: (l, 0, 0, 0)),
                  pl.BlockSpec((1, D, D), lambda i: (l, 0, 0))],
        out_specs=pl.BlockSpec((MG_TM, D), row),
        out_shape=jax.ShapeDtypeStruct((R, D), F32),
        compiler_params=_cp(1),
        name="merge",
    )(x, mods_l, o_f, o_b, proj, og_c, og_l, om_c, om_l, pbg, ng, wb, wo)


FF_TM = 512
FF_CHUNKS = ((0, 1536), (1536, FF_DENSE))


def _ffn_body(x_ref, m_ref, g_ref, w1_ref, w3_ref, w2_ref, o_ref):
    x = x_ref[...]
    h = _modnorm(x, g_ref[...], m_ref[0, 3:4, :], m_ref[0, 4:5, :]).astype(BF16)
    y = None
    for c0, c1 in FF_CHUNKS:
        a = _silu(_dot(h, w1_ref[0, :, c0:c1])) * _dot(h, w3_ref[0, :, c0:c1])
        yc = _dot(a.astype(BF16), w2_ref[0, c0:c1, :])
        y = yc if y is None else y + yc
    o_ref[...] = x + m_ref[0, 5:6, :] * y


def _ffn(x, mods_l, g, w1, w3, w2, j):
    once = pl.Buffered(1)
    return pl.pallas_call(
        _ffn_body,
        grid=(R // FF_TM,),
        in_specs=[pl.BlockSpec((FF_TM, D), lambda i: (i, 0)),
                  pl.BlockSpec((1, 6, D), lambda i: (_mod_group(i * FF_TM), 0, 0)),
                  pl.BlockSpec((1, D), lambda i: (0, 0)),
                  pl.BlockSpec((1, D, FF_DENSE), lambda i: (j, 0, 0), pipeline_mode=once),
                  pl.BlockSpec((1, D, FF_DENSE), lambda i: (j, 0, 0), pipeline_mode=once),
                  pl.BlockSpec((1, FF_DENSE, D), lambda i: (j, 0, 0), pipeline_mode=once)],
        out_specs=pl.BlockSpec((FF_TM, D), lambda i: (i, 0)),
        out_shape=jax.ShapeDtypeStruct((R, D), F32),
        compiler_params=_cp(1),
        name="ffn",
    )(x, mods_l, g, w1, w3, w2)


RT_TM = 512
MOE_TG, MOE_TF = 512, 1792
MOE_NT = 2 * R // MOE_TG + N_EXP
MOE_ROWS = MOE_NT * MOE_TG
CB_TM = 256


def _router_body(x_ref, m_ref, g_ref, rw_ref, rb_ref, hn_ref, route_ref, cnt_ref, base_scr):
    @pl.when(pl.program_id(0) == 0)
    def _():
        base_scr[...] = jnp.zeros_like(base_scr)

    h = _modnorm(x_ref[...], g_ref[...], m_ref[0, 3:4, :], m_ref[0, 4:5, :])
    hn_ref[...] = h
    logits = _dot3(h, rw_ref[...]) + rb_ref[...]
    lane = lax.broadcasted_iota(jnp.int32, logits.shape, 1)
    m1 = jnp.max(logits, axis=-1, keepdims=True)
    i1 = jnp.min(jnp.where(logits == m1, lane, 128), axis=-1, keepdims=True)
    sel1 = lane == i1
    rest = jnp.where(sel1, -jnp.inf, logits)
    m2 = jnp.max(rest, axis=-1, keepdims=True)
    i2 = jnp.min(jnp.where(rest == m2, lane, 128), axis=-1, keepdims=True)
    sel2 = lane == i2
    e2 = jnp.exp(m2 - m1)
    p1 = 1.0 / (1.0 + e2)
    p2 = e2 / (1.0 + e2)

    cnt = jnp.where(jnp.logical_or(sel1, sel2), 1.0, 0.0)
    r = lax.broadcasted_iota(jnp.int32, (RT_TM, RT_TM), 0)
    c = lax.broadcasted_iota(jnp.int32, (RT_TM, RT_TM), 1)
    before = jnp.where(r > c, 1.0, 0.0).astype(BF16)
    seen = base_scr[...] + _dot(before, cnt.astype(BF16))
    rank1 = jnp.sum(jnp.where(sel1, seen, 0.0), axis=-1, keepdims=True)
    rank2 = jnp.sum(jnp.where(sel2, seen, 0.0), axis=-1, keepdims=True)
    vals = (i1.astype(F32), i2.astype(F32), rank1, rank2, p1, p2)
    route = jnp.zeros(logits.shape, F32)
    for k, val in enumerate(vals):
        route = jnp.where(lane == k, val, route)
    route_ref[...] = route
    base_scr[...] += jnp.sum(cnt, axis=0, keepdims=True)
    cnt_ref[...] = base_scr[...]


def _router(x, mods_l, g, rw128, rb128):
    return pl.pallas_call(
        _router_body,
        grid=(R // RT_TM,),
        in_specs=[pl.BlockSpec((RT_TM, D), lambda i: (i, 0)),
                  pl.BlockSpec((1, 6, D), lambda i: (_mod_group(i * RT_TM), 0, 0)),
                  pl.BlockSpec((1, D), lambda i: (0, 0)),
                  pl.BlockSpec((D, 128), lambda i: (0, 0)),
                  pl.BlockSpec((1, 128), lambda i: (0, 0))],
        out_specs=[pl.BlockSpec((RT_TM, D), lambda i: (i, 0)),
                   pl.BlockSpec((RT_TM, 128), lambda i: (i, 0)),
                   pl.BlockSpec((1, 128), lambda i: (0, 0))],
        out_shape=[jax.ShapeDtypeStruct((R, D), F32),
                   jax.ShapeDtypeStruct((R, 128), F32),
                   jax.ShapeDtypeStruct((1, 128), F32)],
        scratch_shapes=[pltpu.VMEM((1, 128), F32)],
        compiler_params=_cp(1),
        name="moe_router",
    )(x, mods_l, g, rw128, rb128)


def _rows_copy(src_hbm, dst, sem, n):
    return pltpu.make_async_copy(src_hbm.at[pl.ds(0, n)], dst, sem)


def _start_row_gather(idx_ref, n, src_hbm, dst, sem, inline=False, both_queues=False):
    def issue(i, priority):
        pltpu.make_async_copy(src_hbm.at[pl.ds(idx_ref[0, 0, i], 1)], dst.at[pl.ds(i, 1)], sem).start(
            priority=priority)

    if inline:
        for i in range(n):
            issue(i, i % 2 if both_queues else 0)
    else:
        def body(i, carry):
            issue(i, 0)
            return carry

        lax.fori_loop(0, n, body, 0, unroll=8)


def _gather_rows(idx_ref, n, src_hbm, dst, sem):
    _start_row_gather(idx_ref, n, src_hbm, dst, sem, inline=True, both_queues=True)
    _rows_copy(src_hbm, dst, sem, n).wait()


def _experts_body(te_ref, tv_ref, src_ref, nsrc_ref, hn_hbm, w1_ref, w3_ref, w2_ref, ys_ref, xg_scr, sems):
    t = pl.program_id(0)
    slot = t % 2
    valid = tv_ref[t] > 0
    requested = jnp.where(t == 0, valid, tv_ref[jnp.maximum(t - 1, 0)] > 0)

    @pl.when(jnp.logical_and(t == 0, valid))
    def _():
        _start_row_gather(src_ref, MOE_TG, hn_hbm, xg_scr.at[0], sems.at[0])

    @pl.when(requested)
    def _():
        _rows_copy(hn_hbm, xg_scr.at[slot], sems.at[slot], MOE_TG).wait()

    @pl.when(valid)
    def _():
        xb = xg_scr[slot].astype(BF16)
        _start_row_gather(nsrc_ref, MOE_TG, hn_hbm, xg_scr.at[1 - slot], sems.at[1 - slot], inline=True)
        y = None
        for c in range(FF_EXP // MOE_TF):
            cs = slice(c * MOE_TF, (c + 1) * MOE_TF)
            a = _silu(_dot(xb, w1_ref[0, 0, :, cs])) * _dot(xb, w3_ref[0, 0, :, cs])
            yc = _dot(a.astype(BF16), w2_ref[0, 0, cs, :])
            y = yc if y is None else y + yc
        ys_ref[...] = y

    @pl.when(jnp.logical_not(valid))
    def _():
        ys_ref[...] = jnp.zeros_like(ys_ref)

    @pl.when(jnp.logical_and(t == MOE_NT - 1, valid))
    def _():
        _rows_copy(hn_hbm, xg_scr.at[1 - slot], sems.at[1 - slot], MOE_TG).wait()


def _experts(tile_e, tile_v, src, hn, w1, w3, w2, j):
    once = pl.Buffered(1)
    grid_spec = pltpu.PrefetchScalarGridSpec(
        num_scalar_prefetch=2,
        grid=(MOE_NT,),
        in_specs=[pl.BlockSpec((1, 1, MOE_TG), lambda t, te, tv: (t, 0, 0), memory_space=pltpu.SMEM),
                  pl.BlockSpec((1, 1, MOE_TG), lambda t, te, tv: (t + 1, 0, 0), memory_space=pltpu.SMEM),
                  pl.BlockSpec(memory_space=pl.ANY),
                  pl.BlockSpec((1, 1, D, FF_EXP), lambda t, te, tv: (j, te[t], 0, 0), pipeline_mode=once),
                  pl.BlockSpec((1, 1, D, FF_EXP), lambda t, te, tv: (j, te[t], 0, 0), pipeline_mode=once),
                  pl.BlockSpec((1, 1, FF_EXP, D), lambda t, te, tv: (j, te[t], 0, 0), pipeline_mode=once)],
        out_specs=pl.BlockSpec((MOE_TG, D), lambda t, te, tv: (t, 0)),
        scratch_shapes=[pltpu.VMEM((2, MOE_TG, D), F32), pltpu.SemaphoreType.DMA((2,))],
    )
    return pl.pallas_call(
        _experts_body,
        grid_spec=grid_spec,
        out_shape=jax.ShapeDtypeStruct((MOE_ROWS, D), F32),
        compiler_params=_cp(1),
        name="moe_experts",
    )(tile_e, tile_v, src, src, hn, w1, w3, w2)


def _combine_body(pos_ref, x_ref, m_ref, route_ref, ys_hbm, o_ref, buf, sem):
    _gather_rows(pos_ref, 2 * CB_TM, ys_hbm, buf, sem)
    route = route_ref[...]
    y = route[:, 4:5] * buf[0:CB_TM, :] + route[:, 5:6] * buf[CB_TM:2 * CB_TM, :]
    o_ref[...] = x_ref[...] + m_ref[0, 5:6, :] * y


def _combine(pos, x, mods_l, route, ys):
    return pl.pallas_call(
        _combine_body,
        grid=(R // CB_TM,),
        in_specs=[pl.BlockSpec((1, 1, 2 * CB_TM), lambda i: (i, 0, 0), memory_space=pltpu.SMEM),
                  pl.BlockSpec((CB_TM, D), lambda i: (i, 0)),
                  pl.BlockSpec((1, 6, D), lambda i: (_mod_group(i * CB_TM), 0, 0)),
                  pl.BlockSpec((CB_TM, 128), lambda i: (i, 0)),
                  pl.BlockSpec(memory_space=pl.ANY)],
        out_specs=pl.BlockSpec((CB_TM, D), lambda i: (i, 0)),
        out_shape=jax.ShapeDtypeStruct((R, D), F32),
        scratch_shapes=[pltpu.VMEM((2 * CB_TM, D), F32), pltpu.SemaphoreType.DMA(())],
        compiler_params=_cp(1),
        name="moe_combine",
    )(pos, x, mods_l, route, ys)


def _moe(x, mods_l, g, rw128, rb128, w1, w3, w2, j):
    hn, route, cnt = _router(x, mods_l, g, rw128, rb128)
    eid = route[:, 0:2].astype(jnp.int32)
    rank = route[:, 2:4].astype(jnp.int32)
    counts = cnt[0, :N_EXP].astype(jnp.int32)
    gsize = (counts + MOE_TG - 1) // MOE_TG * MOE_TG
    gend = jnp.cumsum(gsize)
    pos = (gend - gsize)[eid] + rank
    tile_start = jnp.arange(MOE_NT, dtype=jnp.int32) * MOE_TG
    tile_e = jnp.minimum(jnp.sum(tile_start[:, None] >= gend[None, :], axis=1), N_EXP - 1).astype(jnp.int32)
    tile_v = (tile_start < gend[-1]).astype(jnp.int32)
    tok = jnp.broadcast_to(jnp.arange(R, dtype=jnp.int32)[:, None], (R, 2))
    src = jnp.zeros((MOE_ROWS + MOE_TG,), jnp.int32).at[pos.reshape(-1)].set(tok.reshape(-1))
    ys = _experts(tile_e, tile_v, src.reshape(MOE_NT + 1, 1, MOE_TG), hn, w1, w3, w2, j)
    pos_t = pos.reshape(R // CB_TM, CB_TM, 2).transpose(0, 2, 1).reshape(R // CB_TM, 1, 2 * CB_TM)
    return _combine(pos_t, x, mods_l, route, ys)


FN_TM = 1024


def _final_body(x_ref, g_ref, o_ref):
    x = x_ref[...]
    ms = jnp.mean(x * x, axis=-1, keepdims=True)
    o_ref[...] = x * lax.rsqrt(ms + EPS) * g_ref[...]


def _final_norm(x, g, row_off, n_rows):
    bo = row_off // FN_TM
    return pl.pallas_call(
        _final_body,
        grid=(n_rows // FN_TM,),
        in_specs=[pl.BlockSpec((FN_TM, D), lambda i: (bo + i, 0)), pl.BlockSpec((1, D), lambda i: (0, 0))],
        out_specs=pl.BlockSpec((FN_TM, D), lambda i: (i, 0)),
        out_shape=jax.ShapeDtypeStruct((n_rows, D), F32),
        compiler_params=_cp(1),
        name="final_norm",
    )(x, g)


def _rope_tables(n_tokens, rot_dim):
    t = np.arange(n_tokens)
    row = (t // GRID_W).astype(np.float32)
    col = (t % GRID_W).astype(np.float32)
    n_freq = rot_dim // 4
    inv = (ROPE_THETA ** (-jnp.arange(n_freq, dtype=F32) / n_freq))
    ang = jnp.concatenate([jnp.asarray(row)[:, None] * inv, jnp.asarray(col)[:, None] * inv], axis=-1)
    cos, sin = jnp.cos(ang), jnp.sin(ang)
    return jnp.concatenate([cos, cos], axis=-1), jnp.concatenate([-sin, sin], axis=-1)


def kernel(x_prompt, x_sample, c, cache_gqa_k, cache_gqa_v, cache_mla_ckv, cache_mla_krope, state_delta, c_ctx, w_mod, b_mod, norm1_g, norm2_g, w_in, dn_conv_w, dn_a_log, dn_dt_bias, dn_norm_g, gqa_q_norm, gqa_k_norm, mla_q_norm, mla_kv_norm, mla_w_uq, mla_w_ukv, w_branch, w_out, ffd_w1, ffd_w3, ffd_w2, router_w, router_b, moe_w1, moe_w3, moe_w2, final_g):
    x = jnp.concatenate([x_prompt.reshape(RC, D), x_sample.reshape(RL, D)], axis=0)
    cond8 = jnp.concatenate([c_ctx[None, :], c, jnp.zeros((3, D), F32)], axis=0)
    mods = _mods(cond8, w_mod, b_mod).reshape(DEPTH, 8, 6, D)

    cg, sg = _rope_tables(DEC_SEQ, GQA_HD)
    gqa_rope = (jnp.tile(cg, (1, GQA_H)), jnp.tile(sg, (1, GQA_H)),
                jnp.tile(cg, (1, GQA_KV)), jnp.tile(sg, (1, GQA_KV)))
    cm, sm_ = _rope_tables(DEC_SEQ, MLA_ROPE)
    padk = lambda t: jnp.pad(t, ((0, 0), (16, 128 - 16 - MLA_ROPE)))
    mla_rope = (jnp.tile(cm, (1, MLA_H)), jnp.tile(sm_, (1, MLA_H)),
                jnp.pad(cm, ((0, 0), (16, 128 - 16 - MLA_ROPE)), constant_values=1.0), padk(sm_))

    w_in_bf, w_branch_bf, w_out_bf = w_in.astype(BF16), w_branch.astype(BF16), w_out.astype(BF16)
    ffd_bf = (ffd_w1.astype(BF16), ffd_w3.astype(BF16), ffd_w2.astype(BF16))
    moe_bf = (moe_w1.astype(BF16), moe_w3.astype(BF16), moe_w2.astype(BF16))

    new_k, new_v, new_ckv, new_kr, new_s = [], [], [], [], []
    for l in range(DEPTH):
        pbg, proj = _inproj(x, mods[l], norm1_g[l][None, :], w_in_bf, l)

        pad128 = lambda v: jnp.pad(v.reshape(1, 8), ((0, 0), (8, 112)))
        padt = lambda v: jnp.pad(v.reshape(8, 1), ((8, 0), (0, 0)))
        sm_t = proj[:, C_SM:C_SM + 16].T
        dn_u, dn_w, dn_qg, dn_kd, dn_ai, dn_bg = _dn_prep(
            proj, sm_t, dn_conv_w[l], pad128(dn_a_log[l]), pad128(dn_dt_bias[l]),
            padt(dn_a_log[l]), padt(dn_dt_bias[l]))
        o_dn_f, o_dn_b, s_all = _dn_scan(dn_u, dn_w, dn_qg, dn_kd, dn_ai, dn_bg, state_delta[:, l])
        s_c = s_all[:BATCH]

        qw = jnp.tile(gqa_q_norm[l][None, :], (1, GQA_H))
        kw = jnp.tile(gqa_k_norm[l][None, :], (1, GQA_KV))
        o_g_c, kn_c = _gqa(proj, qw, kw, SEQ, SEQ, BATCH, 0)
        (o_g_l,) = _gqa(proj, qw, kw, DEC_SEQ, 256, DEC_BATCH, RC, rope=gqa_rope,
                        cache=(cache_gqa_k[:, l].reshape(DEC_BATCH, PAST, 128),
                               cache_gqa_v[:, l].reshape(DEC_BATCH, PAST, 128)))

        wq = mla_w_uq[l].reshape(MLA_QL, MLA_H, MLA_DQ)
        wq = jnp.concatenate([wq[:, :, :MLA_NOPE].reshape(MLA_QL, -1), wq[:, :, MLA_NOPE:].reshape(MLA_QL, -1)],
                             axis=1).astype(BF16)
        wkv = mla_w_ukv[l].reshape(MLA_KVL, MLA_H, MLA_NOPE + MLA_V)
        wkv = jnp.concatenate([wkv[:, :, :MLA_NOPE].reshape(MLA_KVL, -1), wkv[:, :, MLA_NOPE:].reshape(MLA_KVL, -1)],
                              axis=1).astype(BF16)
        mqw, mkvw = mla_q_norm[l][None, :], mla_kv_norm[l][None, :]
        o_m_c, ckv_c = _mla(proj, wq, wkv, mqw, mkvw, SEQ, SEQ, BATCH, 0)
        (o_m_l,) = _mla(proj, wq, wkv, mqw, mkvw, DEC_SEQ, 256, DEC_BATCH, RC, rope=mla_rope,
                        cache=(cache_mla_ckv[:, l], cache_mla_krope[:, l]))
        x = _merge(x, mods[l], o_dn_f, o_dn_b, proj, pbg, o_g_c, o_g_l, o_m_c, o_m_l, dn_norm_g[l][None, :],
                   w_branch_bf, w_out_bf, l)

        j = l // 2
        if l % 2 == 0:
            x = _ffn(x, mods[l], norm2_g[l][None, :], *ffd_bf, j)
        else:
            rw128 = jnp.pad(router_w[j], ((0, 0), (0, 128 - N_EXP)))
            rb128 = jnp.pad(router_b[j][None, :], ((0, 0), (0, 128 - N_EXP)), constant_values=-jnp.inf)
            x = _moe(x, mods[l], norm2_g[l][None, :], rw128, rb128, *moe_bf, j)

        new_k.append(kn_c.reshape(BATCH, SEQ, GQA_KV, GQA_HD))
        new_v.append(proj[:RC, C_GKV + 128:C_GKV + 256].reshape(BATCH, SEQ, GQA_KV, GQA_HD))
        new_ckv.append(ckv_c.reshape(BATCH, SEQ, MLA_KVL))
        new_kr.append(proj[:RC, C_SM + 16:C_SM + 16 + MLA_ROPE].reshape(BATCH, SEQ, MLA_ROPE))
        new_s.append(s_c)

    y_c = _final_norm(x, final_g[None, :], 0, RC)
    y_l = _final_norm(x, final_g[None, :], RC, RL)
    return (y_c.reshape(BATCH, SEQ, D), y_l.reshape(DEC_BATCH, DEC_SEQ, D),
            jnp.stack(new_k, axis=1), jnp.stack(new_v, axis=1), jnp.stack(new_ckv, axis=1),
            jnp.stack(new_kr, axis=1), jnp.stack(new_s, axis=1))
```

```python
import functools
import math

import jax
import jax.numpy as jnp
import numpy as np
from jax import lax
from jax.experimental import pallas as pl
from jax.experimental.pallas import tpu as pltpu

F32 = jnp.float32
BF16 = jnp.bfloat16

D = 1024
BATCH, SEQ = 32, 256
DEC_BATCH, DEC_SEQ = 4, 2048
DEPTH = 4
PAST = 512
GRID_W = 64
ROPE_THETA = 10000.0
EPS = 1e-6
DN_H, DN_DK, DN_DV, DN_C = 4, 128, 128, 64
GQA_H, GQA_KV, GQA_HD = 8, 2, 64
MLA_H, MLA_QL, MLA_KVL, MLA_NOPE, MLA_ROPE, MLA_V = 8, 256, 128, 64, 32, 64
FF_DENSE, N_EXP, FF_EXP = 2816, 8, 3584

RC = BATCH * SEQ
RL = DEC_BATCH * DEC_SEQ
R = RC + RL

N_BG = 3 * D
C_QKV, C_DG, C_GQ, C_GKV, C_MCQ, C_MCKV, C_SM = 0, 1536, 2048, 2560, 2816, 3072, 3200
N_PROJ = 3328
NP = N_BG + N_PROJ

VMEM_LIMIT = 56 * 1024 * 1024


def _cp(n_grid):
    return pltpu.CompilerParams(dimension_semantics=("arbitrary",) * n_grid,
                                vmem_limit_bytes=VMEM_LIMIT)


def _dot(a, b):
    return jnp.dot(a, b, preferred_element_type=F32)


def _bdot(a, b):
    return jnp.dot(a.astype(BF16), b.astype(BF16), preferred_element_type=F32)


def _bdot_nt(a, b):
    return lax.dot_general(a.astype(BF16), b.astype(BF16), (((1,), (1,)), ((), ())),
                           preferred_element_type=F32)


def _bdot_tn(a, b):
    return lax.dot_general(a.astype(BF16), b.astype(BF16), (((0,), (0,)), ((), ())),
                           preferred_element_type=F32)


def _split2(a):
    hi = a.astype(BF16)
    lo = (a - hi.astype(F32)).astype(BF16)
    return hi, lo


def _split3(a):
    a1 = a.astype(BF16)
    r1 = a - a1.astype(F32)
    a2 = r1.astype(BF16)
    a3 = (r1 - a2.astype(F32)).astype(BF16)
    return a1, a2, a3


def _dot3(a, b):
    ah, al = _split2(a)
    bh, bl = _split2(b)
    return _dot(ah, bh) + (_dot(ah, bl) + _dot(al, bh))


def _mask_dot_r(mask_bf, g):
    g1, g2, g3 = _split3(g)
    return _dot(mask_bf, g1) + (_dot(mask_bf, g2) + _dot(mask_bf, g3))


def _mask_dot_l(g, mask_bf):
    g1, g2, g3 = _split3(g)
    return _dot(g1, mask_bf) + (_dot(g2, mask_bf) + _dot(g3, mask_bf))


def _sigmoid(x):
    return 1.0 / (1.0 + jnp.exp(-x))


def _silu(x):
    return x * _sigmoid(x)


def _softplus(x):
    return jnp.maximum(x, 0.0) + jnp.log1p(jnp.exp(-jnp.abs(x)))


def _mod_group(row0):
    return jnp.where(row0 < RC, 0, 1 + (row0 - RC) // DEC_SEQ)


MODS_TN = 1536


def _mods_body(c_ref, w_ref, b_ref, o_ref):
    s = _silu(c_ref[...])
    o_ref[0] = _bdot(s, w_ref[0]) + b_ref[0]


def _mods(cond8, w_mod, b_mod):
    nj = 6 * D // MODS_TN
    return pl.pallas_call(
        _mods_body,
        grid=(DEPTH, nj),
        in_specs=[pl.BlockSpec((8, D), lambda l, j: (0, 0)),
                  pl.BlockSpec((1, D, MODS_TN), lambda l, j: (l, 0, j)),
                  pl.BlockSpec((1, 1, MODS_TN), lambda l, j: (l, 0, j))],
        out_specs=pl.BlockSpec((1, 8, MODS_TN), lambda l, j: (l, 0, j)),
        out_shape=jax.ShapeDtypeStruct((DEPTH, 8, 6 * D), F32),
        compiler_params=_cp(2),
        name="mods",
    )(cond8, w_mod, b_mod.reshape(DEPTH, 1, 6 * D))


IN_TM = 512
IN_CHUNK = 1536


def _modnorm(x, g, shift, scale):
    ms = jnp.mean(x * x, axis=-1, keepdims=True)
    y = x * lax.rsqrt(ms + EPS) * g
    return y * (1.0 + scale) + shift


W_IN_PIECES = ((3248, 6320), (0, 2048), (2064, 3216), (2048, 2064), (3216, 3248))
W_IN_COLS = 6320


def _inproj_body(x_ref, m_ref, g_ref, wsrc_ref, bg_ref, o_ref, w_ref):
    @pl.when(pl.program_id(0) == 0)
    def _():
        at = 0
        for a, b in W_IN_PIECES:
            w_ref[:, at:at + (b - a)] = wsrc_ref[0, :, a:b]
            at += b - a
        w_ref[:, at:NP] = jnp.zeros((D, NP - at), BF16)

    h = _modnorm(x_ref[...], g_ref[...], m_ref[0, 0:1, :], m_ref[0, 1:2, :]).astype(BF16)
    for c0 in range(0, N_BG, IN_CHUNK):
        bg_ref[:, c0:c0 + IN_CHUNK] = _dot(h, w_ref[:, c0:c0 + IN_CHUNK]).astype(BF16)
    for c0 in range(0, N_PROJ, IN_CHUNK):
        c1 = min(c0 + IN_CHUNK, N_PROJ)
        o_ref[:, c0:c1] = _dot(h, w_ref[:, N_BG + c0:N_BG + c1])


def _inproj(x, mods_l, g, w_bf, l):
    return pl.pallas_call(
        _inproj_body,
        grid=(R // IN_TM,),
        in_specs=[pl.BlockSpec((IN_TM, D), lambda i: (i, 0)),
                  pl.BlockSpec((1, 6, D), lambda i: (_mod_group(i * IN_TM), 0, 0)),
                  pl.BlockSpec((1, D), lambda i: (0, 0)),
                  pl.BlockSpec((1, D, W_IN_COLS), lambda i: (l, 0, 0), pipeline_mode=pl.Buffered(1))],
        out_specs=[pl.BlockSpec((IN_TM, N_BG), lambda i: (i, 0)),
                   pl.BlockSpec((IN_TM, N_PROJ), lambda i: (i, 0))],
        out_shape=[jax.ShapeDtypeStruct((R, N_BG), BF16), jax.ShapeDtypeStruct((R, N_PROJ), F32)],
        scratch_shapes=[pltpu.VMEM((D, NP), BF16)],
        compiler_params=_cp(1),
        name="inproj",
    )(x, mods_l, g, w_bf)


PREP_TM = 256
PREP_NC = PREP_TM // DN_C
PREP_LOCKSTEP = 4


def _prep_body(x_ref, xp_ref, xn_ref, sm_ref, smt_ref, cw_ref, al_ref, dt_ref, alt_ref, dtt_ref,
               u_o, w_o, qg_o, kd_o, ai_o, bg_o, qkv_o, gr_o):
    i = pl.program_id(0)
    n_ctx = RC // PREP_TM
    per_seq = DEC_SEQ // PREP_TM
    is_ctx = i < n_ctx
    j = (i - n_ctx) % per_seq
    first = jnp.logical_or(is_ctx, j == 0)
    last = jnp.logical_or(is_ctx, j == per_seq - 1)

    x = x_ref[...]
    prev_row = jnp.where(first, 0.0, xp_ref[7:8, :])
    next_row = jnp.where(last, 0.0, xn_ref[0:1, :])
    row = lax.broadcasted_iota(jnp.int32, (PREP_TM, 1), 0)
    xm = jnp.where(row == 0, prev_row, pltpu.roll(x, 1, axis=0))
    xq = jnp.where(row == PREP_TM - 1, next_row, pltpu.roll(x, PREP_TM - 1, axis=0))
    w = cw_ref[...]
    y = _silu(w[0:1] * xm + w[1:2] * x + w[2:3] * xq)

    for h in range(DN_H):
        qh = y[:, h * DN_DK:(h + 1) * DN_DK]
        qn = qh * lax.rsqrt(jnp.sum(qh * qh, axis=-1, keepdims=True) + EPS) * (DN_DK ** -0.5)
        qkv_o[:, h * DN_DK:(h + 1) * DN_DK] = qn
        kh = y[:, 512 + h * DN_DK:512 + (h + 1) * DN_DK]
        kn = kh * lax.rsqrt(jnp.sum(kh * kh, axis=-1, keepdims=True) + EPS)
        qkv_o[:, 512 + h * DN_DK:512 + (h + 1) * DN_DK] = kn
    qkv_o[:, 1024:1536] = y[:, 1024:1536]

    r = lax.broadcasted_iota(jnp.int32, (PREP_TM, PREP_TM), 0)
    c = lax.broadcasted_iota(jnp.int32, (PREP_TM, PREP_TM), 1)
    same = (r // DN_C) == (c // DN_C)
    low = jnp.where(jnp.logical_and(same, r >= c), 1.0, 0.0).astype(BF16)
    upp = jnp.where(jnp.logical_and(same, r <= c), 1.0, 0.0).astype(BF16)

    sm = sm_ref[...]
    beta = _sigmoid(sm)
    g = -jnp.exp(al_ref[...]) * _softplus(sm + dt_ref[...])
    gc_f = _mask_dot_r(low, g)
    gc_b = _mask_dot_r(upp, g)
    lane = lax.broadcasted_iota(jnp.int32, (1, 128), 1)
    bg_o[...] = jnp.where(lane < 8, beta, jnp.where(lane < 12, gc_f, jnp.where(lane < 16, gc_b, 0.0)))

    gt = -jnp.exp(alt_ref[...]) * _softplus(smt_ref[...] + dtt_ref[...])
    gct_f = _mask_dot_l(gt, upp)
    gct_b = _mask_dot_l(gt, low)
    sub = lax.broadcasted_iota(jnp.int32, (16, 1), 0)
    gct = jnp.where(sub < 12, gct_f, gct_b)
    for k in range(PREP_NC):
        gr_o[k] = gct[8:16, k * DN_C:(k + 1) * DN_C]

    rr = lax.broadcasted_iota(jnp.int32, (DN_C, DN_C), 0)
    cc = lax.broadcasted_iota(jnp.int32, (DN_C, DN_C), 1)

    def chunk_group(gi, carry):
        prob = [(cj, d, h) for cj in range(PREP_LOCKSTEP) for d in range(2) for h in range(DN_H)]
        ci = [gi * PREP_LOCKSTEP + cj for cj in range(PREP_LOCKSTEP)]
        rows_c = [pl.ds(pl.multiple_of(c_ * DN_C, DN_C), DN_C) for c_ in ci]
        bgc = [bg_o[r_, :] for r_ in rows_c]
        grow_all = [gr_o[c_] for c_ in ci]
        rows = [rows_c[cj] for cj, d, h in prob]
        q = [qkv_o[rows_c[cj], h * DN_DK:(h + 1) * DN_DK] for cj, d, h in prob]
        k = [qkv_o[rows_c[cj], 512 + h * DN_DK:512 + (h + 1) * DN_DK] for cj, d, h in prob]
        v = [qkv_o[rows_c[cj], 1024 + h * DN_DV:1024 + (h + 1) * DN_DV] for cj, d, h in prob]
        beta = [bgc[cj][:, d * DN_H + h:d * DN_H + h + 1] for cj, d, h in prob]
        gcol = [bgc[cj][:, 8 + d * DN_H + h:9 + d * DN_H + h] for cj, d, h in prob]
        grow = [grow_all[cj][d * DN_H + h:d * DN_H + h + 1, :] for cj, d, h in prob]
        incl = [(rr >= cc) if d == 0 else (rr <= cc) for cj, d, h in prob]
        strict = [(rr > cc) if d == 0 else (rr < cc) for cj, d, h in prob]
        n = len(prob)
        decay = [jnp.exp(jnp.where(incl[i], gcol[i] - grow[i], -1e30)) for i in range(n)]
        kb = [k[i] * beta[i] for i in range(n)]
        kk = [_bdot_nt(kb[i], k[i]) for i in range(n)]
        qk = [_bdot_nt(q[i], k[i]) for i in range(n)]
        a = [jnp.where(strict[i], kk[i] * decay[i], 0.0) for i in range(n)]
        blk = lambda b: (rr // b) == (cc // b)
        eye = jnp.where(rr == cc, 1.0, 0.0)
        p = [jnp.where(blk(8), a[i], 0.0) for i in range(n)]
        t = [eye - p[i] for i in range(n)]
        for _ in range(2):
            p = [_bdot(p[i], p[i]) for i in range(n)]
            t = [t[i] + _bdot(t[i], p[i]) for i in range(n)]
        for b in (16, 32, 64):
            m = jnp.logical_and(blk(b), jnp.logical_not(blk(b // 2)))
            tl = [_bdot(t[i], jnp.where(m, a[i], 0.0)) for i in range(n)]
            t = [t[i] - _bdot(tl[i], t[i]) for i in range(n)]
        egc = [jnp.exp(gcol[i]) for i in range(n)]
        glast = [gcol[i][DN_C - 1:DN_C, :] if prob[i][1] == 0 else gcol[i][0:1, :] for i in range(n)]
        uw = [_bdot(t[i], jnp.concatenate([v[i] * beta[i], kb[i] * egc[i]], axis=1)) for i in range(n)]
        for i, (cj, d, h) in enumerate(prob):
            u_o[d, rows[i], h * DN_DV:(h + 1) * DN_DV] = uw[i][:, 0:DN_DV]
            w_o[d, rows[i], h * DN_DK:(h + 1) * DN_DK] = uw[i][:, DN_DV:DN_DV + DN_DK].astype(BF16)
            qg_o[d, rows[i], h * DN_DK:(h + 1) * DN_DK] = (q[i] * egc[i]).astype(BF16)
            kd_o[d, rows[i], h * DN_DK:(h + 1) * DN_DK] = (k[i] * jnp.exp(glast[i] - gcol[i])).astype(BF16)
            ai_o[d, rows[i], h * DN_C:(h + 1) * DN_C] = (qk[i] * decay[i]).astype(BF16)
        return carry

    lax.fori_loop(0, PREP_NC // PREP_LOCKSTEP, chunk_group, 0)


def _dn_prep(proj, sm_t, conv_w, al128, dt128, al_t, dt_t):
    nb8 = R // 8
    qb = C_QKV // 1536
    wide = lambda dt: jax.ShapeDtypeStruct((2, R, DN_H * DN_DK), dt)
    wide_spec = pl.BlockSpec((2, PREP_TM, DN_H * DN_DK), lambda i: (0, i, 0))
    return pl.pallas_call(
        _prep_body,
        grid=(R // PREP_TM,),
        in_specs=[pl.BlockSpec((PREP_TM, 1536), lambda i: (i, qb)),
                  pl.BlockSpec((8, 1536), lambda i: (jnp.maximum(i * (PREP_TM // 8) - 1, 0), qb)),
                  pl.BlockSpec((8, 1536), lambda i: (jnp.minimum((i + 1) * (PREP_TM // 8), nb8 - 1), qb)),
                  pl.BlockSpec((PREP_TM, 128), lambda i: (i, C_SM // 128)),
                  pl.BlockSpec((16, PREP_TM), lambda i: (0, i)),
                  pl.BlockSpec((3, 1536), lambda i: (0, 0)),
                  pl.BlockSpec((1, 128), lambda i: (0, 0)),
                  pl.BlockSpec((1, 128), lambda i: (0, 0)),
                  pl.BlockSpec((16, 1), lambda i: (0, 0)),
                  pl.BlockSpec((16, 1), lambda i: (0, 0))],
        out_specs=[wide_spec, wide_spec, wide_spec, wide_spec,
                   pl.BlockSpec((2, PREP_TM, DN_H * DN_C), lambda i: (0, i, 0)),
                   pl.BlockSpec((PREP_TM, 128), lambda i: (i, 0))],
        out_shape=[wide(F32), wide(BF16), wide(BF16), wide(BF16),
                   jax.ShapeDtypeStruct((2, R, DN_H * DN_C), BF16),
                   jax.ShapeDtypeStruct((R, 128), F32)],
        scratch_shapes=[pltpu.VMEM((PREP_TM, 1536), F32), pltpu.VMEM((PREP_NC, 8, DN_C), F32)],
        compiler_params=_cp(1),
        name="dn_prep",
    )(proj, proj, proj, proj, sm_t, conv_w, al128, dt128, al_t, dt_t)


SCAN_STEPS = RC // PREP_TM
SCAN_PER = DEC_SEQ // PREP_TM
assert RL // PREP_TM == SCAN_STEPS and SEQ == PREP_TM


def _scan_lat_bwd_block(i):
    return (i // SCAN_PER) * SCAN_PER + (SCAN_PER - 1 - i % SCAN_PER)


def _scan_body(*refs):
    streams = [refs[6 * k:6 * k + 6] for k in range(4)]
    s0_ref = refs[24]
    out_refs = refs[25:29]
    so_ref, s_scr = refs[29], refs[30]
    j = pl.program_id(0) % SCAN_PER

    for h in range(DN_H):
        s_scr[0, h] = jnp.zeros((DN_DK, DN_DV), F32)
        s_scr[1, h] = jnp.zeros((DN_DK, DN_DV), F32)

    @pl.when(j == 0)
    def _():
        for h in range(DN_H):
            s_scr[2, h] = s0_ref[0, 0, h]
            s_scr[3, h] = s0_ref[0, 1, h]

    prob = [(k, h) for k in range(4) for h in range(DN_H)]
    tn = (((0,), (0,)), ((), ()))

    def step(n, carry):
        rows = [pl.ds(pl.multiple_of((n if k % 2 == 0 else PREP_NC - 1 - n) * DN_C, DN_C), DN_C)
                for k in range(4)]
        bgc = [streams[k][5][rows[k], :] for k in range(4)]
        cols = [slice(h * DN_DK, (h + 1) * DN_DK) for k, h in prob]
        st = [s_scr[k, h] for k, h in prob]
        stb = [x.astype(BF16) for x in st]
        ws = [_dot(streams[k][1][0, rows[k], cols[i]], stb[i]) for i, (k, h) in enumerate(prob)]
        qs = [_dot(streams[k][2][0, rows[k], cols[i]], stb[i]) for i, (k, h) in enumerate(prob)]
        vb = [(streams[k][0][0, rows[k], cols[i]] - ws[i]).astype(BF16) for i, (k, h) in enumerate(prob)]
        av = [_dot(streams[k][4][0, rows[k], h * DN_C:(h + 1) * DN_C], vb[i]) for i, (k, h) in enumerate(prob)]
        kv = [lax.dot_general(streams[k][3][0, rows[k], cols[i]], vb[i], tn, preferred_element_type=F32)
              for i, (k, h) in enumerate(prob)]
        for i, (k, h) in enumerate(prob):
            d = k % 2
            gcol = bgc[k][:, 8 + d * DN_H + h:9 + d * DN_H + h]
            glast = gcol[DN_C - 1:DN_C, :] if d == 0 else gcol[0:1, :]
            s_scr[k, h] = st[i] * jnp.exp(glast) + kv[i]
            out_refs[k][rows[k], cols[i]] = qs[i] + av[i]
        return carry

    lax.fori_loop(0, PREP_NC, step, 0)

    for d in range(2):
        for h in range(DN_H):
            so_ref[0, d, h] = s_scr[d, h]


def _dn_scan(u, w, qg, kd, ai, bg, s0):
    row_of = (lambda i: i, lambda i: i,
              lambda i: SCAN_STEPS + i, lambda i: SCAN_STEPS + _scan_lat_bwd_block(i))
    in_specs, args = [], []
    for k in range(4):
        d, rb = k % 2, row_of[k]
        wide = pl.BlockSpec((1, PREP_TM, DN_H * DN_DK), lambda i, d=d, rb=rb: (d, rb(i), 0))
        narrow = pl.BlockSpec((1, PREP_TM, DN_H * DN_C), lambda i, d=d, rb=rb: (d, rb(i), 0))
        in_specs += [wide, wide, wide, wide, narrow, pl.BlockSpec((PREP_TM, 128), lambda i, rb=rb: (rb(i), 0))]
        args += [u, w, qg, kd, ai, bg]
    state_spec = lambda m: pl.BlockSpec((1, 2, DN_H, DN_DK, DN_DV), m)
    in_specs.append(state_spec(lambda i: (i // SCAN_PER, 0, 0, 0, 0)))
    o_spec = lambda m: pl.BlockSpec((PREP_TM, DN_H * DN_DV), m)
    o_shape = lambda n: jax.ShapeDtypeStruct((n, DN_H * DN_DV), F32)
    return pl.pallas_call(
        _scan_body,
        grid=(SCAN_STEPS,),
        in_specs=in_specs,
        out_specs=[o_spec(lambda i: (i, 0)), o_spec(lambda i: (i, 0)), o_spec(lambda i: (i, 0)),
                   o_spec(lambda i: (_scan_lat_bwd_block(i), 0)),
                   state_spec(lambda i: (i, 0, 0, 0, 0))],
        out_shape=[o_shape(RC), o_shape(RC), o_shape(RL), o_shape(RL),
                   jax.ShapeDtypeStruct((BATCH, 2, DN_H, DN_DK, DN_DV), F32)],
        scratch_shapes=[pltpu.VMEM((4, DN_H, DN_DK, DN_DV), F32)],
        compiler_params=_cp(1),
        name="dn_scan",
    )(*args, s0)


def _group_mean_matrix(width, group):
    r = lax.broadcasted_iota(jnp.int32, (width, width), 0)
    c = lax.broadcasted_iota(jnp.int32, (width, width), 1)
    return jnp.where((r // group) == (c // group), 1.0 / group, 0.0).astype(BF16)


def _group_rmsnorm(x, w, group):
    m = _group_mean_matrix(x.shape[-1], group)
    hi, lo = _split2(x * x)
    ms = _dot(hi, m) + _dot(lo, m)
    return x * lax.rsqrt(ms + EPS) * w


def _rope(x, cos, sin_signed, group):
    width = x.shape[-1]
    half = group // 2
    lane = lax.broadcasted_iota(jnp.int32, (1, width), 1)
    swapped = jnp.where((lane % group) < half,
                        pltpu.roll(x, width - half, axis=1), pltpu.roll(x, half, axis=1))
    return x * cos + swapped * sin_signed


def _attention_units(n, scores, values):
    outs = []
    s_next = scores(0)
    for u in range(n):
        s = s_next
        if u + 1 < n:
            s_next = scores(u + 1)
        m = jnp.max(s, axis=-1, keepdims=True)
        p = jnp.exp(s - m)
        l = jnp.sum(p, axis=-1, keepdims=True)
        outs.append(_dot(p.astype(BF16), values(u)) / l)
    return outs


GQA_G = GQA_H // GQA_KV


def _gqa_body(T, TQ, latent, *refs):
    it = iter(refs)
    q_ref, kv_ref, qw_ref, kw_ref = next(it), next(it), next(it), next(it)
    if latent:
        cq_ref, sq_ref, ck_ref, sk_ref, kc_ref, vc_ref = (next(it) for _ in range(6))
    o_ref = next(it)
    kn_ref = None if latent else next(it)
    k_scr, v_scr = next(it), next(it)

    @pl.when(pl.program_id(1) == 0)
    def _():
        kv = kv_ref[...]
        k = _group_rmsnorm(kv[:, 0:128], kw_ref[...], GQA_HD)
        v = kv[:, 128:256]
        if latent:
            k = _rope(k, ck_ref[...], sk_ref[...], GQA_HD)
        else:
            kn_ref[...] = k
        for g in range(GQA_KV):
            k_scr[g, 0:T, :] = k[:, g * GQA_HD:(g + 1) * GQA_HD].astype(BF16)
            v_scr[g, 0:T, :] = v[:, g * GQA_HD:(g + 1) * GQA_HD].astype(BF16)
            if latent:
                k_scr[g, T:T + PAST, :] = kc_ref[0, :, g * GQA_HD:(g + 1) * GQA_HD].astype(BF16)
                v_scr[g, T:T + PAST, :] = vc_ref[0, :, g * GQA_HD:(g + 1) * GQA_HD].astype(BF16)

    q = _group_rmsnorm(q_ref[...], qw_ref[...], GQA_HD)
    if latent:
        q = _rope(q, cq_ref[...], sq_ref[...], GQA_HD)
    q = q * (GQA_HD ** -0.5)
    head = lambda hh: q[:, hh * GQA_HD:(hh + 1) * GQA_HD]
    qu = [jnp.concatenate([head(2 * u), head(2 * u + 1)], axis=0) for u in range(GQA_H // 2)]
    grp = lambda u: (2 * u) // GQA_G
    outs = _attention_units(GQA_H // 2, lambda u: _bdot_nt(qu[u], k_scr[grp(u)]), lambda u: v_scr[grp(u)])
    for u, o in enumerate(outs):
        for j in range(2):
            hh = 2 * u + j
            o_ref[:, hh * GQA_HD:(hh + 1) * GQA_HD] = o[j * TQ:(j + 1) * TQ, :].astype(o_ref.dtype)


def _gqa(proj, qw, kw, T, TQ, n_seq, row_off, rope=None, cache=None):
    latent = rope is not None
    tk = T + (PAST if latent else 0)
    nq = T // TQ
    qo = row_off // TQ
    so = row_off // T
    in_specs = [pl.BlockSpec((TQ, 512), lambda i, j: (qo + i * nq + j, C_GQ // 512)),
                pl.BlockSpec((T, 256), lambda i, j: (so + i, C_GKV // 256)),
                pl.BlockSpec((1, 512), lambda i, j: (0, 0)),
                pl.BlockSpec((1, 128), lambda i, j: (0, 0))]
    args = [proj, proj, qw, kw]
    if latent:
        cq, sq, ck, sk = rope
        kc, vc = cache
        in_specs += [pl.BlockSpec((TQ, 512), lambda i, j: (j, 0)),
                     pl.BlockSpec((TQ, 512), lambda i, j: (j, 0)),
                     pl.BlockSpec((T, 128), lambda i, j: (0, 0)),
                     pl.BlockSpec((T, 128), lambda i, j: (0, 0)),
                     pl.BlockSpec((1, PAST, 128), lambda i, j: (i, 0, 0)),
                     pl.BlockSpec((1, PAST, 128), lambda i, j: (i, 0, 0))]
        args += [cq, sq, ck, sk, kc, vc]
    out_specs = [pl.BlockSpec((TQ, 512), lambda i, j: (i * nq + j, 0))]
    out_shape = [jax.ShapeDtypeStruct((n_seq * T, 512), BF16)]
    if not latent:
        out_specs.append(pl.BlockSpec((T, 128), lambda i, j: (i, 0)))
        out_shape.append(jax.ShapeDtypeStruct((n_seq * T, 128), F32))
    return pl.pallas_call(
        functools.partial(_gqa_body, T, TQ, latent),
        grid=(n_seq, nq),
        in_specs=in_specs,
        out_specs=out_specs,
        out_shape=out_shape,
        scratch_shapes=[pltpu.VMEM((GQA_KV, tk, GQA_HD), BF16), pltpu.VMEM((GQA_KV, tk, GQA_HD), BF16)],
        compiler_params=_cp(2),
        name="gqa_%d" % T,
    )(*args)


MLA_DQ = MLA_NOPE + MLA_ROPE
MLA_KV_ROWS = 512


def _mla_body(T, TQ, latent, *refs):
    it = iter(refs)
    cq_ref, ckv_ref, sm_ref, wq_ref, wkv_ref, qw_ref, kvw_ref = (next(it) for _ in range(7))
    if latent:
        cosq_ref, sinq_ref, cosk_ref, sink_ref, cc_ref, kc_ref = (next(it) for _ in range(6))
    o_ref = next(it)
    cn_ref = None if latent else next(it)
    k_scr, v_scr = next(it), next(it)
    tk = T + (PAST if latent else 0)

    @pl.when(pl.program_id(1) == 0)
    def _():
        x = ckv_ref[...]
        ms = jnp.mean(x * x, axis=-1, keepdims=True)
        ckv = x * lax.rsqrt(ms + EPS) * kvw_ref[...]
        sm = sm_ref[...]
        if latent:
            sm = _rope_small(sm, cosk_ref[...], sink_ref[...])
        else:
            cn_ref[...] = ckv
        kr = sm[:, 16:16 + MLA_ROPE]

        def put(rows0, ckv_rows, kr_rows):
            n = ckv_rows.shape[0]
            kv = _bdot(ckv_rows, wkv_ref[...])
            krb = kr_rows.astype(BF16)
            for h in range(MLA_H):
                k_scr[h, rows0:rows0 + n, 0:MLA_NOPE] = kv[:, h * MLA_NOPE:(h + 1) * MLA_NOPE].astype(BF16)
                k_scr[h, rows0:rows0 + n, MLA_NOPE:MLA_DQ] = krb
                v_scr[h, rows0:rows0 + n, :] = kv[:, 512 + h * MLA_V:512 + (h + 1) * MLA_V].astype(BF16)

        for r0 in range(0, T, MLA_KV_ROWS):
            r1 = min(r0 + MLA_KV_ROWS, T)
            put(r0, ckv[r0:r1], kr[r0:r1])
        if latent:
            put(T, cc_ref[0], kc_ref[0])

    x = cq_ref[...]
    ms = jnp.mean(x * x, axis=-1, keepdims=True)
    cq = x * lax.rsqrt(ms + EPS) * qw_ref[...]
    qf = _bdot(cq, wq_ref[...])
    qn = qf[:, 0:512]
    qr = qf[:, 512:768]
    if latent:
        qr = _rope(qr, cosq_ref[...], sinq_ref[...], MLA_ROPE)
    scale = MLA_DQ ** -0.5
    qh = [jnp.concatenate([qn[:, h * MLA_NOPE:(h + 1) * MLA_NOPE],
                           qr[:, h * MLA_ROPE:(h + 1) * MLA_ROPE]], axis=1) * scale for h in range(MLA_H)]
    outs = _attention_units(MLA_H, lambda h: _bdot_nt(qh[h], k_scr[h]), lambda h: v_scr[h])
    for h, o in enumerate(outs):
        o_ref[:, h * MLA_V:(h + 1) * MLA_V] = o.astype(o_ref.dtype)


def _rope_small(sm, cos, sin_signed):
    lane = lax.broadcasted_iota(jnp.int32, (1, 128), 1)
    half = MLA_ROPE // 2
    swapped = jnp.where(lane < 16 + half, pltpu.roll(sm, 128 - half, axis=1), pltpu.roll(sm, half, axis=1))
    return sm * cos + swapped * sin_signed


def _mla(proj, wq, wkv, qw, kvw, T, TQ, n_seq, row_off, rope=None, cache=None):
    latent = rope is not None
    tk = T + (PAST if latent else 0)
    nq = T // TQ
    qo = row_off // TQ
    so = row_off // T
    in_specs = [pl.BlockSpec((TQ, 256), lambda i, j: (qo + i * nq + j, C_MCQ // 256)),
                pl.BlockSpec((T, 128), lambda i, j: (so + i, C_MCKV // 128)),
                pl.BlockSpec((T, 128), lambda i, j: (so + i, C_SM // 128)),
                pl.BlockSpec((MLA_QL, 768), lambda i, j: (0, 0)),
                pl.BlockSpec((MLA_KVL, 1024), lambda i, j: (0, 0)),
                pl.BlockSpec((1, 256), lambda i, j: (0, 0)),
                pl.BlockSpec((1, 128), lambda i, j: (0, 0))]
    args = [proj, proj, proj, wq, wkv, qw, kvw]
    if latent:
        cosq, sinq, cosk, sink = rope
        cc, kc = cache
        in_specs += [pl.BlockSpec((TQ, 256), lambda i, j: (j, 0)),
                     pl.BlockSpec((TQ, 256), lambda i, j: (j, 0)),
                     pl.BlockSpec((T, 128), lambda i, j: (0, 0)),
                     pl.BlockSpec((T, 128), lambda i, j: (0, 0)),
                     pl.BlockSpec((1, PAST, 128), lambda i, j: (i, 0, 0)),
                     pl.BlockSpec((1, PAST, MLA_ROPE), lambda i, j: (i, 0, 0))]
        args += [cosq, sinq, cosk, sink, cc, kc]
    out_specs = [pl.BlockSpec((TQ, 512), lambda i, j: (i * nq + j, 0))]
    out_shape = [jax.ShapeDtypeStruct((n_seq * T, 512), BF16)]
    if not latent:
        out_specs.append(pl.BlockSpec((T, 128), lambda i, j: (i, 0)))
        out_shape.append(jax.ShapeDtypeStruct((n_seq * T, 128), F32))
    return pl.pallas_call(
        functools.partial(_mla_body, T, TQ, latent),
        grid=(n_seq, nq),
        in_specs=in_specs,
        out_specs=out_specs,
        out_shape=out_shape,
        scratch_shapes=[pltpu.VMEM((MLA_H, tk, MLA_DQ), BF16), pltpu.VMEM((MLA_H, tk, MLA_V), BF16)],
        compiler_params=_cp(2),
        name="mla_%d" % T,
    )(*args)


MG_TM = 512


def _merge_body(x_ref, m_ref, ofc_ref, obc_ref, ofl_ref, obl_ref, dg_ref, ogc_ref, ogl_ref, omc_ref, oml_ref,
                bg_ref, ng_ref, wb_ref, wo_ref, o_ref):
    is_ctx = pl.program_id(0) < RC // MG_TM
    og = jnp.where(is_ctx, ogc_ref[...], ogl_ref[...])
    om = jnp.where(is_ctx, omc_ref[...], oml_ref[...])
    odn = jnp.where(is_ctx, ofc_ref[...] + obc_ref[...], ofl_ref[...] + obl_ref[...])
    dg = dg_ref[...]
    ng = ng_ref[...]
    parts = []
    for h in range(DN_H):
        oh = odn[:, h * DN_DV:(h + 1) * DN_DV]
        ms = jnp.mean(oh * oh, axis=-1, keepdims=True)
        parts.append(oh * lax.rsqrt(ms + EPS) * ng * _silu(dg[:, h * DN_DV:(h + 1) * DN_DV]))
    br0 = jnp.concatenate(parts, axis=1)
    gate = lambda n: _sigmoid(bg_ref[:, n * D:(n + 1) * D].astype(F32))
    merged = gate(0) * _bdot(br0, wb_ref[0, 0])
    merged = merged + gate(1) * _bdot(og, wb_ref[0, 1])
    merged = merged + gate(2) * _bdot(om, wb_ref[0, 2])
    out = _bdot(merged, wo_ref[0])
    o_ref[...] = x_ref[...] + m_ref[0, 2:3, :] * out


def _merge(x, mods_l, dn_o, proj, pbg, og_c, og_l, om_c, om_l, ng, wb, wo, l):
    row = lambda i: (i, 0)
    n_ctx = RC // MG_TM
    ctx_row = lambda i: (jnp.minimum(i, n_ctx - 1), 0)
    lat_row = lambda i: (jnp.maximum(i - n_ctx, 0), 0)
    return pl.pallas_call(
        _merge_body,
        grid=(R // MG_TM,),
        in_specs=[pl.BlockSpec((MG_TM, D), row),
                  pl.BlockSpec((1, 6, D), lambda i: (_mod_group(i * MG_TM), 0, 0)),
                  pl.BlockSpec((MG_TM, 512), ctx_row),
                  pl.BlockSpec((MG_TM, 512), ctx_row),
                  pl.BlockSpec((MG_TM, 512), lat_row),
                  pl.BlockSpec((MG_TM, 512), lat_row),
                  pl.BlockSpec((MG_TM, 512), lambda i: (i, C_DG // 512)),
                  pl.BlockSpec((MG_TM, 512), ctx_row),
                  pl.BlockSpec((MG_TM, 512), lat_row),
                  pl.BlockSpec((MG_TM, 512), ctx_row),
                  pl.BlockSpec((MG_TM, 512), lat_row),
                  pl.BlockSpec((MG_TM, N_BG), row),
                  pl.BlockSpec((1, DN_DV), lambda i: (0, 0)),
                  pl.BlockSpec((1, 3, 512, D), lambda i: (l, 0, 0, 0)),
                  pl.BlockSpec((1, D, D), lambda i: (l, 0, 0))],
        out_specs=pl.BlockSpec((MG_TM, D), row),
        out_shape=jax.ShapeDtypeStruct((R, D), F32),
        compiler_params=_cp(1),
        name="merge",
    )(x, mods_l, *dn_o, proj, og_c, og_l, om_c, om_l, pbg, ng, wb, wo)


FF_TM = 512
FF_CHUNKS = ((0, 1536), (1536, FF_DENSE))


def _ffn_body(x_ref, m_ref, g_ref, w1_ref, w3_ref, w2_ref, o_ref):
    x = x_ref[...]
    h = _modnorm(x, g_ref[...], m_ref[0, 3:4, :], m_ref[0, 4:5, :]).astype(BF16)
    y = None
    for c0, c1 in FF_CHUNKS:
        a = _silu(_dot(h, w1_ref[0, :, c0:c1])) * _dot(h, w3_ref[0, :, c0:c1])
        yc = _dot(a.astype(BF16), w2_ref[0, c0:c1, :])
        y = yc if y is None else y + yc
    o_ref[...] = x + m_ref[0, 5:6, :] * y


def _ffn(x, mods_l, g, w1, w3, w2, j):
    once = pl.Buffered(1)
    return pl.pallas_call(
        _ffn_body,
        grid=(R // FF_TM,),
        in_specs=[pl.BlockSpec((FF_TM, D), lambda i: (i, 0)),
                  pl.BlockSpec((1, 6, D), lambda i: (_mod_group(i * FF_TM), 0, 0)),
                  pl.BlockSpec((1, D), lambda i: (0, 0)),
                  pl.BlockSpec((1, D, FF_DENSE), lambda i: (j, 0, 0), pipeline_mode=once),
                  pl.BlockSpec((1, D, FF_DENSE), lambda i: (j, 0, 0), pipeline_mode=once),
                  pl.BlockSpec((1, FF_DENSE, D), lambda i: (j, 0, 0), pipeline_mode=once)],
        out_specs=pl.BlockSpec((FF_TM, D), lambda i: (i, 0)),
        out_shape=jax.ShapeDtypeStruct((R, D), F32),
        compiler_params=_cp(1),
        name="ffn",
    )(x, mods_l, g, w1, w3, w2)


RT_TM = 512
MOE_TG, MOE_TF = 512, 1792
MOE_NT = 2 * R // MOE_TG + N_EXP
MOE_ROWS = MOE_NT * MOE_TG
CB_TM = 256


def _router_body(x_ref, m_ref, g_ref, rw_ref, rb_ref, hn_ref, route_ref, cnt_ref, base_scr):
    @pl.when(pl.program_id(0) == 0)
    def _():
        base_scr[...] = jnp.zeros_like(base_scr)

    h = _modnorm(x_ref[...], g_ref[...], m_ref[0, 3:4, :], m_ref[0, 4:5, :])
    hn_ref[...] = h
    logits = _dot3(h, rw_ref[...]) + rb_ref[...]
    lane = lax.broadcasted_iota(jnp.int32, logits.shape, 1)
    m1 = jnp.max(logits, axis=-1, keepdims=True)
    i1 = jnp.min(jnp.where(logits == m1, lane, 128), axis=-1, keepdims=True)
    sel1 = lane == i1
    rest = jnp.where(sel1, -jnp.inf, logits)
    m2 = jnp.max(rest, axis=-1, keepdims=True)
    i2 = jnp.min(jnp.where(rest == m2, lane, 128), axis=-1, keepdims=True)
    sel2 = lane == i2
    e2 = jnp.exp(m2 - m1)
    p1 = 1.0 / (1.0 + e2)
    p2 = e2 / (1.0 + e2)

    cnt = jnp.where(jnp.logical_or(sel1, sel2), 1.0, 0.0)
    r = lax.broadcasted_iota(jnp.int32, (RT_TM, RT_TM), 0)
    c = lax.broadcasted_iota(jnp.int32, (RT_TM, RT_TM), 1)
    before = jnp.where(r > c, 1.0, 0.0).astype(BF16)
    seen = base_scr[...] + _dot(before, cnt.astype(BF16))
    rank1 = jnp.sum(jnp.where(sel1, seen, 0.0), axis=-1, keepdims=True)
    rank2 = jnp.sum(jnp.where(sel2, seen, 0.0), axis=-1, keepdims=True)
    vals = (i1.astype(F32), i2.astype(F32), rank1, rank2, p1, p2)
    route = jnp.zeros(logits.shape, F32)
    for k, val in enumerate(vals):
        route = jnp.where(lane == k, val, route)
    route_ref[...] = route
    base_scr[...] += jnp.sum(cnt, axis=0, keepdims=True)
    cnt_ref[...] = base_scr[...]


def _router(x, mods_l, g, rw128, rb128):
    return pl.pallas_call(
        _router_body,
        grid=(R // RT_TM,),
        in_specs=[pl.BlockSpec((RT_TM, D), lambda i: (i, 0)),
                  pl.BlockSpec((1, 6, D), lambda i: (_mod_group(i * RT_TM), 0, 0)),
                  pl.BlockSpec((1, D), lambda i: (0, 0)),
                  pl.BlockSpec((D, 128), lambda i: (0, 0)),
                  pl.BlockSpec((1, 128), lambda i: (0, 0))],
        out_specs=[pl.BlockSpec((RT_TM, D), lambda i: (i, 0)),
                   pl.BlockSpec((RT_TM, 128), lambda i: (i, 0)),
                   pl.BlockSpec((1, 128), lambda i: (0, 0))],
        out_shape=[jax.ShapeDtypeStruct((R, D), F32),
                   jax.ShapeDtypeStruct((R, 128), F32),
                   jax.ShapeDtypeStruct((1, 128), F32)],
        scratch_shapes=[pltpu.VMEM((1, 128), F32)],
        compiler_params=_cp(1),
        name="moe_router",
    )(x, mods_l, g, rw128, rb128)


def _rows_copy(src_hbm, dst, sem, n):
    return pltpu.make_async_copy(src_hbm.at[pl.ds(0, n)], dst, sem)


def _start_row_gather(idx_ref, n, src_hbm, dst, sem, inline=False, both_queues=False):
    def issue(i, priority):
        pltpu.make_async_copy(src_hbm.at[pl.ds(idx_ref[0, 0, i], 1)], dst.at[pl.ds(i, 1)], sem).start(
            priority=priority)

    if inline:
        for i in range(n):
            issue(i, i % 2 if both_queues else 0)
    else:
        def body(i, carry):
            issue(i, 0)
            return carry

        lax.fori_loop(0, n, body, 0, unroll=8)


def _gather_rows(idx_ref, n, src_hbm, dst, sem):
    _start_row_gather(idx_ref, n, src_hbm, dst, sem, inline=True, both_queues=True)
    _rows_copy(src_hbm, dst, sem, n).wait()


def _stage_to_bf16(src_hbm, dst, stage, sems):
    ch = stage.shape[1]
    n = src_hbm.shape[0] // ch
    copy = lambda c: pltpu.make_async_copy(src_hbm.at[pl.ds(c * ch, ch)], stage.at[c % 2], sems.at[c % 2])
    copy(0).start()
    for c in range(n):
        if c + 1 < n:
            copy(c + 1).start()
        copy(c).wait()
        dst[c * ch:(c + 1) * ch, :] = stage[c % 2].astype(BF16)


def _experts_body(j, te_ref, tv_ref, src_ref, nsrc_ref, hn_hbm, w1_hbm, w3_hbm, w2_hbm, ys_ref,
                  xg_scr, w1_ref, w3_ref, w2_ref, st13, st2, sems, wsems):
    t = pl.program_id(0)
    slot = t % 2
    valid = tv_ref[t] > 0
    e = te_ref[t]
    new_expert = jnp.logical_or(t == 0, e != te_ref[jnp.maximum(t - 1, 0)])

    @pl.when(jnp.logical_and(valid, new_expert))
    def _():
        _stage_to_bf16(w1_hbm.at[j, e], w1_ref, st13, wsems)
        _stage_to_bf16(w3_hbm.at[j, e], w3_ref, st13, wsems)
        _stage_to_bf16(w2_hbm.at[j, e], w2_ref, st2, wsems)

    requested = jnp.where(t == 0, valid, tv_ref[jnp.maximum(t - 1, 0)] > 0)

    @pl.when(jnp.logical_and(t == 0, valid))
    def _():
        _start_row_gather(src_ref, MOE_TG, hn_hbm, xg_scr.at[0], sems.at[0])

    @pl.when(requested)
    def _():
        _rows_copy(hn_hbm, xg_scr.at[slot], sems.at[slot], MOE_TG).wait()

    @pl.when(valid)
    def _():
        xb = xg_scr[slot].astype(BF16)
        _start_row_gather(nsrc_ref, MOE_TG, hn_hbm, xg_scr.at[1 - slot], sems.at[1 - slot], inline=True)
        y = None
        for c in range(FF_EXP // MOE_TF):
            cs = slice(c * MOE_TF, (c + 1) * MOE_TF)
            a = _silu(_dot(xb, w1_ref[:, cs])) * _dot(xb, w3_ref[:, cs])
            yc = _dot(a.astype(BF16), w2_ref[cs, :])
            y = yc if y is None else y + yc
        ys_ref[...] = y

    @pl.when(jnp.logical_not(valid))
    def _():
        ys_ref[...] = jnp.zeros_like(ys_ref)

    @pl.when(jnp.logical_and(t == MOE_NT - 1, valid))
    def _():
        _rows_copy(hn_hbm, xg_scr.at[1 - slot], sems.at[1 - slot], MOE_TG).wait()


MOE_WCH13, MOE_WCH2 = 128, 448


def _experts(tile_e, tile_v, src, hn, w1, w3, w2, j):
    hbm = pl.BlockSpec(memory_space=pl.ANY)
    grid_spec = pltpu.PrefetchScalarGridSpec(
        num_scalar_prefetch=2,
        grid=(MOE_NT,),
        in_specs=[pl.BlockSpec((1, 1, MOE_TG), lambda t, te, tv: (t, 0, 0), memory_space=pltpu.SMEM),
                  pl.BlockSpec((1, 1, MOE_TG), lambda t, te, tv: (t + 1, 0, 0), memory_space=pltpu.SMEM),
                  hbm, hbm, hbm, hbm],
        out_specs=pl.BlockSpec((MOE_TG, D), lambda t, te, tv: (t, 0)),
        scratch_shapes=[pltpu.VMEM((2, MOE_TG, D), F32),
                        pltpu.VMEM((D, FF_EXP), BF16), pltpu.VMEM((D, FF_EXP), BF16),
                        pltpu.VMEM((FF_EXP, D), BF16),
                        pltpu.VMEM((2, MOE_WCH13, FF_EXP), F32), pltpu.VMEM((2, MOE_WCH2, D), F32),
                        pltpu.SemaphoreType.DMA((2,)), pltpu.SemaphoreType.DMA((2,))],
    )
    return pl.pallas_call(
        functools.partial(_experts_body, j),
        grid_spec=grid_spec,
        out_shape=jax.ShapeDtypeStruct((MOE_ROWS, D), F32),
        compiler_params=_cp(1),
        name="moe_experts",
    )(tile_e, tile_v, src, src, hn, w1, w3, w2)


def _combine_body(pos_ref, x_ref, m_ref, route_ref, ys_hbm, o_ref, buf, sem):
    _gather_rows(pos_ref, 2 * CB_TM, ys_hbm, buf, sem)
    route = route_ref[...]
    y = route[:, 4:5] * buf[0:CB_TM, :] + route[:, 5:6] * buf[CB_TM:2 * CB_TM, :]
    o_ref[...] = x_ref[...] + m_ref[0, 5:6, :] * y


def _combine(pos, x, mods_l, route, ys):
    return pl.pallas_call(
        _combine_body,
        grid=(R // CB_TM,),
        in_specs=[pl.BlockSpec((1, 1, 2 * CB_TM), lambda i: (i, 0, 0), memory_space=pltpu.SMEM),
                  pl.BlockSpec((CB_TM, D), lambda i: (i, 0)),
                  pl.BlockSpec((1, 6, D), lambda i: (_mod_group(i * CB_TM), 0, 0)),
                  pl.BlockSpec((CB_TM, 128), lambda i: (i, 0)),
                  pl.BlockSpec(memory_space=pl.ANY)],
        out_specs=pl.BlockSpec((CB_TM, D), lambda i: (i, 0)),
        out_shape=jax.ShapeDtypeStruct((R, D), F32),
        scratch_shapes=[pltpu.VMEM((2 * CB_TM, D), F32), pltpu.SemaphoreType.DMA(())],
        compiler_params=_cp(1),
        name="moe_combine",
    )(pos, x, mods_l, route, ys)


def _moe(x, mods_l, g, rw128, rb128, w1, w3, w2, j):
    hn, route, cnt = _router(x, mods_l, g, rw128, rb128)
    eid = route[:, 0:2].astype(jnp.int32)
    rank = route[:, 2:4].astype(jnp.int32)
    counts = cnt[0, :N_EXP].astype(jnp.int32)
    gsize = (counts + MOE_TG - 1) // MOE_TG * MOE_TG
    gend = jnp.cumsum(gsize)
    pos = (gend - gsize)[eid] + rank
    tile_start = jnp.arange(MOE_NT, dtype=jnp.int32) * MOE_TG
    tile_e = jnp.minimum(jnp.sum(tile_start[:, None] >= gend[None, :], axis=1), N_EXP - 1).astype(jnp.int32)
    tile_v = (tile_start < gend[-1]).astype(jnp.int32)
    tok = jnp.broadcast_to(jnp.arange(R, dtype=jnp.int32)[:, None], (R, 2))
    src = jnp.zeros((MOE_ROWS + MOE_TG,), jnp.int32).at[pos.reshape(-1)].set(tok.reshape(-1))
    ys = _experts(tile_e, tile_v, src.reshape(MOE_NT + 1, 1, MOE_TG), hn, w1, w3, w2, j)
    pos_t = pos.reshape(R // CB_TM, CB_TM, 2).transpose(0, 2, 1).reshape(R // CB_TM, 1, 2 * CB_TM)
    return _combine(pos_t, x, mods_l, route, ys)


FN_TM = 1024


def _final_body(x_ref, g_ref, o_ref):
    x = x_ref[...]
    ms = jnp.mean(x * x, axis=-1, keepdims=True)
    o_ref[...] = x * lax.rsqrt(ms + EPS) * g_ref[...]


def _final_norm(x, g, row_off, n_rows):
    bo = row_off // FN_TM
    return pl.pallas_call(
        _final_body,
        grid=(n_rows // FN_TM,),
        in_specs=[pl.BlockSpec((FN_TM, D), lambda i: (bo + i, 0)), pl.BlockSpec((1, D), lambda i: (0, 0))],
        out_specs=pl.BlockSpec((FN_TM, D), lambda i: (i, 0)),
        out_shape=jax.ShapeDtypeStruct((n_rows, D), F32),
        compiler_params=_cp(1),
        name="final_norm",
    )(x, g)


def _rope_tables(n_tokens, rot_dim):
    t = np.arange(n_tokens)
    row = (t // GRID_W).astype(np.float32)
    col = (t % GRID_W).astype(np.float32)
    n_freq = rot_dim // 4
    inv = (ROPE_THETA ** (-jnp.arange(n_freq, dtype=F32) / n_freq))
    ang = jnp.concatenate([jnp.asarray(row)[:, None] * inv, jnp.asarray(col)[:, None] * inv], axis=-1)
    cos, sin = jnp.cos(ang), jnp.sin(ang)
    return jnp.concatenate([cos, cos], axis=-1), jnp.concatenate([-sin, sin], axis=-1)


def kernel(x_prompt, x_sample, c, cache_gqa_k, cache_gqa_v, cache_mla_ckv, cache_mla_krope, state_delta, c_ctx, w_mod, b_mod, norm1_g, norm2_g, w_in, dn_conv_w, dn_a_log, dn_dt_bias, dn_norm_g, gqa_q_norm, gqa_k_norm, mla_q_norm, mla_kv_norm, mla_w_uq, mla_w_ukv, w_branch, w_out, ffd_w1, ffd_w3, ffd_w2, router_w, router_b, moe_w1, moe_w3, moe_w2, final_g):
    x = jnp.concatenate([x_prompt.reshape(RC, D), x_sample.reshape(RL, D)], axis=0)
    cond8 = jnp.concatenate([c_ctx[None, :], c, jnp.zeros((3, D), F32)], axis=0)
    mods = _mods(cond8, w_mod, b_mod).reshape(DEPTH, 8, 6, D)

    cg, sg = _rope_tables(DEC_SEQ, GQA_HD)
    gqa_rope = (jnp.tile(cg, (1, GQA_H)), jnp.tile(sg, (1, GQA_H)),
                jnp.tile(cg, (1, GQA_KV)), jnp.tile(sg, (1, GQA_KV)))
    cm, sm_ = _rope_tables(DEC_SEQ, MLA_ROPE)
    padk = lambda t: jnp.pad(t, ((0, 0), (16, 128 - 16 - MLA_ROPE)))
    mla_rope = (jnp.tile(cm, (1, MLA_H)), jnp.tile(sm_, (1, MLA_H)),
                jnp.pad(cm, ((0, 0), (16, 128 - 16 - MLA_ROPE)), constant_values=1.0), padk(sm_))

    w_in_bf, w_branch_bf, w_out_bf = w_in.astype(BF16), w_branch.astype(BF16), w_out.astype(BF16)
    ffd_bf = (ffd_w1.astype(BF16), ffd_w3.astype(BF16), ffd_w2.astype(BF16))

    new_k, new_v, new_ckv, new_kr, new_s = [], [], [], [], []
    for l in range(DEPTH):
        pbg, proj = _inproj(x, mods[l], norm1_g[l][None, :], w_in_bf, l)

        pad128 = lambda v: jnp.pad(v.reshape(1, 8), ((0, 0), (8, 112)))
        padt = lambda v: jnp.pad(v.reshape(8, 1), ((8, 0), (0, 0)))
        sm_t = proj[:, C_SM:C_SM + 16].T
        dn_u, dn_w, dn_qg, dn_kd, dn_ai, dn_bg = _dn_prep(
            proj, sm_t, dn_conv_w[l], pad128(dn_a_log[l]), pad128(dn_dt_bias[l]),
            padt(dn_a_log[l]), padt(dn_dt_bias[l]))
        *dn_o, s_c = _dn_scan(dn_u, dn_w, dn_qg, dn_kd, dn_ai, dn_bg, state_delta[:, l])

        qw = jnp.tile(gqa_q_norm[l][None, :], (1, GQA_H))
        kw = jnp.tile(gqa_k_norm[l][None, :], (1, GQA_KV))
        o_g_c, kn_c = _gqa(proj, qw, kw, SEQ, SEQ, BATCH, 0)
        (o_g_l,) = _gqa(proj, qw, kw, DEC_SEQ, 256, DEC_BATCH, RC, rope=gqa_rope,
                        cache=(cache_gqa_k[:, l].reshape(DEC_BATCH, PAST, 128),
                               cache_gqa_v[:, l].reshape(DEC_BATCH, PAST, 128)))

        wq = mla_w_uq[l].reshape(MLA_QL, MLA_H, MLA_DQ)
        wq = jnp.concatenate([wq[:, :, :MLA_NOPE].reshape(MLA_QL, -1), wq[:, :, MLA_NOPE:].reshape(MLA_QL, -1)],
                             axis=1).astype(BF16)
        wkv = mla_w_ukv[l].reshape(MLA_KVL, MLA_H, MLA_NOPE + MLA_V)
        wkv = jnp.concatenate([wkv[:, :, :MLA_NOPE].reshape(MLA_KVL, -1), wkv[:, :, MLA_NOPE:].reshape(MLA_KVL, -1)],
                              axis=1).astype(BF16)
        mqw, mkvw = mla_q_norm[l][None, :], mla_kv_norm[l][None, :]
        o_m_c, ckv_c = _mla(proj, wq, wkv, mqw, mkvw, SEQ, SEQ, BATCH, 0)
        (o_m_l,) = _mla(proj, wq, wkv, mqw, mkvw, DEC_SEQ, 256, DEC_BATCH, RC, rope=mla_rope,
                        cache=(cache_mla_ckv[:, l], cache_mla_krope[:, l]))
        x = _merge(x, mods[l], dn_o, proj, pbg, o_g_c, o_g_l, o_m_c, o_m_l, dn_norm_g[l][None, :],
                   w_branch_bf, w_out_bf, l)

        j = l // 2
        if l % 2 == 0:
            x = _ffn(x, mods[l], norm2_g[l][None, :], *ffd_bf, j)
        else:
            rw128 = jnp.pad(router_w[j], ((0, 0), (0, 128 - N_EXP)))
            rb128 = jnp.pad(router_b[j][None, :], ((0, 0), (0, 128 - N_EXP)), constant_values=-jnp.inf)
            x = _moe(x, mods[l], norm2_g[l][None, :], rw128, rb128, moe_w1, moe_w3, moe_w2, j)

        new_k.append(kn_c.reshape(BATCH, SEQ, GQA_KV, GQA_HD))
        new_v.append(proj[:RC, C_GKV + 128:C_GKV + 256].reshape(BATCH, SEQ, GQA_KV, GQA_HD))
        new_ckv.append(ckv_c.reshape(BATCH, SEQ, MLA_KVL))
        new_kr.append(proj[:RC, C_SM + 16:C_SM + 16 + MLA_ROPE].reshape(BATCH, SEQ, MLA_ROPE))
        new_s.append(s_c)

    y_c = _final_norm(x, final_g[None, :], 0, RC)
    y_l = _final_norm(x, final_g[None, :], RC, RL)
    return (y_c.reshape(BATCH, SEQ, D), y_l.reshape(DEC_BATCH, DEC_SEQ, D),
            jnp.stack(new_k, axis=1), jnp.stack(new_v, axis=1), jnp.stack(new_ckv, axis=1),
            jnp.stack(new_kr, axis=1), jnp.stack(new_s, axis=1))
```

```python
import functools
import math

import jax
import jax.numpy as jnp
import numpy as np
from jax import lax
from jax.experimental import pallas as pl
from jax.experimental.pallas import tpu as pltpu

F32 = jnp.float32
BF16 = jnp.bfloat16

D = 1024
BATCH, SEQ = 32, 256
DEC_BATCH, DEC_SEQ = 4, 2048
DEPTH = 4
PAST = 512
GRID_W = 64
ROPE_THETA = 10000.0
EPS = 1e-6
DN_H, DN_DK, DN_DV, DN_C = 4, 128, 128, 64
GQA_H, GQA_KV, GQA_HD = 8, 2, 64
MLA_H, MLA_QL, MLA_KVL, MLA_NOPE, MLA_ROPE, MLA_V = 8, 256, 128, 64, 32, 64
FF_DENSE, N_EXP, FF_EXP = 2816, 8, 3584

RC = BATCH * SEQ
RL = DEC_BATCH * DEC_SEQ
R = RC + RL

N_BG = 3 * D
C_QKV, C_DG, C_GQ, C_GKV, C_MCQ, C_MCKV, C_SM = 0, 1536, 2048, 2560, 2816, 3072, 3200
N_PROJ = 3328
NP = N_BG + N_PROJ

VMEM_LIMIT = 56 * 1024 * 1024


def _cp(n_grid):
    return pltpu.CompilerParams(dimension_semantics=("arbitrary",) * n_grid,
                                vmem_limit_bytes=VMEM_LIMIT)


def _dot(a, b):
    return jnp.dot(a, b, preferred_element_type=F32)


def _bdot(a, b):
    return jnp.dot(a.astype(BF16), b.astype(BF16), preferred_element_type=F32)


def _bdot_nt(a, b):
    return lax.dot_general(a.astype(BF16), b.astype(BF16), (((1,), (1,)), ((), ())),
                           preferred_element_type=F32)


def _bdot_tn(a, b):
    return lax.dot_general(a.astype(BF16), b.astype(BF16), (((0,), (0,)), ((), ())),
                           preferred_element_type=F32)


def _split2(a):
    hi = a.astype(BF16)
    lo = (a - hi.astype(F32)).astype(BF16)
    return hi, lo


def _split3(a):
    a1 = a.astype(BF16)
    r1 = a - a1.astype(F32)
    a2 = r1.astype(BF16)
    a3 = (r1 - a2.astype(F32)).astype(BF16)
    return a1, a2, a3


def _dot3(a, b):
    ah, al = _split2(a)
    bh, bl = _split2(b)
    return _dot(ah, bh) + (_dot(ah, bl) + _dot(al, bh))


def _mask_dot_r(mask_bf, g):
    g1, g2, g3 = _split3(g)
    return _dot(mask_bf, g1) + (_dot(mask_bf, g2) + _dot(mask_bf, g3))


def _mask_dot_l(g, mask_bf):
    g1, g2, g3 = _split3(g)
    return _dot(g1, mask_bf) + (_dot(g2, mask_bf) + _dot(g3, mask_bf))


def _sigmoid(x):
    return 1.0 / (1.0 + jnp.exp(-x))


def _silu(x):
    return x * _sigmoid(x)


def _softplus(x):
    return jnp.maximum(x, 0.0) + jnp.log1p(jnp.exp(-jnp.abs(x)))


def _mod_group(row0):
    return jnp.where(row0 < RC, 0, 1 + (row0 - RC) // DEC_SEQ)


MODS_TN = 1536


def _mods_body(c_ref, w_ref, b_ref, o_ref):
    s = _silu(c_ref[...])
    o_ref[0] = _bdot(s, w_ref[0]) + b_ref[0]


def _mods(cond8, w_mod, b_mod):
    nj = 6 * D // MODS_TN
    return pl.pallas_call(
        _mods_body,
        grid=(DEPTH, nj),
        in_specs=[pl.BlockSpec((8, D), lambda l, j: (0, 0)),
                  pl.BlockSpec((1, D, MODS_TN), lambda l, j: (l, 0, j)),
                  pl.BlockSpec((1, 1, MODS_TN), lambda l, j: (l, 0, j))],
        out_specs=pl.BlockSpec((1, 8, MODS_TN), lambda l, j: (l, 0, j)),
        out_shape=jax.ShapeDtypeStruct((DEPTH, 8, 6 * D), F32),
        compiler_params=_cp(2),
        name="mods",
    )(cond8, w_mod, b_mod.reshape(DEPTH, 1, 6 * D))


IN_TM = 512
IN_CHUNK = 1536


def _modnorm(x, g, shift, scale):
    ms = jnp.mean(x * x, axis=-1, keepdims=True)
    y = x * lax.rsqrt(ms + EPS) * g
    return y * (1.0 + scale) + shift


W_IN_PIECES = ((3248, 6320), (0, 2048), (2064, 3216), (2048, 2064), (3216, 3248))
W_IN_COLS = 6320


def _inproj_body(x_ref, m_ref, g_ref, wsrc_ref, bg_ref, o_ref, w_ref):
    @pl.when(pl.program_id(0) == 0)
    def _():
        at = 0
        for a, b in W_IN_PIECES:
            w_ref[:, at:at + (b - a)] = wsrc_ref[0, :, a:b]
            at += b - a
        w_ref[:, at:NP] = jnp.zeros((D, NP - at), BF16)

    h = _modnorm(x_ref[...], g_ref[...], m_ref[0, 0:1, :], m_ref[0, 1:2, :]).astype(BF16)
    for c0 in range(0, N_BG, IN_CHUNK):
        bg_ref[:, c0:c0 + IN_CHUNK] = _dot(h, w_ref[:, c0:c0 + IN_CHUNK]).astype(BF16)
    for c0 in range(0, N_PROJ, IN_CHUNK):
        c1 = min(c0 + IN_CHUNK, N_PROJ)
        o_ref[:, c0:c1] = _dot(h, w_ref[:, N_BG + c0:N_BG + c1])


def _inproj(x, mods_l, g, w_bf, l):
    return pl.pallas_call(
        _inproj_body,
        grid=(R // IN_TM,),
        in_specs=[pl.BlockSpec((IN_TM, D), lambda i: (i, 0)),
                  pl.BlockSpec((1, 6, D), lambda i: (_mod_group(i * IN_TM), 0, 0)),
                  pl.BlockSpec((1, D), lambda i: (0, 0)),
                  pl.BlockSpec((1, D, W_IN_COLS), lambda i: (l, 0, 0), pipeline_mode=pl.Buffered(1))],
        out_specs=[pl.BlockSpec((IN_TM, N_BG), lambda i: (i, 0)),
                   pl.BlockSpec((IN_TM, N_PROJ), lambda i: (i, 0))],
        out_shape=[jax.ShapeDtypeStruct((R, N_BG), BF16), jax.ShapeDtypeStruct((R, N_PROJ), F32)],
        scratch_shapes=[pltpu.VMEM((D, NP), BF16)],
        compiler_params=_cp(1),
        name="inproj",
    )(x, mods_l, g, w_bf)


PREP_TM = 256
PREP_NC = PREP_TM // DN_C
PREP_LOCKSTEP = 4


def _prep_body(x_ref, xp_ref, xn_ref, sm_ref, smt_ref, cw_ref, al_ref, dt_ref, alt_ref, dtt_ref,
               u_o, w_o, qg_o, kd_o, ai_o, bg_o, qkv_o, gr_o):
    i = pl.program_id(0)
    n_ctx = RC // PREP_TM
    per_seq = DEC_SEQ // PREP_TM
    is_ctx = i < n_ctx
    j = (i - n_ctx) % per_seq
    first = jnp.logical_or(is_ctx, j == 0)
    last = jnp.logical_or(is_ctx, j == per_seq - 1)

    x = x_ref[...]
    prev_row = jnp.where(first, 0.0, xp_ref[7:8, :])
    next_row = jnp.where(last, 0.0, xn_ref[0:1, :])
    row = lax.broadcasted_iota(jnp.int32, (PREP_TM, 1), 0)
    xm = jnp.where(row == 0, prev_row, pltpu.roll(x, 1, axis=0))
    xq = jnp.where(row == PREP_TM - 1, next_row, pltpu.roll(x, PREP_TM - 1, axis=0))
    w = cw_ref[...]
    y = _silu(w[0:1] * xm + w[1:2] * x + w[2:3] * xq)

    for h in range(DN_H):
        qh = y[:, h * DN_DK:(h + 1) * DN_DK]
        qn = qh * lax.rsqrt(jnp.sum(qh * qh, axis=-1, keepdims=True) + EPS) * (DN_DK ** -0.5)
        qkv_o[:, h * DN_DK:(h + 1) * DN_DK] = qn
        kh = y[:, 512 + h * DN_DK:512 + (h + 1) * DN_DK]
        kn = kh * lax.rsqrt(jnp.sum(kh * kh, axis=-1, keepdims=True) + EPS)
        qkv_o[:, 512 + h * DN_DK:512 + (h + 1) * DN_DK] = kn
    qkv_o[:, 1024:1536] = y[:, 1024:1536]

    r = lax.broadcasted_iota(jnp.int32, (PREP_TM, PREP_TM), 0)
    c = lax.broadcasted_iota(jnp.int32, (PREP_TM, PREP_TM), 1)
    same = (r // DN_C) == (c // DN_C)
    low = jnp.where(jnp.logical_and(same, r >= c), 1.0, 0.0).astype(BF16)
    upp = jnp.where(jnp.logical_and(same, r <= c), 1.0, 0.0).astype(BF16)

    sm = sm_ref[...]
    beta = _sigmoid(sm)
    g = -jnp.exp(al_ref[...]) * _softplus(sm + dt_ref[...])
    gc_f = _mask_dot_r(low, g)
    gc_b = _mask_dot_r(upp, g)
    lane = lax.broadcasted_iota(jnp.int32, (1, 128), 1)
    bg_o[...] = jnp.where(lane < 8, beta, jnp.where(lane < 12, gc_f, jnp.where(lane < 16, gc_b, 0.0)))

    gt = -jnp.exp(alt_ref[...]) * _softplus(smt_ref[...] + dtt_ref[...])
    gct_f = _mask_dot_l(gt, upp)
    gct_b = _mask_dot_l(gt, low)
    sub = lax.broadcasted_iota(jnp.int32, (16, 1), 0)
    gct = jnp.where(sub < 12, gct_f, gct_b)
    for k in range(PREP_NC):
        gr_o[k] = gct[8:16, k * DN_C:(k + 1) * DN_C]

    rr = lax.broadcasted_iota(jnp.int32, (DN_C, DN_C), 0)
    cc = lax.broadcasted_iota(jnp.int32, (DN_C, DN_C), 1)

    def chunk_group(gi, carry):
        prob = [(cj, d, h) for cj in range(PREP_LOCKSTEP) for d in range(2) for h in range(DN_H)]
        ci = [gi * PREP_LOCKSTEP + cj for cj in range(PREP_LOCKSTEP)]
        rows_c = [pl.ds(pl.multiple_of(c_ * DN_C, DN_C), DN_C) for c_ in ci]
        bgc = [bg_o[r_, :] for r_ in rows_c]
        grow_all = [gr_o[c_] for c_ in ci]
        rows = [rows_c[cj] for cj, d, h in prob]
        q = [qkv_o[rows_c[cj], h * DN_DK:(h + 1) * DN_DK] for cj, d, h in prob]
        k = [qkv_o[rows_c[cj], 512 + h * DN_DK:512 + (h + 1) * DN_DK] for cj, d, h in prob]
        v = [qkv_o[rows_c[cj], 1024 + h * DN_DV:1024 + (h + 1) * DN_DV] for cj, d, h in prob]
        beta = [bgc[cj][:, d * DN_H + h:d * DN_H + h + 1] for cj, d, h in prob]
        gcol = [bgc[cj][:, 8 + d * DN_H + h:9 + d * DN_H + h] for cj, d, h in prob]
        grow = [grow_all[cj][d * DN_H + h:d * DN_H + h + 1, :] for cj, d, h in prob]
        incl = [(rr >= cc) if d == 0 else (rr <= cc) for cj, d, h in prob]
        strict = [(rr > cc) if d == 0 else (rr < cc) for cj, d, h in prob]
        n = len(prob)
        decay = [jnp.exp(jnp.where(incl[i], gcol[i] - grow[i], -1e30)) for i in range(n)]
        kb = [k[i] * beta[i] for i in range(n)]
        kk = [_bdot_nt(kb[i], k[i]) for i in range(n)]
        qk = [_bdot_nt(q[i], k[i]) for i in range(n)]
        a = [jnp.where(strict[i], kk[i] * decay[i], 0.0) for i in range(n)]
        blk = lambda b: (rr // b) == (cc // b)
        eye = jnp.where(rr == cc, 1.0, 0.0)
        p = [jnp.where(blk(8), a[i], 0.0) for i in range(n)]
        t = [eye - p[i] for i in range(n)]
        for _ in range(2):
            p = [_bdot(p[i], p[i]) for i in range(n)]
            t = [t[i] + _bdot(t[i], p[i]) for i in range(n)]
        for b in (16, 32, 64):
            m = jnp.logical_and(blk(b), jnp.logical_not(blk(b // 2)))
            tl = [_bdot(t[i], jnp.where(m, a[i], 0.0)) for i in range(n)]
            t = [t[i] - _bdot(tl[i], t[i]) for i in range(n)]
        egc = [jnp.exp(gcol[i]) for i in range(n)]
        glast = [gcol[i][DN_C - 1:DN_C, :] if prob[i][1] == 0 else gcol[i][0:1, :] for i in range(n)]
        uw = [_bdot(t[i], jnp.concatenate([v[i] * beta[i], kb[i] * egc[i]], axis=1)) for i in range(n)]
        for i, (cj, d, h) in enumerate(prob):
            u_o[d, rows[i], h * DN_DV:(h + 1) * DN_DV] = uw[i][:, 0:DN_DV]
            w_o[d, rows[i], h * DN_DK:(h + 1) * DN_DK] = uw[i][:, DN_DV:DN_DV + DN_DK].astype(BF16)
            qg_o[d, rows[i], h * DN_DK:(h + 1) * DN_DK] = (q[i] * egc[i]).astype(BF16)
            kd_o[d, rows[i], h * DN_DK:(h + 1) * DN_DK] = (k[i] * jnp.exp(glast[i] - gcol[i])).astype(BF16)
            ai_o[d, rows[i], h * DN_C:(h + 1) * DN_C] = (qk[i] * decay[i]).astype(BF16)
        return carry

    lax.fori_loop(0, PREP_NC // PREP_LOCKSTEP, chunk_group, 0)


def _dn_prep(proj, sm_t, conv_w, al128, dt128, al_t, dt_t):
    nb8 = R // 8
    qb = C_QKV // 1536
    wide = lambda dt: jax.ShapeDtypeStruct((2, R, DN_H * DN_DK), dt)
    wide_spec = pl.BlockSpec((2, PREP_TM, DN_H * DN_DK), lambda i: (0, i, 0))
    return pl.pallas_call(
        _prep_body,
        grid=(R // PREP_TM,),
        in_specs=[pl.BlockSpec((PREP_TM, 1536), lambda i: (i, qb)),
                  pl.BlockSpec((8, 1536), lambda i: (jnp.maximum(i * (PREP_TM // 8) - 1, 0), qb)),
                  pl.BlockSpec((8, 1536), lambda i: (jnp.minimum((i + 1) * (PREP_TM // 8), nb8 - 1), qb)),
                  pl.BlockSpec((PREP_TM, 128), lambda i: (i, C_SM // 128)),
                  pl.BlockSpec((16, PREP_TM), lambda i: (0, i)),
                  pl.BlockSpec((3, 1536), lambda i: (0, 0)),
                  pl.BlockSpec((1, 128), lambda i: (0, 0)),
                  pl.BlockSpec((1, 128), lambda i: (0, 0)),
                  pl.BlockSpec((16, 1), lambda i: (0, 0)),
                  pl.BlockSpec((16, 1), lambda i: (0, 0))],
        out_specs=[wide_spec, wide_spec, wide_spec, wide_spec,
                   pl.BlockSpec((2, PREP_TM, DN_H * DN_C), lambda i: (0, i, 0)),
                   pl.BlockSpec((PREP_TM, 128), lambda i: (i, 0))],
        out_shape=[wide(F32), wide(BF16), wide(BF16), wide(BF16),
                   jax.ShapeDtypeStruct((2, R, DN_H * DN_C), BF16),
                   jax.ShapeDtypeStruct((R, 128), F32)],
        scratch_shapes=[pltpu.VMEM((PREP_TM, 1536), F32), pltpu.VMEM((PREP_NC, 8, DN_C), F32)],
        compiler_params=_cp(1),
        name="dn_prep",
    )(proj, proj, proj, proj, sm_t, conv_w, al128, dt128, al_t, dt_t)


SCAN_STEPS = RC // PREP_TM
SCAN_PER = DEC_SEQ // PREP_TM
assert RL // PREP_TM == SCAN_STEPS and SEQ == PREP_TM


def _scan_lat_bwd_block(i):
    return (i // SCAN_PER) * SCAN_PER + (SCAN_PER - 1 - i % SCAN_PER)


def _scan_body(*refs):
    streams = [refs[6 * k:6 * k + 6] for k in range(4)]
    s0_ref = refs[24]
    out_refs = refs[25:29]
    so_ref, s_scr = refs[29], refs[30]
    j = pl.program_id(0) % SCAN_PER

    for h in range(DN_H):
        s_scr[0, h] = jnp.zeros((DN_DK, DN_DV), F32)
        s_scr[1, h] = jnp.zeros((DN_DK, DN_DV), F32)

    @pl.when(j == 0)
    def _():
        for h in range(DN_H):
            s_scr[2, h] = s0_ref[0, 0, h]
            s_scr[3, h] = s0_ref[0, 1, h]

    prob = [(k, h) for k in range(4) for h in range(DN_H)]
    tn = (((0,), (0,)), ((), ()))

    def step(n, carry):
        rows = [pl.ds(pl.multiple_of((n if k % 2 == 0 else PREP_NC - 1 - n) * DN_C, DN_C), DN_C)
                for k in range(4)]
        bgc = [streams[k][5][rows[k], :] for k in range(4)]
        cols = [slice(h * DN_DK, (h + 1) * DN_DK) for k, h in prob]
        st = [s_scr[k, h] for k, h in prob]
        stb = [x.astype(BF16) for x in st]
        ws = [_dot(streams[k][1][0, rows[k], cols[i]], stb[i]) for i, (k, h) in enumerate(prob)]
        qs = [_dot(streams[k][2][0, rows[k], cols[i]], stb[i]) for i, (k, h) in enumerate(prob)]
        vb = [(streams[k][0][0, rows[k], cols[i]] - ws[i]).astype(BF16) for i, (k, h) in enumerate(prob)]
        av = [_dot(streams[k][4][0, rows[k], h * DN_C:(h + 1) * DN_C], vb[i]) for i, (k, h) in enumerate(prob)]
        kv = [lax.dot_general(streams[k][3][0, rows[k], cols[i]], vb[i], tn, preferred_element_type=F32)
              for i, (k, h) in enumerate(prob)]
        for i, (k, h) in enumerate(prob):
            d = k % 2
            gcol = bgc[k][:, 8 + d * DN_H + h:9 + d * DN_H + h]
            glast = gcol[DN_C - 1:DN_C, :] if d == 0 else gcol[0:1, :]
            s_scr[k, h] = st[i] * jnp.exp(glast) + kv[i]
            out_refs[k][rows[k], cols[i]] = qs[i] + av[i]
        return carry

    lax.fori_loop(0, PREP_NC, step, 0)

    for d in range(2):
        for h in range(DN_H):
            so_ref[0, d, h] = s_scr[d, h]


def _dn_scan(u, w, qg, kd, ai, bg, s0):
    row_of = (lambda i: i, lambda i: i,
              lambda i: SCAN_STEPS + i, lambda i: SCAN_STEPS + _scan_lat_bwd_block(i))
    in_specs, args = [], []
    for k in range(4):
        d, rb = k % 2, row_of[k]
        wide = pl.BlockSpec((1, PREP_TM, DN_H * DN_DK), lambda i, d=d, rb=rb: (d, rb(i), 0))
        narrow = pl.BlockSpec((1, PREP_TM, DN_H * DN_C), lambda i, d=d, rb=rb: (d, rb(i), 0))
        in_specs += [wide, wide, wide, wide, narrow, pl.BlockSpec((PREP_TM, 128), lambda i, rb=rb: (rb(i), 0))]
        args += [u, w, qg, kd, ai, bg]
    state_spec = lambda m: pl.BlockSpec((1, 2, DN_H, DN_DK, DN_DV), m)
    in_specs.append(state_spec(lambda i: (i // SCAN_PER, 0, 0, 0, 0)))
    o_spec = lambda m: pl.BlockSpec((PREP_TM, DN_H * DN_DV), m)
    o_shape = lambda n: jax.ShapeDtypeStruct((n, DN_H * DN_DV), F32)
    return pl.pallas_call(
        _scan_body,
        grid=(SCAN_STEPS,),
        in_specs=in_specs,
        out_specs=[o_spec(lambda i: (i, 0)), o_spec(lambda i: (i, 0)), o_spec(lambda i: (i, 0)),
                   o_spec(lambda i: (_scan_lat_bwd_block(i), 0)),
                   state_spec(lambda i: (i, 0, 0, 0, 0))],
        out_shape=[o_shape(RC), o_shape(RC), o_shape(RL), o_shape(RL),
                   jax.ShapeDtypeStruct((BATCH, 2, DN_H, DN_DK, DN_DV), F32)],
        scratch_shapes=[pltpu.VMEM((4, DN_H, DN_DK, DN_DV), F32)],
        compiler_params=_cp(1),
        name="dn_scan",
    )(*args, s0)


def _group_mean_matrix(width, group):
    r = lax.broadcasted_iota(jnp.int32, (width, width), 0)
    c = lax.broadcasted_iota(jnp.int32, (width, width), 1)
    return jnp.where((r // group) == (c // group), 1.0 / group, 0.0).astype(BF16)


def _group_rmsnorm(x, w, group):
    m = _group_mean_matrix(x.shape[-1], group)
    hi, lo = _split2(x * x)
    ms = _dot(hi, m) + _dot(lo, m)
    return x * lax.rsqrt(ms + EPS) * w


def _rope(x, cos, sin_signed, group):
    width = x.shape[-1]
    half = group // 2
    lane = lax.broadcasted_iota(jnp.int32, (1, width), 1)
    swapped = jnp.where((lane % group) < half,
                        pltpu.roll(x, width - half, axis=1), pltpu.roll(x, half, axis=1))
    return x * cos + swapped * sin_signed


def _attention_units(n, scores, values):
    outs = []
    s_next = scores(0)
    for u in range(n):
        s = s_next
        if u + 1 < n:
            s_next = scores(u + 1)
        m = jnp.max(s, axis=-1, keepdims=True)
        p = jnp.exp(s - m)
        l = jnp.sum(p, axis=-1, keepdims=True)
        outs.append(_dot(p.astype(BF16), values(u)) / l)
    return outs


GQA_G = GQA_H // GQA_KV
CTX_NS = 2


def _gqa_body(T, TQ, NS, latent, *refs):
    it = iter(refs)
    q_ref, kv_ref, qw_ref, kw_ref = next(it), next(it), next(it), next(it)
    if latent:
        cq_ref, sq_ref, ck_ref, sk_ref, kc_ref, vc_ref = (next(it) for _ in range(6))
    o_ref = next(it)
    kn_ref = None if latent else next(it)
    k_scr, v_scr = next(it), next(it)

    @pl.when(pl.program_id(1) == 0)
    def _():
        kv = kv_ref[...]
        k = _group_rmsnorm(kv[:, 0:128], kw_ref[...], GQA_HD)
        v = kv[:, 128:256]
        if latent:
            k = _rope(k, ck_ref[...], sk_ref[...], GQA_HD)
        else:
            kn_ref[...] = k
        for s in range(NS):
            for g in range(GQA_KV):
                sg = s * GQA_KV + g
                k_scr[sg, 0:T, :] = k[s * T:(s + 1) * T, g * GQA_HD:(g + 1) * GQA_HD].astype(BF16)
                v_scr[sg, 0:T, :] = v[s * T:(s + 1) * T, g * GQA_HD:(g + 1) * GQA_HD].astype(BF16)
                if latent:
                    k_scr[sg, T:T + PAST, :] = kc_ref[0, :, g * GQA_HD:(g + 1) * GQA_HD].astype(BF16)
                    v_scr[sg, T:T + PAST, :] = vc_ref[0, :, g * GQA_HD:(g + 1) * GQA_HD].astype(BF16)

    q = _group_rmsnorm(q_ref[...], qw_ref[...], GQA_HD)
    if latent:
        q = _rope(q, cq_ref[...], sq_ref[...], GQA_HD)
    q = q * (GQA_HD ** -0.5)
    head = lambda s, hh: q[s * TQ:(s + 1) * TQ, hh * GQA_HD:(hh + 1) * GQA_HD]
    units = [(s, u) for s in range(NS) for u in range(GQA_H // 2)]
    qu = [jnp.concatenate([head(s, 2 * u), head(s, 2 * u + 1)], axis=0) for s, u in units]
    kv_of = lambda i: units[i][0] * GQA_KV + (2 * units[i][1]) // GQA_G
    outs = _attention_units(len(units), lambda i: _bdot_nt(qu[i], k_scr[kv_of(i)]), lambda i: v_scr[kv_of(i)])
    for (s, u), o in zip(units, outs):
        for j in range(2):
            hh = 2 * u + j
            o_ref[s * TQ:(s + 1) * TQ, hh * GQA_HD:(hh + 1) * GQA_HD] = o[j * TQ:(j + 1) * TQ, :].astype(o_ref.dtype)


def _gqa(proj, qw, kw, T, TQ, n_seq, row_off, rope=None, cache=None, NS=1):
    latent = rope is not None
    tk = T + (PAST if latent else 0)
    nq = T // TQ
    assert NS == 1 or (nq == 1 and not latent)
    n_seq, TQ, TB = n_seq // NS, NS * TQ, NS * T
    qo = row_off // TQ
    so = row_off // TB
    in_specs = [pl.BlockSpec((TQ, 512), lambda i, j: (qo + i * nq + j, C_GQ // 512)),
                pl.BlockSpec((TB, 256), lambda i, j: (so + i, C_GKV // 256)),
                pl.BlockSpec((1, 512), lambda i, j: (0, 0)),
                pl.BlockSpec((1, 128), lambda i, j: (0, 0))]
    args = [proj, proj, qw, kw]
    if latent:
        cq, sq, ck, sk = rope
        kc, vc = cache
        in_specs += [pl.BlockSpec((TQ, 512), lambda i, j: (j, 0)),
                     pl.BlockSpec((TQ, 512), lambda i, j: (j, 0)),
                     pl.BlockSpec((T, 128), lambda i, j: (0, 0)),
                     pl.BlockSpec((T, 128), lambda i, j: (0, 0)),
                     pl.BlockSpec((1, PAST, 128), lambda i, j: (i, 0, 0)),
                     pl.BlockSpec((1, PAST, 128), lambda i, j: (i, 0, 0))]
        args += [cq, sq, ck, sk, kc, vc]
    out_specs = [pl.BlockSpec((TQ, 512), lambda i, j: (i * nq + j, 0))]
    out_shape = [jax.ShapeDtypeStruct((n_seq * TB, 512), BF16)]
    if not latent:
        out_specs.append(pl.BlockSpec((TB, 128), lambda i, j: (i, 0)))
        out_shape.append(jax.ShapeDtypeStruct((n_seq * TB, 128), F32))
    return pl.pallas_call(
        functools.partial(_gqa_body, T, TQ // NS, NS, latent),
        grid=(n_seq, nq),
        in_specs=in_specs,
        out_specs=out_specs,
        out_shape=out_shape,
        scratch_shapes=[pltpu.VMEM((NS * GQA_KV, tk, GQA_HD), BF16), pltpu.VMEM((NS * GQA_KV, tk, GQA_HD), BF16)],
        compiler_params=_cp(2),
        name="gqa_%d" % T,
    )(*args)


MLA_DQ = MLA_NOPE + MLA_ROPE
MLA_KV_ROWS = 512


def _mla_body(T, TQ, NS, latent, *refs):
    it = iter(refs)
    cq_ref, ckv_ref, sm_ref, wq_ref, wkv_ref, qw_ref, kvw_ref = (next(it) for _ in range(7))
    if latent:
        cosq_ref, sinq_ref, cosk_ref, sink_ref, cc_ref, kc_ref = (next(it) for _ in range(6))
    o_ref = next(it)
    cn_ref = None if latent else next(it)
    k_scr, v_scr = next(it), next(it)
    tk = T + (PAST if latent else 0)

    @pl.when(pl.program_id(1) == 0)
    def _():
        x = ckv_ref[...]
        ms = jnp.mean(x * x, axis=-1, keepdims=True)
        ckv = x * lax.rsqrt(ms + EPS) * kvw_ref[...]
        sm = sm_ref[...]
        if latent:
            sm = _rope_small(sm, cosk_ref[...], sink_ref[...])
        else:
            cn_ref[...] = ckv
        kr = sm[:, 16:16 + MLA_ROPE]

        def put(s, rows0, ckv_rows, kr_rows):
            n = ckv_rows.shape[0]
            kv = _bdot(ckv_rows, wkv_ref[...])
            krb = kr_rows.astype(BF16)
            for h in range(MLA_H):
                sh = s * MLA_H + h
                k_scr[sh, rows0:rows0 + n, 0:MLA_NOPE] = kv[:, h * MLA_NOPE:(h + 1) * MLA_NOPE].astype(BF16)
                k_scr[sh, rows0:rows0 + n, MLA_NOPE:MLA_DQ] = krb
                v_scr[sh, rows0:rows0 + n, :] = kv[:, 512 + h * MLA_V:512 + (h + 1) * MLA_V].astype(BF16)

        for s in range(NS):
            for r0 in range(0, T, MLA_KV_ROWS):
                r1 = min(r0 + MLA_KV_ROWS, T)
                put(s, r0, ckv[s * T + r0:s * T + r1], kr[s * T + r0:s * T + r1])
        if latent:
            put(0, T, cc_ref[0], kc_ref[0])

    x = cq_ref[...]
    ms = jnp.mean(x * x, axis=-1, keepdims=True)
    cq = x * lax.rsqrt(ms + EPS) * qw_ref[...]
    qf = _bdot(cq, wq_ref[...])
    qn = qf[:, 0:512]
    qr = qf[:, 512:768]
    if latent:
        qr = _rope(qr, cosq_ref[...], sinq_ref[...], MLA_ROPE)
    scale = MLA_DQ ** -0.5
    units = [(s, h) for s in range(NS) for h in range(MLA_H)]
    rows = lambda s: slice(s * TQ, (s + 1) * TQ)
    qh = [jnp.concatenate([qn[rows(s), h * MLA_NOPE:(h + 1) * MLA_NOPE],
                           qr[rows(s), h * MLA_ROPE:(h + 1) * MLA_ROPE]], axis=1) * scale for s, h in units]
    sh = lambda i: units[i][0] * MLA_H + units[i][1]
    outs = _attention_units(len(units), lambda i: _bdot_nt(qh[i], k_scr[sh(i)]), lambda i: v_scr[sh(i)])
    for (s, h), o in zip(units, outs):
        o_ref[rows(s), h * MLA_V:(h + 1) * MLA_V] = o.astype(o_ref.dtype)


def _rope_small(sm, cos, sin_signed):
    lane = lax.broadcasted_iota(jnp.int32, (1, 128), 1)
    half = MLA_ROPE // 2
    swapped = jnp.where(lane < 16 + half, pltpu.roll(sm, 128 - half, axis=1), pltpu.roll(sm, half, axis=1))
    return sm * cos + swapped * sin_signed


def _mla(proj, wq, wkv, qw, kvw, T, TQ, n_seq, row_off, rope=None, cache=None, NS=1):
    latent = rope is not None
    tk = T + (PAST if latent else 0)
    nq = T // TQ
    assert NS == 1 or (nq == 1 and not latent)
    n_seq, TQ, TB = n_seq // NS, NS * TQ, NS * T
    qo = row_off // TQ
    so = row_off // TB
    in_specs = [pl.BlockSpec((TQ, 256), lambda i, j: (qo + i * nq + j, C_MCQ // 256)),
                pl.BlockSpec((TB, 128), lambda i, j: (so + i, C_MCKV // 128)),
                pl.BlockSpec((TB, 128), lambda i, j: (so + i, C_SM // 128)),
                pl.BlockSpec((MLA_QL, 768), lambda i, j: (0, 0)),
                pl.BlockSpec((MLA_KVL, 1024), lambda i, j: (0, 0)),
                pl.BlockSpec((1, 256), lambda i, j: (0, 0)),
                pl.BlockSpec((1, 128), lambda i, j: (0, 0))]
    args = [proj, proj, proj, wq, wkv, qw, kvw]
    if latent:
        cosq, sinq, cosk, sink = rope
        cc, kc = cache
        in_specs += [pl.BlockSpec((TQ, 256), lambda i, j: (j, 0)),
                     pl.BlockSpec((TQ, 256), lambda i, j: (j, 0)),
                     pl.BlockSpec((T, 128), lambda i, j: (0, 0)),
                     pl.BlockSpec((T, 128), lambda i, j: (0, 0)),
                     pl.BlockSpec((1, PAST, 128), lambda i, j: (i, 0, 0)),
                     pl.BlockSpec((1, PAST, MLA_ROPE), lambda i, j: (i, 0, 0))]
        args += [cosq, sinq, cosk, sink, cc, kc]
    out_specs = [pl.BlockSpec((TQ, 512), lambda i, j: (i * nq + j, 0))]
    out_shape = [jax.ShapeDtypeStruct((n_seq * TB, 512), BF16)]
    if not latent:
        out_specs.append(pl.BlockSpec((TB, 128), lambda i, j: (i, 0)))
        out_shape.append(jax.ShapeDtypeStruct((n_seq * TB, 128), F32))
    return pl.pallas_call(
        functools.partial(_mla_body, T, TQ // NS, NS, latent),
        grid=(n_seq, nq),
        in_specs=in_specs,
        out_specs=out_specs,
        out_shape=out_shape,
        scratch_shapes=[pltpu.VMEM((NS * MLA_H, tk, MLA_DQ), BF16), pltpu.VMEM((NS * MLA_H, tk, MLA_V), BF16)],
        compiler_params=_cp(2),
        name="mla_%d" % T,
    )(*args)


MG_TM = 512


def _merge_body(x_ref, m_ref, ofc_ref, obc_ref, ofl_ref, obl_ref, dg_ref, ogc_ref, ogl_ref, omc_ref, oml_ref,
                bg_ref, ng_ref, wb_ref, wo_ref, o_ref):
    is_ctx = pl.program_id(0) < RC // MG_TM
    og = jnp.where(is_ctx, ogc_ref[...], ogl_ref[...])
    om = jnp.where(is_ctx, omc_ref[...], oml_ref[...])
    odn = jnp.where(is_ctx, ofc_ref[...] + obc_ref[...], ofl_ref[...] + obl_ref[...])
    dg = dg_ref[...]
    ng = ng_ref[...]
    parts = []
    for h in range(DN_H):
        oh = odn[:, h * DN_DV:(h + 1) * DN_DV]
        ms = jnp.mean(oh * oh, axis=-1, keepdims=True)
        parts.append(oh * lax.rsqrt(ms + EPS) * ng * _silu(dg[:, h * DN_DV:(h + 1) * DN_DV]))
    br0 = jnp.concatenate(parts, axis=1)
    gate = lambda n: _sigmoid(bg_ref[:, n * D:(n + 1) * D].astype(F32))
    merged = gate(0) * _bdot(br0, wb_ref[0, 0])
    merged = merged + gate(1) * _bdot(og, wb_ref[0, 1])
    merged = merged + gate(2) * _bdot(om, wb_ref[0, 2])
    out = _bdot(merged, wo_ref[0])
    o_ref[...] = x_ref[...] + m_ref[0, 2:3, :] * out


def _merge(x, mods_l, dn_o, proj, pbg, og_c, og_l, om_c, om_l, ng, wb, wo, l):
    row = lambda i: (i, 0)
    n_ctx = RC // MG_TM
    ctx_row = lambda i: (jnp.minimum(i, n_ctx - 1), 0)
    lat_row = lambda i: (jnp.maximum(i - n_ctx, 0), 0)
    return pl.pallas_call(
        _merge_body,
        grid=(R // MG_TM,),
        in_specs=[pl.BlockSpec((MG_TM, D), row),
                  pl.BlockSpec((1, 6, D), lambda i: (_mod_group(i * MG_TM), 0, 0)),
                  pl.BlockSpec((MG_TM, 512), ctx_row),
                  pl.BlockSpec((MG_TM, 512), ctx_row),
                  pl.BlockSpec((MG_TM, 512), lat_row),
                  pl.BlockSpec((MG_TM, 512), lat_row),
                  pl.BlockSpec((MG_TM, 512), lambda i: (i, C_DG // 512)),
                  pl.BlockSpec((MG_TM, 512), ctx_row),
                  pl.BlockSpec((MG_TM, 512), lat_row),
                  pl.BlockSpec((MG_TM, 512), ctx_row),
                  pl.BlockSpec((MG_TM, 512), lat_row),
                  pl.BlockSpec((MG_TM, N_BG), row),
                  pl.BlockSpec((1, DN_DV), lambda i: (0, 0)),
                  pl.BlockSpec((1, 3, 512, D), lambda i: (l, 0, 0, 0)),
                  pl.BlockSpec((1, D, D), lambda i: (l, 0, 0))],
        out_specs=pl.BlockSpec((MG_TM, D), row),
        out_shape=jax.ShapeDtypeStruct((R, D), F32),
        compiler_params=_cp(1),
        name="merge",
    )(x, mods_l, *dn_o, proj, og_c, og_l, om_c, om_l, pbg, ng, wb, wo)


FF_TM = 512
FF_CHUNKS = ((0, 1536), (1536, FF_DENSE))


def _ffn_body(x_ref, m_ref, g_ref, w1_ref, w3_ref, w2_ref, o_ref):
    x = x_ref[...]
    h = _modnorm(x, g_ref[...], m_ref[0, 3:4, :], m_ref[0, 4:5, :]).astype(BF16)
    y = None
    for c0, c1 in FF_CHUNKS:
        a = _silu(_dot(h, w1_ref[0, :, c0:c1])) * _dot(h, w3_ref[0, :, c0:c1])
        yc = _dot(a.astype(BF16), w2_ref[0, c0:c1, :])
        y = yc if y is None else y + yc
    o_ref[...] = x + m_ref[0, 5:6, :] * y


def _ffn(x, mods_l, g, w1, w3, w2, j):
    once = pl.Buffered(1)
    return pl.pallas_call(
        _ffn_body,
        grid=(R // FF_TM,),
        in_specs=[pl.BlockSpec((FF_TM, D), lambda i: (i, 0)),
                  pl.BlockSpec((1, 6, D), lambda i: (_mod_group(i * FF_TM), 0, 0)),
                  pl.BlockSpec((1, D), lambda i: (0, 0)),
                  pl.BlockSpec((1, D, FF_DENSE), lambda i: (j, 0, 0), pipeline_mode=once),
                  pl.BlockSpec((1, D, FF_DENSE), lambda i: (j, 0, 0), pipeline_mode=once),
                  pl.BlockSpec((1, FF_DENSE, D), lambda i: (j, 0, 0), pipeline_mode=once)],
        out_specs=pl.BlockSpec((FF_TM, D), lambda i: (i, 0)),
        out_shape=jax.ShapeDtypeStruct((R, D), F32),
        compiler_params=_cp(1),
        name="ffn",
    )(x, mods_l, g, w1, w3, w2)


RT_TM = 512
MOE_TG, MOE_TF = 512, 1792
MOE_NT = 2 * R // MOE_TG + N_EXP
MOE_ROWS = MOE_NT * MOE_TG
CB_TM = 256


def _router_body(x_ref, m_ref, g_ref, rw_ref, rb_ref, hn_ref, route_ref, cnt_ref, base_scr):
    @pl.when(pl.program_id(0) == 0)
    def _():
        base_scr[...] = jnp.zeros_like(base_scr)

    h = _modnorm(x_ref[...], g_ref[...], m_ref[0, 3:4, :], m_ref[0, 4:5, :])
    hn_ref[...] = h
    logits = _dot3(h, rw_ref[...]) + rb_ref[...]
    lane = lax.broadcasted_iota(jnp.int32, logits.shape, 1)
    m1 = jnp.max(logits, axis=-1, keepdims=True)
    i1 = jnp.min(jnp.where(logits == m1, lane, 128), axis=-1, keepdims=True)
    sel1 = lane == i1
    rest = jnp.where(sel1, -jnp.inf, logits)
    m2 = jnp.max(rest, axis=-1, keepdims=True)
    i2 = jnp.min(jnp.where(rest == m2, lane, 128), axis=-1, keepdims=True)
    sel2 = lane == i2
    e2 = jnp.exp(m2 - m1)
    p1 = 1.0 / (1.0 + e2)
    p2 = e2 / (1.0 + e2)

    cnt = jnp.where(jnp.logical_or(sel1, sel2), 1.0, 0.0)
    r = lax.broadcasted_iota(jnp.int32, (RT_TM, RT_TM), 0)
    c = lax.broadcasted_iota(jnp.int32, (RT_TM, RT_TM), 1)
    before = jnp.where(r > c, 1.0, 0.0).astype(BF16)
    seen = base_scr[...] + _dot(before, cnt.astype(BF16))
    rank1 = jnp.sum(jnp.where(sel1, seen, 0.0), axis=-1, keepdims=True)
    rank2 = jnp.sum(jnp.where(sel2, seen, 0.0), axis=-1, keepdims=True)
    vals = (i1.astype(F32), i2.astype(F32), rank1, rank2, p1, p2)
    route = jnp.zeros(logits.shape, F32)
    for k, val in enumerate(vals):
        route = jnp.where(lane == k, val, route)
    route_ref[...] = route
    base_scr[...] += jnp.sum(cnt, axis=0, keepdims=True)
    cnt_ref[...] = base_scr[...]


def _router(x, mods_l, g, rw128, rb128):
    return pl.pallas_call(
        _router_body,
        grid=(R // RT_TM,),
        in_specs=[pl.BlockSpec((RT_TM, D), lambda i: (i, 0)),
                  pl.BlockSpec((1, 6, D), lambda i: (_mod_group(i * RT_TM), 0, 0)),
                  pl.BlockSpec((1, D), lambda i: (0, 0)),
                  pl.BlockSpec((D, 128), lambda i: (0, 0)),
                  pl.BlockSpec((1, 128), lambda i: (0, 0))],
        out_specs=[pl.BlockSpec((RT_TM, D), lambda i: (i, 0)),
                   pl.BlockSpec((RT_TM, 128), lambda i: (i, 0)),
                   pl.BlockSpec((1, 128), lambda i: (0, 0))],
        out_shape=[jax.ShapeDtypeStruct((R, D), F32),
                   jax.ShapeDtypeStruct((R, 128), F32),
                   jax.ShapeDtypeStruct((1, 128), F32)],
        scratch_shapes=[pltpu.VMEM((1, 128), F32)],
        compiler_params=_cp(1),
        name="moe_router",
    )(x, mods_l, g, rw128, rb128)


def _rows_copy(src_hbm, dst, sem, n):
    return pltpu.make_async_copy(src_hbm.at[pl.ds(0, n)], dst, sem)


def _start_row_gather(idx_ref, n, src_hbm, dst, sem, inline=False, both_queues=False):
    def issue(i, priority):
        pltpu.make_async_copy(src_hbm.at[pl.ds(idx_ref[0, 0, i], 1)], dst.at[pl.ds(i, 1)], sem).start(
            priority=priority)

    if inline:
        for i in range(n):
            issue(i, i % 2 if both_queues else 0)
    else:
        def body(i, carry):
            issue(i, 0)
            return carry

        lax.fori_loop(0, n, body, 0, unroll=8)


def _stage_to_bf16(src_hbm, dst, stage, sems):
    ch = stage.shape[1]
    n = src_hbm.shape[0] // ch
    copy = lambda c: pltpu.make_async_copy(src_hbm.at[pl.ds(c * ch, ch)], stage.at[c % 2], sems.at[c % 2])
    copy(0).start()
    for c in range(n):
        if c + 1 < n:
            copy(c + 1).start()
        copy(c).wait()
        dst[c * ch:(c + 1) * ch, :] = stage[c % 2].astype(BF16)


def _experts_body(j, te_ref, tv_ref, src_ref, nsrc_ref, hn_hbm, w1_hbm, w3_hbm, w2_hbm, ys_ref,
                  xg_scr, w1_ref, w3_ref, w2_ref, st13, st2, sems, wsems):
    t = pl.program_id(0)
    slot = t % 2
    valid = tv_ref[t] > 0
    e = te_ref[t]
    new_expert = jnp.logical_or(t == 0, e != te_ref[jnp.maximum(t - 1, 0)])

    @pl.when(jnp.logical_and(valid, new_expert))
    def _():
        _stage_to_bf16(w1_hbm.at[j, e], w1_ref, st13, wsems)
        _stage_to_bf16(w3_hbm.at[j, e], w3_ref, st13, wsems)
        _stage_to_bf16(w2_hbm.at[j, e], w2_ref, st2, wsems)

    requested = jnp.where(t == 0, valid, tv_ref[jnp.maximum(t - 1, 0)] > 0)

    @pl.when(jnp.logical_and(t == 0, valid))
    def _():
        _start_row_gather(src_ref, MOE_TG, hn_hbm, xg_scr.at[0], sems.at[0])

    @pl.when(requested)
    def _():
        _rows_copy(hn_hbm, xg_scr.at[slot], sems.at[slot], MOE_TG).wait()

    @pl.when(valid)
    def _():
        xb = xg_scr[slot].astype(BF16)
        _start_row_gather(nsrc_ref, MOE_TG, hn_hbm, xg_scr.at[1 - slot], sems.at[1 - slot], inline=True)
        y = None
        for c in range(FF_EXP // MOE_TF):
            cs = slice(c * MOE_TF, (c + 1) * MOE_TF)
            a = _silu(_dot(xb, w1_ref[:, cs])) * _dot(xb, w3_ref[:, cs])
            yc = _dot(a.astype(BF16), w2_ref[cs, :])
            y = yc if y is None else y + yc
        ys_ref[...] = y

    @pl.when(jnp.logical_not(valid))
    def _():
        ys_ref[...] = jnp.zeros_like(ys_ref)

    @pl.when(jnp.logical_and(t == MOE_NT - 1, valid))
    def _():
        _rows_copy(hn_hbm, xg_scr.at[1 - slot], sems.at[1 - slot], MOE_TG).wait()


MOE_WCH13, MOE_WCH2 = 128, 448


def _experts(tile_e, tile_v, src, hn, w1, w3, w2, j):
    hbm = pl.BlockSpec(memory_space=pl.ANY)
    grid_spec = pltpu.PrefetchScalarGridSpec(
        num_scalar_prefetch=2,
        grid=(MOE_NT,),
        in_specs=[pl.BlockSpec((1, 1, MOE_TG), lambda t, te, tv: (t, 0, 0), memory_space=pltpu.SMEM),
                  pl.BlockSpec((1, 1, MOE_TG), lambda t, te, tv: (t + 1, 0, 0), memory_space=pltpu.SMEM),
                  hbm, hbm, hbm, hbm],
        out_specs=pl.BlockSpec((MOE_TG, D), lambda t, te, tv: (t, 0)),
        scratch_shapes=[pltpu.VMEM((2, MOE_TG, D), F32),
                        pltpu.VMEM((D, FF_EXP), BF16), pltpu.VMEM((D, FF_EXP), BF16),
                        pltpu.VMEM((FF_EXP, D), BF16),
                        pltpu.VMEM((2, MOE_WCH13, FF_EXP), F32), pltpu.VMEM((2, MOE_WCH2, D), F32),
                        pltpu.SemaphoreType.DMA((2,)), pltpu.SemaphoreType.DMA((2,))],
    )
    return pl.pallas_call(
        functools.partial(_experts_body, j),
        grid_spec=grid_spec,
        out_shape=jax.ShapeDtypeStruct((MOE_ROWS, D), F32),
        compiler_params=_cp(1),
        name="moe_experts",
    )(tile_e, tile_v, src, src, hn, w1, w3, w2)


def _combine_body(pos_ref, npos_ref, x_ref, m_ref, route_ref, ys_hbm, o_ref, buf, sems):
    i = pl.program_id(0)
    slot = i % 2
    n_rows = 2 * CB_TM

    @pl.when(i == 0)
    def _():
        _start_row_gather(pos_ref, n_rows, ys_hbm, buf.at[0], sems.at[0])

    @pl.when(i + 1 < pl.num_programs(0))
    def _():
        _start_row_gather(npos_ref, n_rows, ys_hbm, buf.at[1 - slot], sems.at[1 - slot], inline=True,
                          both_queues=True)

    _rows_copy(ys_hbm, buf.at[slot], sems.at[slot], n_rows).wait()
    route = route_ref[...]
    y = route[:, 4:5] * buf[slot, 0:CB_TM, :] + route[:, 5:6] * buf[slot, CB_TM:n_rows, :]
    o_ref[...] = x_ref[...] + m_ref[0, 5:6, :] * y


def _combine(pos, x, mods_l, route, ys):
    n_tiles = R // CB_TM
    return pl.pallas_call(
        _combine_body,
        grid=(n_tiles,),
        in_specs=[pl.BlockSpec((1, 1, 2 * CB_TM), lambda i: (i, 0, 0), memory_space=pltpu.SMEM),
                  pl.BlockSpec((1, 1, 2 * CB_TM), lambda i: (jnp.minimum(i + 1, n_tiles - 1), 0, 0),
                               memory_space=pltpu.SMEM),
                  pl.BlockSpec((CB_TM, D), lambda i: (i, 0)),
                  pl.BlockSpec((1, 6, D), lambda i: (_mod_group(i * CB_TM), 0, 0)),
                  pl.BlockSpec((CB_TM, 128), lambda i: (i, 0)),
                  pl.BlockSpec(memory_space=pl.ANY)],
        out_specs=pl.BlockSpec((CB_TM, D), lambda i: (i, 0)),
        out_shape=jax.ShapeDtypeStruct((R, D), F32),
        scratch_shapes=[pltpu.VMEM((2, 2 * CB_TM, D), F32), pltpu.SemaphoreType.DMA((2,))],
        compiler_params=_cp(1),
        name="moe_combine",
    )(pos, pos, x, mods_l, route, ys)


def _moe(x, mods_l, g, rw128, rb128, w1, w3, w2, j):
    hn, route, cnt = _router(x, mods_l, g, rw128, rb128)
    eid = route[:, 0:2].astype(jnp.int32)
    rank = route[:, 2:4].astype(jnp.int32)
    counts = cnt[0, :N_EXP].astype(jnp.int32)
    gsize = (counts + MOE_TG - 1) // MOE_TG * MOE_TG
    gend = jnp.cumsum(gsize)
    pos = (gend - gsize)[eid] + rank
    tile_start = jnp.arange(MOE_NT, dtype=jnp.int32) * MOE_TG
    tile_e = jnp.minimum(jnp.sum(tile_start[:, None] >= gend[None, :], axis=1), N_EXP - 1).astype(jnp.int32)
    tile_v = (tile_start < gend[-1]).astype(jnp.int32)
    tok = jnp.broadcast_to(jnp.arange(R, dtype=jnp.int32)[:, None], (R, 2))
    src = jnp.zeros((MOE_ROWS + MOE_TG,), jnp.int32).at[pos.reshape(-1)].set(tok.reshape(-1))
    ys = _experts(tile_e, tile_v, src.reshape(MOE_NT + 1, 1, MOE_TG), hn, w1, w3, w2, j)
    pos_t = pos.reshape(R // CB_TM, CB_TM, 2).transpose(0, 2, 1).reshape(R // CB_TM, 1, 2 * CB_TM)
    return _combine(pos_t, x, mods_l, route, ys)


FN_TM = 1024


def _final_body(x_ref, g_ref, o_ref):
    x = x_ref[...]
    ms = jnp.mean(x * x, axis=-1, keepdims=True)
    o_ref[...] = x * lax.rsqrt(ms + EPS) * g_ref[...]


def _final_norm(x, g, row_off, n_rows):
    bo = row_off // FN_TM
    return pl.pallas_call(
        _final_body,
        grid=(n_rows // FN_TM,),
        in_specs=[pl.BlockSpec((FN_TM, D), lambda i: (bo + i, 0)), pl.BlockSpec((1, D), lambda i: (0, 0))],
        out_specs=pl.BlockSpec((FN_TM, D), lambda i: (i, 0)),
        out_shape=jax.ShapeDtypeStruct((n_rows, D), F32),
        compiler_params=_cp(1),
        name="final_norm",
    )(x, g)


def _rope_tables(n_tokens, rot_dim):
    t = np.arange(n_tokens)
    row = (t // GRID_W).astype(np.float32)
    col = (t % GRID_W).astype(np.float32)
    n_freq = rot_dim // 4
    inv = (ROPE_THETA ** (-jnp.arange(n_freq, dtype=F32) / n_freq))
    ang = jnp.concatenate([jnp.asarray(row)[:, None] * inv, jnp.asarray(col)[:, None] * inv], axis=-1)
    cos, sin = jnp.cos(ang), jnp.sin(ang)
    return jnp.concatenate([cos, cos], axis=-1), jnp.concatenate([-sin, sin], axis=-1)


def kernel(x_prompt, x_sample, c, cache_gqa_k, cache_gqa_v, cache_mla_ckv, cache_mla_krope, state_delta, c_ctx, w_mod, b_mod, norm1_g, norm2_g, w_in, dn_conv_w, dn_a_log, dn_dt_bias, dn_norm_g, gqa_q_norm, gqa_k_norm, mla_q_norm, mla_kv_norm, mla_w_uq, mla_w_ukv, w_branch, w_out, ffd_w1, ffd_w3, ffd_w2, router_w, router_b, moe_w1, moe_w3, moe_w2, final_g):
    x = jnp.concatenate([x_prompt.reshape(RC, D), x_sample.reshape(RL, D)], axis=0)
    cond8 = jnp.concatenate([c_ctx[None, :], c, jnp.zeros((3, D), F32)], axis=0)
    mods = _mods(cond8, w_mod, b_mod).reshape(DEPTH, 8, 6, D)

    cg, sg = _rope_tables(DEC_SEQ, GQA_HD)
    gqa_rope = (jnp.tile(cg, (1, GQA_H)), jnp.tile(sg, (1, GQA_H)),
                jnp.tile(cg, (1, GQA_KV)), jnp.tile(sg, (1, GQA_KV)))
    cm, sm_ = _rope_tables(DEC_SEQ, MLA_ROPE)
    padk = lambda t: jnp.pad(t, ((0, 0), (16, 128 - 16 - MLA_ROPE)))
    mla_rope = (jnp.tile(cm, (1, MLA_H)), jnp.tile(sm_, (1, MLA_H)),
                jnp.pad(cm, ((0, 0), (16, 128 - 16 - MLA_ROPE)), constant_values=1.0), padk(sm_))

    w_in_bf, w_branch_bf, w_out_bf = w_in.astype(BF16), w_branch.astype(BF16), w_out.astype(BF16)
    ffd_bf = (ffd_w1.astype(BF16), ffd_w3.astype(BF16), ffd_w2.astype(BF16))

    new_k, new_v, new_ckv, new_kr, new_s = [], [], [], [], []
    for l in range(DEPTH):
        pbg, proj = _inproj(x, mods[l], norm1_g[l][None, :], w_in_bf, l)

        pad128 = lambda v: jnp.pad(v.reshape(1, 8), ((0, 0), (8, 112)))
        padt = lambda v: jnp.pad(v.reshape(8, 1), ((8, 0), (0, 0)))
        sm_t = proj[:, C_SM:C_SM + 16].T
        dn_u, dn_w, dn_qg, dn_kd, dn_ai, dn_bg = _dn_prep(
            proj, sm_t, dn_conv_w[l], pad128(dn_a_log[l]), pad128(dn_dt_bias[l]),
            padt(dn_a_log[l]), padt(dn_dt_bias[l]))
        *dn_o, s_c = _dn_scan(dn_u, dn_w, dn_qg, dn_kd, dn_ai, dn_bg, state_delta[:, l])

        qw = jnp.tile(gqa_q_norm[l][None, :], (1, GQA_H))
        kw = jnp.tile(gqa_k_norm[l][None, :], (1, GQA_KV))
        o_g_c, kn_c = _gqa(proj, qw, kw, SEQ, SEQ, BATCH, 0, NS=CTX_NS)
        (o_g_l,) = _gqa(proj, qw, kw, DEC_SEQ, 256, DEC_BATCH, RC, rope=gqa_rope,
                        cache=(cache_gqa_k[:, l].reshape(DEC_BATCH, PAST, 128),
                               cache_gqa_v[:, l].reshape(DEC_BATCH, PAST, 128)))

        wq = mla_w_uq[l].reshape(MLA_QL, MLA_H, MLA_DQ)
        wq = jnp.concatenate([wq[:, :, :MLA_NOPE].reshape(MLA_QL, -1), wq[:, :, MLA_NOPE:].reshape(MLA_QL, -1)],
                             axis=1).astype(BF16)
        wkv = mla_w_ukv[l].reshape(MLA_KVL, MLA_H, MLA_NOPE + MLA_V)
        wkv = jnp.concatenate([wkv[:, :, :MLA_NOPE].reshape(MLA_KVL, -1), wkv[:, :, MLA_NOPE:].reshape(MLA_KVL, -1)],
                              axis=1).astype(BF16)
        mqw, mkvw = mla_q_norm[l][None, :], mla_kv_norm[l][None, :]
        o_m_c, ckv_c = _mla(proj, wq, wkv, mqw, mkvw, SEQ, SEQ, BATCH, 0, NS=CTX_NS)
        (o_m_l,) = _mla(proj, wq, wkv, mqw, mkvw, DEC_SEQ, 256, DEC_BATCH, RC, rope=mla_rope,
                        cache=(cache_mla_ckv[:, l], cache_mla_krope[:, l]))
        x = _merge(x, mods[l], dn_o, proj, pbg, o_g_c, o_g_l, o_m_c, o_m_l, dn_norm_g[l][None, :],
                   w_branch_bf, w_out_bf, l)

        j = l // 2
        if l % 2 == 0:
            x = _ffn(x, mods[l], norm2_g[l][None, :], *ffd_bf, j)
        else:
            rw128 = jnp.pad(router_w[j], ((0, 0), (0, 128 - N_EXP)))
            rb128 = jnp.pad(router_b[j][None, :], ((0, 0), (0, 128 - N_EXP)), constant_values=-jnp.inf)
            x = _moe(x, mods[l], norm2_g[l][None, :], rw128, rb128, moe_w1, moe_w3, moe_w2, j)

        new_k.append(kn_c.reshape(BATCH, SEQ, GQA_KV, GQA_HD))
        new_v.append(proj[:RC, C_GKV + 128:C_GKV + 256].reshape(BATCH, SEQ, GQA_KV, GQA_HD))
        new_ckv.append(ckv_c.reshape(BATCH, SEQ, MLA_KVL))
        new_kr.append(proj[:RC, C_SM + 16:C_SM + 16 + MLA_ROPE].reshape(BATCH, SEQ, MLA_ROPE))
        new_s.append(s_c)

    y_c = _final_norm(x, final_g[None, :], 0, RC)
    y_l = _final_norm(x, final_g[None, :], RC, RL)
    return (y_c.reshape(BATCH, SEQ, D), y_l.reshape(DEC_BATCH, DEC_SEQ, D),
            jnp.stack(new_k, axis=1), jnp.stack(new_v, axis=1), jnp.stack(new_ckv, axis=1),
            jnp.stack(new_kr, axis=1), jnp.stack(new_s, axis=1))
```

```python
import functools
import math

import jax
import jax.numpy as jnp
import numpy as np
from jax import lax
from jax.experimental import pallas as pl
from jax.experimental.pallas import tpu as pltpu

F32 = jnp.float32
BF16 = jnp.bfloat16

D = 1024
BATCH, SEQ = 32, 256
DEC_BATCH, DEC_SEQ = 4, 2048
DEPTH = 4
PAST = 512
GRID_W = 64
ROPE_THETA = 10000.0
EPS = 1e-6
DN_H, DN_DK, DN_DV, DN_C = 4, 128, 128, 64
GQA_H, GQA_KV, GQA_HD = 8, 2, 64
MLA_H, MLA_QL, MLA_KVL, MLA_NOPE, MLA_ROPE, MLA_V = 8, 256, 128, 64, 32, 64
FF_DENSE, N_EXP, FF_EXP = 2816, 8, 3584

RC = BATCH * SEQ
RL = DEC_BATCH * DEC_SEQ
R = RC + RL

N_BG = 3 * D
C_QKV, C_DG, C_GQ, C_GKV, C_MCQ, C_MCKV, C_SM = 0, 1536, 2048, 2560, 2816, 3072, 3200
N_PROJ = 3328
NP = N_BG + N_PROJ

VMEM_LIMIT = 56 * 1024 * 1024


def _cp(n_grid):
    return pltpu.CompilerParams(dimension_semantics=("arbitrary",) * n_grid,
                                vmem_limit_bytes=VMEM_LIMIT)


def _dot(a, b):
    return jnp.dot(a, b, preferred_element_type=F32)


def _bdot(a, b):
    return jnp.dot(a.astype(BF16), b.astype(BF16), preferred_element_type=F32)


def _bdot_nt(a, b):
    return lax.dot_general(a.astype(BF16), b.astype(BF16), (((1,), (1,)), ((), ())),
                           preferred_element_type=F32)


def _bdot_tn(a, b):
    return lax.dot_general(a.astype(BF16), b.astype(BF16), (((0,), (0,)), ((), ())),
                           preferred_element_type=F32)


def _split2(a):
    hi = a.astype(BF16)
    lo = (a - hi.astype(F32)).astype(BF16)
    return hi, lo


def _split3(a):
    a1 = a.astype(BF16)
    r1 = a - a1.astype(F32)
    a2 = r1.astype(BF16)
    a3 = (r1 - a2.astype(F32)).astype(BF16)
    return a1, a2, a3


def _dot3(a, b):
    ah, al = _split2(a)
    bh, bl = _split2(b)
    return _dot(ah, bh) + (_dot(ah, bl) + _dot(al, bh))


def _mask_dot_r(mask_bf, g):
    g1, g2, g3 = _split3(g)
    return _dot(mask_bf, g1) + (_dot(mask_bf, g2) + _dot(mask_bf, g3))


def _mask_dot_l(g, mask_bf):
    g1, g2, g3 = _split3(g)
    return _dot(g1, mask_bf) + (_dot(g2, mask_bf) + _dot(g3, mask_bf))


def _sigmoid(x):
    return 1.0 / (1.0 + jnp.exp(-x))


def _silu(x):
    return x * _sigmoid(x)


def _softplus(x):
    return jnp.maximum(x, 0.0) + jnp.log1p(jnp.exp(-jnp.abs(x)))


def _mod_group(row0):
    return jnp.where(row0 < RC, 0, 1 + (row0 - RC) // DEC_SEQ)


MODS_TN = 1536


def _mods_body(c_ref, w_ref, b_ref, o_ref):
    s = _silu(c_ref[...])
    o_ref[0] = _bdot(s, w_ref[0]) + b_ref[0]


def _mods(cond8, w_mod, b_mod):
    nj = 6 * D // MODS_TN
    return pl.pallas_call(
        _mods_body,
        grid=(DEPTH, nj),
        in_specs=[pl.BlockSpec((8, D), lambda l, j: (0, 0)),
                  pl.BlockSpec((1, D, MODS_TN), lambda l, j: (l, 0, j)),
                  pl.BlockSpec((1, 1, MODS_TN), lambda l, j: (l, 0, j))],
        out_specs=pl.BlockSpec((1, 8, MODS_TN), lambda l, j: (l, 0, j)),
        out_shape=jax.ShapeDtypeStruct((DEPTH, 8, 6 * D), F32),
        compiler_params=_cp(2),
        name="mods",
    )(cond8, w_mod, b_mod.reshape(DEPTH, 1, 6 * D))


IN_TM = 512
IN_CHUNK = 1536


def _modnorm(x, g, shift, scale):
    ms = jnp.mean(x * x, axis=-1, keepdims=True)
    y = x * lax.rsqrt(ms + EPS) * g
    return y * (1.0 + scale) + shift


W_IN_PIECES = ((3248, 6320), (0, 2048), (2064, 3216), (2048, 2064), (3216, 3248))
W_IN_COLS = 6320


def _inproj_body(x_ref, m_ref, g_ref, wsrc_ref, bg_ref, o_ref, w_ref):
    @pl.when(pl.program_id(0) == 0)
    def _():
        at = 0
        for a, b in W_IN_PIECES:
            w_ref[:, at:at + (b - a)] = wsrc_ref[0, :, a:b]
            at += b - a
        w_ref[:, at:NP] = jnp.zeros((D, NP - at), BF16)

    h = _modnorm(x_ref[...], g_ref[...], m_ref[0, 0:1, :], m_ref[0, 1:2, :]).astype(BF16)
    for c0 in range(0, N_BG, IN_CHUNK):
        bg_ref[:, c0:c0 + IN_CHUNK] = _dot(h, w_ref[:, c0:c0 + IN_CHUNK]).astype(BF16)
    for c0 in range(0, N_PROJ, IN_CHUNK):
        c1 = min(c0 + IN_CHUNK, N_PROJ)
        o_ref[:, c0:c1] = _dot(h, w_ref[:, N_BG + c0:N_BG + c1])


def _inproj(x, mods_l, g, w_bf, l):
    return pl.pallas_call(
        _inproj_body,
        grid=(R // IN_TM,),
        in_specs=[pl.BlockSpec((IN_TM, D), lambda i: (i, 0)),
                  pl.BlockSpec((1, 6, D), lambda i: (_mod_group(i * IN_TM), 0, 0)),
                  pl.BlockSpec((1, D), lambda i: (0, 0)),
                  pl.BlockSpec((1, D, W_IN_COLS), lambda i: (l, 0, 0), pipeline_mode=pl.Buffered(1))],
        out_specs=[pl.BlockSpec((IN_TM, N_BG), lambda i: (i, 0)),
                   pl.BlockSpec((IN_TM, N_PROJ), lambda i: (i, 0))],
        out_shape=[jax.ShapeDtypeStruct((R, N_BG), BF16), jax.ShapeDtypeStruct((R, N_PROJ), F32)],
        scratch_shapes=[pltpu.VMEM((D, NP), BF16)],
        compiler_params=_cp(1),
        name="inproj",
    )(x, mods_l, g, w_bf)


PREP_TM = 256
PREP_NC = PREP_TM // DN_C
PREP_LOCKSTEP = 4


def _prep_body(x_ref, xp_ref, xn_ref, sm_ref, smt_ref, cw_ref, al_ref, dt_ref, alt_ref, dtt_ref,
               u_o, w_o, qg_o, kd_o, ai_o, bg_o, qkv_o, gr_o):
    i = pl.program_id(0)
    n_ctx = RC // PREP_TM
    per_seq = DEC_SEQ // PREP_TM
    is_ctx = i < n_ctx
    j = (i - n_ctx) % per_seq
    first = jnp.logical_or(is_ctx, j == 0)
    last = jnp.logical_or(is_ctx, j == per_seq - 1)

    x = x_ref[...]
    prev_row = jnp.where(first, 0.0, xp_ref[7:8, :])
    next_row = jnp.where(last, 0.0, xn_ref[0:1, :])
    row = lax.broadcasted_iota(jnp.int32, (PREP_TM, 1), 0)
    xm = jnp.where(row == 0, prev_row, pltpu.roll(x, 1, axis=0))
    xq = jnp.where(row == PREP_TM - 1, next_row, pltpu.roll(x, PREP_TM - 1, axis=0))
    w = cw_ref[...]
    y = _silu(w[0:1] * xm + w[1:2] * x + w[2:3] * xq)

    for h in range(DN_H):
        qh = y[:, h * DN_DK:(h + 1) * DN_DK]
        qn = qh * lax.rsqrt(jnp.sum(qh * qh, axis=-1, keepdims=True) + EPS) * (DN_DK ** -0.5)
        qkv_o[:, h * DN_DK:(h + 1) * DN_DK] = qn
        kh = y[:, 512 + h * DN_DK:512 + (h + 1) * DN_DK]
        kn = kh * lax.rsqrt(jnp.sum(kh * kh, axis=-1, keepdims=True) + EPS)
        qkv_o[:, 512 + h * DN_DK:512 + (h + 1) * DN_DK] = kn
    qkv_o[:, 1024:1536] = y[:, 1024:1536]

    r = lax.broadcasted_iota(jnp.int32, (PREP_TM, PREP_TM), 0)
    c = lax.broadcasted_iota(jnp.int32, (PREP_TM, PREP_TM), 1)
    same = (r // DN_C) == (c // DN_C)
    low = jnp.where(jnp.logical_and(same, r >= c), 1.0, 0.0).astype(BF16)
    upp = jnp.where(jnp.logical_and(same, r <= c), 1.0, 0.0).astype(BF16)

    sm = sm_ref[...]
    beta = _sigmoid(sm)
    g = -jnp.exp(al_ref[...]) * _softplus(sm + dt_ref[...])
    gc_f = _mask_dot_r(low, g)
    gc_b = _mask_dot_r(upp, g)
    lane = lax.broadcasted_iota(jnp.int32, (1, 128), 1)
    bg_o[...] = jnp.where(lane < 8, beta, jnp.where(lane < 12, gc_f, jnp.where(lane < 16, gc_b, 0.0)))

    gt = -jnp.exp(alt_ref[...]) * _softplus(smt_ref[...] + dtt_ref[...])
    gct_f = _mask_dot_l(gt, upp)
    gct_b = _mask_dot_l(gt, low)
    sub = lax.broadcasted_iota(jnp.int32, (16, 1), 0)
    gct = jnp.where(sub < 12, gct_f, gct_b)
    for k in range(PREP_NC):
        gr_o[k] = gct[8:16, k * DN_C:(k + 1) * DN_C]

    rr = lax.broadcasted_iota(jnp.int32, (DN_C, DN_C), 0)
    cc = lax.broadcasted_iota(jnp.int32, (DN_C, DN_C), 1)

    def chunk_group(gi, carry):
        prob = [(cj, d, h) for cj in range(PREP_LOCKSTEP) for d in range(2) for h in range(DN_H)]
        ci = [gi * PREP_LOCKSTEP + cj for cj in range(PREP_LOCKSTEP)]
        rows_c = [pl.ds(pl.multiple_of(c_ * DN_C, DN_C), DN_C) for c_ in ci]
        bgc = [bg_o[r_, :] for r_ in rows_c]
        grow_all = [gr_o[c_] for c_ in ci]
        rows = [rows_c[cj] for cj, d, h in prob]
        q = [qkv_o[rows_c[cj], h * DN_DK:(h + 1) * DN_DK] for cj, d, h in prob]
        k = [qkv_o[rows_c[cj], 512 + h * DN_DK:512 + (h + 1) * DN_DK] for cj, d, h in prob]
        v = [qkv_o[rows_c[cj], 1024 + h * DN_DV:1024 + (h + 1) * DN_DV] for cj, d, h in prob]
        beta = [bgc[cj][:, d * DN_H + h:d * DN_H + h + 1] for cj, d, h in prob]
        gcol = [bgc[cj][:, 8 + d * DN_H + h:9 + d * DN_H + h] for cj, d, h in prob]
        grow = [grow_all[cj][d * DN_H + h:d * DN_H + h + 1, :] for cj, d, h in prob]
        incl = [(rr >= cc) if d == 0 else (rr <= cc) for cj, d, h in prob]
        strict = [(rr > cc) if d == 0 else (rr < cc) for cj, d, h in prob]
        n = len(prob)
        decay = [jnp.exp(jnp.where(incl[i], gcol[i] - grow[i], -1e30)) for i in range(n)]
        kb = [k[i] * beta[i] for i in range(n)]
        kk = [_bdot_nt(kb[i], k[i]) for i in range(n)]
        qk = [_bdot_nt(q[i], k[i]) for i in range(n)]
        a = [jnp.where(strict[i], kk[i] * decay[i], 0.0) for i in range(n)]
        blk = lambda b: (rr // b) == (cc // b)
        eye = jnp.where(rr == cc, 1.0, 0.0)
        p = [jnp.where(blk(8), a[i], 0.0) for i in range(n)]
        t = [eye - p[i] for i in range(n)]
        for _ in range(2):
            p = [_bdot(p[i], p[i]) for i in range(n)]
            t = [t[i] + _bdot(t[i], p[i]) for i in range(n)]
        for b in (16, 32, 64):
            m = jnp.logical_and(blk(b), jnp.logical_not(blk(b // 2)))
            tl = [_bdot(t[i], jnp.where(m, a[i], 0.0)) for i in range(n)]
            t = [t[i] - _bdot(tl[i], t[i]) for i in range(n)]
        egc = [jnp.exp(gcol[i]) for i in range(n)]
        glast = [gcol[i][DN_C - 1:DN_C, :] if prob[i][1] == 0 else gcol[i][0:1, :] for i in range(n)]
        uw = [_bdot(t[i], jnp.concatenate([v[i] * beta[i], kb[i] * egc[i]], axis=1)) for i in range(n)]
        for i, (cj, d, h) in enumerate(prob):
            u_o[d, rows[i], h * DN_DV:(h + 1) * DN_DV] = uw[i][:, 0:DN_DV]
            w_o[d, rows[i], h * DN_DK:(h + 1) * DN_DK] = uw[i][:, DN_DV:DN_DV + DN_DK].astype(BF16)
            qg_o[d, rows[i], h * DN_DK:(h + 1) * DN_DK] = (q[i] * egc[i]).astype(BF16)
            kd_o[d, rows[i], h * DN_DK:(h + 1) * DN_DK] = (k[i] * jnp.exp(glast[i] - gcol[i])).astype(BF16)
            ai_o[d, rows[i], h * DN_C:(h + 1) * DN_C] = (qk[i] * decay[i]).astype(BF16)
        return carry

    lax.fori_loop(0, PREP_NC // PREP_LOCKSTEP, chunk_group, 0)


def _dn_prep(proj, sm_t, conv_w, al128, dt128, al_t, dt_t):
    nb8 = R // 8
    qb = C_QKV // 1536
    wide = lambda dt: jax.ShapeDtypeStruct((2, R, DN_H * DN_DK), dt)
    wide_spec = pl.BlockSpec((2, PREP_TM, DN_H * DN_DK), lambda i: (0, i, 0))
    return pl.pallas_call(
        _prep_body,
        grid=(R // PREP_TM,),
        in_specs=[pl.BlockSpec((PREP_TM, 1536), lambda i: (i, qb)),
                  pl.BlockSpec((8, 1536), lambda i: (jnp.maximum(i * (PREP_TM // 8) - 1, 0), qb)),
                  pl.BlockSpec((8, 1536), lambda i: (jnp.minimum((i + 1) * (PREP_TM // 8), nb8 - 1), qb)),
                  pl.BlockSpec((PREP_TM, 128), lambda i: (i, C_SM // 128)),
                  pl.BlockSpec((16, PREP_TM), lambda i: (0, i)),
                  pl.BlockSpec((3, 1536), lambda i: (0, 0)),
                  pl.BlockSpec((1, 128), lambda i: (0, 0)),
                  pl.BlockSpec((1, 128), lambda i: (0, 0)),
                  pl.BlockSpec((16, 1), lambda i: (0, 0)),
                  pl.BlockSpec((16, 1), lambda i: (0, 0))],
        out_specs=[wide_spec, wide_spec, wide_spec, wide_spec,
                   pl.BlockSpec((2, PREP_TM, DN_H * DN_C), lambda i: (0, i, 0)),
                   pl.BlockSpec((PREP_TM, 128), lambda i: (i, 0))],
        out_shape=[wide(F32), wide(BF16), wide(BF16), wide(BF16),
                   jax.ShapeDtypeStruct((2, R, DN_H * DN_C), BF16),
                   jax.ShapeDtypeStruct((R, 128), F32)],
        scratch_shapes=[pltpu.VMEM((PREP_TM, 1536), F32), pltpu.VMEM((PREP_NC, 8, DN_C), F32)],
        compiler_params=_cp(1),
        name="dn_prep",
    )(proj, proj, proj, proj, sm_t, conv_w, al128, dt128, al_t, dt_t)


SCAN_STEPS = RC // PREP_TM
SCAN_PER = DEC_SEQ // PREP_TM
assert RL // PREP_TM == SCAN_STEPS and SEQ == PREP_TM


def _scan_lat_bwd_block(i):
    return (i // SCAN_PER) * SCAN_PER + (SCAN_PER - 1 - i % SCAN_PER)


def _scan_body(*refs):
    streams = [refs[6 * k:6 * k + 6] for k in range(4)]
    s0_ref = refs[24]
    out_refs = refs[25:29]
    so_ref, s_scr = refs[29], refs[30]
    j = pl.program_id(0) % SCAN_PER

    for h in range(DN_H):
        s_scr[0, h] = jnp.zeros((DN_DK, DN_DV), F32)
        s_scr[1, h] = jnp.zeros((DN_DK, DN_DV), F32)

    @pl.when(j == 0)
    def _():
        for h in range(DN_H):
            s_scr[2, h] = s0_ref[0, 0, 0, h]
            s_scr[3, h] = s0_ref[0, 0, 1, h]

    prob = [(k, h) for k in range(4) for h in range(DN_H)]
    tn = (((0,), (0,)), ((), ()))

    def step(n, carry):
        rows = [pl.ds(pl.multiple_of((n if k % 2 == 0 else PREP_NC - 1 - n) * DN_C, DN_C), DN_C)
                for k in range(4)]
        bgc = [streams[k][5][rows[k], :] for k in range(4)]
        cols = [slice(h * DN_DK, (h + 1) * DN_DK) for k, h in prob]
        st = [s_scr[k, h] for k, h in prob]
        stb = [x.astype(BF16) for x in st]
        ws = [_dot(streams[k][1][0, rows[k], cols[i]], stb[i]) for i, (k, h) in enumerate(prob)]
        qs = [_dot(streams[k][2][0, rows[k], cols[i]], stb[i]) for i, (k, h) in enumerate(prob)]
        vb = [(streams[k][0][0, rows[k], cols[i]] - ws[i]).astype(BF16) for i, (k, h) in enumerate(prob)]
        av = [_dot(streams[k][4][0, rows[k], h * DN_C:(h + 1) * DN_C], vb[i]) for i, (k, h) in enumerate(prob)]
        kv = [lax.dot_general(streams[k][3][0, rows[k], cols[i]], vb[i], tn, preferred_element_type=F32)
              for i, (k, h) in enumerate(prob)]
        for i, (k, h) in enumerate(prob):
            d = k % 2
            gcol = bgc[k][:, 8 + d * DN_H + h:9 + d * DN_H + h]
            glast = gcol[DN_C - 1:DN_C, :] if d == 0 else gcol[0:1, :]
            s_scr[k, h] = st[i] * jnp.exp(glast) + kv[i]
            out_refs[k][rows[k], cols[i]] = qs[i] + av[i]
        return carry

    lax.fori_loop(0, PREP_NC, step, 0)

    for d in range(2):
        for h in range(DN_H):
            so_ref[0, d, h] = s_scr[d, h]


def _dn_scan(u, w, qg, kd, ai, bg, s0, layer):
    row_of = (lambda i: i, lambda i: i,
              lambda i: SCAN_STEPS + i, lambda i: SCAN_STEPS + _scan_lat_bwd_block(i))
    in_specs, args = [], []
    for k in range(4):
        d, rb = k % 2, row_of[k]
        wide = pl.BlockSpec((1, PREP_TM, DN_H * DN_DK), lambda i, d=d, rb=rb: (d, rb(i), 0))
        narrow = pl.BlockSpec((1, PREP_TM, DN_H * DN_C), lambda i, d=d, rb=rb: (d, rb(i), 0))
        in_specs += [wide, wide, wide, wide, narrow, pl.BlockSpec((PREP_TM, 128), lambda i, rb=rb: (rb(i), 0))]
        args += [u, w, qg, kd, ai, bg]
    state_spec = lambda m: pl.BlockSpec((1, 2, DN_H, DN_DK, DN_DV), m)
    in_specs.append(pl.BlockSpec((1, 1, 2, DN_H, DN_DK, DN_DV), lambda i: (i // SCAN_PER, layer, 0, 0, 0, 0)))
    o_spec = lambda m: pl.BlockSpec((PREP_TM, DN_H * DN_DV), m)
    o_shape = lambda n: jax.ShapeDtypeStruct((n, DN_H * DN_DV), F32)
    return pl.pallas_call(
        _scan_body,
        grid=(SCAN_STEPS,),
        in_specs=in_specs,
        out_specs=[o_spec(lambda i: (i, 0)), o_spec(lambda i: (i, 0)), o_spec(lambda i: (i, 0)),
                   o_spec(lambda i: (_scan_lat_bwd_block(i), 0)),
                   state_spec(lambda i: (i, 0, 0, 0, 0))],
        out_shape=[o_shape(RC), o_shape(RC), o_shape(RL), o_shape(RL),
                   jax.ShapeDtypeStruct((BATCH, 2, DN_H, DN_DK, DN_DV), F32)],
        scratch_shapes=[pltpu.VMEM((4, DN_H, DN_DK, DN_DV), F32)],
        compiler_params=_cp(1),
        name="dn_scan",
    )(*args, s0)


def _group_mean_matrix(width, group):
    r = lax.broadcasted_iota(jnp.int32, (width, width), 0)
    c = lax.broadcasted_iota(jnp.int32, (width, width), 1)
    return jnp.where((r // group) == (c // group), 1.0 / group, 0.0).astype(BF16)


def _group_rmsnorm(x, w, group):
    m = _group_mean_matrix(x.shape[-1], group)
    hi, lo = _split2(x * x)
    ms = _dot(hi, m) + _dot(lo, m)
    return x * lax.rsqrt(ms + EPS) * w


def _rope(x, cos, sin_signed, group):
    width = x.shape[-1]
    half = group // 2
    lane = lax.broadcasted_iota(jnp.int32, (1, width), 1)
    swapped = jnp.where((lane % group) < half,
                        pltpu.roll(x, width - half, axis=1), pltpu.roll(x, half, axis=1))
    return x * cos + swapped * sin_signed


def _attention_units(n, scores, values):
    outs = []
    s_next = scores(0)
    for u in range(n):
        s = s_next
        if u + 1 < n:
            s_next = scores(u + 1)
        m = jnp.max(s, axis=-1, keepdims=True)
        p = jnp.exp(s - m)
        l = jnp.sum(p, axis=-1, keepdims=True)
        outs.append(_dot(p.astype(BF16), values(u)) / l)
    return outs


GQA_G = GQA_H // GQA_KV
CTX_NS = 2


def _gqa_body(T, TQ, NS, latent, *refs):
    it = iter(refs)
    q_ref, kv_ref, qw_ref, kw_ref = next(it), next(it), next(it), next(it)
    if latent:
        cq_ref, sq_ref, ck_ref, sk_ref, kc_ref, vc_ref = (next(it) for _ in range(6))
    o_ref = next(it)
    kn_ref = None if latent else next(it)
    k_scr, v_scr = next(it), next(it)

    @pl.when(pl.program_id(1) == 0)
    def _():
        kv = kv_ref[...]
        k = _group_rmsnorm(kv[:, 0:128], kw_ref[...], GQA_HD)
        v = kv[:, 128:256]
        if latent:
            k = _rope(k, ck_ref[...], sk_ref[...], GQA_HD)
        else:
            kn_ref[...] = k
        for s in range(NS):
            for g in range(GQA_KV):
                sg = s * GQA_KV + g
                k_scr[sg, 0:T, :] = k[s * T:(s + 1) * T, g * GQA_HD:(g + 1) * GQA_HD].astype(BF16)
                v_scr[sg, 0:T, :] = v[s * T:(s + 1) * T, g * GQA_HD:(g + 1) * GQA_HD].astype(BF16)
                if latent:
                    k_scr[sg, T:T + PAST, :] = kc_ref[0, 0, :, g * GQA_HD:(g + 1) * GQA_HD].astype(BF16)
                    v_scr[sg, T:T + PAST, :] = vc_ref[0, 0, :, g * GQA_HD:(g + 1) * GQA_HD].astype(BF16)

    q = _group_rmsnorm(q_ref[...], qw_ref[...], GQA_HD)
    if latent:
        q = _rope(q, cq_ref[...], sq_ref[...], GQA_HD)
    q = q * (GQA_HD ** -0.5)
    head = lambda s, hh: q[s * TQ:(s + 1) * TQ, hh * GQA_HD:(hh + 1) * GQA_HD]
    units = [(s, u) for s in range(NS) for u in range(GQA_H // 2)]
    qu = [jnp.concatenate([head(s, 2 * u), head(s, 2 * u + 1)], axis=0) for s, u in units]
    kv_of = lambda i: units[i][0] * GQA_KV + (2 * units[i][1]) // GQA_G
    outs = _attention_units(len(units), lambda i: _bdot_nt(qu[i], k_scr[kv_of(i)]), lambda i: v_scr[kv_of(i)])
    for (s, u), o in zip(units, outs):
        for j in range(2):
            hh = 2 * u + j
            o_ref[s * TQ:(s + 1) * TQ, hh * GQA_HD:(hh + 1) * GQA_HD] = o[j * TQ:(j + 1) * TQ, :].astype(o_ref.dtype)


def _gqa(proj, qw, kw, T, TQ, n_seq, row_off, rope=None, cache=None, NS=1, layer=0):
    latent = rope is not None
    tk = T + (PAST if latent else 0)
    nq = T // TQ
    assert NS == 1 or (nq == 1 and not latent)
    n_seq, TQ, TB = n_seq // NS, NS * TQ, NS * T
    qo = row_off // TQ
    so = row_off // TB
    in_specs = [pl.BlockSpec((TQ, 512), lambda i, j: (qo + i * nq + j, C_GQ // 512)),
                pl.BlockSpec((TB, 256), lambda i, j: (so + i, C_GKV // 256)),
                pl.BlockSpec((1, 512), lambda i, j: (0, 0)),
                pl.BlockSpec((1, 128), lambda i, j: (0, 0))]
    args = [proj, proj, qw, kw]
    if latent:
        cq, sq, ck, sk = rope
        kc, vc = cache
        in_specs += [pl.BlockSpec((TQ, 512), lambda i, j: (j, 0)),
                     pl.BlockSpec((TQ, 512), lambda i, j: (j, 0)),
                     pl.BlockSpec((T, 128), lambda i, j: (0, 0)),
                     pl.BlockSpec((T, 128), lambda i, j: (0, 0)),
                     pl.BlockSpec((1, 1, PAST, 128), lambda i, j: (i, layer, 0, 0)),
                     pl.BlockSpec((1, 1, PAST, 128), lambda i, j: (i, layer, 0, 0))]
        args += [cq, sq, ck, sk, kc, vc]
    out_specs = [pl.BlockSpec((TQ, 512), lambda i, j: (i * nq + j, 0))]
    out_shape = [jax.ShapeDtypeStruct((n_seq * TB, 512), BF16)]
    if not latent:
        out_specs.append(pl.BlockSpec((TB, 128), lambda i, j: (i, 0)))
        out_shape.append(jax.ShapeDtypeStruct((n_seq * TB, 128), F32))
    return pl.pallas_call(
        functools.partial(_gqa_body, T, TQ // NS, NS, latent),
        grid=(n_seq, nq),
        in_specs=in_specs,
        out_specs=out_specs,
        out_shape=out_shape,
        scratch_shapes=[pltpu.VMEM((NS * GQA_KV, tk, GQA_HD), BF16), pltpu.VMEM((NS * GQA_KV, tk, GQA_HD), BF16)],
        compiler_params=_cp(2),
        name="gqa_%d" % T,
    )(*args)


MLA_DQ = MLA_NOPE + MLA_ROPE
MLA_KV_ROWS = 512


def _mla_body(T, TQ, NS, latent, *refs):
    it = iter(refs)
    cq_ref, ckv_ref, sm_ref, wq_ref, wkv_ref, qw_ref, kvw_ref = (next(it) for _ in range(7))
    if latent:
        cosq_ref, sinq_ref, cosk_ref, sink_ref, cc_ref, kc_ref = (next(it) for _ in range(6))
    o_ref = next(it)
    cn_ref = None if latent else next(it)
    k_scr, v_scr = next(it), next(it)
    tk = T + (PAST if latent else 0)

    @pl.when(pl.program_id(1) == 0)
    def _():
        x = ckv_ref[...]
        ms = jnp.mean(x * x, axis=-1, keepdims=True)
        ckv = x * lax.rsqrt(ms + EPS) * kvw_ref[...]
        sm = sm_ref[...]
        if latent:
            sm = _rope_small(sm, cosk_ref[...], sink_ref[...])
        else:
            cn_ref[...] = ckv
        kr = sm[:, 16:16 + MLA_ROPE]

        def put(s, rows0, ckv_rows, kr_rows):
            n = ckv_rows.shape[0]
            kv = _bdot(ckv_rows, wkv_ref[...])
            krb = kr_rows.astype(BF16)
            for h in range(MLA_H):
                sh = s * MLA_H + h
                k_scr[sh, rows0:rows0 + n, 0:MLA_NOPE] = kv[:, h * MLA_NOPE:(h + 1) * MLA_NOPE].astype(BF16)
                k_scr[sh, rows0:rows0 + n, MLA_NOPE:MLA_DQ] = krb
                v_scr[sh, rows0:rows0 + n, :] = kv[:, 512 + h * MLA_V:512 + (h + 1) * MLA_V].astype(BF16)

        for s in range(NS):
            for r0 in range(0, T, MLA_KV_ROWS):
                r1 = min(r0 + MLA_KV_ROWS, T)
                put(s, r0, ckv[s * T + r0:s * T + r1], kr[s * T + r0:s * T + r1])
        if latent:
            put(0, T, cc_ref[0, 0], kc_ref[0, 0])

    x = cq_ref[...]
    ms = jnp.mean(x * x, axis=-1, keepdims=True)
    cq = x * lax.rsqrt(ms + EPS) * qw_ref[...]
    qf = _bdot(cq, wq_ref[...])
    qn = qf[:, 0:512]
    qr = qf[:, 512:768]
    if latent:
        qr = _rope(qr, cosq_ref[...], sinq_ref[...], MLA_ROPE)
    scale = MLA_DQ ** -0.5
    units = [(s, h) for s in range(NS) for h in range(MLA_H)]
    rows = lambda s: slice(s * TQ, (s + 1) * TQ)
    qh = [jnp.concatenate([qn[rows(s), h * MLA_NOPE:(h + 1) * MLA_NOPE],
                           qr[rows(s), h * MLA_ROPE:(h + 1) * MLA_ROPE]], axis=1) * scale for s, h in units]
    sh = lambda i: units[i][0] * MLA_H + units[i][1]
    outs = _attention_units(len(units), lambda i: _bdot_nt(qh[i], k_scr[sh(i)]), lambda i: v_scr[sh(i)])
    for (s, h), o in zip(units, outs):
        o_ref[rows(s), h * MLA_V:(h + 1) * MLA_V] = o.astype(o_ref.dtype)


def _rope_small(sm, cos, sin_signed):
    lane = lax.broadcasted_iota(jnp.int32, (1, 128), 1)
    half = MLA_ROPE // 2
    swapped = jnp.where(lane < 16 + half, pltpu.roll(sm, 128 - half, axis=1), pltpu.roll(sm, half, axis=1))
    return sm * cos + swapped * sin_signed


def _mla(proj, wq, wkv, qw, kvw, T, TQ, n_seq, row_off, rope=None, cache=None, NS=1, layer=0):
    latent = rope is not None
    tk = T + (PAST if latent else 0)
    nq = T // TQ
    assert NS == 1 or (nq == 1 and not latent)
    n_seq, TQ, TB = n_seq // NS, NS * TQ, NS * T
    qo = row_off // TQ
    so = row_off // TB
    in_specs = [pl.BlockSpec((TQ, 256), lambda i, j: (qo + i * nq + j, C_MCQ // 256)),
                pl.BlockSpec((TB, 128), lambda i, j: (so + i, C_MCKV // 128)),
                pl.BlockSpec((TB, 128), lambda i, j: (so + i, C_SM // 128)),
                pl.BlockSpec((MLA_QL, 768), lambda i, j: (0, 0)),
                pl.BlockSpec((MLA_KVL, 1024), lambda i, j: (0, 0)),
                pl.BlockSpec((1, 256), lambda i, j: (0, 0)),
                pl.BlockSpec((1, 128), lambda i, j: (0, 0))]
    args = [proj, proj, proj, wq, wkv, qw, kvw]
    if latent:
        cosq, sinq, cosk, sink = rope
        cc, kc = cache
        in_specs += [pl.BlockSpec((TQ, 256), lambda i, j: (j, 0)),
                     pl.BlockSpec((TQ, 256), lambda i, j: (j, 0)),
                     pl.BlockSpec((T, 128), lambda i, j: (0, 0)),
                     pl.BlockSpec((T, 128), lambda i, j: (0, 0)),
                     pl.BlockSpec((1, 1, PAST, 128), lambda i, j: (i, layer, 0, 0)),
                     pl.BlockSpec((1, 1, PAST, MLA_ROPE), lambda i, j: (i, layer, 0, 0))]
        args += [cosq, sinq, cosk, sink, cc, kc]
    out_specs = [pl.BlockSpec((TQ, 512), lambda i, j: (i * nq + j, 0))]
    out_shape = [jax.ShapeDtypeStruct((n_seq * TB, 512), BF16)]
    if not latent:
        out_specs.append(pl.BlockSpec((TB, 128), lambda i, j: (i, 0)))
        out_shape.append(jax.ShapeDtypeStruct((n_seq * TB, 128), F32))
    return pl.pallas_call(
        functools.partial(_mla_body, T, TQ // NS, NS, latent),
        grid=(n_seq, nq),
        in_specs=in_specs,
        out_specs=out_specs,
        out_shape=out_shape,
        scratch_shapes=[pltpu.VMEM((NS * MLA_H, tk, MLA_DQ), BF16), pltpu.VMEM((NS * MLA_H, tk, MLA_V), BF16)],
        compiler_params=_cp(2),
        name="mla_%d" % T,
    )(*args)


MG_TM = 512


def _merge_body(x_ref, m_ref, ofc_ref, obc_ref, ofl_ref, obl_ref, dg_ref, ogc_ref, ogl_ref, omc_ref, oml_ref,
                bg_ref, ng_ref, wb_ref, wo_ref, o_ref):
    is_ctx = pl.program_id(0) < RC // MG_TM
    og = jnp.where(is_ctx, ogc_ref[...], ogl_ref[...])
    om = jnp.where(is_ctx, omc_ref[...], oml_ref[...])
    odn = jnp.where(is_ctx, ofc_ref[...] + obc_ref[...], ofl_ref[...] + obl_ref[...])
    dg = dg_ref[...]
    ng = ng_ref[...]
    parts = []
    for h in range(DN_H):
        oh = odn[:, h * DN_DV:(h + 1) * DN_DV]
        ms = jnp.mean(oh * oh, axis=-1, keepdims=True)
        parts.append(oh * lax.rsqrt(ms + EPS) * ng * _silu(dg[:, h * DN_DV:(h + 1) * DN_DV]))
    br0 = jnp.concatenate(parts, axis=1)
    gate = lambda n: _sigmoid(bg_ref[:, n * D:(n + 1) * D].astype(F32))
    merged = gate(0) * _bdot(br0, wb_ref[0, 0])
    merged = merged + gate(1) * _bdot(og, wb_ref[0, 1])
    merged = merged + gate(2) * _bdot(om, wb_ref[0, 2])
    out = _bdot(merged, wo_ref[0])
    o_ref[...] = x_ref[...] + m_ref[0, 2:3, :] * out


def _merge(x, mods_l, dn_o, proj, pbg, og_c, og_l, om_c, om_l, ng, wb, wo, l):
    row = lambda i: (i, 0)
    n_ctx = RC // MG_TM
    ctx_row = lambda i: (jnp.minimum(i, n_ctx - 1), 0)
    lat_row = lambda i: (jnp.maximum(i - n_ctx, 0), 0)
    return pl.pallas_call(
        _merge_body,
        grid=(R // MG_TM,),
        in_specs=[pl.BlockSpec((MG_TM, D), row),
                  pl.BlockSpec((1, 6, D), lambda i: (_mod_group(i * MG_TM), 0, 0)),
                  pl.BlockSpec((MG_TM, 512), ctx_row),
                  pl.BlockSpec((MG_TM, 512), ctx_row),
                  pl.BlockSpec((MG_TM, 512), lat_row),
                  pl.BlockSpec((MG_TM, 512), lat_row),
                  pl.BlockSpec((MG_TM, 512), lambda i: (i, C_DG // 512)),
                  pl.BlockSpec((MG_TM, 512), ctx_row),
                  pl.BlockSpec((MG_TM, 512), lat_row),
                  pl.BlockSpec((MG_TM, 512), ctx_row),
                  pl.BlockSpec((MG_TM, 512), lat_row),
                  pl.BlockSpec((MG_TM, N_BG), row),
                  pl.BlockSpec((1, DN_DV), lambda i: (0, 0)),
                  pl.BlockSpec((1, 3, 512, D), lambda i: (l, 0, 0, 0)),
                  pl.BlockSpec((1, D, D), lambda i: (l, 0, 0))],
        out_specs=pl.BlockSpec((MG_TM, D), row),
        out_shape=jax.ShapeDtypeStruct((R, D), F32),
        compiler_params=_cp(1),
        name="merge",
    )(x, mods_l, *dn_o, proj, og_c, og_l, om_c, om_l, pbg, ng, wb, wo)


FF_TM = 512
FF_CHUNKS = ((0, 1536), (1536, FF_DENSE))


def _ffn_body(x_ref, m_ref, g_ref, w1_ref, w3_ref, w2_ref, o_ref):
    x = x_ref[...]
    h = _modnorm(x, g_ref[...], m_ref[0, 3:4, :], m_ref[0, 4:5, :]).astype(BF16)
    y = None
    for c0, c1 in FF_CHUNKS:
        a = _silu(_dot(h, w1_ref[0, :, c0:c1])) * _dot(h, w3_ref[0, :, c0:c1])
        yc = _dot(a.astype(BF16), w2_ref[0, c0:c1, :])
        y = yc if y is None else y + yc
    o_ref[...] = x + m_ref[0, 5:6, :] * y


def _ffn(x, mods_l, g, w1, w3, w2, j):
    once = pl.Buffered(1)
    return pl.pallas_call(
        _ffn_body,
        grid=(R // FF_TM,),
        in_specs=[pl.BlockSpec((FF_TM, D), lambda i: (i, 0)),
                  pl.BlockSpec((1, 6, D), lambda i: (_mod_group(i * FF_TM), 0, 0)),
                  pl.BlockSpec((1, D), lambda i: (0, 0)),
                  pl.BlockSpec((1, D, FF_DENSE), lambda i: (j, 0, 0), pipeline_mode=once),
                  pl.BlockSpec((1, D, FF_DENSE), lambda i: (j, 0, 0), pipeline_mode=once),
                  pl.BlockSpec((1, FF_DENSE, D), lambda i: (j, 0, 0), pipeline_mode=once)],
        out_specs=pl.BlockSpec((FF_TM, D), lambda i: (i, 0)),
        out_shape=jax.ShapeDtypeStruct((R, D), F32),
        compiler_params=_cp(1),
        name="ffn",
    )(x, mods_l, g, w1, w3, w2)


RT_TM = 512
MOE_TG, MOE_TF = 512, 1792
MOE_NT = 2 * R // MOE_TG + N_EXP
MOE_ROWS = MOE_NT * MOE_TG
CB_TM = 256


def _router_body(x_ref, m_ref, g_ref, rw_ref, rb_ref, hn_ref, route_ref, cnt_ref, base_scr):
    @pl.when(pl.program_id(0) == 0)
    def _():
        base_scr[...] = jnp.zeros_like(base_scr)

    h = _modnorm(x_ref[...], g_ref[...], m_ref[0, 3:4, :], m_ref[0, 4:5, :])
    hn_ref[...] = h
    logits = _dot3(h, rw_ref[...]) + rb_ref[...]
    lane = lax.broadcasted_iota(jnp.int32, logits.shape, 1)
    m1 = jnp.max(logits, axis=-1, keepdims=True)
    i1 = jnp.min(jnp.where(logits == m1, lane, 128), axis=-1, keepdims=True)
    sel1 = lane == i1
    rest = jnp.where(sel1, -jnp.inf, logits)
    m2 = jnp.max(rest, axis=-1, keepdims=True)
    i2 = jnp.min(jnp.where(rest == m2, lane, 128), axis=-1, keepdims=True)
    sel2 = lane == i2
    e2 = jnp.exp(m2 - m1)
    p1 = 1.0 / (1.0 + e2)
    p2 = e2 / (1.0 + e2)

    cnt = jnp.where(jnp.logical_or(sel1, sel2), 1.0, 0.0)
    r = lax.broadcasted_iota(jnp.int32, (RT_TM, RT_TM), 0)
    c = lax.broadcasted_iota(jnp.int32, (RT_TM, RT_TM), 1)
    before = jnp.where(r > c, 1.0, 0.0).astype(BF16)
    seen = base_scr[...] + _dot(before, cnt.astype(BF16))
    rank1 = jnp.sum(jnp.where(sel1, seen, 0.0), axis=-1, keepdims=True)
    rank2 = jnp.sum(jnp.where(sel2, seen, 0.0), axis=-1, keepdims=True)
    vals = (i1.astype(F32), i2.astype(F32), rank1, rank2, p1, p2)
    route = jnp.zeros(logits.shape, F32)
    for k, val in enumerate(vals):
        route = jnp.where(lane == k, val, route)
    route_ref[...] = route
    base_scr[...] += jnp.sum(cnt, axis=0, keepdims=True)
    cnt_ref[...] = base_scr[...]


def _router(x, mods_l, g, rw128, rb128):
    return pl.pallas_call(
        _router_body,
        grid=(R // RT_TM,),
        in_specs=[pl.BlockSpec((RT_TM, D), lambda i: (i, 0)),
                  pl.BlockSpec((1, 6, D), lambda i: (_mod_group(i * RT_TM), 0, 0)),
                  pl.BlockSpec((1, D), lambda i: (0, 0)),
                  pl.BlockSpec((D, 128), lambda i: (0, 0)),
                  pl.BlockSpec((1, 128), lambda i: (0, 0))],
        out_specs=[pl.BlockSpec((RT_TM, D), lambda i: (i, 0)),
                   pl.BlockSpec((RT_TM, 128), lambda i: (i, 0)),
                   pl.BlockSpec((1, 128), lambda i: (0, 0))],
        out_shape=[jax.ShapeDtypeStruct((R, D), F32),
                   jax.ShapeDtypeStruct((R, 128), F32),
                   jax.ShapeDtypeStruct((1, 128), F32)],
        scratch_shapes=[pltpu.VMEM((1, 128), F32)],
        compiler_params=_cp(1),
        name="moe_router",
    )(x, mods_l, g, rw128, rb128)


def _rows_copy(src_hbm, dst, sem, n):
    return pltpu.make_async_copy(src_hbm.at[pl.ds(0, n)], dst, sem)


def _start_row_gather(idx_ref, n, src_hbm, dst, sem, inline=False, both_queues=False):
    def issue(i, priority):
        pltpu.make_async_copy(src_hbm.at[pl.ds(idx_ref[0, 0, i], 1)], dst.at[pl.ds(i, 1)], sem).start(
            priority=priority)

    if inline:
        for i in range(n):
            issue(i, i % 2 if both_queues else 0)
    else:
        def body(i, carry):
            issue(i, 0)
            return carry

        lax.fori_loop(0, n, body, 0, unroll=8)


def _stage_to_bf16(src_hbm, dst, stage, sems):
    ch = stage.shape[1]
    n = src_hbm.shape[0] // ch
    copy = lambda c: pltpu.make_async_copy(src_hbm.at[pl.ds(c * ch, ch)], stage.at[c % 2], sems.at[c % 2])
    copy(0).start()
    for c in range(n):
        if c + 1 < n:
            copy(c + 1).start()
        copy(c).wait()
        dst[c * ch:(c + 1) * ch, :] = stage[c % 2].astype(BF16)


def _experts_body(j, te_ref, tv_ref, src_ref, nsrc_ref, hn_hbm, w1_hbm, w3_hbm, w2_hbm, ys_ref,
                  xg_scr, w1_ref, w3_ref, w2_ref, st13, st2, sems, wsems):
    t = pl.program_id(0)
    slot = t % 2
    valid = tv_ref[t] > 0
    e = te_ref[t]
    new_expert = jnp.logical_or(t == 0, e != te_ref[jnp.maximum(t - 1, 0)])

    @pl.when(jnp.logical_and(valid, new_expert))
    def _():
        _stage_to_bf16(w1_hbm.at[j, e], w1_ref, st13, wsems)
        _stage_to_bf16(w3_hbm.at[j, e], w3_ref, st13, wsems)
        _stage_to_bf16(w2_hbm.at[j, e], w2_ref, st2, wsems)

    requested = jnp.where(t == 0, valid, tv_ref[jnp.maximum(t - 1, 0)] > 0)

    @pl.when(jnp.logical_and(t == 0, valid))
    def _():
        _start_row_gather(src_ref, MOE_TG, hn_hbm, xg_scr.at[0], sems.at[0])

    @pl.when(requested)
    def _():
        _rows_copy(hn_hbm, xg_scr.at[slot], sems.at[slot], MOE_TG).wait()

    @pl.when(valid)
    def _():
        xb = xg_scr[slot].astype(BF16)
        _start_row_gather(nsrc_ref, MOE_TG, hn_hbm, xg_scr.at[1 - slot], sems.at[1 - slot], inline=True)
        y = None
        for c in range(FF_EXP // MOE_TF):
            cs = slice(c * MOE_TF, (c + 1) * MOE_TF)
            a = _silu(_dot(xb, w1_ref[:, cs])) * _dot(xb, w3_ref[:, cs])
            yc = _dot(a.astype(BF16), w2_ref[cs, :])
            y = yc if y is None else y + yc
        ys_ref[...] = y

    @pl.when(jnp.logical_not(valid))
    def _():
        ys_ref[...] = jnp.zeros_like(ys_ref)

    @pl.when(jnp.logical_and(t == MOE_NT - 1, valid))
    def _():
        _rows_copy(hn_hbm, xg_scr.at[1 - slot], sems.at[1 - slot], MOE_TG).wait()


MOE_WCH13, MOE_WCH2 = 128, 448


def _experts(tile_e, tile_v, src, hn, w1, w3, w2, j):
    hbm = pl.BlockSpec(memory_space=pl.ANY)
    grid_spec = pltpu.PrefetchScalarGridSpec(
        num_scalar_prefetch=2,
        grid=(MOE_NT,),
        in_specs=[pl.BlockSpec((1, 1, MOE_TG), lambda t, te, tv: (t, 0, 0), memory_space=pltpu.SMEM),
                  pl.BlockSpec((1, 1, MOE_TG), lambda t, te, tv: (t + 1, 0, 0), memory_space=pltpu.SMEM),
                  hbm, hbm, hbm, hbm],
        out_specs=pl.BlockSpec((MOE_TG, D), lambda t, te, tv: (t, 0)),
        scratch_shapes=[pltpu.VMEM((2, MOE_TG, D), F32),
                        pltpu.VMEM((D, FF_EXP), BF16), pltpu.VMEM((D, FF_EXP), BF16),
                        pltpu.VMEM((FF_EXP, D), BF16),
                        pltpu.VMEM((2, MOE_WCH13, FF_EXP), F32), pltpu.VMEM((2, MOE_WCH2, D), F32),
                        pltpu.SemaphoreType.DMA((2,)), pltpu.SemaphoreType.DMA((2,))],
    )
    return pl.pallas_call(
        functools.partial(_experts_body, j),
        grid_spec=grid_spec,
        out_shape=jax.ShapeDtypeStruct((MOE_ROWS, D), F32),
        compiler_params=_cp(1),
        name="moe_experts",
    )(tile_e, tile_v, src, src, hn, w1, w3, w2)


def _combine_body(pos_ref, npos_ref, x_ref, m_ref, route_ref, ys_hbm, o_ref, buf, sems):
    i = pl.program_id(0)
    slot = i % 2
    n_rows = 2 * CB_TM

    @pl.when(i == 0)
    def _():
        _start_row_gather(pos_ref, n_rows, ys_hbm, buf.at[0], sems.at[0])

    @pl.when(i + 1 < pl.num_programs(0))
    def _():
        _start_row_gather(npos_ref, n_rows, ys_hbm, buf.at[1 - slot], sems.at[1 - slot], inline=True,
                          both_queues=True)

    _rows_copy(ys_hbm, buf.at[slot], sems.at[slot], n_rows).wait()
    route = route_ref[...]
    y = route[:, 4:5] * buf[slot, 0:CB_TM, :] + route[:, 5:6] * buf[slot, CB_TM:n_rows, :]
    o_ref[...] = x_ref[...] + m_ref[0, 5:6, :] * y


def _combine(pos, x, mods_l, route, ys):
    n_tiles = R // CB_TM
    return pl.pallas_call(
        _combine_body,
        grid=(n_tiles,),
        in_specs=[pl.BlockSpec((1, 1, 2 * CB_TM), lambda i: (i, 0, 0), memory_space=pltpu.SMEM),
                  pl.BlockSpec((1, 1, 2 * CB_TM), lambda i: (jnp.minimum(i + 1, n_tiles - 1), 0, 0),
                               memory_space=pltpu.SMEM),
                  pl.BlockSpec((CB_TM, D), lambda i: (i, 0)),
                  pl.BlockSpec((1, 6, D), lambda i: (_mod_group(i * CB_TM), 0, 0)),
                  pl.BlockSpec((CB_TM, 128), lambda i: (i, 0)),
                  pl.BlockSpec(memory_space=pl.ANY)],
        out_specs=pl.BlockSpec((CB_TM, D), lambda i: (i, 0)),
        out_shape=jax.ShapeDtypeStruct((R, D), F32),
        scratch_shapes=[pltpu.VMEM((2, 2 * CB_TM, D), F32), pltpu.SemaphoreType.DMA((2,))],
        compiler_params=_cp(1),
        name="moe_combine",
    )(pos, pos, x, mods_l, route, ys)


def _moe(x, mods_l, g, rw128, rb128, w1, w3, w2, j):
    hn, route, cnt = _router(x, mods_l, g, rw128, rb128)
    eid = route[:, 0:2].astype(jnp.int32)
    rank = route[:, 2:4].astype(jnp.int32)
    counts = cnt[0, :N_EXP].astype(jnp.int32)
    gsize = (counts + MOE_TG - 1) // MOE_TG * MOE_TG
    gend = jnp.cumsum(gsize)
    pos = (gend - gsize)[eid] + rank
    tile_start = jnp.arange(MOE_NT, dtype=jnp.int32) * MOE_TG
    tile_e = jnp.minimum(jnp.sum(tile_start[:, None] >= gend[None, :], axis=1), N_EXP - 1).astype(jnp.int32)
    tile_v = (tile_start < gend[-1]).astype(jnp.int32)
    tok = jnp.broadcast_to(jnp.arange(R, dtype=jnp.int32)[:, None], (R, 2))
    src = jnp.zeros((MOE_ROWS + MOE_TG,), jnp.int32).at[pos.reshape(-1)].set(
        tok.reshape(-1), unique_indices=True, mode="promise_in_bounds")
    ys = _experts(tile_e, tile_v, src.reshape(MOE_NT + 1, 1, MOE_TG), hn, w1, w3, w2, j)
    pos_t = pos.reshape(R // CB_TM, CB_TM, 2).transpose(0, 2, 1).reshape(R // CB_TM, 1, 2 * CB_TM)
    return _combine(pos_t, x, mods_l, route, ys)


FN_TM = 1024


def _final_body(x_ref, g_ref, o_ref):
    x = x_ref[...]
    ms = jnp.mean(x * x, axis=-1, keepdims=True)
    o_ref[...] = x * lax.rsqrt(ms + EPS) * g_ref[...]


def _final_norm(x, g, row_off, n_rows):
    bo = row_off // FN_TM
    return pl.pallas_call(
        _final_body,
        grid=(n_rows // FN_TM,),
        in_specs=[pl.BlockSpec((FN_TM, D), lambda i: (bo + i, 0)), pl.BlockSpec((1, D), lambda i: (0, 0))],
        out_specs=pl.BlockSpec((FN_TM, D), lambda i: (i, 0)),
        out_shape=jax.ShapeDtypeStruct((n_rows, D), F32),
        compiler_params=_cp(1),
        name="final_norm",
    )(x, g)


def _rope_tables(n_tokens, rot_dim):
    t = np.arange(n_tokens)
    row = (t // GRID_W).astype(np.float32)
    col = (t % GRID_W).astype(np.float32)
    n_freq = rot_dim // 4
    inv = (ROPE_THETA ** (-jnp.arange(n_freq, dtype=F32) / n_freq))
    ang = jnp.concatenate([jnp.asarray(row)[:, None] * inv, jnp.asarray(col)[:, None] * inv], axis=-1)
    cos, sin = jnp.cos(ang), jnp.sin(ang)
    return jnp.concatenate([cos, cos], axis=-1), jnp.concatenate([-sin, sin], axis=-1)


def kernel(x_prompt, x_sample, c, cache_gqa_k, cache_gqa_v, cache_mla_ckv, cache_mla_krope, state_delta, c_ctx, w_mod, b_mod, norm1_g, norm2_g, w_in, dn_conv_w, dn_a_log, dn_dt_bias, dn_norm_g, gqa_q_norm, gqa_k_norm, mla_q_norm, mla_kv_norm, mla_w_uq, mla_w_ukv, w_branch, w_out, ffd_w1, ffd_w3, ffd_w2, router_w, router_b, moe_w1, moe_w3, moe_w2, final_g):
    x = jnp.concatenate([x_prompt.reshape(RC, D), x_sample.reshape(RL, D)], axis=0)
    cond8 = jnp.concatenate([c_ctx[None, :], c, jnp.zeros((3, D), F32)], axis=0)
    mods = _mods(cond8, w_mod, b_mod).reshape(DEPTH, 8, 6, D)

    cg, sg = _rope_tables(DEC_SEQ, GQA_HD)
    gqa_rope = (jnp.tile(cg, (1, GQA_H)), jnp.tile(sg, (1, GQA_H)),
                jnp.tile(cg, (1, GQA_KV)), jnp.tile(sg, (1, GQA_KV)))
    cm, sm_ = _rope_tables(DEC_SEQ, MLA_ROPE)
    padk = lambda t: jnp.pad(t, ((0, 0), (16, 128 - 16 - MLA_ROPE)))
    mla_rope = (jnp.tile(cm, (1, MLA_H)), jnp.tile(sm_, (1, MLA_H)),
                jnp.pad(cm, ((0, 0), (16, 128 - 16 - MLA_ROPE)), constant_values=1.0), padk(sm_))

    w_in_bf, w_branch_bf, w_out_bf = w_in.astype(BF16), w_branch.astype(BF16), w_out.astype(BF16)
    ffd_bf = (ffd_w1.astype(BF16), ffd_w3.astype(BF16), ffd_w2.astype(BF16))
    gqa_cache = (cache_gqa_k.reshape(DEC_BATCH, DEPTH, PAST, 128), cache_gqa_v.reshape(DEC_BATCH, DEPTH, PAST, 128))

    new_k, new_v, new_ckv, new_kr, new_s = [], [], [], [], []
    for l in range(DEPTH):
        pbg, proj = _inproj(x, mods[l], norm1_g[l][None, :], w_in_bf, l)

        pad128 = lambda v: jnp.pad(v.reshape(1, 8), ((0, 0), (8, 112)))
        padt = lambda v: jnp.pad(v.reshape(8, 1), ((8, 0), (0, 0)))
        sm_t = proj[:, C_SM:C_SM + 16].T
        dn_u, dn_w, dn_qg, dn_kd, dn_ai, dn_bg = _dn_prep(
            proj, sm_t, dn_conv_w[l], pad128(dn_a_log[l]), pad128(dn_dt_bias[l]),
            padt(dn_a_log[l]), padt(dn_dt_bias[l]))
        *dn_o, s_c = _dn_scan(dn_u, dn_w, dn_qg, dn_kd, dn_ai, dn_bg, state_delta, l)

        qw = jnp.tile(gqa_q_norm[l][None, :], (1, GQA_H))
        kw = jnp.tile(gqa_k_norm[l][None, :], (1, GQA_KV))
        o_g_c, kn_c = _gqa(proj, qw, kw, SEQ, SEQ, BATCH, 0, NS=CTX_NS)
        (o_g_l,) = _gqa(proj, qw, kw, DEC_SEQ, 256, DEC_BATCH, RC, rope=gqa_rope, cache=gqa_cache, layer=l)

        wq = mla_w_uq[l].reshape(MLA_QL, MLA_H, MLA_DQ)
        wq = jnp.concatenate([wq[:, :, :MLA_NOPE].reshape(MLA_QL, -1), wq[:, :, MLA_NOPE:].reshape(MLA_QL, -1)],
                             axis=1).astype(BF16)
        wkv = mla_w_ukv[l].reshape(MLA_KVL, MLA_H, MLA_NOPE + MLA_V)
        wkv = jnp.concatenate([wkv[:, :, :MLA_NOPE].reshape(MLA_KVL, -1), wkv[:, :, MLA_NOPE:].reshape(MLA_KVL, -1)],
                              axis=1).astype(BF16)
        mqw, mkvw = mla_q_norm[l][None, :], mla_kv_norm[l][None, :]
        o_m_c, ckv_c = _mla(proj, wq, wkv, mqw, mkvw, SEQ, SEQ, BATCH, 0, NS=CTX_NS)
        (o_m_l,) = _mla(proj, wq, wkv, mqw, mkvw, DEC_SEQ, 256, DEC_BATCH, RC, rope=mla_rope,
                        cache=(cache_mla_ckv, cache_mla_krope), layer=l)
        x = _merge(x, mods[l], dn_o, proj, pbg, o_g_c, o_g_l, o_m_c, o_m_l, dn_norm_g[l][None, :],
                   w_branch_bf, w_out_bf, l)

        j = l // 2
        if l % 2 == 0:
            x = _ffn(x, mods[l], norm2_g[l][None, :], *ffd_bf, j)
        else:
            rw128 = jnp.pad(router_w[j], ((0, 0), (0, 128 - N_EXP)))
            rb128 = jnp.pad(router_b[j][None, :], ((0, 0), (0, 128 - N_EXP)), constant_values=-jnp.inf)
            x = _moe(x, mods[l], norm2_g[l][None, :], rw128, rb128, moe_w1, moe_w3, moe_w2, j)

        new_k.append(kn_c.reshape(BATCH, SEQ, GQA_KV, GQA_HD))
        new_v.append(proj[:RC, C_GKV + 128:C_GKV + 256].reshape(BATCH, SEQ, GQA_KV, GQA_HD))
        new_ckv.append(ckv_c.reshape(BATCH, SEQ, MLA_KVL))
        new_kr.append(proj[:RC, C_SM + 16:C_SM + 16 + MLA_ROPE].reshape(BATCH, SEQ, MLA_ROPE))
        new_s.append(s_c)

    y_c = _final_norm(x, final_g[None, :], 0, RC)
    y_l = _final_norm(x, final_g[None, :], RC, RL)
    return (y_c.reshape(BATCH, SEQ, D), y_l.reshape(DEC_BATCH, DEC_SEQ, D),
            jnp.stack(new_k, axis=1), jnp.stack(new_v, axis=1), jnp.stack(new_ckv, axis=1),
            jnp.stack(new_kr, axis=1), jnp.stack(new_s, axis=1))
```

```python
import functools
import math

import jax
import jax.numpy as jnp
import numpy as np
from jax import lax
from jax.experimental import pallas as pl
from jax.experimental.pallas import tpu as pltpu

F32 = jnp.float32
BF16 = jnp.bfloat16

D = 1024
BATCH, SEQ = 32, 256
DEC_BATCH, DEC_SEQ = 4, 2048
DEPTH = 4
PAST = 512
GRID_W = 64
ROPE_THETA = 10000.0
EPS = 1e-6
DN_H, DN_DK, DN_DV, DN_C = 4, 128, 128, 64
GQA_H, GQA_KV, GQA_HD = 8, 2, 64
MLA_H, MLA_QL, MLA_KVL, MLA_NOPE, MLA_ROPE, MLA_V = 8, 256, 128, 64, 32, 64
FF_DENSE, N_EXP, FF_EXP = 2816, 8, 3584

RC = BATCH * SEQ
RL = DEC_BATCH * DEC_SEQ
R = RC + RL

N_BG = 3 * D
C_QKV, C_DG, C_GQ, C_GKV, C_MCQ, C_MCKV, C_SM = 0, 1536, 2048, 2560, 2816, 3072, 3200
N_PROJ = 3328
NP = N_BG + N_PROJ

VMEM_LIMIT = 56 * 1024 * 1024


def _cp(n_grid):
    return pltpu.CompilerParams(dimension_semantics=("arbitrary",) * n_grid,
                                vmem_limit_bytes=VMEM_LIMIT)


def _dot(a, b):
    return jnp.dot(a, b, preferred_element_type=F32)


def _bdot(a, b):
    return jnp.dot(a.astype(BF16), b.astype(BF16), preferred_element_type=F32)


def _bdot_nt(a, b):
    return lax.dot_general(a.astype(BF16), b.astype(BF16), (((1,), (1,)), ((), ())),
                           preferred_element_type=F32)


def _bdot_tn(a, b):
    return lax.dot_general(a.astype(BF16), b.astype(BF16), (((0,), (0,)), ((), ())),
                           preferred_element_type=F32)


def _split2(a):
    hi = a.astype(BF16)
    lo = (a - hi.astype(F32)).astype(BF16)
    return hi, lo


def _split3(a):
    a1 = a.astype(BF16)
    r1 = a - a1.astype(F32)
    a2 = r1.astype(BF16)
    a3 = (r1 - a2.astype(F32)).astype(BF16)
    return a1, a2, a3


def _dot3(a, b):
    ah, al = _split2(a)
    bh, bl = _split2(b)
    return _dot(ah, bh) + (_dot(ah, bl) + _dot(al, bh))


def _mask_dot_r(mask_bf, g):
    g1, g2, g3 = _split3(g)
    return _dot(mask_bf, g1) + (_dot(mask_bf, g2) + _dot(mask_bf, g3))


def _mask_dot_l(g, mask_bf):
    g1, g2, g3 = _split3(g)
    return _dot(g1, mask_bf) + (_dot(g2, mask_bf) + _dot(g3, mask_bf))


def _sigmoid(x):
    return 1.0 / (1.0 + jnp.exp(-x))


def _silu(x):
    return x * _sigmoid(x)


def _softplus(x):
    return jnp.maximum(x, 0.0) + jnp.log1p(jnp.exp(-jnp.abs(x)))


def _mod_group(row0):
    return jnp.where(row0 < RC, 0, 1 + (row0 - RC) // DEC_SEQ)


MODS_TN = 1536


def _mods_body(c_ref, w_ref, b_ref, o_ref):
    s = _silu(c_ref[...])
    o_ref[0] = _bdot(s, w_ref[0]) + b_ref[0]


def _mods(cond8, w_mod, b_mod):
    nj = 6 * D // MODS_TN
    return pl.pallas_call(
        _mods_body,
        grid=(DEPTH, nj),
        in_specs=[pl.BlockSpec((8, D), lambda l, j: (0, 0)),
                  pl.BlockSpec((1, D, MODS_TN), lambda l, j: (l, 0, j)),
                  pl.BlockSpec((1, 1, MODS_TN), lambda l, j: (l, 0, j))],
        out_specs=pl.BlockSpec((1, 8, MODS_TN), lambda l, j: (l, 0, j)),
        out_shape=jax.ShapeDtypeStruct((DEPTH, 8, 6 * D), F32),
        compiler_params=_cp(2),
        name="mods",
    )(cond8, w_mod, b_mod.reshape(DEPTH, 1, 6 * D))


IN_TM = 512
IN_CHUNK = 1536


def _modnorm(x, g, shift, scale):
    ms = jnp.mean(x * x, axis=-1, keepdims=True)
    y = x * lax.rsqrt(ms + EPS) * g
    return y * (1.0 + scale) + shift


W_IN_PIECES = ((3248, 6320), (0, 2048), (2064, 3216), (2048, 2064), (3216, 3248))
W_IN_COLS = 6320


def _row_specs(x, tm):
    if not isinstance(x, tuple):
        return [pl.BlockSpec((tm, D), lambda i: (i, 0))], [x]
    n_ctx = RC // tm
    return ([pl.BlockSpec((tm, D), lambda i: (jnp.minimum(i, n_ctx - 1), 0)),
             pl.BlockSpec((tm, D), lambda i: (jnp.maximum(i - n_ctx, 0), 0))], list(x))


def _row_tile(x_refs, tm):
    if len(x_refs) == 1:
        return x_refs[0][...]
    return jnp.where(pl.program_id(0) < RC // tm, x_refs[0][...], x_refs[1][...])


def _inproj_body(n_x, *refs):
    x_refs = refs[:n_x]
    m_ref, g_ref, wsrc_ref, bg_ref, o_ref, w_ref = refs[n_x:]

    @pl.when(pl.program_id(0) == 0)
    def _():
        at = 0
        for a, b in W_IN_PIECES:
            w_ref[:, at:at + (b - a)] = wsrc_ref[0, :, a:b]
            at += b - a
        w_ref[:, at:NP] = jnp.zeros((D, NP - at), BF16)

    h = _modnorm(_row_tile(x_refs, IN_TM), g_ref[...], m_ref[0, 0:1, :], m_ref[0, 1:2, :]).astype(BF16)
    for c0 in range(0, N_BG, IN_CHUNK):
        bg_ref[:, c0:c0 + IN_CHUNK] = _dot(h, w_ref[:, c0:c0 + IN_CHUNK]).astype(BF16)
    for c0 in range(0, N_PROJ, IN_CHUNK):
        c1 = min(c0 + IN_CHUNK, N_PROJ)
        o_ref[:, c0:c1] = _dot(h, w_ref[:, N_BG + c0:N_BG + c1])


def _inproj(x, mods_l, g, w_bf, l):
    x_specs, x_args = _row_specs(x, IN_TM)
    return pl.pallas_call(
        functools.partial(_inproj_body, len(x_args)),
        grid=(R // IN_TM,),
        in_specs=x_specs + [
                  pl.BlockSpec((1, 6, D), lambda i: (_mod_group(i * IN_TM), 0, 0)),
                  pl.BlockSpec((1, D), lambda i: (0, 0)),
                  pl.BlockSpec((1, D, W_IN_COLS), lambda i: (l, 0, 0), pipeline_mode=pl.Buffered(1))],
        out_specs=[pl.BlockSpec((IN_TM, N_BG), lambda i: (i, 0)),
                   pl.BlockSpec((IN_TM, N_PROJ), lambda i: (i, 0))],
        out_shape=[jax.ShapeDtypeStruct((R, N_BG), BF16), jax.ShapeDtypeStruct((R, N_PROJ), F32)],
        scratch_shapes=[pltpu.VMEM((D, NP), BF16)],
        compiler_params=_cp(1),
        name="inproj",
    )(*x_args, mods_l, g, w_bf)


PREP_TM = 256
PREP_NC = PREP_TM // DN_C
PREP_LOCKSTEP = 4


def _prep_body(x_ref, xp_ref, xn_ref, sm_ref, smt_ref, cw_ref, al_ref, dt_ref, alt_ref, dtt_ref,
               u_o, w_o, qg_o, kd_o, ai_o, bg_o, qkv_o, gr_o):
    i = pl.program_id(0)
    n_ctx = RC // PREP_TM
    per_seq = DEC_SEQ // PREP_TM
    is_ctx = i < n_ctx
    j = (i - n_ctx) % per_seq
    first = jnp.logical_or(is_ctx, j == 0)
    last = jnp.logical_or(is_ctx, j == per_seq - 1)

    x = x_ref[...]
    prev_row = jnp.where(first, 0.0, xp_ref[7:8, :])
    next_row = jnp.where(last, 0.0, xn_ref[0:1, :])
    row = lax.broadcasted_iota(jnp.int32, (PREP_TM, 1), 0)
    xm = jnp.where(row == 0, prev_row, pltpu.roll(x, 1, axis=0))
    xq = jnp.where(row == PREP_TM - 1, next_row, pltpu.roll(x, PREP_TM - 1, axis=0))
    w = cw_ref[...]
    y = _silu(w[0:1] * xm + w[1:2] * x + w[2:3] * xq)

    for h in range(DN_H):
        qh = y[:, h * DN_DK:(h + 1) * DN_DK]
        qn = qh * lax.rsqrt(jnp.sum(qh * qh, axis=-1, keepdims=True) + EPS) * (DN_DK ** -0.5)
        qkv_o[:, h * DN_DK:(h + 1) * DN_DK] = qn
        kh = y[:, 512 + h * DN_DK:512 + (h + 1) * DN_DK]
        kn = kh * lax.rsqrt(jnp.sum(kh * kh, axis=-1, keepdims=True) + EPS)
        qkv_o[:, 512 + h * DN_DK:512 + (h + 1) * DN_DK] = kn
    qkv_o[:, 1024:1536] = y[:, 1024:1536]

    r = lax.broadcasted_iota(jnp.int32, (PREP_TM, PREP_TM), 0)
    c = lax.broadcasted_iota(jnp.int32, (PREP_TM, PREP_TM), 1)
    same = (r // DN_C) == (c // DN_C)
    low = jnp.where(jnp.logical_and(same, r >= c), 1.0, 0.0).astype(BF16)
    upp = jnp.where(jnp.logical_and(same, r <= c), 1.0, 0.0).astype(BF16)

    sm = sm_ref[...]
    beta = _sigmoid(sm)
    g = -jnp.exp(al_ref[...]) * _softplus(sm + dt_ref[...])
    gc_f = _mask_dot_r(low, g)
    gc_b = _mask_dot_r(upp, g)
    lane = lax.broadcasted_iota(jnp.int32, (1, 128), 1)
    bg_o[...] = jnp.where(lane < 8, beta, jnp.where(lane < 12, gc_f, jnp.where(lane < 16, gc_b, 0.0)))

    gt = -jnp.exp(alt_ref[...]) * _softplus(smt_ref[...] + dtt_ref[...])
    gct_f = _mask_dot_l(gt, upp)
    gct_b = _mask_dot_l(gt, low)
    sub = lax.broadcasted_iota(jnp.int32, (16, 1), 0)
    gct = jnp.where(sub < 12, gct_f, gct_b)
    for k in range(PREP_NC):
        gr_o[k] = gct[8:16, k * DN_C:(k + 1) * DN_C]

    rr = lax.broadcasted_iota(jnp.int32, (DN_C, DN_C), 0)
    cc = lax.broadcasted_iota(jnp.int32, (DN_C, DN_C), 1)

    def chunk_group(gi, carry):
        prob = [(cj, d, h) for cj in range(PREP_LOCKSTEP) for d in range(2) for h in range(DN_H)]
        ci = [gi * PREP_LOCKSTEP + cj for cj in range(PREP_LOCKSTEP)]
        rows_c = [pl.ds(pl.multiple_of(c_ * DN_C, DN_C), DN_C) for c_ in ci]
        bgc = [bg_o[r_, :] for r_ in rows_c]
        grow_all = [gr_o[c_] for c_ in ci]
        rows = [rows_c[cj] for cj, d, h in prob]
        q = [qkv_o[rows_c[cj], h * DN_DK:(h + 1) * DN_DK] for cj, d, h in prob]
        k = [qkv_o[rows_c[cj], 512 + h * DN_DK:512 + (h + 1) * DN_DK] for cj, d, h in prob]
        v = [qkv_o[rows_c[cj], 1024 + h * DN_DV:1024 + (h + 1) * DN_DV] for cj, d, h in prob]
        beta = [bgc[cj][:, d * DN_H + h:d * DN_H + h + 1] for cj, d, h in prob]
        gcol = [bgc[cj][:, 8 + d * DN_H + h:9 + d * DN_H + h] for cj, d, h in prob]
        grow = [grow_all[cj][d * DN_H + h:d * DN_H + h + 1, :] for cj, d, h in prob]
        incl = [(rr >= cc) if d == 0 else (rr <= cc) for cj, d, h in prob]
        strict = [(rr > cc) if d == 0 else (rr < cc) for cj, d, h in prob]
        n = len(prob)
        decay = [jnp.exp(jnp.where(incl[i], gcol[i] - grow[i], -1e30)) for i in range(n)]
        kb = [k[i] * beta[i] for i in range(n)]
        kk = [_bdot_nt(kb[i], k[i]) for i in range(n)]
        qk = [_bdot_nt(q[i], k[i]) for i in range(n)]
        a = [jnp.where(strict[i], kk[i] * decay[i], 0.0) for i in range(n)]
        blk = lambda b: (rr // b) == (cc // b)
        eye = jnp.where(rr == cc, 1.0, 0.0)
        p = [jnp.where(blk(8), a[i], 0.0) for i in range(n)]
        t = [eye - p[i] for i in range(n)]
        for _ in range(2):
            p = [_bdot(p[i], p[i]) for i in range(n)]
            t = [t[i] + _bdot(t[i], p[i]) for i in range(n)]
        for b in (16, 32, 64):
            m = jnp.logical_and(blk(b), jnp.logical_not(blk(b // 2)))
            tl = [_bdot(t[i], jnp.where(m, a[i], 0.0)) for i in range(n)]
            t = [t[i] - _bdot(tl[i], t[i]) for i in range(n)]
        egc = [jnp.exp(gcol[i]) for i in range(n)]
        glast = [gcol[i][DN_C - 1:DN_C, :] if prob[i][1] == 0 else gcol[i][0:1, :] for i in range(n)]
        uw = [_bdot(t[i], jnp.concatenate([v[i] * beta[i], kb[i] * egc[i]], axis=1)) for i in range(n)]
        for i, (cj, d, h) in enumerate(prob):
            u_o[d, rows[i], h * DN_DV:(h + 1) * DN_DV] = uw[i][:, 0:DN_DV]
            w_o[d, rows[i], h * DN_DK:(h + 1) * DN_DK] = uw[i][:, DN_DV:DN_DV + DN_DK].astype(BF16)
            qg_o[d, rows[i], h * DN_DK:(h + 1) * DN_DK] = (q[i] * egc[i]).astype(BF16)
            kd_o[d, rows[i], h * DN_DK:(h + 1) * DN_DK] = (k[i] * jnp.exp(glast[i] - gcol[i])).astype(BF16)
            ai_o[d, rows[i], h * DN_C:(h + 1) * DN_C] = (qk[i] * decay[i]).astype(BF16)
        return carry

    lax.fori_loop(0, PREP_NC // PREP_LOCKSTEP, chunk_group, 0)


def _dn_prep(proj, sm_t, conv_w, al128, dt128, al_t, dt_t):
    nb8 = R // 8
    qb = C_QKV // 1536
    wide = lambda dt: jax.ShapeDtypeStruct((2, R, DN_H * DN_DK), dt)
    wide_spec = pl.BlockSpec((2, PREP_TM, DN_H * DN_DK), lambda i: (0, i, 0))
    return pl.pallas_call(
        _prep_body,
        grid=(R // PREP_TM,),
        in_specs=[pl.BlockSpec((PREP_TM, 1536), lambda i: (i, qb)),
                  pl.BlockSpec((8, 1536), lambda i: (jnp.maximum(i * (PREP_TM // 8) - 1, 0), qb)),
                  pl.BlockSpec((8, 1536), lambda i: (jnp.minimum((i + 1) * (PREP_TM // 8), nb8 - 1), qb)),
                  pl.BlockSpec((PREP_TM, 128), lambda i: (i, C_SM // 128)),
                  pl.BlockSpec((16, PREP_TM), lambda i: (0, i)),
                  pl.BlockSpec((3, 1536), lambda i: (0, 0)),
                  pl.BlockSpec((1, 128), lambda i: (0, 0)),
                  pl.BlockSpec((1, 128), lambda i: (0, 0)),
                  pl.BlockSpec((16, 1), lambda i: (0, 0)),
                  pl.BlockSpec((16, 1), lambda i: (0, 0))],
        out_specs=[wide_spec, wide_spec, wide_spec, wide_spec,
                   pl.BlockSpec((2, PREP_TM, DN_H * DN_C), lambda i: (0, i, 0)),
                   pl.BlockSpec((PREP_TM, 128), lambda i: (i, 0))],
        out_shape=[wide(F32), wide(BF16), wide(BF16), wide(BF16),
                   jax.ShapeDtypeStruct((2, R, DN_H * DN_C), BF16),
                   jax.ShapeDtypeStruct((R, 128), F32)],
        scratch_shapes=[pltpu.VMEM((PREP_TM, 1536), F32), pltpu.VMEM((PREP_NC, 8, DN_C), F32)],
        compiler_params=_cp(1),
        name="dn_prep",
    )(proj, proj, proj, proj, sm_t, conv_w, al128, dt128, al_t, dt_t)


SCAN_STEPS = RC // PREP_TM
SCAN_PER = DEC_SEQ // PREP_TM
assert RL // PREP_TM == SCAN_STEPS and SEQ == PREP_TM


def _scan_lat_bwd_block(i):
    return (i // SCAN_PER) * SCAN_PER + (SCAN_PER - 1 - i % SCAN_PER)


def _scan_body(*refs):
    streams = [refs[6 * k:6 * k + 6] for k in range(4)]
    s0_ref = refs[24]
    out_refs = refs[25:29]
    so_ref, s_scr = refs[29], refs[30]
    j = pl.program_id(0) % SCAN_PER

    for h in range(DN_H):
        s_scr[0, h] = jnp.zeros((DN_DK, DN_DV), F32)
        s_scr[1, h] = jnp.zeros((DN_DK, DN_DV), F32)

    @pl.when(j == 0)
    def _():
        for h in range(DN_H):
            s_scr[2, h] = s0_ref[0, 0, 0, h]
            s_scr[3, h] = s0_ref[0, 0, 1, h]

    prob = [(k, h) for k in range(4) for h in range(DN_H)]
    tn = (((0,), (0,)), ((), ()))

    def step(n, carry):
        rows = [pl.ds(pl.multiple_of((n if k % 2 == 0 else PREP_NC - 1 - n) * DN_C, DN_C), DN_C)
                for k in range(4)]
        bgc = [streams[k][5][rows[k], :] for k in range(4)]
        cols = [slice(h * DN_DK, (h + 1) * DN_DK) for k, h in prob]
        st = [s_scr[k, h] for k, h in prob]
        stb = [x.astype(BF16) for x in st]
        ws = [_dot(streams[k][1][0, rows[k], cols[i]], stb[i]) for i, (k, h) in enumerate(prob)]
        qs = [_dot(streams[k][2][0, rows[k], cols[i]], stb[i]) for i, (k, h) in enumerate(prob)]
        vb = [(streams[k][0][0, rows[k], cols[i]] - ws[i]).astype(BF16) for i, (k, h) in enumerate(prob)]
        av = [_dot(streams[k][4][0, rows[k], h * DN_C:(h + 1) * DN_C], vb[i]) for i, (k, h) in enumerate(prob)]
        kv = [lax.dot_general(streams[k][3][0, rows[k], cols[i]], vb[i], tn, preferred_element_type=F32)
              for i, (k, h) in enumerate(prob)]
        for i, (k, h) in enumerate(prob):
            d = k % 2
            gcol = bgc[k][:, 8 + d * DN_H + h:9 + d * DN_H + h]
            glast = gcol[DN_C - 1:DN_C, :] if d == 0 else gcol[0:1, :]
            s_scr[k, h] = st[i] * jnp.exp(glast) + kv[i]
            out_refs[k][rows[k], cols[i]] = qs[i] + av[i]
        return carry

    lax.fori_loop(0, PREP_NC, step, 0)

    for d in range(2):
        for h in range(DN_H):
            so_ref[0, d, h] = s_scr[d, h]


def _dn_scan(u, w, qg, kd, ai, bg, s0, layer):
    row_of = (lambda i: i, lambda i: i,
              lambda i: SCAN_STEPS + i, lambda i: SCAN_STEPS + _scan_lat_bwd_block(i))
    in_specs, args = [], []
    for k in range(4):
        d, rb = k % 2, row_of[k]
        wide = pl.BlockSpec((1, PREP_TM, DN_H * DN_DK), lambda i, d=d, rb=rb: (d, rb(i), 0))
        narrow = pl.BlockSpec((1, PREP_TM, DN_H * DN_C), lambda i, d=d, rb=rb: (d, rb(i), 0))
        in_specs += [wide, wide, wide, wide, narrow, pl.BlockSpec((PREP_TM, 128), lambda i, rb=rb: (rb(i), 0))]
        args += [u, w, qg, kd, ai, bg]
    state_spec = lambda m: pl.BlockSpec((1, 2, DN_H, DN_DK, DN_DV), m)
    in_specs.append(pl.BlockSpec((1, 1, 2, DN_H, DN_DK, DN_DV), lambda i: (i // SCAN_PER, layer, 0, 0, 0, 0)))
    o_spec = lambda m: pl.BlockSpec((PREP_TM, DN_H * DN_DV), m)
    o_shape = lambda n: jax.ShapeDtypeStruct((n, DN_H * DN_DV), F32)
    return pl.pallas_call(
        _scan_body,
        grid=(SCAN_STEPS,),
        in_specs=in_specs,
        out_specs=[o_spec(lambda i: (i, 0)), o_spec(lambda i: (i, 0)), o_spec(lambda i: (i, 0)),
                   o_spec(lambda i: (_scan_lat_bwd_block(i), 0)),
                   state_spec(lambda i: (i, 0, 0, 0, 0))],
        out_shape=[o_shape(RC), o_shape(RC), o_shape(RL), o_shape(RL),
                   jax.ShapeDtypeStruct((BATCH, 2, DN_H, DN_DK, DN_DV), F32)],
        scratch_shapes=[pltpu.VMEM((4, DN_H, DN_DK, DN_DV), F32)],
        compiler_params=_cp(1),
        name="dn_scan",
    )(*args, s0)


def _group_mean_matrix(width, group):
    r = lax.broadcasted_iota(jnp.int32, (width, width), 0)
    c = lax.broadcasted_iota(jnp.int32, (width, width), 1)
    return jnp.where((r // group) == (c // group), 1.0 / group, 0.0).astype(BF16)


def _group_rmsnorm(x, w, group):
    m = _group_mean_matrix(x.shape[-1], group)
    hi, lo = _split2(x * x)
    ms = _dot(hi, m) + _dot(lo, m)
    return x * lax.rsqrt(ms + EPS) * w


def _rope(x, cos, sin_signed, group):
    width = x.shape[-1]
    half = group // 2
    lane = lax.broadcasted_iota(jnp.int32, (1, width), 1)
    swapped = jnp.where((lane % group) < half,
                        pltpu.roll(x, width - half, axis=1), pltpu.roll(x, half, axis=1))
    return x * cos + swapped * sin_signed


def _attention_units(n, scores, values):
    outs = []
    s_next = scores(0)
    for u in range(n):
        s = s_next
        if u + 1 < n:
            s_next = scores(u + 1)
        m = jnp.max(s, axis=-1, keepdims=True)
        p = jnp.exp(s - m)
        l = jnp.sum(p, axis=-1, keepdims=True)
        outs.append(_dot(p.astype(BF16), values(u)) / l)
    return outs


GQA_G = GQA_H // GQA_KV
CTX_NS = 2


def _gqa_body(T, TQ, NS, latent, *refs):
    it = iter(refs)
    q_ref, kv_ref, qw_ref, kw_ref = next(it), next(it), next(it), next(it)
    if latent:
        cq_ref, sq_ref, ck_ref, sk_ref, kc_ref, vc_ref = (next(it) for _ in range(6))
    o_ref = next(it)
    kn_ref = None if latent else next(it)
    k_scr, v_scr = next(it), next(it)

    @pl.when(pl.program_id(1) == 0)
    def _():
        kv = kv_ref[...]
        k = _group_rmsnorm(kv[:, 0:128], kw_ref[...], GQA_HD)
        v = kv[:, 128:256]
        if latent:
            k = _rope(k, ck_ref[...], sk_ref[...], GQA_HD)
        else:
            kn_ref[...] = k
        for s in range(NS):
            for g in range(GQA_KV):
                sg = s * GQA_KV + g
                k_scr[sg, 0:T, :] = k[s * T:(s + 1) * T, g * GQA_HD:(g + 1) * GQA_HD].astype(BF16)
                v_scr[sg, 0:T, :] = v[s * T:(s + 1) * T, g * GQA_HD:(g + 1) * GQA_HD].astype(BF16)
                if latent:
                    k_scr[sg, T:T + PAST, :] = kc_ref[0, 0, :, g * GQA_HD:(g + 1) * GQA_HD].astype(BF16)
                    v_scr[sg, T:T + PAST, :] = vc_ref[0, 0, :, g * GQA_HD:(g + 1) * GQA_HD].astype(BF16)

    q = _group_rmsnorm(q_ref[...], qw_ref[...], GQA_HD)
    if latent:
        q = _rope(q, cq_ref[...], sq_ref[...], GQA_HD)
    q = q * (GQA_HD ** -0.5)
    head = lambda s, hh: q[s * TQ:(s + 1) * TQ, hh * GQA_HD:(hh + 1) * GQA_HD]
    units = [(s, u) for s in range(NS) for u in range(GQA_H // 2)]
    qu = [jnp.concatenate([head(s, 2 * u), head(s, 2 * u + 1)], axis=0) for s, u in units]
    kv_of = lambda i: units[i][0] * GQA_KV + (2 * units[i][1]) // GQA_G
    outs = _attention_units(len(units), lambda i: _bdot_nt(qu[i], k_scr[kv_of(i)]), lambda i: v_scr[kv_of(i)])
    for (s, u), o in zip(units, outs):
        for j in range(2):
            hh = 2 * u + j
            o_ref[s * TQ:(s + 1) * TQ, hh * GQA_HD:(hh + 1) * GQA_HD] = o[j * TQ:(j + 1) * TQ, :].astype(o_ref.dtype)


def _gqa(proj, qw, kw, T, TQ, n_seq, row_off, rope=None, cache=None, NS=1, layer=0):
    latent = rope is not None
    tk = T + (PAST if latent else 0)
    nq = T // TQ
    assert NS == 1 or (nq == 1 and not latent)
    n_seq, TQ, TB = n_seq // NS, NS * TQ, NS * T
    qo = row_off // TQ
    so = row_off // TB
    in_specs = [pl.BlockSpec((TQ, 512), lambda i, j: (qo + i * nq + j, C_GQ // 512)),
                pl.BlockSpec((TB, 256), lambda i, j: (so + i, C_GKV // 256)),
                pl.BlockSpec((1, 512), lambda i, j: (0, 0)),
                pl.BlockSpec((1, 128), lambda i, j: (0, 0))]
    args = [proj, proj, qw, kw]
    if latent:
        cq, sq, ck, sk = rope
        kc, vc = cache
        in_specs += [pl.BlockSpec((TQ, 512), lambda i, j: (j, 0)),
                     pl.BlockSpec((TQ, 512), lambda i, j: (j, 0)),
                     pl.BlockSpec((T, 128), lambda i, j: (0, 0)),
                     pl.BlockSpec((T, 128), lambda i, j: (0, 0)),
                     pl.BlockSpec((1, 1, PAST, 128), lambda i, j: (i, layer, 0, 0)),
                     pl.BlockSpec((1, 1, PAST, 128), lambda i, j: (i, layer, 0, 0))]
        args += [cq, sq, ck, sk, kc, vc]
    out_specs = [pl.BlockSpec((TQ, 512), lambda i, j: (i * nq + j, 0))]
    out_shape = [jax.ShapeDtypeStruct((n_seq * TB, 512), BF16)]
    if not latent:
        out_specs.append(pl.BlockSpec((TB, 128), lambda i, j: (i, 0)))
        out_shape.append(jax.ShapeDtypeStruct((n_seq * TB, 128), F32))
    return pl.pallas_call(
        functools.partial(_gqa_body, T, TQ // NS, NS, latent),
        grid=(n_seq, nq),
        in_specs=in_specs,
        out_specs=out_specs,
        out_shape=out_shape,
        scratch_shapes=[pltpu.VMEM((NS * GQA_KV, tk, GQA_HD), BF16), pltpu.VMEM((NS * GQA_KV, tk, GQA_HD), BF16)],
        compiler_params=_cp(2),
        name="gqa_%d" % T,
    )(*args)


MLA_DQ = MLA_NOPE + MLA_ROPE
MLA_KV_ROWS = 512


def _mla_body(T, TQ, NS, latent, *refs):
    it = iter(refs)
    cq_ref, ckv_ref, sm_ref, wq_ref, wkv_ref, qw_ref, kvw_ref = (next(it) for _ in range(7))
    if latent:
        cosq_ref, sinq_ref, cosk_ref, sink_ref, cc_ref, kc_ref = (next(it) for _ in range(6))
    o_ref = next(it)
    cn_ref = None if latent else next(it)
    k_scr, v_scr = next(it), next(it)
    tk = T + (PAST if latent else 0)

    @pl.when(pl.program_id(1) == 0)
    def _():
        x = ckv_ref[...]
        ms = jnp.mean(x * x, axis=-1, keepdims=True)
        ckv = x * lax.rsqrt(ms + EPS) * kvw_ref[...]
        sm = sm_ref[...]
        if latent:
            sm = _rope_small(sm, cosk_ref[...], sink_ref[...])
        else:
            cn_ref[...] = ckv
        kr = sm[:, 16:16 + MLA_ROPE]

        def put(s, rows0, ckv_rows, kr_rows):
            n = ckv_rows.shape[0]
            kv = _bdot(ckv_rows, wkv_ref[...])
            krb = kr_rows.astype(BF16)
            for h in range(MLA_H):
                sh = s * MLA_H + h
                k_scr[sh, rows0:rows0 + n, 0:MLA_NOPE] = kv[:, h * MLA_NOPE:(h + 1) * MLA_NOPE].astype(BF16)
                k_scr[sh, rows0:rows0 + n, MLA_NOPE:MLA_DQ] = krb
                v_scr[sh, rows0:rows0 + n, :] = kv[:, 512 + h * MLA_V:512 + (h + 1) * MLA_V].astype(BF16)

        for s in range(NS):
            for r0 in range(0, T, MLA_KV_ROWS):
                r1 = min(r0 + MLA_KV_ROWS, T)
                put(s, r0, ckv[s * T + r0:s * T + r1], kr[s * T + r0:s * T + r1])
        if latent:
            put(0, T, cc_ref[0, 0], kc_ref[0, 0])

    x = cq_ref[...]
    ms = jnp.mean(x * x, axis=-1, keepdims=True)
    cq = x * lax.rsqrt(ms + EPS) * qw_ref[...]
    qf = _bdot(cq, wq_ref[...])
    qn = qf[:, 0:512]
    qr = qf[:, 512:768]
    if latent:
        qr = _rope(qr, cosq_ref[...], sinq_ref[...], MLA_ROPE)
    scale = MLA_DQ ** -0.5
    units = [(s, h) for s in range(NS) for h in range(MLA_H)]
    rows = lambda s: slice(s * TQ, (s + 1) * TQ)
    qh = [jnp.concatenate([qn[rows(s), h * MLA_NOPE:(h + 1) * MLA_NOPE],
                           qr[rows(s), h * MLA_ROPE:(h + 1) * MLA_ROPE]], axis=1) * scale for s, h in units]
    sh = lambda i: units[i][0] * MLA_H + units[i][1]
    outs = _attention_units(len(units), lambda i: _bdot_nt(qh[i], k_scr[sh(i)]), lambda i: v_scr[sh(i)])
    for (s, h), o in zip(units, outs):
        o_ref[rows(s), h * MLA_V:(h + 1) * MLA_V] = o.astype(o_ref.dtype)


def _rope_small(sm, cos, sin_signed):
    lane = lax.broadcasted_iota(jnp.int32, (1, 128), 1)
    half = MLA_ROPE // 2
    swapped = jnp.where(lane < 16 + half, pltpu.roll(sm, 128 - half, axis=1), pltpu.roll(sm, half, axis=1))
    return sm * cos + swapped * sin_signed


def _mla(proj, wq, wkv, qw, kvw, T, TQ, n_seq, row_off, rope=None, cache=None, NS=1, layer=0):
    latent = rope is not None
    tk = T + (PAST if latent else 0)
    nq = T // TQ
    assert NS == 1 or (nq == 1 and not latent)
    n_seq, TQ, TB = n_seq // NS, NS * TQ, NS * T
    qo = row_off // TQ
    so = row_off // TB
    in_specs = [pl.BlockSpec((TQ, 256), lambda i, j: (qo + i * nq + j, C_MCQ // 256)),
                pl.BlockSpec((TB, 128), lambda i, j: (so + i, C_MCKV // 128)),
                pl.BlockSpec((TB, 128), lambda i, j: (so + i, C_SM // 128)),
                pl.BlockSpec((MLA_QL, 768), lambda i, j: (0, 0)),
                pl.BlockSpec((MLA_KVL, 1024), lambda i, j: (0, 0)),
                pl.BlockSpec((1, 256), lambda i, j: (0, 0)),
                pl.BlockSpec((1, 128), lambda i, j: (0, 0))]
    args = [proj, proj, proj, wq, wkv, qw, kvw]
    if latent:
        cosq, sinq, cosk, sink = rope
        cc, kc = cache
        in_specs += [pl.BlockSpec((TQ, 256), lambda i, j: (j, 0)),
                     pl.BlockSpec((TQ, 256), lambda i, j: (j, 0)),
                     pl.BlockSpec((T, 128), lambda i, j: (0, 0)),
                     pl.BlockSpec((T, 128), lambda i, j: (0, 0)),
                     pl.BlockSpec((1, 1, PAST, 128), lambda i, j: (i, layer, 0, 0)),
                     pl.BlockSpec((1, 1, PAST, MLA_ROPE), lambda i, j: (i, layer, 0, 0))]
        args += [cosq, sinq, cosk, sink, cc, kc]
    out_specs = [pl.BlockSpec((TQ, 512), lambda i, j: (i * nq + j, 0))]
    out_shape = [jax.ShapeDtypeStruct((n_seq * TB, 512), BF16)]
    if not latent:
        out_specs.append(pl.BlockSpec((TB, 128), lambda i, j: (i, 0)))
        out_shape.append(jax.ShapeDtypeStruct((n_seq * TB, 128), F32))
    return pl.pallas_call(
        functools.partial(_mla_body, T, TQ // NS, NS, latent),
        grid=(n_seq, nq),
        in_specs=in_specs,
        out_specs=out_specs,
        out_shape=out_shape,
        scratch_shapes=[pltpu.VMEM((NS * MLA_H, tk, MLA_DQ), BF16), pltpu.VMEM((NS * MLA_H, tk, MLA_V), BF16)],
        compiler_params=_cp(2),
        name="mla_%d" % T,
    )(*args)


MG_TM = 512


def _merge_body(n_x, *refs):
    x_refs = refs[:n_x]
    (m_ref, ofc_ref, obc_ref, ofl_ref, obl_ref, dg_ref, ogc_ref, ogl_ref, omc_ref, oml_ref,
     bg_ref, ng_ref, wb_ref, wo_ref, o_ref) = refs[n_x:]
    is_ctx = pl.program_id(0) < RC // MG_TM
    og = jnp.where(is_ctx, ogc_ref[...], ogl_ref[...])
    om = jnp.where(is_ctx, omc_ref[...], oml_ref[...])
    odn = jnp.where(is_ctx, ofc_ref[...] + obc_ref[...], ofl_ref[...] + obl_ref[...])
    dg = dg_ref[...]
    ng = ng_ref[...]
    parts = []
    for h in range(DN_H):
        oh = odn[:, h * DN_DV:(h + 1) * DN_DV]
        ms = jnp.mean(oh * oh, axis=-1, keepdims=True)
        parts.append(oh * lax.rsqrt(ms + EPS) * ng * _silu(dg[:, h * DN_DV:(h + 1) * DN_DV]))
    br0 = jnp.concatenate(parts, axis=1)
    gate = lambda n: _sigmoid(bg_ref[:, n * D:(n + 1) * D].astype(F32))
    merged = gate(0) * _bdot(br0, wb_ref[0, 0])
    merged = merged + gate(1) * _bdot(og, wb_ref[0, 1])
    merged = merged + gate(2) * _bdot(om, wb_ref[0, 2])
    out = _bdot(merged, wo_ref[0])
    o_ref[...] = _row_tile(x_refs, MG_TM) + m_ref[0, 2:3, :] * out


def _merge(x, mods_l, dn_o, proj, pbg, og_c, og_l, om_c, om_l, ng, wb, wo, l):
    row = lambda i: (i, 0)
    n_ctx = RC // MG_TM
    ctx_row = lambda i: (jnp.minimum(i, n_ctx - 1), 0)
    lat_row = lambda i: (jnp.maximum(i - n_ctx, 0), 0)
    x_specs, x_args = _row_specs(x, MG_TM)
    return pl.pallas_call(
        functools.partial(_merge_body, len(x_args)),
        grid=(R // MG_TM,),
        in_specs=x_specs + [
                  pl.BlockSpec((1, 6, D), lambda i: (_mod_group(i * MG_TM), 0, 0)),
                  pl.BlockSpec((MG_TM, 512), ctx_row),
                  pl.BlockSpec((MG_TM, 512), ctx_row),
                  pl.BlockSpec((MG_TM, 512), lat_row),
                  pl.BlockSpec((MG_TM, 512), lat_row),
                  pl.BlockSpec((MG_TM, 512), lambda i: (i, C_DG // 512)),
                  pl.BlockSpec((MG_TM, 512), ctx_row),
                  pl.BlockSpec((MG_TM, 512), lat_row),
                  pl.BlockSpec((MG_TM, 512), ctx_row),
                  pl.BlockSpec((MG_TM, 512), lat_row),
                  pl.BlockSpec((MG_TM, N_BG), row),
                  pl.BlockSpec((1, DN_DV), lambda i: (0, 0)),
                  pl.BlockSpec((1, 3, 512, D), lambda i: (l, 0, 0, 0)),
                  pl.BlockSpec((1, D, D), lambda i: (l, 0, 0))],
        out_specs=pl.BlockSpec((MG_TM, D), row),
        out_shape=jax.ShapeDtypeStruct((R, D), F32),
        compiler_params=_cp(1),
        name="merge",
    )(*x_args, mods_l, *dn_o, proj, og_c, og_l, om_c, om_l, pbg, ng, wb, wo)


FF_TM = 512
FF_CHUNKS = ((0, 1536), (1536, FF_DENSE))


def _ffn_body(x_ref, m_ref, g_ref, w1_ref, w3_ref, w2_ref, o_ref):
    x = x_ref[...]
    h = _modnorm(x, g_ref[...], m_ref[0, 3:4, :], m_ref[0, 4:5, :]).astype(BF16)
    y = None
    for c0, c1 in FF_CHUNKS:
        a = _silu(_dot(h, w1_ref[0, :, c0:c1])) * _dot(h, w3_ref[0, :, c0:c1])
        yc = _dot(a.astype(BF16), w2_ref[0, c0:c1, :])
        y = yc if y is None else y + yc
    o_ref[...] = x + m_ref[0, 5:6, :] * y


def _ffn(x, mods_l, g, w1, w3, w2, j):
    once = pl.Buffered(1)
    return pl.pallas_call(
        _ffn_body,
        grid=(R // FF_TM,),
        in_specs=[pl.BlockSpec((FF_TM, D), lambda i: (i, 0)),
                  pl.BlockSpec((1, 6, D), lambda i: (_mod_group(i * FF_TM), 0, 0)),
                  pl.BlockSpec((1, D), lambda i: (0, 0)),
                  pl.BlockSpec((1, D, FF_DENSE), lambda i: (j, 0, 0), pipeline_mode=once),
                  pl.BlockSpec((1, D, FF_DENSE), lambda i: (j, 0, 0), pipeline_mode=once),
                  pl.BlockSpec((1, FF_DENSE, D), lambda i: (j, 0, 0), pipeline_mode=once)],
        out_specs=pl.BlockSpec((FF_TM, D), lambda i: (i, 0)),
        out_shape=jax.ShapeDtypeStruct((R, D), F32),
        compiler_params=_cp(1),
        name="ffn",
    )(x, mods_l, g, w1, w3, w2)


RT_TM = 512
MOE_TG, MOE_TF = 512, 1792
MOE_NT = 2 * R // MOE_TG + N_EXP
MOE_ROWS = MOE_NT * MOE_TG
CB_TM = 256


def _router_body(x_ref, m_ref, g_ref, rw_ref, rb_ref, hn_ref, route_ref, cnt_ref, base_scr):
    @pl.when(pl.program_id(0) == 0)
    def _():
        base_scr[...] = jnp.zeros_like(base_scr)

    h = _modnorm(x_ref[...], g_ref[...], m_ref[0, 3:4, :], m_ref[0, 4:5, :])
    hn_ref[...] = h
    logits = _dot3(h, rw_ref[...]) + rb_ref[...]
    lane = lax.broadcasted_iota(jnp.int32, logits.shape, 1)
    m1 = jnp.max(logits, axis=-1, keepdims=True)
    i1 = jnp.min(jnp.where(logits == m1, lane, 128), axis=-1, keepdims=True)
    sel1 = lane == i1
    rest = jnp.where(sel1, -jnp.inf, logits)
    m2 = jnp.max(rest, axis=-1, keepdims=True)
    i2 = jnp.min(jnp.where(rest == m2, lane, 128), axis=-1, keepdims=True)
    sel2 = lane == i2
    e2 = jnp.exp(m2 - m1)
    p1 = 1.0 / (1.0 + e2)
    p2 = e2 / (1.0 + e2)

    cnt = jnp.where(jnp.logical_or(sel1, sel2), 1.0, 0.0)
    r = lax.broadcasted_iota(jnp.int32, (RT_TM, RT_TM), 0)
    c = lax.broadcasted_iota(jnp.int32, (RT_TM, RT_TM), 1)
    before = jnp.where(r > c, 1.0, 0.0).astype(BF16)
    seen = base_scr[...] + _dot(before, cnt.astype(BF16))
    rank1 = jnp.sum(jnp.where(sel1, seen, 0.0), axis=-1, keepdims=True)
    rank2 = jnp.sum(jnp.where(sel2, seen, 0.0), axis=-1, keepdims=True)
    vals = (i1.astype(F32), i2.astype(F32), rank1, rank2, p1, p2)
    route = jnp.zeros(logits.shape, F32)
    for k, val in enumerate(vals):
        route = jnp.where(lane == k, val, route)
    route_ref[...] = route
    base_scr[...] += jnp.sum(cnt, axis=0, keepdims=True)
    cnt_ref[...] = base_scr[...]


def _router(x, mods_l, g, rw128, rb128):
    return pl.pallas_call(
        _router_body,
        grid=(R // RT_TM,),
        in_specs=[pl.BlockSpec((RT_TM, D), lambda i: (i, 0)),
                  pl.BlockSpec((1, 6, D), lambda i: (_mod_group(i * RT_TM), 0, 0)),
                  pl.BlockSpec((1, D), lambda i: (0, 0)),
                  pl.BlockSpec((D, 128), lambda i: (0, 0)),
                  pl.BlockSpec((1, 128), lambda i: (0, 0))],
        out_specs=[pl.BlockSpec((RT_TM, D), lambda i: (i, 0)),
                   pl.BlockSpec((RT_TM, 128), lambda i: (i, 0)),
                   pl.BlockSpec((1, 128), lambda i: (0, 0))],
        out_shape=[jax.ShapeDtypeStruct((R, D), F32),
                   jax.ShapeDtypeStruct((R, 128), F32),
                   jax.ShapeDtypeStruct((1, 128), F32)],
        scratch_shapes=[pltpu.VMEM((1, 128), F32)],
        compiler_params=_cp(1),
        name="moe_router",
    )(x, mods_l, g, rw128, rb128)


def _rows_copy(src_hbm, dst, sem, n):
    return pltpu.make_async_copy(src_hbm.at[pl.ds(0, n)], dst, sem)


def _start_row_gather(idx_ref, n, src_hbm, dst, sem, inline=False, both_queues=False):
    def issue(i, priority):
        pltpu.make_async_copy(src_hbm.at[pl.ds(idx_ref[0, 0, i], 1)], dst.at[pl.ds(i, 1)], sem).start(
            priority=priority)

    if inline:
        for i in range(n):
            issue(i, i % 2 if both_queues else 0)
    else:
        def body(i, carry):
            issue(i, 0)
            return carry

        lax.fori_loop(0, n, body, 0, unroll=8)


def _stage_to_bf16(src_hbm, dst, stage, sems):
    ch = stage.shape[1]
    n = src_hbm.shape[0] // ch
    copy = lambda c: pltpu.make_async_copy(src_hbm.at[pl.ds(c * ch, ch)], stage.at[c % 2], sems.at[c % 2])
    copy(0).start()
    for c in range(n):
        if c + 1 < n:
            copy(c + 1).start()
        copy(c).wait()
        dst[c * ch:(c + 1) * ch, :] = stage[c % 2].astype(BF16)


def _experts_body(j, te_ref, tv_ref, src_ref, nsrc_ref, hn_hbm, w1_hbm, w3_hbm, w2_hbm, ys_ref,
                  xg_scr, w1_ref, w3_ref, w2_ref, st13, st2, sems, wsems):
    t = pl.program_id(0)
    slot = t % 2
    valid = tv_ref[t] > 0
    e = te_ref[t]
    new_expert = jnp.logical_or(t == 0, e != te_ref[jnp.maximum(t - 1, 0)])

    @pl.when(jnp.logical_and(valid, new_expert))
    def _():
        _stage_to_bf16(w1_hbm.at[j, e], w1_ref, st13, wsems)
        _stage_to_bf16(w3_hbm.at[j, e], w3_ref, st13, wsems)
        _stage_to_bf16(w2_hbm.at[j, e], w2_ref, st2, wsems)

    requested = jnp.where(t == 0, valid, tv_ref[jnp.maximum(t - 1, 0)] > 0)

    @pl.when(jnp.logical_and(t == 0, valid))
    def _():
        _start_row_gather(src_ref, MOE_TG, hn_hbm, xg_scr.at[0], sems.at[0])

    @pl.when(requested)
    def _():
        _rows_copy(hn_hbm, xg_scr.at[slot], sems.at[slot], MOE_TG).wait()

    @pl.when(valid)
    def _():
        xb = xg_scr[slot].astype(BF16)
        _start_row_gather(nsrc_ref, MOE_TG, hn_hbm, xg_scr.at[1 - slot], sems.at[1 - slot], inline=True)
        y = None
        for c in range(FF_EXP // MOE_TF):
            cs = slice(c * MOE_TF, (c + 1) * MOE_TF)
            a = _silu(_dot(xb, w1_ref[:, cs])) * _dot(xb, w3_ref[:, cs])
            yc = _dot(a.astype(BF16), w2_ref[cs, :])
            y = yc if y is None else y + yc
        ys_ref[...] = y

    @pl.when(jnp.logical_not(valid))
    def _():
        ys_ref[...] = jnp.zeros_like(ys_ref)

    @pl.when(jnp.logical_and(t == MOE_NT - 1, valid))
    def _():
        _rows_copy(hn_hbm, xg_scr.at[1 - slot], sems.at[1 - slot], MOE_TG).wait()


MOE_WCH13, MOE_WCH2 = 128, 448


def _experts(tile_e, tile_v, src, hn, w1, w3, w2, j):
    hbm = pl.BlockSpec(memory_space=pl.ANY)
    grid_spec = pltpu.PrefetchScalarGridSpec(
        num_scalar_prefetch=2,
        grid=(MOE_NT,),
        in_specs=[pl.BlockSpec((1, 1, MOE_TG), lambda t, te, tv: (t, 0, 0), memory_space=pltpu.SMEM),
                  pl.BlockSpec((1, 1, MOE_TG), lambda t, te, tv: (t + 1, 0, 0), memory_space=pltpu.SMEM),
                  hbm, hbm, hbm, hbm],
        out_specs=pl.BlockSpec((MOE_TG, D), lambda t, te, tv: (t, 0)),
        scratch_shapes=[pltpu.VMEM((2, MOE_TG, D), F32),
                        pltpu.VMEM((D, FF_EXP), BF16), pltpu.VMEM((D, FF_EXP), BF16),
                        pltpu.VMEM((FF_EXP, D), BF16),
                        pltpu.VMEM((2, MOE_WCH13, FF_EXP), F32), pltpu.VMEM((2, MOE_WCH2, D), F32),
                        pltpu.SemaphoreType.DMA((2,)), pltpu.SemaphoreType.DMA((2,))],
    )
    return pl.pallas_call(
        functools.partial(_experts_body, j),
        grid_spec=grid_spec,
        out_shape=jax.ShapeDtypeStruct((MOE_ROWS, D), F32),
        compiler_params=_cp(1),
        name="moe_experts",
    )(tile_e, tile_v, src, src, hn, w1, w3, w2)


def _combine_body(final, pos_ref, npos_ref, x_ref, m_ref, route_ref, ys_hbm, *rest):
    if final:
        fg_ref, oc_ref, ol_ref, buf, sems = rest
    else:
        o_ref, buf, sems = rest
    i = pl.program_id(0)
    slot = i % 2
    n_rows = 2 * CB_TM

    @pl.when(i == 0)
    def _():
        _start_row_gather(pos_ref, n_rows, ys_hbm, buf.at[0], sems.at[0])

    @pl.when(i + 1 < pl.num_programs(0))
    def _():
        _start_row_gather(npos_ref, n_rows, ys_hbm, buf.at[1 - slot], sems.at[1 - slot], inline=True,
                          both_queues=True)

    _rows_copy(ys_hbm, buf.at[slot], sems.at[slot], n_rows).wait()
    route = route_ref[...]
    y = route[:, 4:5] * buf[slot, 0:CB_TM, :] + route[:, 5:6] * buf[slot, CB_TM:n_rows, :]
    out = x_ref[...] + m_ref[0, 5:6, :] * y
    if not final:
        o_ref[...] = out
        return
    ms = jnp.mean(out * out, axis=-1, keepdims=True)
    out = out * lax.rsqrt(ms + EPS) * fg_ref[...]
    is_ctx = i < RC // CB_TM

    @pl.when(is_ctx)
    def _():
        oc_ref[...] = out

    @pl.when(jnp.logical_not(is_ctx))
    def _():
        ol_ref[...] = out


def _combine(pos, x, mods_l, route, ys, final_g=None):
    n_tiles = R // CB_TM
    n_ctx = RC // CB_TM
    final = final_g is not None
    in_specs = [pl.BlockSpec((1, 1, 2 * CB_TM), lambda i: (i, 0, 0), memory_space=pltpu.SMEM),
                pl.BlockSpec((1, 1, 2 * CB_TM), lambda i: (jnp.minimum(i + 1, n_tiles - 1), 0, 0),
                             memory_space=pltpu.SMEM),
                pl.BlockSpec((CB_TM, D), lambda i: (i, 0)),
                pl.BlockSpec((1, 6, D), lambda i: (_mod_group(i * CB_TM), 0, 0)),
                pl.BlockSpec((CB_TM, 128), lambda i: (i, 0)),
                pl.BlockSpec(memory_space=pl.ANY)]
    args = [pos, pos, x, mods_l, route, ys]
    if final:
        in_specs.append(pl.BlockSpec((1, D), lambda i: (0, 0)))
        args.append(final_g)
        out_specs = [pl.BlockSpec((CB_TM, D), lambda i: (jnp.minimum(i, n_ctx - 1), 0)),
                     pl.BlockSpec((CB_TM, D), lambda i: (jnp.maximum(i - n_ctx, 0), 0))]
        out_shape = [jax.ShapeDtypeStruct((RC, D), F32), jax.ShapeDtypeStruct((RL, D), F32)]
    else:
        out_specs = pl.BlockSpec((CB_TM, D), lambda i: (i, 0))
        out_shape = jax.ShapeDtypeStruct((R, D), F32)
    return pl.pallas_call(
        functools.partial(_combine_body, final),
        grid=(n_tiles,),
        in_specs=in_specs,
        out_specs=out_specs,
        out_shape=out_shape,
        scratch_shapes=[pltpu.VMEM((2, 2 * CB_TM, D), F32), pltpu.SemaphoreType.DMA((2,))],
        compiler_params=_cp(1),
        name="moe_combine",
    )(*args)


def _moe(x, mods_l, g, rw128, rb128, w1, w3, w2, j, final_g=None):
    hn, route, cnt = _router(x, mods_l, g, rw128, rb128)
    eid = route[:, 0:2].astype(jnp.int32)
    rank = route[:, 2:4].astype(jnp.int32)
    counts = cnt[0, :N_EXP].astype(jnp.int32)
    gsize = (counts + MOE_TG - 1) // MOE_TG * MOE_TG
    gend = jnp.cumsum(gsize)
    pos = (gend - gsize)[eid] + rank
    tile_start = jnp.arange(MOE_NT, dtype=jnp.int32) * MOE_TG
    tile_e = jnp.minimum(jnp.sum(tile_start[:, None] >= gend[None, :], axis=1), N_EXP - 1).astype(jnp.int32)
    tile_v = (tile_start < gend[-1]).astype(jnp.int32)
    tok = jnp.broadcast_to(jnp.arange(R, dtype=jnp.int32)[:, None], (R, 2))
    src = jnp.zeros((MOE_ROWS + MOE_TG,), jnp.int32).at[pos.reshape(-1)].set(
        tok.reshape(-1), unique_indices=True, mode="promise_in_bounds")
    ys = _experts(tile_e, tile_v, src.reshape(MOE_NT + 1, 1, MOE_TG), hn, w1, w3, w2, j)
    pos_t = pos.reshape(R // CB_TM, CB_TM, 2).transpose(0, 2, 1).reshape(R // CB_TM, 1, 2 * CB_TM)
    return _combine(pos_t, x, mods_l, route, ys, final_g)


FN_TM = 1024


def _final_body(x_ref, g_ref, o_ref):
    x = x_ref[...]
    ms = jnp.mean(x * x, axis=-1, keepdims=True)
    o_ref[...] = x * lax.rsqrt(ms + EPS) * g_ref[...]


def _final_norm(x, g, row_off, n_rows):
    bo = row_off // FN_TM
    return pl.pallas_call(
        _final_body,
        grid=(n_rows // FN_TM,),
        in_specs=[pl.BlockSpec((FN_TM, D), lambda i: (bo + i, 0)), pl.BlockSpec((1, D), lambda i: (0, 0))],
        out_specs=pl.BlockSpec((FN_TM, D), lambda i: (i, 0)),
        out_shape=jax.ShapeDtypeStruct((n_rows, D), F32),
        compiler_params=_cp(1),
        name="final_norm",
    )(x, g)


def _rope_tables(n_tokens, rot_dim):
    t = np.arange(n_tokens)
    row = (t // GRID_W).astype(np.float32)
    col = (t % GRID_W).astype(np.float32)
    n_freq = rot_dim // 4
    inv = (ROPE_THETA ** (-jnp.arange(n_freq, dtype=F32) / n_freq))
    ang = jnp.concatenate([jnp.asarray(row)[:, None] * inv, jnp.asarray(col)[:, None] * inv], axis=-1)
    cos, sin = jnp.cos(ang), jnp.sin(ang)
    return jnp.concatenate([cos, cos], axis=-1), jnp.concatenate([-sin, sin], axis=-1)


def kernel(x_prompt, x_sample, c, cache_gqa_k, cache_gqa_v, cache_mla_ckv, cache_mla_krope, state_delta, c_ctx, w_mod, b_mod, norm1_g, norm2_g, w_in, dn_conv_w, dn_a_log, dn_dt_bias, dn_norm_g, gqa_q_norm, gqa_k_norm, mla_q_norm, mla_kv_norm, mla_w_uq, mla_w_ukv, w_branch, w_out, ffd_w1, ffd_w3, ffd_w2, router_w, router_b, moe_w1, moe_w3, moe_w2, final_g):
    x = (x_prompt.reshape(RC, D), x_sample.reshape(RL, D))
    cond8 = jnp.concatenate([c_ctx[None, :], c, jnp.zeros((3, D), F32)], axis=0)
    mods = _mods(cond8, w_mod, b_mod).reshape(DEPTH, 8, 6, D)

    cg, sg = _rope_tables(DEC_SEQ, GQA_HD)
    gqa_rope = (jnp.tile(cg, (1, GQA_H)), jnp.tile(sg, (1, GQA_H)),
                jnp.tile(cg, (1, GQA_KV)), jnp.tile(sg, (1, GQA_KV)))
    cm, sm_ = _rope_tables(DEC_SEQ, MLA_ROPE)
    padk = lambda t: jnp.pad(t, ((0, 0), (16, 128 - 16 - MLA_ROPE)))
    mla_rope = (jnp.tile(cm, (1, MLA_H)), jnp.tile(sm_, (1, MLA_H)),
                jnp.pad(cm, ((0, 0), (16, 128 - 16 - MLA_ROPE)), constant_values=1.0), padk(sm_))

    w_in_bf, w_branch_bf, w_out_bf = w_in.astype(BF16), w_branch.astype(BF16), w_out.astype(BF16)
    ffd_bf = (ffd_w1.astype(BF16), ffd_w3.astype(BF16), ffd_w2.astype(BF16))
    gqa_cache = (cache_gqa_k.reshape(DEC_BATCH, DEPTH, PAST, 128), cache_gqa_v.reshape(DEC_BATCH, DEPTH, PAST, 128))

    new_k, new_v, new_ckv, new_kr, new_s = [], [], [], [], []
    for l in range(DEPTH):
        pbg, proj = _inproj(x, mods[l], norm1_g[l][None, :], w_in_bf, l)

        pad128 = lambda v: jnp.pad(v.reshape(1, 8), ((0, 0), (8, 112)))
        padt = lambda v: jnp.pad(v.reshape(8, 1), ((8, 0), (0, 0)))
        sm_t = proj[:, C_SM:C_SM + 16].T
        dn_u, dn_w, dn_qg, dn_kd, dn_ai, dn_bg = _dn_prep(
            proj, sm_t, dn_conv_w[l], pad128(dn_a_log[l]), pad128(dn_dt_bias[l]),
            padt(dn_a_log[l]), padt(dn_dt_bias[l]))
        *dn_o, s_c = _dn_scan(dn_u, dn_w, dn_qg, dn_kd, dn_ai, dn_bg, state_delta, l)

        qw = jnp.tile(gqa_q_norm[l][None, :], (1, GQA_H))
        kw = jnp.tile(gqa_k_norm[l][None, :], (1, GQA_KV))
        o_g_c, kn_c = _gqa(proj, qw, kw, SEQ, SEQ, BATCH, 0, NS=CTX_NS)
        (o_g_l,) = _gqa(proj, qw, kw, DEC_SEQ, 256, DEC_BATCH, RC, rope=gqa_rope, cache=gqa_cache, layer=l)

        wq = mla_w_uq[l].reshape(MLA_QL, MLA_H, MLA_DQ)
        wq = jnp.concatenate([wq[:, :, :MLA_NOPE].reshape(MLA_QL, -1), wq[:, :, MLA_NOPE:].reshape(MLA_QL, -1)],
                             axis=1).astype(BF16)
        wkv = mla_w_ukv[l].reshape(MLA_KVL, MLA_H, MLA_NOPE + MLA_V)
        wkv = jnp.concatenate([wkv[:, :, :MLA_NOPE].reshape(MLA_KVL, -1), wkv[:, :, MLA_NOPE:].reshape(MLA_KVL, -1)],
                              axis=1).astype(BF16)
        mqw, mkvw = mla_q_norm[l][None, :], mla_kv_norm[l][None, :]
        o_m_c, ckv_c = _mla(proj, wq, wkv, mqw, mkvw, SEQ, SEQ, BATCH, 0, NS=CTX_NS)
        (o_m_l,) = _mla(proj, wq, wkv, mqw, mkvw, DEC_SEQ, 256, DEC_BATCH, RC, rope=mla_rope,
                        cache=(cache_mla_ckv, cache_mla_krope), layer=l)
        x = _merge(x, mods[l], dn_o, proj, pbg, o_g_c, o_g_l, o_m_c, o_m_l, dn_norm_g[l][None, :],
                   w_branch_bf, w_out_bf, l)

        j = l // 2
        if l % 2 == 0:
            x = _ffn(x, mods[l], norm2_g[l][None, :], *ffd_bf, j)
        else:
            rw128 = jnp.pad(router_w[j], ((0, 0), (0, 128 - N_EXP)))
            rb128 = jnp.pad(router_b[j][None, :], ((0, 0), (0, 128 - N_EXP)), constant_values=-jnp.inf)
            x = _moe(x, mods[l], norm2_g[l][None, :], rw128, rb128, moe_w1, moe_w3, moe_w2, j,
                     final_g=final_g[None, :] if l == DEPTH - 1 else None)

        new_k.append(kn_c.reshape(BATCH, SEQ, GQA_KV, GQA_HD))
        new_v.append(proj[:RC, C_GKV + 128:C_GKV + 256].reshape(BATCH, SEQ, GQA_KV, GQA_HD))
        new_ckv.append(ckv_c.reshape(BATCH, SEQ, MLA_KVL))
        new_kr.append(proj[:RC, C_SM + 16:C_SM + 16 + MLA_ROPE].reshape(BATCH, SEQ, MLA_ROPE))
        new_s.append(s_c)

    if DEPTH % 2 == 0:
        y_c, y_l = x
    else:
        y_c = _final_norm(x, final_g[None, :], 0, RC)
        y_l = _final_norm(x, final_g[None, :], RC, RL)
    return (y_c.reshape(BATCH, SEQ, D), y_l.reshape(DEC_BATCH, DEC_SEQ, D),
            jnp.stack(new_k, axis=1), jnp.stack(new_v, axis=1), jnp.stack(new_ckv, axis=1),
            jnp.stack(new_kr, axis=1), jnp.stack(new_s, axis=1))
```

```python
import functools
import math

import jax
import jax.numpy as jnp
import numpy as np
from jax import lax
from jax.experimental import pallas as pl
from jax.experimental.pallas import tpu as pltpu

F32 = jnp.float32
BF16 = jnp.bfloat16

D = 1024
BATCH, SEQ = 32, 256
DEC_BATCH, DEC_SEQ = 4, 2048
DEPTH = 4
PAST = 512
GRID_W = 64
ROPE_THETA = 10000.0
EPS = 1e-6
DN_H, DN_DK, DN_DV, DN_C = 4, 128, 128, 64
GQA_H, GQA_KV, GQA_HD = 8, 2, 64
MLA_H, MLA_QL, MLA_KVL, MLA_NOPE, MLA_ROPE, MLA_V = 8, 256, 128, 64, 32, 64
FF_DENSE, N_EXP, FF_EXP = 2816, 8, 3584

assert DEPTH % 2 == 0
RC = BATCH * SEQ
RL = DEC_BATCH * DEC_SEQ
R = RC + RL

N_BG = 3 * D
C_QKV, C_DG, C_GQ, C_GKV, C_MCQ, C_MCKV, C_SM = 0, 1536, 2048, 2560, 2816, 3072, 3200
N_PROJ = 3328
NP = N_BG + N_PROJ

VMEM_LIMIT = 56 * 1024 * 1024


def _cp(n_grid):
    return pltpu.CompilerParams(dimension_semantics=("arbitrary",) * n_grid,
                                vmem_limit_bytes=VMEM_LIMIT)


def _dot(a, b):
    return jnp.dot(a, b, preferred_element_type=F32)


def _bdot(a, b):
    return jnp.dot(a.astype(BF16), b.astype(BF16), preferred_element_type=F32)


def _bdot_nt(a, b):
    return lax.dot_general(a.astype(BF16), b.astype(BF16), (((1,), (1,)), ((), ())),
                           preferred_element_type=F32)


def _split2(a):
    hi = a.astype(BF16)
    lo = (a - hi.astype(F32)).astype(BF16)
    return hi, lo


def _split3(a):
    a1 = a.astype(BF16)
    r1 = a - a1.astype(F32)
    a2 = r1.astype(BF16)
    a3 = (r1 - a2.astype(F32)).astype(BF16)
    return a1, a2, a3


def _dot3(a, b):
    ah, al = _split2(a)
    bh, bl = _split2(b)
    return _dot(ah, bh) + (_dot(ah, bl) + _dot(al, bh))


def _mask_dot_r(mask_bf, g):
    g1, g2, g3 = _split3(g)
    return _dot(mask_bf, g1) + (_dot(mask_bf, g2) + _dot(mask_bf, g3))


def _mask_dot_l(g, mask_bf):
    g1, g2, g3 = _split3(g)
    return _dot(g1, mask_bf) + (_dot(g2, mask_bf) + _dot(g3, mask_bf))


def _sigmoid(x):
    return 1.0 / (1.0 + jnp.exp(-x))


def _silu(x):
    return x * _sigmoid(x)


def _softplus(x):
    return jnp.maximum(x, 0.0) + jnp.log1p(jnp.exp(-jnp.abs(x)))


def _mod_group(row0):
    return jnp.where(row0 < RC, 0, 1 + (row0 - RC) // DEC_SEQ)


MODS_TN = 1536


def _mods_body(c_ref, w_ref, b_ref, o_ref):
    s = _silu(c_ref[...])
    o_ref[0] = _bdot(s, w_ref[0]) + b_ref[0]


def _mods(cond8, w_mod, b_mod):
    nj = 6 * D // MODS_TN
    return pl.pallas_call(
        _mods_body,
        grid=(DEPTH, nj),
        in_specs=[pl.BlockSpec((8, D), lambda l, j: (0, 0)),
                  pl.BlockSpec((1, D, MODS_TN), lambda l, j: (l, 0, j)),
                  pl.BlockSpec((1, 1, MODS_TN), lambda l, j: (l, 0, j))],
        out_specs=pl.BlockSpec((1, 8, MODS_TN), lambda l, j: (l, 0, j)),
        out_shape=jax.ShapeDtypeStruct((DEPTH, 8, 6 * D), F32),
        compiler_params=_cp(2),
        name="mods",
    )(cond8, w_mod, b_mod.reshape(DEPTH, 1, 6 * D))


IN_TM = 512
IN_CHUNK = 1536


def _modnorm(x, g, shift, scale):
    ms = jnp.mean(x * x, axis=-1, keepdims=True)
    y = x * lax.rsqrt(ms + EPS) * g
    return y * (1.0 + scale) + shift


W_IN_PIECES = ((3248, 6320), (0, 2048), (2064, 3216), (2048, 2064), (3216, 3248))
W_IN_COLS = 6320


def _row_specs(x, tm):
    if not isinstance(x, tuple):
        return [pl.BlockSpec((tm, D), lambda i: (i, 0))], [x]
    n_ctx = RC // tm
    return ([pl.BlockSpec((tm, D), lambda i: (jnp.minimum(i, n_ctx - 1), 0)),
             pl.BlockSpec((tm, D), lambda i: (jnp.maximum(i - n_ctx, 0), 0))], list(x))


def _row_tile(x_refs, tm):
    if len(x_refs) == 1:
        return x_refs[0][...]
    return jnp.where(pl.program_id(0) < RC // tm, x_refs[0][...], x_refs[1][...])


def _inproj_body(n_x, *refs):
    x_refs = refs[:n_x]
    m_ref, g_ref, wsrc_ref, bg_ref, o_ref, w_ref = refs[n_x:]

    @pl.when(pl.program_id(0) == 0)
    def _():
        at = 0
        for a, b in W_IN_PIECES:
            w_ref[:, at:at + (b - a)] = wsrc_ref[0, :, a:b]
            at += b - a
        w_ref[:, at:NP] = jnp.zeros((D, NP - at), BF16)

    h = _modnorm(_row_tile(x_refs, IN_TM), g_ref[...], m_ref[0, 0:1, :], m_ref[0, 1:2, :]).astype(BF16)
    for c0 in range(0, N_BG, IN_CHUNK):
        bg_ref[:, c0:c0 + IN_CHUNK] = _dot(h, w_ref[:, c0:c0 + IN_CHUNK]).astype(BF16)
    for c0 in range(0, N_PROJ, IN_CHUNK):
        c1 = min(c0 + IN_CHUNK, N_PROJ)
        o_ref[:, c0:c1] = _dot(h, w_ref[:, N_BG + c0:N_BG + c1])


def _inproj(x, mods_l, g, w_bf, l):
    x_specs, x_args = _row_specs(x, IN_TM)
    return pl.pallas_call(
        functools.partial(_inproj_body, len(x_args)),
        grid=(R // IN_TM,),
        in_specs=x_specs + [
                  pl.BlockSpec((1, 6, D), lambda i: (_mod_group(i * IN_TM), 0, 0)),
                  pl.BlockSpec((1, D), lambda i: (0, 0)),
                  pl.BlockSpec((1, D, W_IN_COLS), lambda i: (l, 0, 0), pipeline_mode=pl.Buffered(1))],
        out_specs=[pl.BlockSpec((IN_TM, N_BG), lambda i: (i, 0)),
                   pl.BlockSpec((IN_TM, N_PROJ), lambda i: (i, 0))],
        out_shape=[jax.ShapeDtypeStruct((R, N_BG), BF16), jax.ShapeDtypeStruct((R, N_PROJ), F32)],
        scratch_shapes=[pltpu.VMEM((D, NP), BF16)],
        compiler_params=_cp(1),
        name="inproj",
    )(*x_args, mods_l, g, w_bf)


PREP_TM = 256
PREP_NC = PREP_TM // DN_C
PREP_LOCKSTEP = 4


def _prep_body(x_ref, xp_ref, xn_ref, sm_ref, smt_ref, cw_ref, al_ref, dt_ref, alt_ref, dtt_ref,
               u_o, w_o, qg_o, kd_o, ai_o, bg_o, qkv_o, gr_o):
    i = pl.program_id(0)
    n_ctx = RC // PREP_TM
    per_seq = DEC_SEQ // PREP_TM
    is_ctx = i < n_ctx
    j = (i - n_ctx) % per_seq
    first = jnp.logical_or(is_ctx, j == 0)
    last = jnp.logical_or(is_ctx, j == per_seq - 1)

    x = x_ref[...]
    prev_row = jnp.where(first, 0.0, xp_ref[7:8, :])
    next_row = jnp.where(last, 0.0, xn_ref[0:1, :])
    row = lax.broadcasted_iota(jnp.int32, (PREP_TM, 1), 0)
    xm = jnp.where(row == 0, prev_row, pltpu.roll(x, 1, axis=0))
    xq = jnp.where(row == PREP_TM - 1, next_row, pltpu.roll(x, PREP_TM - 1, axis=0))
    w = cw_ref[...]
    y = _silu(w[0:1] * xm + w[1:2] * x + w[2:3] * xq)

    for h in range(DN_H):
        qh = y[:, h * DN_DK:(h + 1) * DN_DK]
        qn = qh * lax.rsqrt(jnp.sum(qh * qh, axis=-1, keepdims=True) + EPS) * (DN_DK ** -0.5)
        qkv_o[:, h * DN_DK:(h + 1) * DN_DK] = qn
        kh = y[:, 512 + h * DN_DK:512 + (h + 1) * DN_DK]
        kn = kh * lax.rsqrt(jnp.sum(kh * kh, axis=-1, keepdims=True) + EPS)
        qkv_o[:, 512 + h * DN_DK:512 + (h + 1) * DN_DK] = kn
    qkv_o[:, 1024:1536] = y[:, 1024:1536]

    r = lax.broadcasted_iota(jnp.int32, (PREP_TM, PREP_TM), 0)
    c = lax.broadcasted_iota(jnp.int32, (PREP_TM, PREP_TM), 1)
    same = (r // DN_C) == (c // DN_C)
    low = jnp.where(jnp.logical_and(same, r >= c), 1.0, 0.0).astype(BF16)
    upp = jnp.where(jnp.logical_and(same, r <= c), 1.0, 0.0).astype(BF16)

    sm = sm_ref[...]
    beta = _sigmoid(sm)
    g = -jnp.exp(al_ref[...]) * _softplus(sm + dt_ref[...])
    gc_f = _mask_dot_r(low, g)
    gc_b = _mask_dot_r(upp, g)
    lane = lax.broadcasted_iota(jnp.int32, (1, 128), 1)
    bg_o[...] = jnp.where(lane < 8, beta, jnp.where(lane < 12, gc_f, jnp.where(lane < 16, gc_b, 0.0)))

    gt = -jnp.exp(alt_ref[...]) * _softplus(smt_ref[...] + dtt_ref[...])
    gct_f = _mask_dot_l(gt, upp)
    gct_b = _mask_dot_l(gt, low)
    sub = lax.broadcasted_iota(jnp.int32, (16, 1), 0)
    gct = jnp.where(sub < 12, gct_f, gct_b)
    for k in range(PREP_NC):
        gr_o[k] = gct[8:16, k * DN_C:(k + 1) * DN_C]

    rr = lax.broadcasted_iota(jnp.int32, (DN_C, DN_C), 0)
    cc = lax.broadcasted_iota(jnp.int32, (DN_C, DN_C), 1)

    def chunk_group(gi, carry):
        prob = [(cj, d, h) for cj in range(PREP_LOCKSTEP) for d in range(2) for h in range(DN_H)]
        ci = [gi * PREP_LOCKSTEP + cj for cj in range(PREP_LOCKSTEP)]
        rows_c = [pl.ds(pl.multiple_of(c_ * DN_C, DN_C), DN_C) for c_ in ci]
        bgc = [bg_o[r_, :] for r_ in rows_c]
        grow_all = [gr_o[c_] for c_ in ci]
        rows = [rows_c[cj] for cj, d, h in prob]
        q = [qkv_o[rows_c[cj], h * DN_DK:(h + 1) * DN_DK] for cj, d, h in prob]
        k = [qkv_o[rows_c[cj], 512 + h * DN_DK:512 + (h + 1) * DN_DK] for cj, d, h in prob]
        v = [qkv_o[rows_c[cj], 1024 + h * DN_DV:1024 + (h + 1) * DN_DV] for cj, d, h in prob]
        beta = [bgc[cj][:, d * DN_H + h:d * DN_H + h + 1] for cj, d, h in prob]
        gcol = [bgc[cj][:, 8 + d * DN_H + h:9 + d * DN_H + h] for cj, d, h in prob]
        grow = [grow_all[cj][d * DN_H + h:d * DN_H + h + 1, :] for cj, d, h in prob]
        incl = [(rr >= cc) if d == 0 else (rr <= cc) for cj, d, h in prob]
        strict = [(rr > cc) if d == 0 else (rr < cc) for cj, d, h in prob]
        n = len(prob)
        decay = [jnp.exp(jnp.where(incl[i], gcol[i] - grow[i], -1e30)) for i in range(n)]
        kb = [k[i] * beta[i] for i in range(n)]
        kk = [_bdot_nt(kb[i], k[i]) for i in range(n)]
        qk = [_bdot_nt(q[i], k[i]) for i in range(n)]
        a = [jnp.where(strict[i], kk[i] * decay[i], 0.0) for i in range(n)]
        blk = lambda b: (rr // b) == (cc // b)
        eye = jnp.where(rr == cc, 1.0, 0.0)
        p = [jnp.where(blk(8), a[i], 0.0) for i in range(n)]
        t = [eye - p[i] for i in range(n)]
        for _ in range(2):
            p = [_bdot(p[i], p[i]) for i in range(n)]
            t = [t[i] + _bdot(t[i], p[i]) for i in range(n)]
        for b in (16, 32, 64):
            m = jnp.logical_and(blk(b), jnp.logical_not(blk(b // 2)))
            tl = [_bdot(t[i], jnp.where(m, a[i], 0.0)) for i in range(n)]
            t = [t[i] - _bdot(tl[i], t[i]) for i in range(n)]
        egc = [jnp.exp(gcol[i]) for i in range(n)]
        glast = [gcol[i][DN_C - 1:DN_C, :] if prob[i][1] == 0 else gcol[i][0:1, :] for i in range(n)]
        uw = [_bdot(t[i], jnp.concatenate([v[i] * beta[i], kb[i] * egc[i]], axis=1)) for i in range(n)]
        for i, (cj, d, h) in enumerate(prob):
            u_o[d, rows[i], h * DN_DV:(h + 1) * DN_DV] = uw[i][:, 0:DN_DV]
            w_o[d, rows[i], h * DN_DK:(h + 1) * DN_DK] = uw[i][:, DN_DV:DN_DV + DN_DK].astype(BF16)
            qg_o[d, rows[i], h * DN_DK:(h + 1) * DN_DK] = (q[i] * egc[i]).astype(BF16)
            kd_o[d, rows[i], h * DN_DK:(h + 1) * DN_DK] = (k[i] * jnp.exp(glast[i] - gcol[i])).astype(BF16)
            ai_o[d, rows[i], h * DN_C:(h + 1) * DN_C] = (qk[i] * decay[i]).astype(BF16)
        return carry

    lax.fori_loop(0, PREP_NC // PREP_LOCKSTEP, chunk_group, 0)


def _dn_prep(proj, sm_t, conv_w, al128, dt128, al_t, dt_t):
    nb8 = R // 8
    qb = C_QKV // 1536
    wide = lambda dt: jax.ShapeDtypeStruct((2, R, DN_H * DN_DK), dt)
    wide_spec = pl.BlockSpec((2, PREP_TM, DN_H * DN_DK), lambda i: (0, i, 0))
    return pl.pallas_call(
        _prep_body,
        grid=(R // PREP_TM,),
        in_specs=[pl.BlockSpec((PREP_TM, 1536), lambda i: (i, qb)),
                  pl.BlockSpec((8, 1536), lambda i: (jnp.maximum(i * (PREP_TM // 8) - 1, 0), qb)),
                  pl.BlockSpec((8, 1536), lambda i: (jnp.minimum((i + 1) * (PREP_TM // 8), nb8 - 1), qb)),
                  pl.BlockSpec((PREP_TM, 128), lambda i: (i, C_SM // 128)),
                  pl.BlockSpec((16, PREP_TM), lambda i: (0, i)),
                  pl.BlockSpec((3, 1536), lambda i: (0, 0)),
                  pl.BlockSpec((1, 128), lambda i: (0, 0)),
                  pl.BlockSpec((1, 128), lambda i: (0, 0)),
                  pl.BlockSpec((16, 1), lambda i: (0, 0)),
                  pl.BlockSpec((16, 1), lambda i: (0, 0))],
        out_specs=[wide_spec, wide_spec, wide_spec, wide_spec,
                   pl.BlockSpec((2, PREP_TM, DN_H * DN_C), lambda i: (0, i, 0)),
                   pl.BlockSpec((PREP_TM, 128), lambda i: (i, 0))],
        out_shape=[wide(F32), wide(BF16), wide(BF16), wide(BF16),
                   jax.ShapeDtypeStruct((2, R, DN_H * DN_C), BF16),
                   jax.ShapeDtypeStruct((R, 128), F32)],
        scratch_shapes=[pltpu.VMEM((PREP_TM, 1536), F32), pltpu.VMEM((PREP_NC, 8, DN_C), F32)],
        compiler_params=_cp(1),
        name="dn_prep",
    )(proj, proj, proj, proj, sm_t, conv_w, al128, dt128, al_t, dt_t)


SCAN_STEPS = RC // PREP_TM
SCAN_PER = DEC_SEQ // PREP_TM
assert RL // PREP_TM == SCAN_STEPS and SEQ == PREP_TM


def _scan_lat_bwd_block(i):
    return (i // SCAN_PER) * SCAN_PER + (SCAN_PER - 1 - i % SCAN_PER)


def _scan_body(*refs):
    streams = [refs[6 * k:6 * k + 6] for k in range(4)]
    s0_ref = refs[24]
    out_refs = refs[25:29]
    so_ref, s_scr = refs[29], refs[30]
    j = pl.program_id(0) % SCAN_PER

    for h in range(DN_H):
        s_scr[0, h] = jnp.zeros((DN_DK, DN_DV), F32)
        s_scr[1, h] = jnp.zeros((DN_DK, DN_DV), F32)

    @pl.when(j == 0)
    def _():
        for h in range(DN_H):
            s_scr[2, h] = s0_ref[0, 0, 0, h]
            s_scr[3, h] = s0_ref[0, 0, 1, h]

    prob = [(k, h) for k in range(4) for h in range(DN_H)]
    tn = (((0,), (0,)), ((), ()))

    def step(n, carry):
        rows = [pl.ds(pl.multiple_of((n if k % 2 == 0 else PREP_NC - 1 - n) * DN_C, DN_C), DN_C)
                for k in range(4)]
        bgc = [streams[k][5][rows[k], :] for k in range(4)]
        cols = [slice(h * DN_DK, (h + 1) * DN_DK) for k, h in prob]
        st = [s_scr[k, h] for k, h in prob]
        stb = [x.astype(BF16) for x in st]
        ws = [_dot(streams[k][1][0, rows[k], cols[i]], stb[i]) for i, (k, h) in enumerate(prob)]
        qs = [_dot(streams[k][2][0, rows[k], cols[i]], stb[i]) for i, (k, h) in enumerate(prob)]
        vb = [(streams[k][0][0, rows[k], cols[i]] - ws[i]).astype(BF16) for i, (k, h) in enumerate(prob)]
        av = [_dot(streams[k][4][0, rows[k], h * DN_C:(h + 1) * DN_C], vb[i]) for i, (k, h) in enumerate(prob)]
        kv = [lax.dot_general(streams[k][3][0, rows[k], cols[i]], vb[i], tn, preferred_element_type=F32)
              for i, (k, h) in enumerate(prob)]
        for i, (k, h) in enumerate(prob):
            d = k % 2
            gcol = bgc[k][:, 8 + d * DN_H + h:9 + d * DN_H + h]
            glast = gcol[DN_C - 1:DN_C, :] if d == 0 else gcol[0:1, :]
            s_scr[k, h] = st[i] * jnp.exp(glast) + kv[i]
            out_refs[k][rows[k], cols[i]] = (qs[i] + av[i]).astype(out_refs[k].dtype)
        return carry

    lax.fori_loop(0, PREP_NC, step, 0)

    for d in range(2):
        for h in range(DN_H):
            so_ref[0, d, h] = s_scr[d, h]


def _dn_scan(u, w, qg, kd, ai, bg, s0, layer):
    row_of = (lambda i: i, lambda i: i,
              lambda i: SCAN_STEPS + i, lambda i: SCAN_STEPS + _scan_lat_bwd_block(i))
    in_specs, args = [], []
    for k in range(4):
        d, rb = k % 2, row_of[k]
        wide = pl.BlockSpec((1, PREP_TM, DN_H * DN_DK), lambda i, d=d, rb=rb: (d, rb(i), 0))
        narrow = pl.BlockSpec((1, PREP_TM, DN_H * DN_C), lambda i, d=d, rb=rb: (d, rb(i), 0))
        in_specs += [wide, wide, wide, wide, narrow, pl.BlockSpec((PREP_TM, 128), lambda i, rb=rb: (rb(i), 0))]
        args += [u, w, qg, kd, ai, bg]
    state_spec = lambda m: pl.BlockSpec((1, 2, DN_H, DN_DK, DN_DV), m)
    in_specs.append(pl.BlockSpec((1, 1, 2, DN_H, DN_DK, DN_DV), lambda i: (i // SCAN_PER, layer, 0, 0, 0, 0)))
    o_spec = lambda m: pl.BlockSpec((PREP_TM, DN_H * DN_DV), m)
    o_shape = lambda n: jax.ShapeDtypeStruct((n, DN_H * DN_DV), BF16)
    return pl.pallas_call(
        _scan_body,
        grid=(SCAN_STEPS,),
        in_specs=in_specs,
        out_specs=[o_spec(lambda i: (i, 0)), o_spec(lambda i: (i, 0)), o_spec(lambda i: (i, 0)),
                   o_spec(lambda i: (_scan_lat_bwd_block(i), 0)),
                   state_spec(lambda i: (i, 0, 0, 0, 0))],
        out_shape=[o_shape(RC), o_shape(RC), o_shape(RL), o_shape(RL),
                   jax.ShapeDtypeStruct((BATCH, 2, DN_H, DN_DK, DN_DV), F32)],
        scratch_shapes=[pltpu.VMEM((4, DN_H, DN_DK, DN_DV), F32)],
        compiler_params=_cp(1),
        name="dn_scan",
    )(*args, s0)


def _group_mean_matrix(width, group):
    r = lax.broadcasted_iota(jnp.int32, (width, width), 0)
    c = lax.broadcasted_iota(jnp.int32, (width, width), 1)
    return jnp.where((r // group) == (c // group), 1.0 / group, 0.0).astype(BF16)


def _group_rmsnorm(x, w, group):
    m = _group_mean_matrix(x.shape[-1], group)
    hi, lo = _split2(x * x)
    ms = _dot(hi, m) + _dot(lo, m)
    return x * lax.rsqrt(ms + EPS) * w


def _rope(x, cos, sin_signed, group):
    width = x.shape[-1]
    half = group // 2
    lane = lax.broadcasted_iota(jnp.int32, (1, width), 1)
    swapped = jnp.where((lane % group) < half,
                        pltpu.roll(x, width - half, axis=1), pltpu.roll(x, half, axis=1))
    return x * cos + swapped * sin_signed


def _attention_units(n, scores, values):
    outs = []
    s_next = scores(0)
    for u in range(n):
        s = s_next
        if u + 1 < n:
            s_next = scores(u + 1)
        m = jnp.max(s, axis=-1, keepdims=True)
        p = jnp.exp(s - m)
        l = jnp.sum(p, axis=-1, keepdims=True)
        outs.append(_dot(p.astype(BF16), values(u)) / l)
    return outs


GQA_G = GQA_H // GQA_KV
CTX_NS = 2


def _gqa_body(T, TQ, NS, latent, *refs):
    it = iter(refs)
    q_ref, kv_ref, qw_ref, kw_ref = next(it), next(it), next(it), next(it)
    if latent:
        cq_ref, sq_ref, ck_ref, sk_ref, kc_ref, vc_ref = (next(it) for _ in range(6))
    o_ref = next(it)
    kn_ref = None if latent else next(it)
    k_scr, v_scr = next(it), next(it)

    @pl.when(pl.program_id(1) == 0)
    def _():
        kv = kv_ref[...]
        k = _group_rmsnorm(kv[:, 0:128], kw_ref[...], GQA_HD)
        v = kv[:, 128:256]
        if latent:
            k = _rope(k, ck_ref[...], sk_ref[...], GQA_HD)
        else:
            kn_ref[...] = k
        for s in range(NS):
            for g in range(GQA_KV):
                sg = s * GQA_KV + g
                k_scr[sg, 0:T, :] = k[s * T:(s + 1) * T, g * GQA_HD:(g + 1) * GQA_HD].astype(BF16)
                v_scr[sg, 0:T, :] = v[s * T:(s + 1) * T, g * GQA_HD:(g + 1) * GQA_HD].astype(BF16)
                if latent:
                    k_scr[sg, T:T + PAST, :] = kc_ref[0, 0, :, g * GQA_HD:(g + 1) * GQA_HD].astype(BF16)
                    v_scr[sg, T:T + PAST, :] = vc_ref[0, 0, :, g * GQA_HD:(g + 1) * GQA_HD].astype(BF16)

    q = _group_rmsnorm(q_ref[...], qw_ref[...], GQA_HD)
    if latent:
        q = _rope(q, cq_ref[...], sq_ref[...], GQA_HD)
    q = q * (GQA_HD ** -0.5)
    head = lambda s, hh: q[s * TQ:(s + 1) * TQ, hh * GQA_HD:(hh + 1) * GQA_HD]
    units = [(s, u) for s in range(NS) for u in range(GQA_H // 2)]
    qu = [jnp.concatenate([head(s, 2 * u), head(s, 2 * u + 1)], axis=0) for s, u in units]
    kv_of = lambda i: units[i][0] * GQA_KV + (2 * units[i][1]) // GQA_G
    outs = _attention_units(len(units), lambda i: _bdot_nt(qu[i], k_scr[kv_of(i)]), lambda i: v_scr[kv_of(i)])
    for (s, u), o in zip(units, outs):
        for j in range(2):
            hh = 2 * u + j
            o_ref[s * TQ:(s + 1) * TQ, hh * GQA_HD:(hh + 1) * GQA_HD] = o[j * TQ:(j + 1) * TQ, :].astype(o_ref.dtype)


def _gqa(proj, qw, kw, T, TQ, n_seq, row_off, rope=None, cache=None, NS=1, layer=0):
    latent = rope is not None
    tk = T + (PAST if latent else 0)
    nq = T // TQ
    assert NS == 1 or (nq == 1 and not latent)
    n_seq, TQ, TB = n_seq // NS, NS * TQ, NS * T
    qo = row_off // TQ
    so = row_off // TB
    in_specs = [pl.BlockSpec((TQ, 512), lambda i, j: (qo + i * nq + j, C_GQ // 512)),
                pl.BlockSpec((TB, 256), lambda i, j: (so + i, C_GKV // 256)),
                pl.BlockSpec((1, 512), lambda i, j: (0, 0)),
                pl.BlockSpec((1, 128), lambda i, j: (0, 0))]
    args = [proj, proj, qw, kw]
    if latent:
        cq, sq, ck, sk = rope
        kc, vc = cache
        in_specs += [pl.BlockSpec((TQ, 512), lambda i, j: (j, 0)),
                     pl.BlockSpec((TQ, 512), lambda i, j: (j, 0)),
                     pl.BlockSpec((T, 128), lambda i, j: (0, 0)),
                     pl.BlockSpec((T, 128), lambda i, j: (0, 0)),
                     pl.BlockSpec((1, 1, PAST, 128), lambda i, j: (i, layer, 0, 0)),
                     pl.BlockSpec((1, 1, PAST, 128), lambda i, j: (i, layer, 0, 0))]
        args += [cq, sq, ck, sk, kc, vc]
    out_specs = [pl.BlockSpec((TQ, 512), lambda i, j: (i * nq + j, 0))]
    out_shape = [jax.ShapeDtypeStruct((n_seq * TB, 512), BF16)]
    if not latent:
        out_specs.append(pl.BlockSpec((TB, 128), lambda i, j: (i, 0)))
        out_shape.append(jax.ShapeDtypeStruct((n_seq * TB, 128), F32))
    return pl.pallas_call(
        functools.partial(_gqa_body, T, TQ // NS, NS, latent),
        grid=(n_seq, nq),
        in_specs=in_specs,
        out_specs=out_specs,
        out_shape=out_shape,
        scratch_shapes=[pltpu.VMEM((NS * GQA_KV, tk, GQA_HD), BF16), pltpu.VMEM((NS * GQA_KV, tk, GQA_HD), BF16)],
        compiler_params=_cp(2),
        name="gqa_%d" % T,
    )(*args)


MLA_DQ = MLA_NOPE + MLA_ROPE
MLA_KV_ROWS = 512


def _mla_body(T, TQ, NS, latent, *refs):
    it = iter(refs)
    cq_ref, ckv_ref, sm_ref, wq_ref, wkv_ref, qw_ref, kvw_ref = (next(it) for _ in range(7))
    if latent:
        cosq_ref, sinq_ref, cosk_ref, sink_ref, cc_ref, kc_ref = (next(it) for _ in range(6))
    o_ref = next(it)
    cn_ref = None if latent else next(it)
    k_scr, v_scr = next(it), next(it)
    tk = T + (PAST if latent else 0)

    @pl.when(pl.program_id(1) == 0)
    def _():
        x = ckv_ref[...]
        ms = jnp.mean(x * x, axis=-1, keepdims=True)
        ckv = x * lax.rsqrt(ms + EPS) * kvw_ref[...]
        sm = sm_ref[...]
        if latent:
            sm = _rope_small(sm, cosk_ref[...], sink_ref[...])
        else:
            cn_ref[...] = ckv
        kr = sm[:, 16:16 + MLA_ROPE]

        def put(s, rows0, ckv_rows, kr_rows):
            n = ckv_rows.shape[0]
            kv = _bdot(ckv_rows, wkv_ref[...])
            krb = kr_rows.astype(BF16)
            for h in range(MLA_H):
                sh = s * MLA_H + h
                k_scr[sh, rows0:rows0 + n, 0:MLA_NOPE] = kv[:, h * MLA_NOPE:(h + 1) * MLA_NOPE].astype(BF16)
                k_scr[sh, rows0:rows0 + n, MLA_NOPE:MLA_DQ] = krb
                v_scr[sh, rows0:rows0 + n, :] = kv[:, 512 + h * MLA_V:512 + (h + 1) * MLA_V].astype(BF16)

        for s in range(NS):
            for r0 in range(0, T, MLA_KV_ROWS):
                r1 = min(r0 + MLA_KV_ROWS, T)
                put(s, r0, ckv[s * T + r0:s * T + r1], kr[s * T + r0:s * T + r1])
        if latent:
            put(0, T, cc_ref[0, 0], kc_ref[0, 0])

    x = cq_ref[...]
    ms = jnp.mean(x * x, axis=-1, keepdims=True)
    cq = x * lax.rsqrt(ms + EPS) * qw_ref[...]
    qf = _bdot(cq, wq_ref[...])
    qn = qf[:, 0:512]
    qr = qf[:, 512:768]
    if latent:
        qr = _rope(qr, cosq_ref[...], sinq_ref[...], MLA_ROPE)
    scale = MLA_DQ ** -0.5
    units = [(s, h) for s in range(NS) for h in range(MLA_H)]
    rows = lambda s: slice(s * TQ, (s + 1) * TQ)
    qh = [jnp.concatenate([qn[rows(s), h * MLA_NOPE:(h + 1) * MLA_NOPE],
                           qr[rows(s), h * MLA_ROPE:(h + 1) * MLA_ROPE]], axis=1) * scale for s, h in units]
    sh = lambda i: units[i][0] * MLA_H + units[i][1]
    outs = _attention_units(len(units), lambda i: _bdot_nt(qh[i], k_scr[sh(i)]), lambda i: v_scr[sh(i)])
    for (s, h), o in zip(units, outs):
        o_ref[rows(s), h * MLA_V:(h + 1) * MLA_V] = o.astype(o_ref.dtype)


def _rope_small(sm, cos, sin_signed):
    lane = lax.broadcasted_iota(jnp.int32, (1, 128), 1)
    half = MLA_ROPE // 2
    swapped = jnp.where(lane < 16 + half, pltpu.roll(sm, 128 - half, axis=1), pltpu.roll(sm, half, axis=1))
    return sm * cos + swapped * sin_signed


def _mla(proj, wq, wkv, qw, kvw, T, TQ, n_seq, row_off, rope=None, cache=None, NS=1, layer=0):
    latent = rope is not None
    tk = T + (PAST if latent else 0)
    nq = T // TQ
    assert NS == 1 or (nq == 1 and not latent)
    n_seq, TQ, TB = n_seq // NS, NS * TQ, NS * T
    qo = row_off // TQ
    so = row_off // TB
    in_specs = [pl.BlockSpec((TQ, 256), lambda i, j: (qo + i * nq + j, C_MCQ // 256)),
                pl.BlockSpec((TB, 128), lambda i, j: (so + i, C_MCKV // 128)),
                pl.BlockSpec((TB, 128), lambda i, j: (so + i, C_SM // 128)),
                pl.BlockSpec((MLA_QL, 768), lambda i, j: (0, 0)),
                pl.BlockSpec((MLA_KVL, 1024), lambda i, j: (0, 0)),
                pl.BlockSpec((1, 256), lambda i, j: (0, 0)),
                pl.BlockSpec((1, 128), lambda i, j: (0, 0))]
    args = [proj, proj, proj, wq, wkv, qw, kvw]
    if latent:
        cosq, sinq, cosk, sink = rope
        cc, kc = cache
        in_specs += [pl.BlockSpec((TQ, 256), lambda i, j: (j, 0)),
                     pl.BlockSpec((TQ, 256), lambda i, j: (j, 0)),
                     pl.BlockSpec((T, 128), lambda i, j: (0, 0)),
                     pl.BlockSpec((T, 128), lambda i, j: (0, 0)),
                     pl.BlockSpec((1, 1, PAST, 128), lambda i, j: (i, layer, 0, 0)),
                     pl.BlockSpec((1, 1, PAST, MLA_ROPE), lambda i, j: (i, layer, 0, 0))]
        args += [cosq, sinq, cosk, sink, cc, kc]
    out_specs = [pl.BlockSpec((TQ, 512), lambda i, j: (i * nq + j, 0))]
    out_shape = [jax.ShapeDtypeStruct((n_seq * TB, 512), BF16)]
    if not latent:
        out_specs.append(pl.BlockSpec((TB, 128), lambda i, j: (i, 0)))
        out_shape.append(jax.ShapeDtypeStruct((n_seq * TB, 128), F32))
    return pl.pallas_call(
        functools.partial(_mla_body, T, TQ // NS, NS, latent),
        grid=(n_seq, nq),
        in_specs=in_specs,
        out_specs=out_specs,
        out_shape=out_shape,
        scratch_shapes=[pltpu.VMEM((NS * MLA_H, tk, MLA_DQ), BF16), pltpu.VMEM((NS * MLA_H, tk, MLA_V), BF16)],
        compiler_params=_cp(2),
        name="mla_%d" % T,
    )(*args)


MG_TM = 512


def _merge_body(n_x, *refs):
    x_refs = refs[:n_x]
    (m_ref, ofc_ref, obc_ref, ofl_ref, obl_ref, dg_ref, ogc_ref, ogl_ref, omc_ref, oml_ref,
     bg_ref, ng_ref, wb_ref, wo_ref, o_ref) = refs[n_x:]
    is_ctx = pl.program_id(0) < RC // MG_TM
    og = jnp.where(is_ctx, ogc_ref[...], ogl_ref[...])
    om = jnp.where(is_ctx, omc_ref[...], oml_ref[...])
    f32 = lambda r: r[...].astype(F32)
    odn = jnp.where(is_ctx, f32(ofc_ref) + f32(obc_ref), f32(ofl_ref) + f32(obl_ref))
    dg = dg_ref[...]
    ng = ng_ref[...]
    parts = []
    for h in range(DN_H):
        oh = odn[:, h * DN_DV:(h + 1) * DN_DV]
        ms = jnp.mean(oh * oh, axis=-1, keepdims=True)
        parts.append(oh * lax.rsqrt(ms + EPS) * ng * _silu(dg[:, h * DN_DV:(h + 1) * DN_DV]))
    br0 = jnp.concatenate(parts, axis=1)
    gate = lambda n: _sigmoid(bg_ref[:, n * D:(n + 1) * D].astype(F32))
    merged = gate(0) * _bdot(br0, wb_ref[0, 0])
    merged = merged + gate(1) * _bdot(og, wb_ref[0, 1])
    merged = merged + gate(2) * _bdot(om, wb_ref[0, 2])
    out = _bdot(merged, wo_ref[0])
    o_ref[...] = _row_tile(x_refs, MG_TM) + m_ref[0, 2:3, :] * out


def _merge(x, mods_l, dn_o, proj, pbg, og_c, og_l, om_c, om_l, ng, wb, wo, l):
    row = lambda i: (i, 0)
    n_ctx = RC // MG_TM
    ctx_row = lambda i: (jnp.minimum(i, n_ctx - 1), 0)
    lat_row = lambda i: (jnp.maximum(i - n_ctx, 0), 0)
    x_specs, x_args = _row_specs(x, MG_TM)
    return pl.pallas_call(
        functools.partial(_merge_body, len(x_args)),
        grid=(R // MG_TM,),
        in_specs=x_specs + [
                  pl.BlockSpec((1, 6, D), lambda i: (_mod_group(i * MG_TM), 0, 0)),
                  pl.BlockSpec((MG_TM, 512), ctx_row),
                  pl.BlockSpec((MG_TM, 512), ctx_row),
                  pl.BlockSpec((MG_TM, 512), lat_row),
                  pl.BlockSpec((MG_TM, 512), lat_row),
                  pl.BlockSpec((MG_TM, 512), lambda i: (i, C_DG // 512)),
                  pl.BlockSpec((MG_TM, 512), ctx_row),
                  pl.BlockSpec((MG_TM, 512), lat_row),
                  pl.BlockSpec((MG_TM, 512), ctx_row),
                  pl.BlockSpec((MG_TM, 512), lat_row),
                  pl.BlockSpec((MG_TM, N_BG), row),
                  pl.BlockSpec((1, DN_DV), lambda i: (0, 0)),
                  pl.BlockSpec((1, 3, 512, D), lambda i: (l, 0, 0, 0)),
                  pl.BlockSpec((1, D, D), lambda i: (l, 0, 0))],
        out_specs=pl.BlockSpec((MG_TM, D), row),
        out_shape=jax.ShapeDtypeStruct((R, D), F32),
        compiler_params=_cp(1),
        name="merge",
    )(*x_args, mods_l, *dn_o, proj, og_c, og_l, om_c, om_l, pbg, ng, wb, wo)


FF_TM = 512
FF_CHUNKS = ((0, 1536), (1536, FF_DENSE))


FF_WCH13, FF_WCH2 = 128, 352


def _ffn_body(j, x_ref, m_ref, g_ref, w1_hbm, w3_hbm, w2_hbm, o_ref, w1_ref, w3_ref, w2_ref, st13, st2, wsems):
    @pl.when(pl.program_id(0) == 0)
    def _():
        _stage_to_bf16(w1_hbm.at[j], w1_ref, st13, wsems)
        _stage_to_bf16(w3_hbm.at[j], w3_ref, st13, wsems)
        _stage_to_bf16(w2_hbm.at[j], w2_ref, st2, wsems)

    x = x_ref[...]
    h = _modnorm(x, g_ref[...], m_ref[0, 3:4, :], m_ref[0, 4:5, :]).astype(BF16)
    y = None
    for c0, c1 in FF_CHUNKS:
        a = _silu(_dot(h, w1_ref[:, c0:c1])) * _dot(h, w3_ref[:, c0:c1])
        yc = _dot(a.astype(BF16), w2_ref[c0:c1, :])
        y = yc if y is None else y + yc
    o_ref[...] = x + m_ref[0, 5:6, :] * y


def _ffn(x, mods_l, g, w1, w3, w2, j):
    hbm = pl.BlockSpec(memory_space=pl.ANY)
    return pl.pallas_call(
        functools.partial(_ffn_body, j),
        grid=(R // FF_TM,),
        in_specs=[pl.BlockSpec((FF_TM, D), lambda i: (i, 0)),
                  pl.BlockSpec((1, 6, D), lambda i: (_mod_group(i * FF_TM), 0, 0)),
                  pl.BlockSpec((1, D), lambda i: (0, 0)),
                  hbm, hbm, hbm],
        out_specs=pl.BlockSpec((FF_TM, D), lambda i: (i, 0)),
        out_shape=jax.ShapeDtypeStruct((R, D), F32),
        scratch_shapes=[pltpu.VMEM((D, FF_DENSE), BF16), pltpu.VMEM((D, FF_DENSE), BF16),
                        pltpu.VMEM((FF_DENSE, D), BF16),
                        pltpu.VMEM((2, FF_WCH13, FF_DENSE), F32), pltpu.VMEM((2, FF_WCH2, D), F32),
                        pltpu.SemaphoreType.DMA((2,))],
        compiler_params=_cp(1),
        name="ffn",
    )(x, mods_l, g, w1, w3, w2)


RT_TM = 512
MOE_TG, MOE_TF = 512, 1792
MOE_NT = 2 * R // MOE_TG + N_EXP
MOE_ROWS = MOE_NT * MOE_TG
CB_TM = 256


def _router_body(x_ref, m_ref, g_ref, rw_ref, rb_ref, hn_ref, route_ref, cnt_ref, base_scr):
    @pl.when(pl.program_id(0) == 0)
    def _():
        base_scr[...] = jnp.zeros_like(base_scr)

    h = _modnorm(x_ref[...], g_ref[...], m_ref[0, 3:4, :], m_ref[0, 4:5, :])
    hn_ref[...] = h
    logits = _dot3(h, rw_ref[...]) + rb_ref[...]
    lane = lax.broadcasted_iota(jnp.int32, logits.shape, 1)
    m1 = jnp.max(logits, axis=-1, keepdims=True)
    i1 = jnp.min(jnp.where(logits == m1, lane, 128), axis=-1, keepdims=True)
    sel1 = lane == i1
    rest = jnp.where(sel1, -jnp.inf, logits)
    m2 = jnp.max(rest, axis=-1, keepdims=True)
    i2 = jnp.min(jnp.where(rest == m2, lane, 128), axis=-1, keepdims=True)
    sel2 = lane == i2
    e2 = jnp.exp(m2 - m1)
    p1 = 1.0 / (1.0 + e2)
    p2 = e2 / (1.0 + e2)

    cnt = jnp.where(jnp.logical_or(sel1, sel2), 1.0, 0.0)
    r = lax.broadcasted_iota(jnp.int32, (RT_TM, RT_TM), 0)
    c = lax.broadcasted_iota(jnp.int32, (RT_TM, RT_TM), 1)
    before = jnp.where(r > c, 1.0, 0.0).astype(BF16)
    seen = base_scr[...] + _dot(before, cnt.astype(BF16))
    rank1 = jnp.sum(jnp.where(sel1, seen, 0.0), axis=-1, keepdims=True)
    rank2 = jnp.sum(jnp.where(sel2, seen, 0.0), axis=-1, keepdims=True)
    vals = (i1.astype(F32), i2.astype(F32), rank1, rank2, p1, p2)
    route = jnp.zeros(logits.shape, F32)
    for k, val in enumerate(vals):
        route = jnp.where(lane == k, val, route)
    route_ref[...] = route
    base_scr[...] += jnp.sum(cnt, axis=0, keepdims=True)
    cnt_ref[...] = base_scr[...]


def _router(x, mods_l, g, rw128, rb128):
    return pl.pallas_call(
        _router_body,
        grid=(R // RT_TM,),
        in_specs=[pl.BlockSpec((RT_TM, D), lambda i: (i, 0)),
                  pl.BlockSpec((1, 6, D), lambda i: (_mod_group(i * RT_TM), 0, 0)),
                  pl.BlockSpec((1, D), lambda i: (0, 0)),
                  pl.BlockSpec((D, 128), lambda i: (0, 0)),
                  pl.BlockSpec((1, 128), lambda i: (0, 0))],
        out_specs=[pl.BlockSpec((RT_TM, D), lambda i: (i, 0)),
                   pl.BlockSpec((RT_TM, 128), lambda i: (i, 0)),
                   pl.BlockSpec((1, 128), lambda i: (0, 0))],
        out_shape=[jax.ShapeDtypeStruct((R, D), F32),
                   jax.ShapeDtypeStruct((R, 128), F32),
                   jax.ShapeDtypeStruct((1, 128), F32)],
        scratch_shapes=[pltpu.VMEM((1, 128), F32)],
        compiler_params=_cp(1),
        name="moe_router",
    )(x, mods_l, g, rw128, rb128)


def _rows_copy(src_hbm, dst, sem, n):
    return pltpu.make_async_copy(src_hbm.at[pl.ds(0, n)], dst, sem)


def _start_row_gather(idx_ref, n, src_hbm, dst, sem, inline=False, both_queues=False):
    def issue(i, priority):
        pltpu.make_async_copy(src_hbm.at[pl.ds(idx_ref[0, 0, i], 1)], dst.at[pl.ds(i, 1)], sem).start(
            priority=priority)

    if inline:
        for i in range(n):
            issue(i, i % 2 if both_queues else 0)
    else:
        def body(i, carry):
            issue(i, 0)
            return carry

        lax.fori_loop(0, n, body, 0, unroll=8)


def _stage_to_bf16(src_hbm, dst, stage, sems):
    ch = stage.shape[1]
    n = src_hbm.shape[0] // ch
    copy = lambda c: pltpu.make_async_copy(src_hbm.at[pl.ds(c * ch, ch)], stage.at[c % 2], sems.at[c % 2])
    copy(0).start()
    for c in range(n):
        if c + 1 < n:
            copy(c + 1).start()
        copy(c).wait()
        dst[c * ch:(c + 1) * ch, :] = stage[c % 2].astype(BF16)


def _experts_body(j, te_ref, tv_ref, src_ref, nsrc_ref, hn_hbm, w1_hbm, w3_hbm, w2_hbm, ys_ref,
                  xg_scr, w1_ref, w3_ref, w2_ref, st13, st2, sems, wsems):
    t = pl.program_id(0)
    slot = t % 2
    valid = tv_ref[t] > 0
    e = te_ref[t]
    new_expert = jnp.logical_or(t == 0, e != te_ref[jnp.maximum(t - 1, 0)])

    @pl.when(jnp.logical_and(valid, new_expert))
    def _():
        _stage_to_bf16(w1_hbm.at[j, e], w1_ref, st13, wsems)
        _stage_to_bf16(w3_hbm.at[j, e], w3_ref, st13, wsems)
        _stage_to_bf16(w2_hbm.at[j, e], w2_ref, st2, wsems)

    requested = jnp.where(t == 0, valid, tv_ref[jnp.maximum(t - 1, 0)] > 0)

    @pl.when(jnp.logical_and(t == 0, valid))
    def _():
        _start_row_gather(src_ref, MOE_TG, hn_hbm, xg_scr.at[0], sems.at[0])

    @pl.when(requested)
    def _():
        _rows_copy(hn_hbm, xg_scr.at[slot], sems.at[slot], MOE_TG).wait()

    @pl.when(valid)
    def _():
        xb = xg_scr[slot].astype(BF16)
        _start_row_gather(nsrc_ref, MOE_TG, hn_hbm, xg_scr.at[1 - slot], sems.at[1 - slot], inline=True)
        y = None
        for c in range(FF_EXP // MOE_TF):
            cs = slice(c * MOE_TF, (c + 1) * MOE_TF)
            a = _silu(_dot(xb, w1_ref[:, cs])) * _dot(xb, w3_ref[:, cs])
            yc = _dot(a.astype(BF16), w2_ref[cs, :])
            y = yc if y is None else y + yc
        ys_ref[...] = y

    @pl.when(jnp.logical_not(valid))
    def _():
        ys_ref[...] = jnp.zeros_like(ys_ref)

    @pl.when(jnp.logical_and(t == MOE_NT - 1, valid))
    def _():
        _rows_copy(hn_hbm, xg_scr.at[1 - slot], sems.at[1 - slot], MOE_TG).wait()


MOE_WCH13, MOE_WCH2 = 128, 448


def _experts(tile_e, tile_v, src, hn, w1, w3, w2, j):
    hbm = pl.BlockSpec(memory_space=pl.ANY)
    grid_spec = pltpu.PrefetchScalarGridSpec(
        num_scalar_prefetch=2,
        grid=(MOE_NT,),
        in_specs=[pl.BlockSpec((1, 1, MOE_TG), lambda t, te, tv: (t, 0, 0), memory_space=pltpu.SMEM),
                  pl.BlockSpec((1, 1, MOE_TG), lambda t, te, tv: (t + 1, 0, 0), memory_space=pltpu.SMEM),
                  hbm, hbm, hbm, hbm],
        out_specs=pl.BlockSpec((MOE_TG, D), lambda t, te, tv: (t, 0)),
        scratch_shapes=[pltpu.VMEM((2, MOE_TG, D), F32),
                        pltpu.VMEM((D, FF_EXP), BF16), pltpu.VMEM((D, FF_EXP), BF16),
                        pltpu.VMEM((FF_EXP, D), BF16),
                        pltpu.VMEM((2, MOE_WCH13, FF_EXP), F32), pltpu.VMEM((2, MOE_WCH2, D), F32),
                        pltpu.SemaphoreType.DMA((2,)), pltpu.SemaphoreType.DMA((2,))],
    )
    return pl.pallas_call(
        functools.partial(_experts_body, j),
        grid_spec=grid_spec,
        out_shape=jax.ShapeDtypeStruct((MOE_ROWS, D), F32),
        compiler_params=_cp(1),
        name="moe_experts",
    )(tile_e, tile_v, src, src, hn, w1, w3, w2)


def _combine_body(final, pos_ref, npos_ref, x_ref, m_ref, route_ref, ys_hbm, *rest):
    if final:
        fg_ref, oc_ref, ol_ref, buf, sems = rest
    else:
        o_ref, buf, sems = rest
    i = pl.program_id(0)
    slot = i % 2
    n_rows = 2 * CB_TM

    @pl.when(i == 0)
    def _():
        _start_row_gather(pos_ref, n_rows, ys_hbm, buf.at[0], sems.at[0])

    @pl.when(i + 1 < pl.num_programs(0))
    def _():
        _start_row_gather(npos_ref, n_rows, ys_hbm, buf.at[1 - slot], sems.at[1 - slot], inline=True,
                          both_queues=True)

    _rows_copy(ys_hbm, buf.at[slot], sems.at[slot], n_rows).wait()
    route = route_ref[...]
    y = route[:, 4:5] * buf[slot, 0:CB_TM, :] + route[:, 5:6] * buf[slot, CB_TM:n_rows, :]
    out = x_ref[...] + m_ref[0, 5:6, :] * y
    if not final:
        o_ref[...] = out
        return
    ms = jnp.mean(out * out, axis=-1, keepdims=True)
    out = out * lax.rsqrt(ms + EPS) * fg_ref[...]
    is_ctx = i < RC // CB_TM

    @pl.when(is_ctx)
    def _():
        oc_ref[...] = out

    @pl.when(jnp.logical_not(is_ctx))
    def _():
        ol_ref[...] = out


def _combine(pos, x, mods_l, route, ys, final_g=None):
    n_tiles = R // CB_TM
    n_ctx = RC // CB_TM
    final = final_g is not None
    in_specs = [pl.BlockSpec((1, 1, 2 * CB_TM), lambda i: (i, 0, 0), memory_space=pltpu.SMEM),
                pl.BlockSpec((1, 1, 2 * CB_TM), lambda i: (jnp.minimum(i + 1, n_tiles - 1), 0, 0),
                             memory_space=pltpu.SMEM),
                pl.BlockSpec((CB_TM, D), lambda i: (i, 0)),
                pl.BlockSpec((1, 6, D), lambda i: (_mod_group(i * CB_TM), 0, 0)),
                pl.BlockSpec((CB_TM, 128), lambda i: (i, 0)),
                pl.BlockSpec(memory_space=pl.ANY)]
    args = [pos, pos, x, mods_l, route, ys]
    if final:
        in_specs.append(pl.BlockSpec((1, D), lambda i: (0, 0)))
        args.append(final_g)
        out_specs = [pl.BlockSpec((CB_TM, D), lambda i: (jnp.minimum(i, n_ctx - 1), 0)),
                     pl.BlockSpec((CB_TM, D), lambda i: (jnp.maximum(i - n_ctx, 0), 0))]
        out_shape = [jax.ShapeDtypeStruct((RC, D), F32), jax.ShapeDtypeStruct((RL, D), F32)]
    else:
        out_specs = pl.BlockSpec((CB_TM, D), lambda i: (i, 0))
        out_shape = jax.ShapeDtypeStruct((R, D), F32)
    return pl.pallas_call(
        functools.partial(_combine_body, final),
        grid=(n_tiles,),
        in_specs=in_specs,
        out_specs=out_specs,
        out_shape=out_shape,
        scratch_shapes=[pltpu.VMEM((2, 2 * CB_TM, D), F32), pltpu.SemaphoreType.DMA((2,))],
        compiler_params=_cp(1),
        name="moe_combine",
    )(*args)


def _moe(x, mods_l, g, rw128, rb128, w1, w3, w2, j, final_g=None):
    hn, route, cnt = _router(x, mods_l, g, rw128, rb128)
    eid = route[:, 0:2].astype(jnp.int32)
    rank = route[:, 2:4].astype(jnp.int32)
    counts = cnt[0, :N_EXP].astype(jnp.int32)
    gsize = (counts + MOE_TG - 1) // MOE_TG * MOE_TG
    gend = jnp.cumsum(gsize)
    pos = (gend - gsize)[eid] + rank
    tile_start = jnp.arange(MOE_NT, dtype=jnp.int32) * MOE_TG
    tile_e = jnp.minimum(jnp.sum(tile_start[:, None] >= gend[None, :], axis=1), N_EXP - 1).astype(jnp.int32)
    tile_v = (tile_start < gend[-1]).astype(jnp.int32)
    tok = jnp.broadcast_to(jnp.arange(R, dtype=jnp.int32)[:, None], (R, 2))
    src = jnp.zeros((MOE_ROWS + MOE_TG,), jnp.int32).at[pos.reshape(-1)].set(
        tok.reshape(-1), unique_indices=True, mode="promise_in_bounds")
    ys = _experts(tile_e, tile_v, src.reshape(MOE_NT + 1, 1, MOE_TG), hn, w1, w3, w2, j)
    pos_t = pos.reshape(R // CB_TM, CB_TM, 2).transpose(0, 2, 1).reshape(R // CB_TM, 1, 2 * CB_TM)
    return _combine(pos_t, x, mods_l, route, ys, final_g)


def _rope_tables(n_tokens, rot_dim):
    t = np.arange(n_tokens)
    row = (t // GRID_W).astype(np.float32)
    col = (t % GRID_W).astype(np.float32)
    n_freq = rot_dim // 4
    inv = (ROPE_THETA ** (-jnp.arange(n_freq, dtype=F32) / n_freq))
    ang = jnp.concatenate([jnp.asarray(row)[:, None] * inv, jnp.asarray(col)[:, None] * inv], axis=-1)
    cos, sin = jnp.cos(ang), jnp.sin(ang)
    return jnp.concatenate([cos, cos], axis=-1), jnp.concatenate([-sin, sin], axis=-1)


def kernel(x_prompt, x_sample, c, cache_gqa_k, cache_gqa_v, cache_mla_ckv, cache_mla_krope, state_delta, c_ctx, w_mod, b_mod, norm1_g, norm2_g, w_in, dn_conv_w, dn_a_log, dn_dt_bias, dn_norm_g, gqa_q_norm, gqa_k_norm, mla_q_norm, mla_kv_norm, mla_w_uq, mla_w_ukv, w_branch, w_out, ffd_w1, ffd_w3, ffd_w2, router_w, router_b, moe_w1, moe_w3, moe_w2, final_g):
    x = (x_prompt.reshape(RC, D), x_sample.reshape(RL, D))
    cond8 = jnp.concatenate([c_ctx[None, :], c, jnp.zeros((3, D), F32)], axis=0)
    mods = _mods(cond8, w_mod, b_mod).reshape(DEPTH, 8, 6, D)

    cg, sg = _rope_tables(DEC_SEQ, GQA_HD)
    gqa_rope = (jnp.tile(cg, (1, GQA_H)), jnp.tile(sg, (1, GQA_H)),
                jnp.tile(cg, (1, GQA_KV)), jnp.tile(sg, (1, GQA_KV)))
    cm, sm_ = _rope_tables(DEC_SEQ, MLA_ROPE)
    padk = lambda t: jnp.pad(t, ((0, 0), (16, 128 - 16 - MLA_ROPE)))
    mla_rope = (jnp.tile(cm, (1, MLA_H)), jnp.tile(sm_, (1, MLA_H)),
                jnp.pad(cm, ((0, 0), (16, 128 - 16 - MLA_ROPE)), constant_values=1.0), padk(sm_))

    w_in_bf, w_branch_bf, w_out_bf = w_in.astype(BF16), w_branch.astype(BF16), w_out.astype(BF16)
    gqa_cache = (cache_gqa_k.reshape(DEC_BATCH, DEPTH, PAST, 128), cache_gqa_v.reshape(DEC_BATCH, DEPTH, PAST, 128))

    new_k, new_v, new_ckv, new_kr, new_s = [], [], [], [], []
    for l in range(DEPTH):
        pbg, proj = _inproj(x, mods[l], norm1_g[l][None, :], w_in_bf, l)

        pad128 = lambda v: jnp.pad(v.reshape(1, 8), ((0, 0), (8, 112)))
        padt = lambda v: jnp.pad(v.reshape(8, 1), ((8, 0), (0, 0)))
        sm_t = proj[:, C_SM:C_SM + 16].T
        dn_u, dn_w, dn_qg, dn_kd, dn_ai, dn_bg = _dn_prep(
            proj, sm_t, dn_conv_w[l], pad128(dn_a_log[l]), pad128(dn_dt_bias[l]),
            padt(dn_a_log[l]), padt(dn_dt_bias[l]))
        *dn_o, s_c = _dn_scan(dn_u, dn_w, dn_qg, dn_kd, dn_ai, dn_bg, state_delta, l)

        qw = jnp.tile(gqa_q_norm[l][None, :], (1, GQA_H))
        kw = jnp.tile(gqa_k_norm[l][None, :], (1, GQA_KV))
        o_g_c, kn_c = _gqa(proj, qw, kw, SEQ, SEQ, BATCH, 0, NS=CTX_NS)
        (o_g_l,) = _gqa(proj, qw, kw, DEC_SEQ, 256, DEC_BATCH, RC, rope=gqa_rope, cache=gqa_cache, layer=l)

        wq = mla_w_uq[l].reshape(MLA_QL, MLA_H, MLA_DQ)
        wq = jnp.concatenate([wq[:, :, :MLA_NOPE].reshape(MLA_QL, -1), wq[:, :, MLA_NOPE:].reshape(MLA_QL, -1)],
                             axis=1).astype(BF16)
        wkv = mla_w_ukv[l].reshape(MLA_KVL, MLA_H, MLA_NOPE + MLA_V)
        wkv = jnp.concatenate([wkv[:, :, :MLA_NOPE].reshape(MLA_KVL, -1), wkv[:, :, MLA_NOPE:].reshape(MLA_KVL, -1)],
                              axis=1).astype(BF16)
        mqw, mkvw = mla_q_norm[l][None, :], mla_kv_norm[l][None, :]
        o_m_c, ckv_c = _mla(proj, wq, wkv, mqw, mkvw, SEQ, SEQ, BATCH, 0, NS=CTX_NS)
        (o_m_l,) = _mla(proj, wq, wkv, mqw, mkvw, DEC_SEQ, 256, DEC_BATCH, RC, rope=mla_rope,
                        cache=(cache_mla_ckv, cache_mla_krope), layer=l)
        x = _merge(x, mods[l], dn_o, proj, pbg, o_g_c, o_g_l, o_m_c, o_m_l, dn_norm_g[l][None, :],
                   w_branch_bf, w_out_bf, l)

        j = l // 2
        if l % 2 == 0:
            x = _ffn(x, mods[l], norm2_g[l][None, :], ffd_w1, ffd_w3, ffd_w2, j)
        else:
            rw128 = jnp.pad(router_w[j], ((0, 0), (0, 128 - N_EXP)))
            rb128 = jnp.pad(router_b[j][None, :], ((0, 0), (0, 128 - N_EXP)), constant_values=-jnp.inf)
            x = _moe(x, mods[l], norm2_g[l][None, :], rw128, rb128, moe_w1, moe_w3, moe_w2, j,
                     final_g=final_g[None, :] if l == DEPTH - 1 else None)

        new_k.append(kn_c.reshape(BATCH, SEQ, GQA_KV, GQA_HD))
        new_v.append(proj[:RC, C_GKV + 128:C_GKV + 256].reshape(BATCH, SEQ, GQA_KV, GQA_HD))
        new_ckv.append(ckv_c.reshape(BATCH, SEQ, MLA_KVL))
        new_kr.append(proj[:RC, C_SM + 16:C_SM + 16 + MLA_ROPE].reshape(BATCH, SEQ, MLA_ROPE))
        new_s.append(s_c)

    y_c, y_l = x
    return (y_c.reshape(BATCH, SEQ, D), y_l.reshape(DEC_BATCH, DEC_SEQ, D),
            jnp.stack(new_k, axis=1), jnp.stack(new_v, axis=1), jnp.stack(new_ckv, axis=1),
            jnp.stack(new_kr, axis=1), jnp.stack(new_s, axis=1))
```

```python
import functools
import math

import jax
import jax.numpy as jnp
import numpy as np
from jax import lax
from jax.experimental import pallas as pl
from jax.experimental.pallas import tpu as pltpu

F32 = jnp.float32
BF16 = jnp.bfloat16

D = 1024
BATCH, SEQ = 32, 256
DEC_BATCH, DEC_SEQ = 4, 2048
DEPTH = 4
PAST = 512
GRID_W = 64
ROPE_THETA = 10000.0
EPS = 1e-6
DN_H, DN_DK, DN_DV, DN_C = 4, 128, 128, 64
GQA_H, GQA_KV, GQA_HD = 8, 2, 64
MLA_H, MLA_QL, MLA_KVL, MLA_NOPE, MLA_ROPE, MLA_V = 8, 256, 128, 64, 32, 64
FF_DENSE, N_EXP, FF_EXP = 2816, 8, 3584

assert DEPTH % 2 == 0
RC = BATCH * SEQ
RL = DEC_BATCH * DEC_SEQ
R = RC + RL

N_BG = 3 * D
C_QKV, C_DG, C_GQ, C_GKV, C_MCQ, C_MCKV, C_SM = 0, 1536, 2048, 2560, 2816, 3072, 3200
N_PROJ = 3328
NP = N_BG + N_PROJ

VMEM_LIMIT = 56 * 1024 * 1024


def _cp(n_grid):
    return pltpu.CompilerParams(dimension_semantics=("arbitrary",) * n_grid,
                                vmem_limit_bytes=VMEM_LIMIT)


def _dot(a, b):
    return jnp.dot(a, b, preferred_element_type=F32)


def _bdot(a, b):
    return jnp.dot(a.astype(BF16), b.astype(BF16), preferred_element_type=F32)


def _bdot_nt(a, b):
    return lax.dot_general(a.astype(BF16), b.astype(BF16), (((1,), (1,)), ((), ())),
                           preferred_element_type=F32)


def _split2(a):
    hi = a.astype(BF16)
    lo = (a - hi.astype(F32)).astype(BF16)
    return hi, lo


def _split3(a):
    a1 = a.astype(BF16)
    r1 = a - a1.astype(F32)
    a2 = r1.astype(BF16)
    a3 = (r1 - a2.astype(F32)).astype(BF16)
    return a1, a2, a3


def _dot3(a, b):
    ah, al = _split2(a)
    bh, bl = _split2(b)
    return _dot(ah, bh) + (_dot(ah, bl) + _dot(al, bh))


def _mask_dot_r(mask_bf, g):
    g1, g2, g3 = _split3(g)
    return _dot(mask_bf, g1) + (_dot(mask_bf, g2) + _dot(mask_bf, g3))


def _mask_dot_l(g, mask_bf):
    g1, g2, g3 = _split3(g)
    return _dot(g1, mask_bf) + (_dot(g2, mask_bf) + _dot(g3, mask_bf))


def _sigmoid(x):
    return 1.0 / (1.0 + jnp.exp(-x))


def _silu(x):
    return x * _sigmoid(x)


def _softplus(x):
    return jnp.maximum(x, 0.0) + jnp.log1p(jnp.exp(-jnp.abs(x)))


def _mod_group(row0):
    return jnp.where(row0 < RC, 0, 1 + (row0 - RC) // DEC_SEQ)


MODS_TN = 1536


def _mods_body(c_ref, w_ref, b_ref, o_ref):
    s = _silu(c_ref[...])
    o_ref[0] = _bdot(s, w_ref[0]) + b_ref[0]


def _mods(cond8, w_mod, b_mod):
    nj = 6 * D // MODS_TN
    return pl.pallas_call(
        _mods_body,
        grid=(DEPTH, nj),
        in_specs=[pl.BlockSpec((8, D), lambda l, j: (0, 0)),
                  pl.BlockSpec((1, D, MODS_TN), lambda l, j: (l, 0, j)),
                  pl.BlockSpec((1, 1, MODS_TN), lambda l, j: (l, 0, j))],
        out_specs=pl.BlockSpec((1, 8, MODS_TN), lambda l, j: (l, 0, j)),
        out_shape=jax.ShapeDtypeStruct((DEPTH, 8, 6 * D), F32),
        compiler_params=_cp(2),
        name="mods",
    )(cond8, w_mod, b_mod.reshape(DEPTH, 1, 6 * D))


IN_TM = 512
IN_CHUNK = 1536


def _modnorm(x, g, shift, scale):
    ms = jnp.mean(x * x, axis=-1, keepdims=True)
    y = x * lax.rsqrt(ms + EPS) * g
    return y * (1.0 + scale) + shift


W_IN_PIECES = ((3248, 6320), (0, 2048), (2064, 3216), (2048, 2064), (3216, 3248))
W_IN_COLS = 6320


def _row_specs(x, tm):
    if not isinstance(x, tuple):
        return [pl.BlockSpec((tm, D), lambda i: (i, 0))], [x]
    n_ctx = RC // tm
    return ([pl.BlockSpec((tm, D), lambda i: (jnp.minimum(i, n_ctx - 1), 0)),
             pl.BlockSpec((tm, D), lambda i: (jnp.maximum(i - n_ctx, 0), 0))], list(x))


def _row_tile(x_refs, tm):
    if len(x_refs) == 1:
        return x_refs[0][...]
    return jnp.where(pl.program_id(0) < RC // tm, x_refs[0][...], x_refs[1][...])


IN_WCH = 128


def _inproj_body(l, n_x, *refs):
    x_refs = refs[:n_x]
    m_ref, g_ref, w_hbm, bg_ref, o_ref, w_ref, stage, wsems = refs[n_x:]

    @pl.when(pl.program_id(0) == 0)
    def _():
        n = D // IN_WCH
        copy = lambda c: pltpu.make_async_copy(w_hbm.at[l, pl.ds(c * IN_WCH, IN_WCH)], stage.at[c % 2],
                                               wsems.at[c % 2])
        copy(0).start()
        for c in range(n):
            if c + 1 < n:
                copy(c + 1).start()
            copy(c).wait()
            rows = slice(c * IN_WCH, (c + 1) * IN_WCH)
            at = 0
            for a, b in W_IN_PIECES:
                w_ref[rows, at:at + (b - a)] = stage[c % 2, :, a:b].astype(BF16)
                at += b - a
            w_ref[rows, at:NP] = jnp.zeros((IN_WCH, NP - at), BF16)

    h = _modnorm(_row_tile(x_refs, IN_TM), g_ref[...], m_ref[0, 0:1, :], m_ref[0, 1:2, :]).astype(BF16)
    for c0 in range(0, N_BG, IN_CHUNK):
        bg_ref[:, c0:c0 + IN_CHUNK] = _dot(h, w_ref[:, c0:c0 + IN_CHUNK]).astype(BF16)
    for c0 in range(0, N_PROJ, IN_CHUNK):
        c1 = min(c0 + IN_CHUNK, N_PROJ)
        o_ref[:, c0:c1] = _dot(h, w_ref[:, N_BG + c0:N_BG + c1])


def _inproj(x, mods_l, g, w_in, l):
    x_specs, x_args = _row_specs(x, IN_TM)
    return pl.pallas_call(
        functools.partial(_inproj_body, l, len(x_args)),
        grid=(R // IN_TM,),
        in_specs=x_specs + [
                  pl.BlockSpec((1, 6, D), lambda i: (_mod_group(i * IN_TM), 0, 0)),
                  pl.BlockSpec((1, D), lambda i: (0, 0)),
                  pl.BlockSpec(memory_space=pl.ANY)],
        out_specs=[pl.BlockSpec((IN_TM, N_BG), lambda i: (i, 0)),
                   pl.BlockSpec((IN_TM, N_PROJ), lambda i: (i, 0))],
        out_shape=[jax.ShapeDtypeStruct((R, N_BG), BF16), jax.ShapeDtypeStruct((R, N_PROJ), F32)],
        scratch_shapes=[pltpu.VMEM((D, NP), BF16), pltpu.VMEM((2, IN_WCH, W_IN_COLS), F32),
                        pltpu.SemaphoreType.DMA((2,))],
        compiler_params=_cp(1),
        name="inproj",
    )(*x_args, mods_l, g, w_in)


PREP_TM = 256
PREP_NC = PREP_TM // DN_C
PREP_LOCKSTEP = 4


def _prep_body(x_ref, xp_ref, xn_ref, sm_ref, smt_ref, cw_ref, al_ref, dt_ref, alt_ref, dtt_ref,
               u_o, w_o, qg_o, kd_o, ai_o, bg_o, qkv_o, gr_o):
    i = pl.program_id(0)
    n_ctx = RC // PREP_TM
    per_seq = DEC_SEQ // PREP_TM
    is_ctx = i < n_ctx
    j = (i - n_ctx) % per_seq
    first = jnp.logical_or(is_ctx, j == 0)
    last = jnp.logical_or(is_ctx, j == per_seq - 1)

    x = x_ref[...]
    prev_row = jnp.where(first, 0.0, xp_ref[7:8, :])
    next_row = jnp.where(last, 0.0, xn_ref[0:1, :])
    row = lax.broadcasted_iota(jnp.int32, (PREP_TM, 1), 0)
    xm = jnp.where(row == 0, prev_row, pltpu.roll(x, 1, axis=0))
    xq = jnp.where(row == PREP_TM - 1, next_row, pltpu.roll(x, PREP_TM - 1, axis=0))
    w = cw_ref[...]
    y = _silu(w[0:1] * xm + w[1:2] * x + w[2:3] * xq)

    for h in range(DN_H):
        qh = y[:, h * DN_DK:(h + 1) * DN_DK]
        qn = qh * lax.rsqrt(jnp.sum(qh * qh, axis=-1, keepdims=True) + EPS) * (DN_DK ** -0.5)
        qkv_o[:, h * DN_DK:(h + 1) * DN_DK] = qn
        kh = y[:, 512 + h * DN_DK:512 + (h + 1) * DN_DK]
        kn = kh * lax.rsqrt(jnp.sum(kh * kh, axis=-1, keepdims=True) + EPS)
        qkv_o[:, 512 + h * DN_DK:512 + (h + 1) * DN_DK] = kn
    qkv_o[:, 1024:1536] = y[:, 1024:1536]

    r = lax.broadcasted_iota(jnp.int32, (PREP_TM, PREP_TM), 0)
    c = lax.broadcasted_iota(jnp.int32, (PREP_TM, PREP_TM), 1)
    same = (r // DN_C) == (c // DN_C)
    low = jnp.where(jnp.logical_and(same, r >= c), 1.0, 0.0).astype(BF16)
    upp = jnp.where(jnp.logical_and(same, r <= c), 1.0, 0.0).astype(BF16)

    sm = sm_ref[...]
    beta = _sigmoid(sm)
    g = -jnp.exp(al_ref[...]) * _softplus(sm + dt_ref[...])
    gc_f = _mask_dot_r(low, g)
    gc_b = _mask_dot_r(upp, g)
    lane = lax.broadcasted_iota(jnp.int32, (1, 128), 1)
    bg_o[...] = jnp.where(lane < 8, beta, jnp.where(lane < 12, gc_f, jnp.where(lane < 16, gc_b, 0.0)))

    gt = -jnp.exp(alt_ref[...]) * _softplus(smt_ref[...] + dtt_ref[...])
    gct_f = _mask_dot_l(gt, upp)
    gct_b = _mask_dot_l(gt, low)
    sub = lax.broadcasted_iota(jnp.int32, (16, 1), 0)
    gct = jnp.where(sub < 12, gct_f, gct_b)
    for k in range(PREP_NC):
        gr_o[k] = gct[8:16, k * DN_C:(k + 1) * DN_C]

    rr = lax.broadcasted_iota(jnp.int32, (DN_C, DN_C), 0)
    cc = lax.broadcasted_iota(jnp.int32, (DN_C, DN_C), 1)

    def chunk_group(gi, carry):
        prob = [(cj, d, h) for cj in range(PREP_LOCKSTEP) for d in range(2) for h in range(DN_H)]
        ci = [gi * PREP_LOCKSTEP + cj for cj in range(PREP_LOCKSTEP)]
        rows_c = [pl.ds(pl.multiple_of(c_ * DN_C, DN_C), DN_C) for c_ in ci]
        bgc = [bg_o[r_, :] for r_ in rows_c]
        grow_all = [gr_o[c_] for c_ in ci]
        rows = [rows_c[cj] for cj, d, h in prob]
        q = [qkv_o[rows_c[cj], h * DN_DK:(h + 1) * DN_DK] for cj, d, h in prob]
        k = [qkv_o[rows_c[cj], 512 + h * DN_DK:512 + (h + 1) * DN_DK] for cj, d, h in prob]
        v = [qkv_o[rows_c[cj], 1024 + h * DN_DV:1024 + (h + 1) * DN_DV] for cj, d, h in prob]
        beta = [bgc[cj][:, d * DN_H + h:d * DN_H + h + 1] for cj, d, h in prob]
        gcol = [bgc[cj][:, 8 + d * DN_H + h:9 + d * DN_H + h] for cj, d, h in prob]
        grow = [grow_all[cj][d * DN_H + h:d * DN_H + h + 1, :] for cj, d, h in prob]
        incl = [(rr >= cc) if d == 0 else (rr <= cc) for cj, d, h in prob]
        strict = [(rr > cc) if d == 0 else (rr < cc) for cj, d, h in prob]
        n = len(prob)
        decay = [jnp.exp(jnp.where(incl[i], gcol[i] - grow[i], -1e30)) for i in range(n)]
        kb = [k[i] * beta[i] for i in range(n)]
        kk = [_bdot_nt(kb[i], k[i]) for i in range(n)]
        qk = [_bdot_nt(q[i], k[i]) for i in range(n)]
        a = [jnp.where(strict[i], kk[i] * decay[i], 0.0) for i in range(n)]
        blk = lambda b: (rr // b) == (cc // b)
        eye = jnp.where(rr == cc, 1.0, 0.0)
        p = [jnp.where(blk(8), a[i], 0.0) for i in range(n)]
        t = [eye - p[i] for i in range(n)]
        for _ in range(2):
            p = [_bdot(p[i], p[i]) for i in range(n)]
            t = [t[i] + _bdot(t[i], p[i]) for i in range(n)]
        for b in (16, 32, 64):
            m = jnp.logical_and(blk(b), jnp.logical_not(blk(b // 2)))
            tl = [_bdot(t[i], jnp.where(m, a[i], 0.0)) for i in range(n)]
            t = [t[i] - _bdot(tl[i], t[i]) for i in range(n)]
        egc = [jnp.exp(gcol[i]) for i in range(n)]
        glast = [gcol[i][DN_C - 1:DN_C, :] if prob[i][1] == 0 else gcol[i][0:1, :] for i in range(n)]
        uw = [_bdot(t[i], jnp.concatenate([v[i] * beta[i], kb[i] * egc[i]], axis=1)) for i in range(n)]
        for i, (cj, d, h) in enumerate(prob):
            u_o[d, rows[i], h * DN_DV:(h + 1) * DN_DV] = uw[i][:, 0:DN_DV]
            w_o[d, rows[i], h * DN_DK:(h + 1) * DN_DK] = uw[i][:, DN_DV:DN_DV + DN_DK].astype(BF16)
            qg_o[d, rows[i], h * DN_DK:(h + 1) * DN_DK] = (q[i] * egc[i]).astype(BF16)
            kd_o[d, rows[i], h * DN_DK:(h + 1) * DN_DK] = (k[i] * jnp.exp(glast[i] - gcol[i])).astype(BF16)
            ai_o[d, rows[i], h * DN_C:(h + 1) * DN_C] = (qk[i] * decay[i]).astype(BF16)
        return carry

    lax.fori_loop(0, PREP_NC // PREP_LOCKSTEP, chunk_group, 0)


def _dn_prep(proj, sm_t, conv_w, al128, dt128, al_t, dt_t):
    nb8 = R // 8
    qb = C_QKV // 1536
    wide = lambda dt: jax.ShapeDtypeStruct((2, R, DN_H * DN_DK), dt)
    wide_spec = pl.BlockSpec((2, PREP_TM, DN_H * DN_DK), lambda i: (0, i, 0))
    return pl.pallas_call(
        _prep_body,
        grid=(R // PREP_TM,),
        in_specs=[pl.BlockSpec((PREP_TM, 1536), lambda i: (i, qb)),
                  pl.BlockSpec((8, 1536), lambda i: (jnp.maximum(i * (PREP_TM // 8) - 1, 0), qb)),
                  pl.BlockSpec((8, 1536), lambda i: (jnp.minimum((i + 1) * (PREP_TM // 8), nb8 - 1), qb)),
                  pl.BlockSpec((PREP_TM, 128), lambda i: (i, C_SM // 128)),
                  pl.BlockSpec((16, PREP_TM), lambda i: (0, i)),
                  pl.BlockSpec((3, 1536), lambda i: (0, 0)),
                  pl.BlockSpec((1, 128), lambda i: (0, 0)),
                  pl.BlockSpec((1, 128), lambda i: (0, 0)),
                  pl.BlockSpec((16, 1), lambda i: (0, 0)),
                  pl.BlockSpec((16, 1), lambda i: (0, 0))],
        out_specs=[wide_spec, wide_spec, wide_spec, wide_spec,
                   pl.BlockSpec((2, PREP_TM, DN_H * DN_C), lambda i: (0, i, 0)),
                   pl.BlockSpec((PREP_TM, 128), lambda i: (i, 0))],
        out_shape=[wide(F32), wide(BF16), wide(BF16), wide(BF16),
                   jax.ShapeDtypeStruct((2, R, DN_H * DN_C), BF16),
                   jax.ShapeDtypeStruct((R, 128), F32)],
        scratch_shapes=[pltpu.VMEM((PREP_TM, 1536), F32), pltpu.VMEM((PREP_NC, 8, DN_C), F32)],
        compiler_params=_cp(1),
        name="dn_prep",
    )(proj, proj, proj, proj, sm_t, conv_w, al128, dt128, al_t, dt_t)


SCAN_STEPS = RC // PREP_TM
SCAN_PER = DEC_SEQ // PREP_TM
assert RL // PREP_TM == SCAN_STEPS and SEQ == PREP_TM


def _scan_lat_bwd_block(i):
    return (i // SCAN_PER) * SCAN_PER + (SCAN_PER - 1 - i % SCAN_PER)


def _scan_body(*refs):
    streams = [refs[6 * k:6 * k + 6] for k in range(4)]
    s0_ref = refs[24]
    out_refs = refs[25:29]
    so_ref, s_scr = refs[29], refs[30]
    j = pl.program_id(0) % SCAN_PER

    for h in range(DN_H):
        s_scr[0, h] = jnp.zeros((DN_DK, DN_DV), F32)
        s_scr[1, h] = jnp.zeros((DN_DK, DN_DV), F32)

    @pl.when(j == 0)
    def _():
        for h in range(DN_H):
            s_scr[2, h] = s0_ref[0, 0, 0, h]
            s_scr[3, h] = s0_ref[0, 0, 1, h]

    prob = [(k, h) for k in range(4) for h in range(DN_H)]
    tn = (((0,), (0,)), ((), ()))

    def step(n, carry):
        rows = [pl.ds(pl.multiple_of((n if k % 2 == 0 else PREP_NC - 1 - n) * DN_C, DN_C), DN_C)
                for k in range(4)]
        bgc = [streams[k][5][rows[k], :] for k in range(4)]
        cols = [slice(h * DN_DK, (h + 1) * DN_DK) for k, h in prob]
        st = [s_scr[k, h] for k, h in prob]
        stb = [x.astype(BF16) for x in st]
        ws = [_dot(streams[k][1][0, rows[k], cols[i]], stb[i]) for i, (k, h) in enumerate(prob)]
        qs = [_dot(streams[k][2][0, rows[k], cols[i]], stb[i]) for i, (k, h) in enumerate(prob)]
        vb = [(streams[k][0][0, rows[k], cols[i]] - ws[i]).astype(BF16) for i, (k, h) in enumerate(prob)]
        av = [_dot(streams[k][4][0, rows[k], h * DN_C:(h + 1) * DN_C], vb[i]) for i, (k, h) in enumerate(prob)]
        kv = [lax.dot_general(streams[k][3][0, rows[k], cols[i]], vb[i], tn, preferred_element_type=F32)
              for i, (k, h) in enumerate(prob)]
        for i, (k, h) in enumerate(prob):
            d = k % 2
            gcol = bgc[k][:, 8 + d * DN_H + h:9 + d * DN_H + h]
            glast = gcol[DN_C - 1:DN_C, :] if d == 0 else gcol[0:1, :]
            s_scr[k, h] = st[i] * jnp.exp(glast) + kv[i]
            out_refs[k][rows[k], cols[i]] = (qs[i] + av[i]).astype(out_refs[k].dtype)
        return carry

    lax.fori_loop(0, PREP_NC, step, 0)

    for d in range(2):
        for h in range(DN_H):
            so_ref[0, d, h] = s_scr[d, h]


def _dn_scan(u, w, qg, kd, ai, bg, s0, layer):
    row_of = (lambda i: i, lambda i: i,
              lambda i: SCAN_STEPS + i, lambda i: SCAN_STEPS + _scan_lat_bwd_block(i))
    in_specs, args = [], []
    for k in range(4):
        d, rb = k % 2, row_of[k]
        wide = pl.BlockSpec((1, PREP_TM, DN_H * DN_DK), lambda i, d=d, rb=rb: (d, rb(i), 0))
        narrow = pl.BlockSpec((1, PREP_TM, DN_H * DN_C), lambda i, d=d, rb=rb: (d, rb(i), 0))
        in_specs += [wide, wide, wide, wide, narrow, pl.BlockSpec((PREP_TM, 128), lambda i, rb=rb: (rb(i), 0))]
        args += [u, w, qg, kd, ai, bg]
    state_spec = lambda m: pl.BlockSpec((1, 2, DN_H, DN_DK, DN_DV), m)
    in_specs.append(pl.BlockSpec((1, 1, 2, DN_H, DN_DK, DN_DV), lambda i: (i // SCAN_PER, layer, 0, 0, 0, 0)))
    o_spec = lambda m: pl.BlockSpec((PREP_TM, DN_H * DN_DV), m)
    o_shape = lambda n: jax.ShapeDtypeStruct((n, DN_H * DN_DV), BF16)
    return pl.pallas_call(
        _scan_body,
        grid=(SCAN_STEPS,),
        in_specs=in_specs,
        out_specs=[o_spec(lambda i: (i, 0)), o_spec(lambda i: (i, 0)), o_spec(lambda i: (i, 0)),
                   o_spec(lambda i: (_scan_lat_bwd_block(i), 0)),
                   state_spec(lambda i: (i, 0, 0, 0, 0))],
        out_shape=[o_shape(RC), o_shape(RC), o_shape(RL), o_shape(RL),
                   jax.ShapeDtypeStruct((BATCH, 2, DN_H, DN_DK, DN_DV), F32)],
        scratch_shapes=[pltpu.VMEM((4, DN_H, DN_DK, DN_DV), F32)],
        compiler_params=_cp(1),
        name="dn_scan",
    )(*args, s0)


def _group_mean_matrix(width, group):
    r = lax.broadcasted_iota(jnp.int32, (width, width), 0)
    c = lax.broadcasted_iota(jnp.int32, (width, width), 1)
    return jnp.where((r // group) == (c // group), 1.0 / group, 0.0).astype(BF16)


def _group_rmsnorm(x, w, group):
    m = _group_mean_matrix(x.shape[-1], group)
    hi, lo = _split2(x * x)
    ms = _dot(hi, m) + _dot(lo, m)
    return x * lax.rsqrt(ms + EPS) * w


def _rope(x, cos, sin_signed, group):
    width = x.shape[-1]
    half = group // 2
    lane = lax.broadcasted_iota(jnp.int32, (1, width), 1)
    swapped = jnp.where((lane % group) < half,
                        pltpu.roll(x, width - half, axis=1), pltpu.roll(x, half, axis=1))
    return x * cos + swapped * sin_signed


def _attention_units(n, scores, values):
    outs = []
    s_next = scores(0)
    for u in range(n):
        s = s_next
        if u + 1 < n:
            s_next = scores(u + 1)
        m = jnp.max(s, axis=-1, keepdims=True)
        p = jnp.exp(s - m)
        l = jnp.sum(p, axis=-1, keepdims=True)
        outs.append(_dot(p.astype(BF16), values(u)) / l)
    return outs


GQA_G = GQA_H // GQA_KV
CTX_NS = 2


def _gqa_body(T, TQ, NS, latent, *refs):
    it = iter(refs)
    q_ref, kv_ref, qw_ref, kw_ref = next(it), next(it), next(it), next(it)
    if latent:
        cq_ref, sq_ref, ck_ref, sk_ref, kc_ref, vc_ref = (next(it) for _ in range(6))
    o_ref = next(it)
    kn_ref = None if latent else next(it)
    k_scr, v_scr = next(it), next(it)

    @pl.when(pl.program_id(1) == 0)
    def _():
        kv = kv_ref[...]
        k = _group_rmsnorm(kv[:, 0:128], kw_ref[...], GQA_HD)
        v = kv[:, 128:256]
        if latent:
            k = _rope(k, ck_ref[...], sk_ref[...], GQA_HD)
        else:
            kn_ref[...] = k
        for s in range(NS):
            for g in range(GQA_KV):
                sg = s * GQA_KV + g
                k_scr[sg, 0:T, :] = k[s * T:(s + 1) * T, g * GQA_HD:(g + 1) * GQA_HD].astype(BF16)
                v_scr[sg, 0:T, :] = v[s * T:(s + 1) * T, g * GQA_HD:(g + 1) * GQA_HD].astype(BF16)
                if latent:
                    k_scr[sg, T:T + PAST, :] = kc_ref[0, 0, :, g * GQA_HD:(g + 1) * GQA_HD].astype(BF16)
                    v_scr[sg, T:T + PAST, :] = vc_ref[0, 0, :, g * GQA_HD:(g + 1) * GQA_HD].astype(BF16)

    q = _group_rmsnorm(q_ref[...], qw_ref[...], GQA_HD)
    if latent:
        q = _rope(q, cq_ref[...], sq_ref[...], GQA_HD)
    q = q * (GQA_HD ** -0.5)
    head = lambda s, hh: q[s * TQ:(s + 1) * TQ, hh * GQA_HD:(hh + 1) * GQA_HD]
    units = [(s, u) for s in range(NS) for u in range(GQA_H // 2)]
    qu = [jnp.concatenate([head(s, 2 * u), head(s, 2 * u + 1)], axis=0) for s, u in units]
    kv_of = lambda i: units[i][0] * GQA_KV + (2 * units[i][1]) // GQA_G
    outs = _attention_units(len(units), lambda i: _bdot_nt(qu[i], k_scr[kv_of(i)]), lambda i: v_scr[kv_of(i)])
    for (s, u), o in zip(units, outs):
        for j in range(2):
            hh = 2 * u + j
            o_ref[s * TQ:(s + 1) * TQ, hh * GQA_HD:(hh + 1) * GQA_HD] = o[j * TQ:(j + 1) * TQ, :].astype(o_ref.dtype)


def _gqa(proj, qw, kw, T, TQ, n_seq, row_off, rope=None, cache=None, NS=1, layer=0):
    latent = rope is not None
    tk = T + (PAST if latent else 0)
    nq = T // TQ
    assert NS == 1 or (nq == 1 and not latent)
    n_seq, TQ, TB = n_seq // NS, NS * TQ, NS * T
    qo = row_off // TQ
    so = row_off // TB
    in_specs = [pl.BlockSpec((TQ, 512), lambda i, j: (qo + i * nq + j, C_GQ // 512)),
                pl.BlockSpec((TB, 256), lambda i, j: (so + i, C_GKV // 256)),
                pl.BlockSpec((1, 512), lambda i, j: (0, 0)),
                pl.BlockSpec((1, 128), lambda i, j: (0, 0))]
    args = [proj, proj, qw, kw]
    if latent:
        cq, sq, ck, sk = rope
        kc, vc = cache
        in_specs += [pl.BlockSpec((TQ, 512), lambda i, j: (j, 0)),
                     pl.BlockSpec((TQ, 512), lambda i, j: (j, 0)),
                     pl.BlockSpec((T, 128), lambda i, j: (0, 0)),
                     pl.BlockSpec((T, 128), lambda i, j: (0, 0)),
                     pl.BlockSpec((1, 1, PAST, 128), lambda i, j: (i, layer, 0, 0)),
                     pl.BlockSpec((1, 1, PAST, 128), lambda i, j: (i, layer, 0, 0))]
        args += [cq, sq, ck, sk, kc, vc]
    out_specs = [pl.BlockSpec((TQ, 512), lambda i, j: (i * nq + j, 0))]
    out_shape = [jax.ShapeDtypeStruct((n_seq * TB, 512), BF16)]
    if not latent:
        out_specs.append(pl.BlockSpec((TB, 128), lambda i, j: (i, 0)))
        out_shape.append(jax.ShapeDtypeStruct((n_seq * TB, 128), F32))
    return pl.pallas_call(
        functools.partial(_gqa_body, T, TQ // NS, NS, latent),
        grid=(n_seq, nq),
        in_specs=in_specs,
        out_specs=out_specs,
        out_shape=out_shape,
        scratch_shapes=[pltpu.VMEM((NS * GQA_KV, tk, GQA_HD), BF16), pltpu.VMEM((NS * GQA_KV, tk, GQA_HD), BF16)],
        compiler_params=_cp(2),
        name="gqa_%d" % T,
    )(*args)


MLA_DQ = MLA_NOPE + MLA_ROPE
MLA_KV_ROWS = 512


def _mla_body(T, TQ, NS, latent, *refs):
    it = iter(refs)
    cq_ref, ckv_ref, sm_ref, wq_ref, wkv_ref, qw_ref, kvw_ref = (next(it) for _ in range(7))
    if latent:
        cosq_ref, sinq_ref, cosk_ref, sink_ref, cc_ref, kc_ref = (next(it) for _ in range(6))
    o_ref = next(it)
    cn_ref = None if latent else next(it)
    k_scr, v_scr = next(it), next(it)
    tk = T + (PAST if latent else 0)

    @pl.when(pl.program_id(1) == 0)
    def _():
        x = ckv_ref[...]
        ms = jnp.mean(x * x, axis=-1, keepdims=True)
        ckv = x * lax.rsqrt(ms + EPS) * kvw_ref[...]
        sm = sm_ref[...]
        if latent:
            sm = _rope_small(sm, cosk_ref[...], sink_ref[...])
        else:
            cn_ref[...] = ckv
        kr = sm[:, 16:16 + MLA_ROPE]

        def put(s, rows0, ckv_rows, kr_rows):
            n = ckv_rows.shape[0]
            kv = _bdot(ckv_rows, wkv_ref[...])
            krb = kr_rows.astype(BF16)
            for h in range(MLA_H):
                sh = s * MLA_H + h
                k_scr[sh, rows0:rows0 + n, 0:MLA_NOPE] = kv[:, h * MLA_NOPE:(h + 1) * MLA_NOPE].astype(BF16)
                k_scr[sh, rows0:rows0 + n, MLA_NOPE:MLA_DQ] = krb
                v_scr[sh, rows0:rows0 + n, :] = kv[:, 512 + h * MLA_V:512 + (h + 1) * MLA_V].astype(BF16)

        for s in range(NS):
            for r0 in range(0, T, MLA_KV_ROWS):
                r1 = min(r0 + MLA_KV_ROWS, T)
                put(s, r0, ckv[s * T + r0:s * T + r1], kr[s * T + r0:s * T + r1])
        if latent:
            put(0, T, cc_ref[0, 0], kc_ref[0, 0])

    x = cq_ref[...]
    ms = jnp.mean(x * x, axis=-1, keepdims=True)
    cq = x * lax.rsqrt(ms + EPS) * qw_ref[...]
    qf = _bdot(cq, wq_ref[...])
    qn = qf[:, 0:512]
    qr = qf[:, 512:768]
    if latent:
        qr = _rope(qr, cosq_ref[...], sinq_ref[...], MLA_ROPE)
    scale = MLA_DQ ** -0.5
    units = [(s, h) for s in range(NS) for h in range(MLA_H)]
    rows = lambda s: slice(s * TQ, (s + 1) * TQ)
    qh = [jnp.concatenate([qn[rows(s), h * MLA_NOPE:(h + 1) * MLA_NOPE],
                           qr[rows(s), h * MLA_ROPE:(h + 1) * MLA_ROPE]], axis=1) * scale for s, h in units]
    sh = lambda i: units[i][0] * MLA_H + units[i][1]
    outs = _attention_units(len(units), lambda i: _bdot_nt(qh[i], k_scr[sh(i)]), lambda i: v_scr[sh(i)])
    for (s, h), o in zip(units, outs):
        o_ref[rows(s), h * MLA_V:(h + 1) * MLA_V] = o.astype(o_ref.dtype)


def _rope_small(sm, cos, sin_signed):
    lane = lax.broadcasted_iota(jnp.int32, (1, 128), 1)
    half = MLA_ROPE // 2
    swapped = jnp.where(lane < 16 + half, pltpu.roll(sm, 128 - half, axis=1), pltpu.roll(sm, half, axis=1))
    return sm * cos + swapped * sin_signed


def _mla(proj, wq, wkv, qw, kvw, T, TQ, n_seq, row_off, rope=None, cache=None, NS=1, layer=0):
    latent = rope is not None
    tk = T + (PAST if latent else 0)
    nq = T // TQ
    assert NS == 1 or (nq == 1 and not latent)
    n_seq, TQ, TB = n_seq // NS, NS * TQ, NS * T
    qo = row_off // TQ
    so = row_off // TB
    in_specs = [pl.BlockSpec((TQ, 256), lambda i, j: (qo + i * nq + j, C_MCQ // 256)),
                pl.BlockSpec((TB, 128), lambda i, j: (so + i, C_MCKV // 128)),
                pl.BlockSpec((TB, 128), lambda i, j: (so + i, C_SM // 128)),
                pl.BlockSpec((MLA_QL, 768), lambda i, j: (0, 0)),
                pl.BlockSpec((MLA_KVL, 1024), lambda i, j: (0, 0)),
                pl.BlockSpec((1, 256), lambda i, j: (0, 0)),
                pl.BlockSpec((1, 128), lambda i, j: (0, 0))]
    args = [proj, proj, proj, wq, wkv, qw, kvw]
    if latent:
        cosq, sinq, cosk, sink = rope
        cc, kc = cache
        in_specs += [pl.BlockSpec((TQ, 256), lambda i, j: (j, 0)),
                     pl.BlockSpec((TQ, 256), lambda i, j: (j, 0)),
                     pl.BlockSpec((T, 128), lambda i, j: (0, 0)),
                     pl.BlockSpec((T, 128), lambda i, j: (0, 0)),
                     pl.BlockSpec((1, 1, PAST, 128), lambda i, j: (i, layer, 0, 0)),
                     pl.BlockSpec((1, 1, PAST, MLA_ROPE), lambda i, j: (i, layer, 0, 0))]
        args += [cosq, sinq, cosk, sink, cc, kc]
    out_specs = [pl.BlockSpec((TQ, 512), lambda i, j: (i * nq + j, 0))]
    out_shape = [jax.ShapeDtypeStruct((n_seq * TB, 512), BF16)]
    if not latent:
        out_specs.append(pl.BlockSpec((TB, 128), lambda i, j: (i, 0)))
        out_shape.append(jax.ShapeDtypeStruct((n_seq * TB, 128), F32))
    return pl.pallas_call(
        functools.partial(_mla_body, T, TQ // NS, NS, latent),
        grid=(n_seq, nq),
        in_specs=in_specs,
        out_specs=out_specs,
        out_shape=out_shape,
        scratch_shapes=[pltpu.VMEM((NS * MLA_H, tk, MLA_DQ), BF16), pltpu.VMEM((NS * MLA_H, tk, MLA_V), BF16)],
        compiler_params=_cp(2),
        name="mla_%d" % T,
    )(*args)


MG_TM = 512


def _merge_body(n_x, *refs):
    x_refs = refs[:n_x]
    (m_ref, ofc_ref, obc_ref, ofl_ref, obl_ref, dg_ref, ogc_ref, ogl_ref, omc_ref, oml_ref,
     bg_ref, ng_ref, wb_ref, wo_ref, o_ref) = refs[n_x:]
    is_ctx = pl.program_id(0) < RC // MG_TM
    og = jnp.where(is_ctx, ogc_ref[...], ogl_ref[...])
    om = jnp.where(is_ctx, omc_ref[...], oml_ref[...])
    f32 = lambda r: r[...].astype(F32)
    odn = jnp.where(is_ctx, f32(ofc_ref) + f32(obc_ref), f32(ofl_ref) + f32(obl_ref))
    dg = dg_ref[...]
    ng = ng_ref[...]
    parts = []
    for h in range(DN_H):
        oh = odn[:, h * DN_DV:(h + 1) * DN_DV]
        ms = jnp.mean(oh * oh, axis=-1, keepdims=True)
        parts.append(oh * lax.rsqrt(ms + EPS) * ng * _silu(dg[:, h * DN_DV:(h + 1) * DN_DV]))
    br0 = jnp.concatenate(parts, axis=1)
    gate = lambda n: _sigmoid(bg_ref[:, n * D:(n + 1) * D].astype(F32))
    merged = gate(0) * _bdot(br0, wb_ref[0, 0])
    merged = merged + gate(1) * _bdot(og, wb_ref[0, 1])
    merged = merged + gate(2) * _bdot(om, wb_ref[0, 2])
    out = _bdot(merged, wo_ref[0])
    o_ref[...] = _row_tile(x_refs, MG_TM) + m_ref[0, 2:3, :] * out


def _merge(x, mods_l, dn_o, proj, pbg, og_c, og_l, om_c, om_l, ng, wb, wo, l):
    row = lambda i: (i, 0)
    n_ctx = RC // MG_TM
    ctx_row = lambda i: (jnp.minimum(i, n_ctx - 1), 0)
    lat_row = lambda i: (jnp.maximum(i - n_ctx, 0), 0)
    x_specs, x_args = _row_specs(x, MG_TM)
    return pl.pallas_call(
        functools.partial(_merge_body, len(x_args)),
        grid=(R // MG_TM,),
        in_specs=x_specs + [
                  pl.BlockSpec((1, 6, D), lambda i: (_mod_group(i * MG_TM), 0, 0)),
                  pl.BlockSpec((MG_TM, 512), ctx_row),
                  pl.BlockSpec((MG_TM, 512), ctx_row),
                  pl.BlockSpec((MG_TM, 512), lat_row),
                  pl.BlockSpec((MG_TM, 512), lat_row),
                  pl.BlockSpec((MG_TM, 512), lambda i: (i, C_DG // 512)),
                  pl.BlockSpec((MG_TM, 512), ctx_row),
                  pl.BlockSpec((MG_TM, 512), lat_row),
                  pl.BlockSpec((MG_TM, 512), ctx_row),
                  pl.BlockSpec((MG_TM, 512), lat_row),
                  pl.BlockSpec((MG_TM, N_BG), row),
                  pl.BlockSpec((1, DN_DV), lambda i: (0, 0)),
                  pl.BlockSpec((1, 3, 512, D), lambda i: (l, 0, 0, 0)),
                  pl.BlockSpec((1, D, D), lambda i: (l, 0, 0))],
        out_specs=pl.BlockSpec((MG_TM, D), row),
        out_shape=jax.ShapeDtypeStruct((R, D), F32),
        compiler_params=_cp(1),
        name="merge",
    )(*x_args, mods_l, *dn_o, proj, og_c, og_l, om_c, om_l, pbg, ng, wb, wo)


FF_TM = 512
FF_CHUNKS = ((0, 1536), (1536, FF_DENSE))


FF_WCH13, FF_WCH2 = 128, 352


def _ffn_body(j, x_ref, m_ref, g_ref, w1_hbm, w3_hbm, w2_hbm, o_ref, w1_ref, w3_ref, w2_ref, st13, st2, wsems):
    @pl.when(pl.program_id(0) == 0)
    def _():
        _stage_to_bf16(w1_hbm.at[j], w1_ref, st13, wsems)
        _stage_to_bf16(w3_hbm.at[j], w3_ref, st13, wsems)
        _stage_to_bf16(w2_hbm.at[j], w2_ref, st2, wsems)

    x = x_ref[...]
    h = _modnorm(x, g_ref[...], m_ref[0, 3:4, :], m_ref[0, 4:5, :]).astype(BF16)
    y = None
    for c0, c1 in FF_CHUNKS:
        a = _silu(_dot(h, w1_ref[:, c0:c1])) * _dot(h, w3_ref[:, c0:c1])
        yc = _dot(a.astype(BF16), w2_ref[c0:c1, :])
        y = yc if y is None else y + yc
    o_ref[...] = x + m_ref[0, 5:6, :] * y


def _ffn(x, mods_l, g, w1, w3, w2, j):
    hbm = pl.BlockSpec(memory_space=pl.ANY)
    return pl.pallas_call(
        functools.partial(_ffn_body, j),
        grid=(R // FF_TM,),
        in_specs=[pl.BlockSpec((FF_TM, D), lambda i: (i, 0)),
                  pl.BlockSpec((1, 6, D), lambda i: (_mod_group(i * FF_TM), 0, 0)),
                  pl.BlockSpec((1, D), lambda i: (0, 0)),
                  hbm, hbm, hbm],
        out_specs=pl.BlockSpec((FF_TM, D), lambda i: (i, 0)),
        out_shape=jax.ShapeDtypeStruct((R, D), F32),
        scratch_shapes=[pltpu.VMEM((D, FF_DENSE), BF16), pltpu.VMEM((D, FF_DENSE), BF16),
                        pltpu.VMEM((FF_DENSE, D), BF16),
                        pltpu.VMEM((2, FF_WCH13, FF_DENSE), F32), pltpu.VMEM((2, FF_WCH2, D), F32),
                        pltpu.SemaphoreType.DMA((2,))],
        compiler_params=_cp(1),
        name="ffn",
    )(x, mods_l, g, w1, w3, w2)


RT_TM = 512
MOE_TG, MOE_TF = 512, 1792
MOE_NT = 2 * R // MOE_TG + N_EXP
MOE_ROWS = MOE_NT * MOE_TG
CB_TM = 256


def _router_body(x_ref, m_ref, g_ref, rw_ref, rb_ref, hn_ref, route_ref, cnt_ref, base_scr):
    @pl.when(pl.program_id(0) == 0)
    def _():
        base_scr[...] = jnp.zeros_like(base_scr)

    h = _modnorm(x_ref[...], g_ref[...], m_ref[0, 3:4, :], m_ref[0, 4:5, :])
    hn_ref[...] = h
    logits = _dot3(h, rw_ref[...]) + rb_ref[...]
    lane = lax.broadcasted_iota(jnp.int32, logits.shape, 1)
    m1 = jnp.max(logits, axis=-1, keepdims=True)
    i1 = jnp.min(jnp.where(logits == m1, lane, 128), axis=-1, keepdims=True)
    sel1 = lane == i1
    rest = jnp.where(sel1, -jnp.inf, logits)
    m2 = jnp.max(rest, axis=-1, keepdims=True)
    i2 = jnp.min(jnp.where(rest == m2, lane, 128), axis=-1, keepdims=True)
    sel2 = lane == i2
    e2 = jnp.exp(m2 - m1)
    p1 = 1.0 / (1.0 + e2)
    p2 = e2 / (1.0 + e2)

    cnt = jnp.where(jnp.logical_or(sel1, sel2), 1.0, 0.0)
    r = lax.broadcasted_iota(jnp.int32, (RT_TM, RT_TM), 0)
    c = lax.broadcasted_iota(jnp.int32, (RT_TM, RT_TM), 1)
    before = jnp.where(r > c, 1.0, 0.0).astype(BF16)
    seen = base_scr[...] + _dot(before, cnt.astype(BF16))
    rank1 = jnp.sum(jnp.where(sel1, seen, 0.0), axis=-1, keepdims=True)
    rank2 = jnp.sum(jnp.where(sel2, seen, 0.0), axis=-1, keepdims=True)
    vals = (i1.astype(F32), i2.astype(F32), rank1, rank2, p1, p2)
    route = jnp.zeros(logits.shape, F32)
    for k, val in enumerate(vals):
        route = jnp.where(lane == k, val, route)
    route_ref[...] = route
    base_scr[...] += jnp.sum(cnt, axis=0, keepdims=True)
    cnt_ref[...] = base_scr[...]


def _router(x, mods_l, g, rw128, rb128):
    return pl.pallas_call(
        _router_body,
        grid=(R // RT_TM,),
        in_specs=[pl.BlockSpec((RT_TM, D), lambda i: (i, 0)),
                  pl.BlockSpec((1, 6, D), lambda i: (_mod_group(i * RT_TM), 0, 0)),
                  pl.BlockSpec((1, D), lambda i: (0, 0)),
                  pl.BlockSpec((D, 128), lambda i: (0, 0)),
                  pl.BlockSpec((1, 128), lambda i: (0, 0))],
        out_specs=[pl.BlockSpec((RT_TM, D), lambda i: (i, 0)),
                   pl.BlockSpec((RT_TM, 128), lambda i: (i, 0)),
                   pl.BlockSpec((1, 128), lambda i: (0, 0))],
        out_shape=[jax.ShapeDtypeStruct((R, D), F32),
                   jax.ShapeDtypeStruct((R, 128), F32),
                   jax.ShapeDtypeStruct((1, 128), F32)],
        scratch_shapes=[pltpu.VMEM((1, 128), F32)],
        compiler_params=_cp(1),
        name="moe_router",
    )(x, mods_l, g, rw128, rb128)


def _rows_copy(src_hbm, dst, sem, n):
    return pltpu.make_async_copy(src_hbm.at[pl.ds(0, n)], dst, sem)


def _start_row_gather(idx_ref, n, src_hbm, dst, sem, inline=False, both_queues=False):
    def issue(i, priority):
        pltpu.make_async_copy(src_hbm.at[pl.ds(idx_ref[0, 0, i], 1)], dst.at[pl.ds(i, 1)], sem).start(
            priority=priority)

    if inline:
        for i in range(n):
            issue(i, i % 2 if both_queues else 0)
    else:
        def body(i, carry):
            issue(i, 0)
            return carry

        lax.fori_loop(0, n, body, 0, unroll=8)


def _stage_to_bf16(src_hbm, dst, stage, sems):
    ch = stage.shape[1]
    n = src_hbm.shape[0] // ch
    copy = lambda c: pltpu.make_async_copy(src_hbm.at[pl.ds(c * ch, ch)], stage.at[c % 2], sems.at[c % 2])
    copy(0).start()
    for c in range(n):
        if c + 1 < n:
            copy(c + 1).start()
        copy(c).wait()
        dst[c * ch:(c + 1) * ch, :] = stage[c % 2].astype(BF16)


def _experts_body(j, te_ref, tv_ref, src_ref, nsrc_ref, hn_hbm, w1_hbm, w3_hbm, w2_hbm, ys_ref,
                  xg_scr, w1_ref, w3_ref, w2_ref, st13, st2, sems, wsems):
    t = pl.program_id(0)
    slot = t % 2
    valid = tv_ref[t] > 0
    e = te_ref[t]
    new_expert = jnp.logical_or(t == 0, e != te_ref[jnp.maximum(t - 1, 0)])

    @pl.when(jnp.logical_and(valid, new_expert))
    def _():
        _stage_to_bf16(w1_hbm.at[j, e], w1_ref, st13, wsems)
        _stage_to_bf16(w3_hbm.at[j, e], w3_ref, st13, wsems)
        _stage_to_bf16(w2_hbm.at[j, e], w2_ref, st2, wsems)

    requested = jnp.where(t == 0, valid, tv_ref[jnp.maximum(t - 1, 0)] > 0)

    @pl.when(jnp.logical_and(t == 0, valid))
    def _():
        _start_row_gather(src_ref, MOE_TG, hn_hbm, xg_scr.at[0], sems.at[0])

    @pl.when(requested)
    def _():
        _rows_copy(hn_hbm, xg_scr.at[slot], sems.at[slot], MOE_TG).wait()

    @pl.when(valid)
    def _():
        xb = xg_scr[slot].astype(BF16)
        _start_row_gather(nsrc_ref, MOE_TG, hn_hbm, xg_scr.at[1 - slot], sems.at[1 - slot], inline=True)
        y = None
        for c in range(FF_EXP // MOE_TF):
            cs = slice(c * MOE_TF, (c + 1) * MOE_TF)
            a = _silu(_dot(xb, w1_ref[:, cs])) * _dot(xb, w3_ref[:, cs])
            yc = _dot(a.astype(BF16), w2_ref[cs, :])
            y = yc if y is None else y + yc
        ys_ref[...] = y

    @pl.when(jnp.logical_not(valid))
    def _():
        ys_ref[...] = jnp.zeros_like(ys_ref)

    @pl.when(jnp.logical_and(t == MOE_NT - 1, valid))
    def _():
        _rows_copy(hn_hbm, xg_scr.at[1 - slot], sems.at[1 - slot], MOE_TG).wait()


MOE_WCH13, MOE_WCH2 = 128, 448


def _experts(tile_e, tile_v, src, hn, w1, w3, w2, j):
    hbm = pl.BlockSpec(memory_space=pl.ANY)
    grid_spec = pltpu.PrefetchScalarGridSpec(
        num_scalar_prefetch=2,
        grid=(MOE_NT,),
        in_specs=[pl.BlockSpec((1, 1, MOE_TG), lambda t, te, tv: (t, 0, 0), memory_space=pltpu.SMEM),
                  pl.BlockSpec((1, 1, MOE_TG), lambda t, te, tv: (t + 1, 0, 0), memory_space=pltpu.SMEM),
                  hbm, hbm, hbm, hbm],
        out_specs=pl.BlockSpec((MOE_TG, D), lambda t, te, tv: (t, 0)),
        scratch_shapes=[pltpu.VMEM((2, MOE_TG, D), F32),
                        pltpu.VMEM((D, FF_EXP), BF16), pltpu.VMEM((D, FF_EXP), BF16),
                        pltpu.VMEM((FF_EXP, D), BF16),
                        pltpu.VMEM((2, MOE_WCH13, FF_EXP), F32), pltpu.VMEM((2, MOE_WCH2, D), F32),
                        pltpu.SemaphoreType.DMA((2,)), pltpu.SemaphoreType.DMA((2,))],
    )
    return pl.pallas_call(
        functools.partial(_experts_body, j),
        grid_spec=grid_spec,
        out_shape=jax.ShapeDtypeStruct((MOE_ROWS, D), F32),
        compiler_params=_cp(1),
        name="moe_experts",
    )(tile_e, tile_v, src, src, hn, w1, w3, w2)


def _combine_body(final, pos_ref, npos_ref, x_ref, m_ref, route_ref, ys_hbm, *rest):
    if final:
        fg_ref, oc_ref, ol_ref, buf, sems = rest
    else:
        o_ref, buf, sems = rest
    i = pl.program_id(0)
    slot = i % 2
    n_rows = 2 * CB_TM

    @pl.when(i == 0)
    def _():
        _start_row_gather(pos_ref, n_rows, ys_hbm, buf.at[0], sems.at[0])

    @pl.when(i + 1 < pl.num_programs(0))
    def _():
        _start_row_gather(npos_ref, n_rows, ys_hbm, buf.at[1 - slot], sems.at[1 - slot], inline=True,
                          both_queues=True)

    _rows_copy(ys_hbm, buf.at[slot], sems.at[slot], n_rows).wait()
    route = route_ref[...]
    y = route[:, 4:5] * buf[slot, 0:CB_TM, :] + route[:, 5:6] * buf[slot, CB_TM:n_rows, :]
    out = x_ref[...] + m_ref[0, 5:6, :] * y
    if not final:
        o_ref[...] = out
        return
    ms = jnp.mean(out * out, axis=-1, keepdims=True)
    out = out * lax.rsqrt(ms + EPS) * fg_ref[...]
    is_ctx = i < RC // CB_TM

    @pl.when(is_ctx)
    def _():
        oc_ref[...] = out

    @pl.when(jnp.logical_not(is_ctx))
    def _():
        ol_ref[...] = out


def _combine(pos, x, mods_l, route, ys, final_g=None):
    n_tiles = R // CB_TM
    n_ctx = RC // CB_TM
    final = final_g is not None
    in_specs = [pl.BlockSpec((1, 1, 2 * CB_TM), lambda i: (i, 0, 0), memory_space=pltpu.SMEM),
                pl.BlockSpec((1, 1, 2 * CB_TM), lambda i: (jnp.minimum(i + 1, n_tiles - 1), 0, 0),
                             memory_space=pltpu.SMEM),
                pl.BlockSpec((CB_TM, D), lambda i: (i, 0)),
                pl.BlockSpec((1, 6, D), lambda i: (_mod_group(i * CB_TM), 0, 0)),
                pl.BlockSpec((CB_TM, 128), lambda i: (i, 0)),
                pl.BlockSpec(memory_space=pl.ANY)]
    args = [pos, pos, x, mods_l, route, ys]
    if final:
        in_specs.append(pl.BlockSpec((1, D), lambda i: (0, 0)))
        args.append(final_g)
        out_specs = [pl.BlockSpec((CB_TM, D), lambda i: (jnp.minimum(i, n_ctx - 1), 0)),
                     pl.BlockSpec((CB_TM, D), lambda i: (jnp.maximum(i - n_ctx, 0), 0))]
        out_shape = [jax.ShapeDtypeStruct((RC, D), F32), jax.ShapeDtypeStruct((RL, D), F32)]
    else:
        out_specs = pl.BlockSpec((CB_TM, D), lambda i: (i, 0))
        out_shape = jax.ShapeDtypeStruct((R, D), F32)
    return pl.pallas_call(
        functools.partial(_combine_body, final),
        grid=(n_tiles,),
        in_specs=in_specs,
        out_specs=out_specs,
        out_shape=out_shape,
        scratch_shapes=[pltpu.VMEM((2, 2 * CB_TM, D), F32), pltpu.SemaphoreType.DMA((2,))],
        compiler_params=_cp(1),
        name="moe_combine",
    )(*args)


def _moe(x, mods_l, g, rw128, rb128, w1, w3, w2, j, final_g=None):
    hn, route, cnt = _router(x, mods_l, g, rw128, rb128)
    eid = route[:, 0:2].astype(jnp.int32)
    rank = route[:, 2:4].astype(jnp.int32)
    counts = cnt[0, :N_EXP].astype(jnp.int32)
    gsize = (counts + MOE_TG - 1) // MOE_TG * MOE_TG
    gend = jnp.cumsum(gsize)
    pos = (gend - gsize)[eid] + rank
    tile_start = jnp.arange(MOE_NT, dtype=jnp.int32) * MOE_TG
    tile_e = jnp.minimum(jnp.sum(tile_start[:, None] >= gend[None, :], axis=1), N_EXP - 1).astype(jnp.int32)
    tile_v = (tile_start < gend[-1]).astype(jnp.int32)
    tok = jnp.broadcast_to(jnp.arange(R, dtype=jnp.int32)[:, None], (R, 2))
    src = jnp.zeros((MOE_ROWS + MOE_TG,), jnp.int32).at[pos.reshape(-1)].set(
        tok.reshape(-1), unique_indices=True, mode="promise_in_bounds")
    ys = _experts(tile_e, tile_v, src.reshape(MOE_NT + 1, 1, MOE_TG), hn, w1, w3, w2, j)
    pos_t = pos.reshape(R // CB_TM, CB_TM, 2).transpose(0, 2, 1).reshape(R // CB_TM, 1, 2 * CB_TM)
    return _combine(pos_t, x, mods_l, route, ys, final_g)


def _rope_tables(n_tokens, rot_dim):
    t = np.arange(n_tokens)
    row = (t // GRID_W).astype(np.float32)
    col = (t % GRID_W).astype(np.float32)
    n_freq = rot_dim // 4
    inv = (ROPE_THETA ** (-jnp.arange(n_freq, dtype=F32) / n_freq))
    ang = jnp.concatenate([jnp.asarray(row)[:, None] * inv, jnp.asarray(col)[:, None] * inv], axis=-1)
    cos, sin = jnp.cos(ang), jnp.sin(ang)
    return jnp.concatenate([cos, cos], axis=-1), jnp.concatenate([-sin, sin], axis=-1)


def kernel(x_prompt, x_sample, c, cache_gqa_k, cache_gqa_v, cache_mla_ckv, cache_mla_krope, state_delta, c_ctx, w_mod, b_mod, norm1_g, norm2_g, w_in, dn_conv_w, dn_a_log, dn_dt_bias, dn_norm_g, gqa_q_norm, gqa_k_norm, mla_q_norm, mla_kv_norm, mla_w_uq, mla_w_ukv, w_branch, w_out, ffd_w1, ffd_w3, ffd_w2, router_w, router_b, moe_w1, moe_w3, moe_w2, final_g):
    x = (x_prompt.reshape(RC, D), x_sample.reshape(RL, D))
    cond8 = jnp.concatenate([c_ctx[None, :], c, jnp.zeros((3, D), F32)], axis=0)
    mods = _mods(cond8, w_mod, b_mod).reshape(DEPTH, 8, 6, D)

    cg, sg = _rope_tables(DEC_SEQ, GQA_HD)
    gqa_rope = (jnp.tile(cg, (1, GQA_H)), jnp.tile(sg, (1, GQA_H)),
                jnp.tile(cg, (1, GQA_KV)), jnp.tile(sg, (1, GQA_KV)))
    cm, sm_ = _rope_tables(DEC_SEQ, MLA_ROPE)
    padk = lambda t: jnp.pad(t, ((0, 0), (16, 128 - 16 - MLA_ROPE)))
    mla_rope = (jnp.tile(cm, (1, MLA_H)), jnp.tile(sm_, (1, MLA_H)),
                jnp.pad(cm, ((0, 0), (16, 128 - 16 - MLA_ROPE)), constant_values=1.0), padk(sm_))

    w_branch_bf, w_out_bf = w_branch.astype(BF16), w_out.astype(BF16)
    gqa_cache = (cache_gqa_k.reshape(DEC_BATCH, DEPTH, PAST, 128), cache_gqa_v.reshape(DEC_BATCH, DEPTH, PAST, 128))

    new_k, new_v, new_ckv, new_kr, new_s = [], [], [], [], []
    for l in range(DEPTH):
        pbg, proj = _inproj(x, mods[l], norm1_g[l][None, :], w_in, l)

        pad128 = lambda v: jnp.pad(v.reshape(1, 8), ((0, 0), (8, 112)))
        padt = lambda v: jnp.pad(v.reshape(8, 1), ((8, 0), (0, 0)))
        sm_t = proj[:, C_SM:C_SM + 16].T
        dn_u, dn_w, dn_qg, dn_kd, dn_ai, dn_bg = _dn_prep(
            proj, sm_t, dn_conv_w[l], pad128(dn_a_log[l]), pad128(dn_dt_bias[l]),
            padt(dn_a_log[l]), padt(dn_dt_bias[l]))
        *dn_o, s_c = _dn_scan(dn_u, dn_w, dn_qg, dn_kd, dn_ai, dn_bg, state_delta, l)

        qw = jnp.tile(gqa_q_norm[l][None, :], (1, GQA_H))
        kw = jnp.tile(gqa_k_norm[l][None, :], (1, GQA_KV))
        o_g_c, kn_c = _gqa(proj, qw, kw, SEQ, SEQ, BATCH, 0, NS=CTX_NS)
        (o_g_l,) = _gqa(proj, qw, kw, DEC_SEQ, 256, DEC_BATCH, RC, rope=gqa_rope, cache=gqa_cache, layer=l)

        wq = mla_w_uq[l].reshape(MLA_QL, MLA_H, MLA_DQ)
        wq = jnp.concatenate([wq[:, :, :MLA_NOPE].reshape(MLA_QL, -1), wq[:, :, MLA_NOPE:].reshape(MLA_QL, -1)],
                             axis=1).astype(BF16)
        wkv = mla_w_ukv[l].reshape(MLA_KVL, MLA_H, MLA_NOPE + MLA_V)
        wkv = jnp.concatenate([wkv[:, :, :MLA_NOPE].reshape(MLA_KVL, -1), wkv[:, :, MLA_NOPE:].reshape(MLA_KVL, -1)],
                              axis=1).astype(BF16)
        mqw, mkvw = mla_q_norm[l][None, :], mla_kv_norm[l][None, :]
        o_m_c, ckv_c = _mla(proj, wq, wkv, mqw, mkvw, SEQ, SEQ, BATCH, 0, NS=CTX_NS)
        (o_m_l,) = _mla(proj, wq, wkv, mqw, mkvw, DEC_SEQ, 256, DEC_BATCH, RC, rope=mla_rope,
                        cache=(cache_mla_ckv, cache_mla_krope), layer=l)
        x = _merge(x, mods[l], dn_o, proj, pbg, o_g_c, o_g_l, o_m_c, o_m_l, dn_norm_g[l][None, :],
                   w_branch_bf, w_out_bf, l)

        j = l // 2
        if l % 2 == 0:
            x = _ffn(x, mods[l], norm2_g[l][None, :], ffd_w1, ffd_w3, ffd_w2, j)
        else:
            rw128 = jnp.pad(router_w[j], ((0, 0), (0, 128 - N_EXP)))
            rb128 = jnp.pad(router_b[j][None, :], ((0, 0), (0, 128 - N_EXP)), constant_values=-jnp.inf)
            x = _moe(x, mods[l], norm2_g[l][None, :], rw128, rb128, moe_w1, moe_w3, moe_w2, j,
                     final_g=final_g[None, :] if l == DEPTH - 1 else None)

        new_k.append(kn_c.reshape(BATCH, SEQ, GQA_KV, GQA_HD))
        new_v.append(proj[:RC, C_GKV + 128:C_GKV + 256].reshape(BATCH, SEQ, GQA_KV, GQA_HD))
        new_ckv.append(ckv_c.reshape(BATCH, SEQ, MLA_KVL))
        new_kr.append(proj[:RC, C_SM + 16:C_SM + 16 + MLA_ROPE].reshape(BATCH, SEQ, MLA_ROPE))
        new_s.append(s_c)

    y_c, y_l = x
    return (y_c.reshape(BATCH, SEQ, D), y_l.reshape(DEC_BATCH, DEC_SEQ, D),
            jnp.stack(new_k, axis=1), jnp.stack(new_v, axis=1), jnp.stack(new_ckv, axis=1),
            jnp.stack(new_kr, axis=1), jnp.stack(new_s, axis=1))
```

```python
import functools
import math

import jax
import jax.numpy as jnp
import numpy as np
from jax import lax
from jax.experimental import pallas as pl
from jax.experimental.pallas import tpu as pltpu

F32 = jnp.float32
BF16 = jnp.bfloat16

D = 1024
BATCH, SEQ = 32, 256
DEC_BATCH, DEC_SEQ = 4, 2048
DEPTH = 4
PAST = 512
GRID_W = 64
ROPE_THETA = 10000.0
EPS = 1e-6
DN_H, DN_DK, DN_DV, DN_C = 4, 128, 128, 64
GQA_H, GQA_KV, GQA_HD = 8, 2, 64
MLA_H, MLA_QL, MLA_KVL, MLA_NOPE, MLA_ROPE, MLA_V = 8, 256, 128, 64, 32, 64
FF_DENSE, N_EXP, FF_EXP = 2816, 8, 3584

assert DEPTH % 2 == 0
RC = BATCH * SEQ
RL = DEC_BATCH * DEC_SEQ
R = RC + RL

N_BG = 3 * D
C_QKV, C_DG, C_GQ, C_GKV, C_MCQ, C_MCKV, C_SM = 0, 1536, 2048, 2560, 2816, 3072, 3200
N_PROJ = 3328
NP = N_BG + N_PROJ

VMEM_LIMIT = 56 * 1024 * 1024


def _cp(n_grid):
    return pltpu.CompilerParams(dimension_semantics=("arbitrary",) * n_grid,
                                vmem_limit_bytes=VMEM_LIMIT)


def _dot(a, b):
    return jnp.dot(a, b, preferred_element_type=F32)


def _bdot(a, b):
    return jnp.dot(a.astype(BF16), b.astype(BF16), preferred_element_type=F32)


def _bdot_nt(a, b):
    return lax.dot_general(a.astype(BF16), b.astype(BF16), (((1,), (1,)), ((), ())),
                           preferred_element_type=F32)


def _split2(a):
    hi = a.astype(BF16)
    lo = (a - hi.astype(F32)).astype(BF16)
    return hi, lo


def _split3(a):
    a1 = a.astype(BF16)
    r1 = a - a1.astype(F32)
    a2 = r1.astype(BF16)
    a3 = (r1 - a2.astype(F32)).astype(BF16)
    return a1, a2, a3


def _dot3(a, b):
    ah, al = _split2(a)
    bh, bl = _split2(b)
    return _dot(ah, bh) + (_dot(ah, bl) + _dot(al, bh))


def _mask_dot_r(mask_bf, g):
    g1, g2, g3 = _split3(g)
    return _dot(mask_bf, g1) + (_dot(mask_bf, g2) + _dot(mask_bf, g3))


def _mask_dot_l(g, mask_bf):
    g1, g2, g3 = _split3(g)
    return _dot(g1, mask_bf) + (_dot(g2, mask_bf) + _dot(g3, mask_bf))


def _sigmoid(x):
    return 1.0 / (1.0 + jnp.exp(-x))


def _silu(x):
    return x * _sigmoid(x)


def _softplus(x):
    return jnp.maximum(x, 0.0) + jnp.log1p(jnp.exp(-jnp.abs(x)))


def _mod_group(row0):
    return jnp.where(row0 < RC, 0, 1 + (row0 - RC) // DEC_SEQ)


MODS_TN = 1536


def _mods_body(c_ref, w_ref, b_ref, o_ref):
    s = _silu(c_ref[...])
    o_ref[0] = _bdot(s, w_ref[0]) + b_ref[0]


def _mods(cond8, w_mod, b_mod):
    nj = 6 * D // MODS_TN
    return pl.pallas_call(
        _mods_body,
        grid=(DEPTH, nj),
        in_specs=[pl.BlockSpec((8, D), lambda l, j: (0, 0)),
                  pl.BlockSpec((1, D, MODS_TN), lambda l, j: (l, 0, j)),
                  pl.BlockSpec((1, 1, MODS_TN), lambda l, j: (l, 0, j))],
        out_specs=pl.BlockSpec((1, 8, MODS_TN), lambda l, j: (l, 0, j)),
        out_shape=jax.ShapeDtypeStruct((DEPTH, 8, 6 * D), F32),
        compiler_params=_cp(2),
        name="mods",
    )(cond8, w_mod, b_mod.reshape(DEPTH, 1, 6 * D))


IN_TM = 512
IN_CHUNK = 1536


def _modnorm(x, g, shift, scale):
    ms = jnp.mean(x * x, axis=-1, keepdims=True)
    y = x * lax.rsqrt(ms + EPS) * g
    return y * (1.0 + scale) + shift


W_IN_PIECES = ((3248, 6320), (0, 2048), (2064, 3216), (2048, 2064), (3216, 3248))
W_IN_COLS = 6320


def _row_specs(x, tm):
    if not isinstance(x, tuple):
        return [pl.BlockSpec((tm, D), lambda i: (i, 0))], [x]
    n_ctx = RC // tm
    return ([pl.BlockSpec((tm, D), lambda i: (jnp.minimum(i, n_ctx - 1), 0)),
             pl.BlockSpec((tm, D), lambda i: (jnp.maximum(i - n_ctx, 0), 0))], list(x))


def _row_tile(x_refs, tm):
    if len(x_refs) == 1:
        return x_refs[0][...]
    return jnp.where(pl.program_id(0) < RC // tm, x_refs[0][...], x_refs[1][...])


def _inproj_body(n_x, *refs):
    x_refs = refs[:n_x]
    m_ref, g_ref, wsrc_ref, bg_ref, o_ref, w_ref = refs[n_x:]

    @pl.when(pl.program_id(0) == 0)
    def _():
        at = 0
        for a, b in W_IN_PIECES:
            w_ref[:, at:at + (b - a)] = wsrc_ref[0, :, a:b]
            at += b - a
        w_ref[:, at:NP] = jnp.zeros((D, NP - at), BF16)

    h = _modnorm(_row_tile(x_refs, IN_TM), g_ref[...], m_ref[0, 0:1, :], m_ref[0, 1:2, :]).astype(BF16)
    for c0 in range(0, N_BG, IN_CHUNK):
        bg_ref[:, c0:c0 + IN_CHUNK] = _dot(h, w_ref[:, c0:c0 + IN_CHUNK]).astype(BF16)
    for c0 in range(0, N_PROJ, IN_CHUNK):
        c1 = min(c0 + IN_CHUNK, N_PROJ)
        o_ref[:, c0:c1] = _dot(h, w_ref[:, N_BG + c0:N_BG + c1])


def _inproj(x, mods_l, g, w_bf, l):
    x_specs, x_args = _row_specs(x, IN_TM)
    return pl.pallas_call(
        functools.partial(_inproj_body, len(x_args)),
        grid=(R // IN_TM,),
        in_specs=x_specs + [
                  pl.BlockSpec((1, 6, D), lambda i: (_mod_group(i * IN_TM), 0, 0)),
                  pl.BlockSpec((1, D), lambda i: (0, 0)),
                  pl.BlockSpec((1, D, W_IN_COLS), lambda i: (l, 0, 0), pipeline_mode=pl.Buffered(1))],
        out_specs=[pl.BlockSpec((IN_TM, N_BG), lambda i: (i, 0)),
                   pl.BlockSpec((IN_TM, N_PROJ), lambda i: (i, 0))],
        out_shape=[jax.ShapeDtypeStruct((R, N_BG), BF16), jax.ShapeDtypeStruct((R, N_PROJ), F32)],
        scratch_shapes=[pltpu.VMEM((D, NP), BF16)],
        compiler_params=_cp(1),
        name="inproj",
    )(*x_args, mods_l, g, w_bf)


PREP_TM = 256
PREP_NC = PREP_TM // DN_C
PREP_LOCKSTEP = 4


def _prep_body(x_ref, xp_ref, xn_ref, sm_ref, smt_ref, cw_ref, al_ref, dt_ref, alt_ref, dtt_ref,
               u_o, w_o, qg_o, kd_o, ai_o, bg_o, qkv_o, gr_o):
    i = pl.program_id(0)
    n_ctx = RC // PREP_TM
    per_seq = DEC_SEQ // PREP_TM
    is_ctx = i < n_ctx
    j = (i - n_ctx) % per_seq
    first = jnp.logical_or(is_ctx, j == 0)
    last = jnp.logical_or(is_ctx, j == per_seq - 1)

    x = x_ref[...]
    prev_row = jnp.where(first, 0.0, xp_ref[7:8, :])
    next_row = jnp.where(last, 0.0, xn_ref[0:1, :])
    row = lax.broadcasted_iota(jnp.int32, (PREP_TM, 1), 0)
    xm = jnp.where(row == 0, prev_row, pltpu.roll(x, 1, axis=0))
    xq = jnp.where(row == PREP_TM - 1, next_row, pltpu.roll(x, PREP_TM - 1, axis=0))
    w = cw_ref[...]
    y = _silu(w[0:1] * xm + w[1:2] * x + w[2:3] * xq)

    for h in range(DN_H):
        qh = y[:, h * DN_DK:(h + 1) * DN_DK]
        qn = qh * lax.rsqrt(jnp.sum(qh * qh, axis=-1, keepdims=True) + EPS) * (DN_DK ** -0.5)
        qkv_o[:, h * DN_DK:(h + 1) * DN_DK] = qn
        kh = y[:, 512 + h * DN_DK:512 + (h + 1) * DN_DK]
        kn = kh * lax.rsqrt(jnp.sum(kh * kh, axis=-1, keepdims=True) + EPS)
        qkv_o[:, 512 + h * DN_DK:512 + (h + 1) * DN_DK] = kn
    qkv_o[:, 1024:1536] = y[:, 1024:1536]

    r = lax.broadcasted_iota(jnp.int32, (PREP_TM, PREP_TM), 0)
    c = lax.broadcasted_iota(jnp.int32, (PREP_TM, PREP_TM), 1)
    same = (r // DN_C) == (c // DN_C)
    low = jnp.where(jnp.logical_and(same, r >= c), 1.0, 0.0).astype(BF16)
    upp = jnp.where(jnp.logical_and(same, r <= c), 1.0, 0.0).astype(BF16)

    sm = sm_ref[...]
    beta = _sigmoid(sm)
    g = -jnp.exp(al_ref[...]) * _softplus(sm + dt_ref[...])
    gc_f = _mask_dot_r(low, g)
    gc_b = _mask_dot_r(upp, g)
    lane = lax.broadcasted_iota(jnp.int32, (1, 128), 1)
    bg_o[...] = jnp.where(lane < 8, beta, jnp.where(lane < 12, gc_f, jnp.where(lane < 16, gc_b, 0.0)))

    gt = -jnp.exp(alt_ref[...]) * _softplus(smt_ref[...] + dtt_ref[...])
    gct_f = _mask_dot_l(gt, upp)
    gct_b = _mask_dot_l(gt, low)
    sub = lax.broadcasted_iota(jnp.int32, (16, 1), 0)
    gct = jnp.where(sub < 12, gct_f, gct_b)
    for k in range(PREP_NC):
        gr_o[k] = gct[8:16, k * DN_C:(k + 1) * DN_C]

    rr = lax.broadcasted_iota(jnp.int32, (DN_C, DN_C), 0)
    cc = lax.broadcasted_iota(jnp.int32, (DN_C, DN_C), 1)

    def chunk_group(gi, carry):
        prob = [(cj, d, h) for cj in range(PREP_LOCKSTEP) for d in range(2) for h in range(DN_H)]
        ci = [gi * PREP_LOCKSTEP + cj for cj in range(PREP_LOCKSTEP)]
        rows_c = [pl.ds(pl.multiple_of(c_ * DN_C, DN_C), DN_C) for c_ in ci]
        bgc = [bg_o[r_, :] for r_ in rows_c]
        grow_all = [gr_o[c_] for c_ in ci]
        rows = [rows_c[cj] for cj, d, h in prob]
        q = [qkv_o[rows_c[cj], h * DN_DK:(h + 1) * DN_DK] for cj, d, h in prob]
        k = [qkv_o[rows_c[cj], 512 + h * DN_DK:512 + (h + 1) * DN_DK] for cj, d, h in prob]
        v = [qkv_o[rows_c[cj], 1024 + h * DN_DV:1024 + (h + 1) * DN_DV] for cj, d, h in prob]
        beta = [bgc[cj][:, d * DN_H + h:d * DN_H + h + 1] for cj, d, h in prob]
        gcol = [bgc[cj][:, 8 + d * DN_H + h:9 + d * DN_H + h] for cj, d, h in prob]
        grow = [grow_all[cj][d * DN_H + h:d * DN_H + h + 1, :] for cj, d, h in prob]
        incl = [(rr >= cc) if d == 0 else (rr <= cc) for cj, d, h in prob]
        strict = [(rr > cc) if d == 0 else (rr < cc) for cj, d, h in prob]
        n = len(prob)
        decay = [jnp.exp(jnp.where(incl[i], gcol[i] - grow[i], -1e30)) for i in range(n)]
        kb = [k[i] * beta[i] for i in range(n)]
        kk = [_bdot_nt(kb[i], k[i]) for i in range(n)]
        qk = [_bdot_nt(q[i], k[i]) for i in range(n)]
        a = [jnp.where(strict[i], kk[i] * decay[i], 0.0) for i in range(n)]
        blk = lambda b: (rr // b) == (cc // b)
        eye = jnp.where(rr == cc, 1.0, 0.0)
        p = [jnp.where(blk(8), a[i], 0.0) for i in range(n)]
        t = [eye - p[i] for i in range(n)]
        for _ in range(2):
            p = [_bdot(p[i], p[i]) for i in range(n)]
            t = [t[i] + _bdot(t[i], p[i]) for i in range(n)]
        for b in (16, 32, 64):
            m = jnp.logical_and(blk(b), jnp.logical_not(blk(b // 2)))
            tl = [_bdot(t[i], jnp.where(m, a[i], 0.0)) for i in range(n)]
            t = [t[i] - _bdot(tl[i], t[i]) for i in range(n)]
        egc = [jnp.exp(gcol[i]) for i in range(n)]
        glast = [gcol[i][DN_C - 1:DN_C, :] if prob[i][1] == 0 else gcol[i][0:1, :] for i in range(n)]
        uw = [_bdot(t[i], jnp.concatenate([v[i] * beta[i], kb[i] * egc[i]], axis=1)) for i in range(n)]
        for i, (cj, d, h) in enumerate(prob):
            u_o[d, rows[i], h * DN_DV:(h + 1) * DN_DV] = uw[i][:, 0:DN_DV].astype(BF16)
            w_o[d, rows[i], h * DN_DK:(h + 1) * DN_DK] = uw[i][:, DN_DV:DN_DV + DN_DK].astype(BF16)
            qg_o[d, rows[i], h * DN_DK:(h + 1) * DN_DK] = (q[i] * egc[i]).astype(BF16)
            kd_o[d, rows[i], h * DN_DK:(h + 1) * DN_DK] = (k[i] * jnp.exp(glast[i] - gcol[i])).astype(BF16)
            ai_o[d, rows[i], h * DN_C:(h + 1) * DN_C] = (qk[i] * decay[i]).astype(BF16)
        return carry

    lax.fori_loop(0, PREP_NC // PREP_LOCKSTEP, chunk_group, 0)


def _dn_prep(proj, sm_t, conv_w, al128, dt128, al_t, dt_t):
    nb8 = R // 8
    qb = C_QKV // 1536
    wide = lambda dt: jax.ShapeDtypeStruct((2, R, DN_H * DN_DK), dt)
    wide_spec = pl.BlockSpec((2, PREP_TM, DN_H * DN_DK), lambda i: (0, i, 0))
    return pl.pallas_call(
        _prep_body,
        grid=(R // PREP_TM,),
        in_specs=[pl.BlockSpec((PREP_TM, 1536), lambda i: (i, qb)),
                  pl.BlockSpec((8, 1536), lambda i: (jnp.maximum(i * (PREP_TM // 8) - 1, 0), qb)),
                  pl.BlockSpec((8, 1536), lambda i: (jnp.minimum((i + 1) * (PREP_TM // 8), nb8 - 1), qb)),
                  pl.BlockSpec((PREP_TM, 128), lambda i: (i, C_SM // 128)),
                  pl.BlockSpec((16, PREP_TM), lambda i: (0, i)),
                  pl.BlockSpec((3, 1536), lambda i: (0, 0)),
                  pl.BlockSpec((1, 128), lambda i: (0, 0)),
                  pl.BlockSpec((1, 128), lambda i: (0, 0)),
                  pl.BlockSpec((16, 1), lambda i: (0, 0)),
                  pl.BlockSpec((16, 1), lambda i: (0, 0))],
        out_specs=[wide_spec, wide_spec, wide_spec, wide_spec,
                   pl.BlockSpec((2, PREP_TM, DN_H * DN_C), lambda i: (0, i, 0)),
                   pl.BlockSpec((PREP_TM, 128), lambda i: (i, 0))],
        out_shape=[wide(BF16), wide(BF16), wide(BF16), wide(BF16),
                   jax.ShapeDtypeStruct((2, R, DN_H * DN_C), BF16),
                   jax.ShapeDtypeStruct((R, 128), F32)],
        scratch_shapes=[pltpu.VMEM((PREP_TM, 1536), F32), pltpu.VMEM((PREP_NC, 8, DN_C), F32)],
        compiler_params=_cp(1),
        name="dn_prep",
    )(proj, proj, proj, proj, sm_t, conv_w, al128, dt128, al_t, dt_t)


SCAN_STEPS = RC // PREP_TM
SCAN_PER = DEC_SEQ // PREP_TM
assert RL // PREP_TM == SCAN_STEPS and SEQ == PREP_TM


def _scan_lat_bwd_block(i):
    return (i // SCAN_PER) * SCAN_PER + (SCAN_PER - 1 - i % SCAN_PER)


def _scan_body(*refs):
    streams = [refs[6 * k:6 * k + 6] for k in range(4)]
    s0_ref = refs[24]
    out_refs = refs[25:29]
    so_ref, s_scr = refs[29], refs[30]
    j = pl.program_id(0) % SCAN_PER

    for h in range(DN_H):
        s_scr[0, h] = jnp.zeros((DN_DK, DN_DV), F32)
        s_scr[1, h] = jnp.zeros((DN_DK, DN_DV), F32)

    @pl.when(j == 0)
    def _():
        for h in range(DN_H):
            s_scr[2, h] = s0_ref[0, 0, 0, h]
            s_scr[3, h] = s0_ref[0, 0, 1, h]

    prob = [(k, h) for k in range(4) for h in range(DN_H)]
    tn = (((0,), (0,)), ((), ()))

    def step(n, carry):
        rows = [pl.ds(pl.multiple_of((n if k % 2 == 0 else PREP_NC - 1 - n) * DN_C, DN_C), DN_C)
                for k in range(4)]
        bgc = [streams[k][5][rows[k], :] for k in range(4)]
        cols = [slice(h * DN_DK, (h + 1) * DN_DK) for k, h in prob]
        st = [s_scr[k, h] for k, h in prob]
        stb = [x.astype(BF16) for x in st]
        ws = [_dot(streams[k][1][0, rows[k], cols[i]], stb[i]) for i, (k, h) in enumerate(prob)]
        qs = [_dot(streams[k][2][0, rows[k], cols[i]], stb[i]) for i, (k, h) in enumerate(prob)]
        vb = [(streams[k][0][0, rows[k], cols[i]].astype(F32) - ws[i]).astype(BF16)
              for i, (k, h) in enumerate(prob)]
        av = [_dot(streams[k][4][0, rows[k], h * DN_C:(h + 1) * DN_C], vb[i]) for i, (k, h) in enumerate(prob)]
        kv = [lax.dot_general(streams[k][3][0, rows[k], cols[i]], vb[i], tn, preferred_element_type=F32)
              for i, (k, h) in enumerate(prob)]
        for i, (k, h) in enumerate(prob):
            d = k % 2
            gcol = bgc[k][:, 8 + d * DN_H + h:9 + d * DN_H + h]
            glast = gcol[DN_C - 1:DN_C, :] if d == 0 else gcol[0:1, :]
            s_scr[k, h] = st[i] * jnp.exp(glast) + kv[i]
            out_refs[k][rows[k], cols[i]] = (qs[i] + av[i]).astype(out_refs[k].dtype)
        return carry

    lax.fori_loop(0, PREP_NC, step, 0)

    for d in range(2):
        for h in range(DN_H):
            so_ref[0, d, h] = s_scr[d, h]


def _dn_scan(u, w, qg, kd, ai, bg, s0, layer):
    row_of = (lambda i: i, lambda i: i,
              lambda i: SCAN_STEPS + i, lambda i: SCAN_STEPS + _scan_lat_bwd_block(i))
    in_specs, args = [], []
    for k in range(4):
        d, rb = k % 2, row_of[k]
        wide = pl.BlockSpec((1, PREP_TM, DN_H * DN_DK), lambda i, d=d, rb=rb: (d, rb(i), 0))
        narrow = pl.BlockSpec((1, PREP_TM, DN_H * DN_C), lambda i, d=d, rb=rb: (d, rb(i), 0))
        in_specs += [wide, wide, wide, wide, narrow, pl.BlockSpec((PREP_TM, 128), lambda i, rb=rb: (rb(i), 0))]
        args += [u, w, qg, kd, ai, bg]
    state_spec = lambda m: pl.BlockSpec((1, 2, DN_H, DN_DK, DN_DV), m)
    in_specs.append(pl.BlockSpec((1, 1, 2, DN_H, DN_DK, DN_DV), lambda i: (i // SCAN_PER, layer, 0, 0, 0, 0)))
    o_spec = lambda m: pl.BlockSpec((PREP_TM, DN_H * DN_DV), m)
    o_shape = lambda n: jax.ShapeDtypeStruct((n, DN_H * DN_DV), BF16)
    return pl.pallas_call(
        _scan_body,
        grid=(SCAN_STEPS,),
        in_specs=in_specs,
        out_specs=[o_spec(lambda i: (i, 0)), o_spec(lambda i: (i, 0)), o_spec(lambda i: (i, 0)),
                   o_spec(lambda i: (_scan_lat_bwd_block(i), 0)),
                   state_spec(lambda i: (i, 0, 0, 0, 0))],
        out_shape=[o_shape(RC), o_shape(RC), o_shape(RL), o_shape(RL),
                   jax.ShapeDtypeStruct((BATCH, 2, DN_H, DN_DK, DN_DV), F32)],
        scratch_shapes=[pltpu.VMEM((4, DN_H, DN_DK, DN_DV), F32)],
        compiler_params=_cp(1),
        name="dn_scan",
    )(*args, s0)


def _group_mean_matrix(width, group):
    r = lax.broadcasted_iota(jnp.int32, (width, width), 0)
    c = lax.broadcasted_iota(jnp.int32, (width, width), 1)
    return jnp.where((r // group) == (c // group), 1.0 / group, 0.0).astype(BF16)


def _group_rmsnorm(x, w, group):
    m = _group_mean_matrix(x.shape[-1], group)
    hi, lo = _split2(x * x)
    ms = _dot(hi, m) + _dot(lo, m)
    return x * lax.rsqrt(ms + EPS) * w


def _rope(x, cos, sin_signed, group):
    width = x.shape[-1]
    half = group // 2
    lane = lax.broadcasted_iota(jnp.int32, (1, width), 1)
    swapped = jnp.where((lane % group) < half,
                        pltpu.roll(x, width - half, axis=1), pltpu.roll(x, half, axis=1))
    return x * cos + swapped * sin_signed


def _attention_units(n, scores, values):
    outs = []
    s_next = scores(0)
    for u in range(n):
        s = s_next
        if u + 1 < n:
            s_next = scores(u + 1)
        m = jnp.max(s, axis=-1, keepdims=True)
        p = jnp.exp(s - m)
        l = jnp.sum(p, axis=-1, keepdims=True)
        outs.append(_dot(p.astype(BF16), values(u)) / l)
    return outs


GQA_G = GQA_H // GQA_KV
CTX_NS = 2


def _gqa_body(T, TQ, NS, latent, *refs):
    it = iter(refs)
    q_ref, kv_ref, qw_ref, kw_ref = next(it), next(it), next(it), next(it)
    if latent:
        cq_ref, sq_ref, ck_ref, sk_ref, kc_ref, vc_ref = (next(it) for _ in range(6))
    o_ref = next(it)
    kn_ref = None if latent else next(it)
    k_scr, v_scr = next(it), next(it)

    @pl.when(pl.program_id(1) == 0)
    def _():
        kv = kv_ref[...]
        k = _group_rmsnorm(kv[:, 0:128], kw_ref[...], GQA_HD)
        v = kv[:, 128:256]
        if latent:
            k = _rope(k, ck_ref[...], sk_ref[...], GQA_HD)
        else:
            kn_ref[...] = k
        for s in range(NS):
            for g in range(GQA_KV):
                sg = s * GQA_KV + g
                k_scr[sg, 0:T, :] = k[s * T:(s + 1) * T, g * GQA_HD:(g + 1) * GQA_HD].astype(BF16)
                v_scr[sg, 0:T, :] = v[s * T:(s + 1) * T, g * GQA_HD:(g + 1) * GQA_HD].astype(BF16)
                if latent:
                    k_scr[sg, T:T + PAST, :] = kc_ref[0, 0, :, g * GQA_HD:(g + 1) * GQA_HD].astype(BF16)
                    v_scr[sg, T:T + PAST, :] = vc_ref[0, 0, :, g * GQA_HD:(g + 1) * GQA_HD].astype(BF16)

    q = _group_rmsnorm(q_ref[...], qw_ref[...], GQA_HD)
    if latent:
        q = _rope(q, cq_ref[...], sq_ref[...], GQA_HD)
    q = q * (GQA_HD ** -0.5)
    head = lambda s, hh: q[s * TQ:(s + 1) * TQ, hh * GQA_HD:(hh + 1) * GQA_HD]
    units = [(s, u) for s in range(NS) for u in range(GQA_H // 2)]
    qu = [jnp.concatenate([head(s, 2 * u), head(s, 2 * u + 1)], axis=0) for s, u in units]
    kv_of = lambda i: units[i][0] * GQA_KV + (2 * units[i][1]) // GQA_G
    outs = _attention_units(len(units), lambda i: _bdot_nt(qu[i], k_scr[kv_of(i)]), lambda i: v_scr[kv_of(i)])
    for (s, u), o in zip(units, outs):
        for j in range(2):
            hh = 2 * u + j
            o_ref[s * TQ:(s + 1) * TQ, hh * GQA_HD:(hh + 1) * GQA_HD] = o[j * TQ:(j + 1) * TQ, :].astype(o_ref.dtype)


def _gqa(proj, qw, kw, T, TQ, n_seq, row_off, rope=None, cache=None, NS=1, layer=0):
    latent = rope is not None
    tk = T + (PAST if latent else 0)
    nq = T // TQ
    assert NS == 1 or (nq == 1 and not latent)
    n_seq, TQ, TB = n_seq // NS, NS * TQ, NS * T
    qo = row_off // TQ
    so = row_off // TB
    in_specs = [pl.BlockSpec((TQ, 512), lambda i, j: (qo + i * nq + j, C_GQ // 512)),
                pl.BlockSpec((TB, 256), lambda i, j: (so + i, C_GKV // 256)),
                pl.BlockSpec((1, 512), lambda i, j: (0, 0)),
                pl.BlockSpec((1, 128), lambda i, j: (0, 0))]
    args = [proj, proj, qw, kw]
    if latent:
        cq, sq, ck, sk = rope
        kc, vc = cache
        in_specs += [pl.BlockSpec((TQ, 512), lambda i, j: (j, 0)),
                     pl.BlockSpec((TQ, 512), lambda i, j: (j, 0)),
                     pl.BlockSpec((T, 128), lambda i, j: (0, 0)),
                     pl.BlockSpec((T, 128), lambda i, j: (0, 0)),
                     pl.BlockSpec((1, 1, PAST, 128), lambda i, j: (i, layer, 0, 0)),
                     pl.BlockSpec((1, 1, PAST, 128), lambda i, j: (i, layer, 0, 0))]
        args += [cq, sq, ck, sk, kc, vc]
    out_specs = [pl.BlockSpec((TQ, 512), lambda i, j: (i * nq + j, 0))]
    out_shape = [jax.ShapeDtypeStruct((n_seq * TB, 512), BF16)]
    if not latent:
        out_specs.append(pl.BlockSpec((TB, 128), lambda i, j: (i, 0)))
        out_shape.append(jax.ShapeDtypeStruct((n_seq * TB, 128), F32))
    return pl.pallas_call(
        functools.partial(_gqa_body, T, TQ // NS, NS, latent),
        grid=(n_seq, nq),
        in_specs=in_specs,
        out_specs=out_specs,
        out_shape=out_shape,
        scratch_shapes=[pltpu.VMEM((NS * GQA_KV, tk, GQA_HD), BF16), pltpu.VMEM((NS * GQA_KV, tk, GQA_HD), BF16)],
        compiler_params=_cp(2),
        name="gqa_%d" % T,
    )(*args)


MLA_DQ = MLA_NOPE + MLA_ROPE
MLA_KV_ROWS = 512


def _mla_body(T, TQ, NS, latent, *refs):
    it = iter(refs)
    cq_ref, ckv_ref, sm_ref, wq_ref, wkv_ref, qw_ref, kvw_ref = (next(it) for _ in range(7))
    if latent:
        cosq_ref, sinq_ref, cosk_ref, sink_ref, cc_ref, kc_ref = (next(it) for _ in range(6))
    o_ref = next(it)
    cn_ref = None if latent else next(it)
    k_scr, v_scr = next(it), next(it)
    tk = T + (PAST if latent else 0)

    @pl.when(pl.program_id(1) == 0)
    def _():
        x = ckv_ref[...]
        ms = jnp.mean(x * x, axis=-1, keepdims=True)
        ckv = x * lax.rsqrt(ms + EPS) * kvw_ref[...]
        sm = sm_ref[...]
        if latent:
            sm = _rope_small(sm, cosk_ref[...], sink_ref[...])
        else:
            cn_ref[...] = ckv
        kr = sm[:, 16:16 + MLA_ROPE]

        def put(s, rows0, ckv_rows, kr_rows):
            n = ckv_rows.shape[0]
            kv = _bdot(ckv_rows, wkv_ref[...])
            krb = kr_rows.astype(BF16)
            for h in range(MLA_H):
                sh = s * MLA_H + h
                k_scr[sh, rows0:rows0 + n, 0:MLA_NOPE] = kv[:, h * MLA_NOPE:(h + 1) * MLA_NOPE].astype(BF16)
                k_scr[sh, rows0:rows0 + n, MLA_NOPE:MLA_DQ] = krb
                v_scr[sh, rows0:rows0 + n, :] = kv[:, 512 + h * MLA_V:512 + (h + 1) * MLA_V].astype(BF16)

        for s in range(NS):
            for r0 in range(0, T, MLA_KV_ROWS):
                r1 = min(r0 + MLA_KV_ROWS, T)
                put(s, r0, ckv[s * T + r0:s * T + r1], kr[s * T + r0:s * T + r1])
        if latent:
            put(0, T, cc_ref[0, 0], kc_ref[0, 0])

    x = cq_ref[...]
    ms = jnp.mean(x * x, axis=-1, keepdims=True)
    cq = x * lax.rsqrt(ms + EPS) * qw_ref[...]
    qf = _bdot(cq, wq_ref[...])
    qn = qf[:, 0:512]
    qr = qf[:, 512:768]
    if latent:
        qr = _rope(qr, cosq_ref[...], sinq_ref[...], MLA_ROPE)
    scale = MLA_DQ ** -0.5
    units = [(s, h) for s in range(NS) for h in range(MLA_H)]
    rows = lambda s: slice(s * TQ, (s + 1) * TQ)
    qh = [jnp.concatenate([qn[rows(s), h * MLA_NOPE:(h + 1) * MLA_NOPE],
                           qr[rows(s), h * MLA_ROPE:(h + 1) * MLA_ROPE]], axis=1) * scale for s, h in units]
    sh = lambda i: units[i][0] * MLA_H + units[i][1]
    outs = _attention_units(len(units), lambda i: _bdot_nt(qh[i], k_scr[sh(i)]), lambda i: v_scr[sh(i)])
    for (s, h), o in zip(units, outs):
        o_ref[rows(s), h * MLA_V:(h + 1) * MLA_V] = o.astype(o_ref.dtype)


def _rope_small(sm, cos, sin_signed):
    lane = lax.broadcasted_iota(jnp.int32, (1, 128), 1)
    half = MLA_ROPE // 2
    swapped = jnp.where(lane < 16 + half, pltpu.roll(sm, 128 - half, axis=1), pltpu.roll(sm, half, axis=1))
    return sm * cos + swapped * sin_signed


def _mla(proj, wq, wkv, qw, kvw, T, TQ, n_seq, row_off, rope=None, cache=None, NS=1, layer=0):
    latent = rope is not None
    tk = T + (PAST if latent else 0)
    nq = T // TQ
    assert NS == 1 or (nq == 1 and not latent)
    n_seq, TQ, TB = n_seq // NS, NS * TQ, NS * T
    qo = row_off // TQ
    so = row_off // TB
    in_specs = [pl.BlockSpec((TQ, 256), lambda i, j: (qo + i * nq + j, C_MCQ // 256)),
                pl.BlockSpec((TB, 128), lambda i, j: (so + i, C_MCKV // 128)),
                pl.BlockSpec((TB, 128), lambda i, j: (so + i, C_SM // 128)),
                pl.BlockSpec((MLA_QL, 768), lambda i, j: (0, 0)),
                pl.BlockSpec((MLA_KVL, 1024), lambda i, j: (0, 0)),
                pl.BlockSpec((1, 256), lambda i, j: (0, 0)),
                pl.BlockSpec((1, 128), lambda i, j: (0, 0))]
    args = [proj, proj, proj, wq, wkv, qw, kvw]
    if latent:
        cosq, sinq, cosk, sink = rope
        cc, kc = cache
        in_specs += [pl.BlockSpec((TQ, 256), lambda i, j: (j, 0)),
                     pl.BlockSpec((TQ, 256), lambda i, j: (j, 0)),
                     pl.BlockSpec((T, 128), lambda i, j: (0, 0)),
                     pl.BlockSpec((T, 128), lambda i, j: (0, 0)),
                     pl.BlockSpec((1, 1, PAST, 128), lambda i, j: (i, layer, 0, 0)),
                     pl.BlockSpec((1, 1, PAST, MLA_ROPE), lambda i, j: (i, layer, 0, 0))]
        args += [cosq, sinq, cosk, sink, cc, kc]
    out_specs = [pl.BlockSpec((TQ, 512), lambda i, j: (i * nq + j, 0))]
    out_shape = [jax.ShapeDtypeStruct((n_seq * TB, 512), BF16)]
    if not latent:
        out_specs.append(pl.BlockSpec((TB, 128), lambda i, j: (i, 0)))
        out_shape.append(jax.ShapeDtypeStruct((n_seq * TB, 128), F32))
    return pl.pallas_call(
        functools.partial(_mla_body, T, TQ // NS, NS, latent),
        grid=(n_seq, nq),
        in_specs=in_specs,
        out_specs=out_specs,
        out_shape=out_shape,
        scratch_shapes=[pltpu.VMEM((NS * MLA_H, tk, MLA_DQ), BF16), pltpu.VMEM((NS * MLA_H, tk, MLA_V), BF16)],
        compiler_params=_cp(2),
        name="mla_%d" % T,
    )(*args)


MG_TM = 512


def _merge_body(n_x, *refs):
    x_refs = refs[:n_x]
    (m_ref, ofc_ref, obc_ref, ofl_ref, obl_ref, dg_ref, ogc_ref, ogl_ref, omc_ref, oml_ref,
     bg_ref, ng_ref, wb_ref, wo_ref, o_ref) = refs[n_x:]
    is_ctx = pl.program_id(0) < RC // MG_TM
    og = jnp.where(is_ctx, ogc_ref[...], ogl_ref[...])
    om = jnp.where(is_ctx, omc_ref[...], oml_ref[...])
    f32 = lambda r: r[...].astype(F32)
    odn = jnp.where(is_ctx, f32(ofc_ref) + f32(obc_ref), f32(ofl_ref) + f32(obl_ref))
    dg = dg_ref[...]
    ng = ng_ref[...]
    parts = []
    for h in range(DN_H):
        oh = odn[:, h * DN_DV:(h + 1) * DN_DV]
        ms = jnp.mean(oh * oh, axis=-1, keepdims=True)
        parts.append(oh * lax.rsqrt(ms + EPS) * ng * _silu(dg[:, h * DN_DV:(h + 1) * DN_DV]))
    br0 = jnp.concatenate(parts, axis=1)
    gate = lambda n: _sigmoid(bg_ref[:, n * D:(n + 1) * D].astype(F32))
    merged = gate(0) * _bdot(br0, wb_ref[0, 0])
    merged = merged + gate(1) * _bdot(og, wb_ref[0, 1])
    merged = merged + gate(2) * _bdot(om, wb_ref[0, 2])
    out = _bdot(merged, wo_ref[0])
    o_ref[...] = _row_tile(x_refs, MG_TM) + m_ref[0, 2:3, :] * out


def _merge(x, mods_l, dn_o, proj, pbg, og_c, og_l, om_c, om_l, ng, wb, wo, l):
    row = lambda i: (i, 0)
    n_ctx = RC // MG_TM
    ctx_row = lambda i: (jnp.minimum(i, n_ctx - 1), 0)
    lat_row = lambda i: (jnp.maximum(i - n_ctx, 0), 0)
    x_specs, x_args = _row_specs(x, MG_TM)
    return pl.pallas_call(
        functools.partial(_merge_body, len(x_args)),
        grid=(R // MG_TM,),
        in_specs=x_specs + [
                  pl.BlockSpec((1, 6, D), lambda i: (_mod_group(i * MG_TM), 0, 0)),
                  pl.BlockSpec((MG_TM, 512), ctx_row),
                  pl.BlockSpec((MG_TM, 512), ctx_row),
                  pl.BlockSpec((MG_TM, 512), lat_row),
                  pl.BlockSpec((MG_TM, 512), lat_row),
                  pl.BlockSpec((MG_TM, 512), lambda i: (i, C_DG // 512)),
                  pl.BlockSpec((MG_TM, 512), ctx_row),
                  pl.BlockSpec((MG_TM, 512), lat_row),
                  pl.BlockSpec((MG_TM, 512), ctx_row),
                  pl.BlockSpec((MG_TM, 512), lat_row),
                  pl.BlockSpec((MG_TM, N_BG), row),
                  pl.BlockSpec((1, DN_DV), lambda i: (0, 0)),
                  pl.BlockSpec((1, 3, 512, D), lambda i: (l, 0, 0, 0)),
                  pl.BlockSpec((1, D, D), lambda i: (l, 0, 0))],
        out_specs=pl.BlockSpec((MG_TM, D), row),
        out_shape=jax.ShapeDtypeStruct((R, D), F32),
        compiler_params=_cp(1),
        name="merge",
    )(*x_args, mods_l, *dn_o, proj, og_c, og_l, om_c, om_l, pbg, ng, wb, wo)


FF_TM = 512
FF_CHUNKS = ((0, 1536), (1536, FF_DENSE))


FF_WCH13, FF_WCH2 = 128, 352


def _ffn_body(j, x_ref, m_ref, g_ref, w1_hbm, w3_hbm, w2_hbm, o_ref, w1_ref, w3_ref, w2_ref, st13, st2, wsems):
    @pl.when(pl.program_id(0) == 0)
    def _():
        _stage_to_bf16(w1_hbm.at[j], w1_ref, st13, wsems)
        _stage_to_bf16(w3_hbm.at[j], w3_ref, st13, wsems)
        _stage_to_bf16(w2_hbm.at[j], w2_ref, st2, wsems)

    x = x_ref[...]
    h = _modnorm(x, g_ref[...], m_ref[0, 3:4, :], m_ref[0, 4:5, :]).astype(BF16)
    y = None
    for c0, c1 in FF_CHUNKS:
        a = _silu(_dot(h, w1_ref[:, c0:c1])) * _dot(h, w3_ref[:, c0:c1])
        yc = _dot(a.astype(BF16), w2_ref[c0:c1, :])
        y = yc if y is None else y + yc
    o_ref[...] = x + m_ref[0, 5:6, :] * y


def _ffn(x, mods_l, g, w1, w3, w2, j):
    hbm = pl.BlockSpec(memory_space=pl.ANY)
    return pl.pallas_call(
        functools.partial(_ffn_body, j),
        grid=(R // FF_TM,),
        in_specs=[pl.BlockSpec((FF_TM, D), lambda i: (i, 0)),
                  pl.BlockSpec((1, 6, D), lambda i: (_mod_group(i * FF_TM), 0, 0)),
                  pl.BlockSpec((1, D), lambda i: (0, 0)),
                  hbm, hbm, hbm],
        out_specs=pl.BlockSpec((FF_TM, D), lambda i: (i, 0)),
        out_shape=jax.ShapeDtypeStruct((R, D), F32),
        scratch_shapes=[pltpu.VMEM((D, FF_DENSE), BF16), pltpu.VMEM((D, FF_DENSE), BF16),
                        pltpu.VMEM((FF_DENSE, D), BF16),
                        pltpu.VMEM((2, FF_WCH13, FF_DENSE), F32), pltpu.VMEM((2, FF_WCH2, D), F32),
                        pltpu.SemaphoreType.DMA((2,))],
        compiler_params=_cp(1),
        name="ffn",
    )(x, mods_l, g, w1, w3, w2)


RT_TM = 512
MOE_TG, MOE_TF = 512, 1792
MOE_NT = 2 * R // MOE_TG + N_EXP
MOE_ROWS = MOE_NT * MOE_TG
CB_TM = 256


def _router_body(x_ref, m_ref, g_ref, rw_ref, rb_ref, hn_ref, route_ref, cnt_ref, base_scr):
    @pl.when(pl.program_id(0) == 0)
    def _():
        base_scr[...] = jnp.zeros_like(base_scr)

    h = _modnorm(x_ref[...], g_ref[...], m_ref[0, 3:4, :], m_ref[0, 4:5, :])
    hn_ref[...] = h
    logits = _dot3(h, rw_ref[...]) + rb_ref[...]
    lane = lax.broadcasted_iota(jnp.int32, logits.shape, 1)
    m1 = jnp.max(logits, axis=-1, keepdims=True)
    i1 = jnp.min(jnp.where(logits == m1, lane, 128), axis=-1, keepdims=True)
    sel1 = lane == i1
    rest = jnp.where(sel1, -jnp.inf, logits)
    m2 = jnp.max(rest, axis=-1, keepdims=True)
    i2 = jnp.min(jnp.where(rest == m2, lane, 128), axis=-1, keepdims=True)
    sel2 = lane == i2
    e2 = jnp.exp(m2 - m1)
    p1 = 1.0 / (1.0 + e2)
    p2 = e2 / (1.0 + e2)

    cnt = jnp.where(jnp.logical_or(sel1, sel2), 1.0, 0.0)
    r = lax.broadcasted_iota(jnp.int32, (RT_TM, RT_TM), 0)
    c = lax.broadcasted_iota(jnp.int32, (RT_TM, RT_TM), 1)
    before = jnp.where(r > c, 1.0, 0.0).astype(BF16)
    seen = base_scr[...] + _dot(before, cnt.astype(BF16))
    rank1 = jnp.sum(jnp.where(sel1, seen, 0.0), axis=-1, keepdims=True)
    rank2 = jnp.sum(jnp.where(sel2, seen, 0.0), axis=-1, keepdims=True)
    vals = (i1.astype(F32), i2.astype(F32), rank1, rank2, p1, p2)
    route = jnp.zeros(logits.shape, F32)
    for k, val in enumerate(vals):
        route = jnp.where(lane == k, val, route)
    route_ref[...] = route
    base_scr[...] += jnp.sum(cnt, axis=0, keepdims=True)
    cnt_ref[...] = base_scr[...]


def _router(x, mods_l, g, rw128, rb128):
    return pl.pallas_call(
        _router_body,
        grid=(R // RT_TM,),
        in_specs=[pl.BlockSpec((RT_TM, D), lambda i: (i, 0)),
                  pl.BlockSpec((1, 6, D), lambda i: (_mod_group(i * RT_TM), 0, 0)),
                  pl.BlockSpec((1, D), lambda i: (0, 0)),
                  pl.BlockSpec((D, 128), lambda i: (0, 0)),
                  pl.BlockSpec((1, 128), lambda i: (0, 0))],
        out_specs=[pl.BlockSpec((RT_TM, D), lambda i: (i, 0)),
                   pl.BlockSpec((RT_TM, 128), lambda i: (i, 0)),
                   pl.BlockSpec((1, 128), lambda i: (0, 0))],
        out_shape=[jax.ShapeDtypeStruct((R, D), F32),
                   jax.ShapeDtypeStruct((R, 128), F32),
                   jax.ShapeDtypeStruct((1, 128), F32)],
        scratch_shapes=[pltpu.VMEM((1, 128), F32)],
        compiler_params=_cp(1),
        name="moe_router",
    )(x, mods_l, g, rw128, rb128)


def _rows_copy(src_hbm, dst, sem, n):
    return pltpu.make_async_copy(src_hbm.at[pl.ds(0, n)], dst, sem)


def _start_row_gather(idx_ref, n, src_hbm, dst, sem, inline=False, both_queues=False):
    def issue(i, priority):
        pltpu.make_async_copy(src_hbm.at[pl.ds(idx_ref[0, 0, i], 1)], dst.at[pl.ds(i, 1)], sem).start(
            priority=priority)

    if inline:
        for i in range(n):
            issue(i, i % 2 if both_queues else 0)
    else:
        def body(i, carry):
            issue(i, 0)
            return carry

        lax.fori_loop(0, n, body, 0, unroll=8)


def _stage_to_bf16(src_hbm, dst, stage, sems):
    ch = stage.shape[1]
    n = src_hbm.shape[0] // ch
    copy = lambda c: pltpu.make_async_copy(src_hbm.at[pl.ds(c * ch, ch)], stage.at[c % 2], sems.at[c % 2])
    copy(0).start()
    for c in range(n):
        if c + 1 < n:
            copy(c + 1).start()
        copy(c).wait()
        dst[c * ch:(c + 1) * ch, :] = stage[c % 2].astype(BF16)


def _experts_body(j, te_ref, tv_ref, src_ref, nsrc_ref, hn_hbm, w1_hbm, w3_hbm, w2_hbm, ys_ref,
                  xg_scr, w1_ref, w3_ref, w2_ref, st13, st2, sems, wsems):
    t = pl.program_id(0)
    slot = t % 2
    valid = tv_ref[t] > 0
    e = te_ref[t]
    new_expert = jnp.logical_or(t == 0, e != te_ref[jnp.maximum(t - 1, 0)])

    @pl.when(jnp.logical_and(valid, new_expert))
    def _():
        _stage_to_bf16(w1_hbm.at[j, e], w1_ref, st13, wsems)
        _stage_to_bf16(w3_hbm.at[j, e], w3_ref, st13, wsems)
        _stage_to_bf16(w2_hbm.at[j, e], w2_ref, st2, wsems)

    requested = jnp.where(t == 0, valid, tv_ref[jnp.maximum(t - 1, 0)] > 0)

    @pl.when(jnp.logical_and(t == 0, valid))
    def _():
        _start_row_gather(src_ref, MOE_TG, hn_hbm, xg_scr.at[0], sems.at[0])

    @pl.when(requested)
    def _():
        _rows_copy(hn_hbm, xg_scr.at[slot], sems.at[slot], MOE_TG).wait()

    @pl.when(valid)
    def _():
        xb = xg_scr[slot].astype(BF16)
        _start_row_gather(nsrc_ref, MOE_TG, hn_hbm, xg_scr.at[1 - slot], sems.at[1 - slot], inline=True)
        y = None
        for c in range(FF_EXP // MOE_TF):
            cs = slice(c * MOE_TF, (c + 1) * MOE_TF)
            a = _silu(_dot(xb, w1_ref[:, cs])) * _dot(xb, w3_ref[:, cs])
            yc = _dot(a.astype(BF16), w2_ref[cs, :])
            y = yc if y is None else y + yc
        ys_ref[...] = y

    @pl.when(jnp.logical_not(valid))
    def _():
        ys_ref[...] = jnp.zeros_like(ys_ref)

    @pl.when(jnp.logical_and(t == MOE_NT - 1, valid))
    def _():
        _rows_copy(hn_hbm, xg_scr.at[1 - slot], sems.at[1 - slot], MOE_TG).wait()


MOE_WCH13, MOE_WCH2 = 128, 448


def _experts(tile_e, tile_v, src, hn, w1, w3, w2, j):
    hbm = pl.BlockSpec(memory_space=pl.ANY)
    grid_spec = pltpu.PrefetchScalarGridSpec(
        num_scalar_prefetch=2,
        grid=(MOE_NT,),
        in_specs=[pl.BlockSpec((1, 1, MOE_TG), lambda t, te, tv: (t, 0, 0), memory_space=pltpu.SMEM),
                  pl.BlockSpec((1, 1, MOE_TG), lambda t, te, tv: (t + 1, 0, 0), memory_space=pltpu.SMEM),
                  hbm, hbm, hbm, hbm],
        out_specs=pl.BlockSpec((MOE_TG, D), lambda t, te, tv: (t, 0)),
        scratch_shapes=[pltpu.VMEM((2, MOE_TG, D), F32),
                        pltpu.VMEM((D, FF_EXP), BF16), pltpu.VMEM((D, FF_EXP), BF16),
                        pltpu.VMEM((FF_EXP, D), BF16),
                        pltpu.VMEM((2, MOE_WCH13, FF_EXP), F32), pltpu.VMEM((2, MOE_WCH2, D), F32),
                        pltpu.SemaphoreType.DMA((2,)), pltpu.SemaphoreType.DMA((2,))],
    )
    return pl.pallas_call(
        functools.partial(_experts_body, j),
        grid_spec=grid_spec,
        out_shape=jax.ShapeDtypeStruct((MOE_ROWS, D), F32),
        compiler_params=_cp(1),
        name="moe_experts",
    )(tile_e, tile_v, src, src, hn, w1, w3, w2)


def _combine_body(final, pos_ref, npos_ref, x_ref, m_ref, route_ref, ys_hbm, *rest):
    if final:
        fg_ref, oc_ref, ol_ref, buf, sems = rest
    else:
        o_ref, buf, sems = rest
    i = pl.program_id(0)
    slot = i % 2
    n_rows = 2 * CB_TM

    @pl.when(i == 0)
    def _():
        _start_row_gather(pos_ref, n_rows, ys_hbm, buf.at[0], sems.at[0])

    @pl.when(i + 1 < pl.num_programs(0))
    def _():
        _start_row_gather(npos_ref, n_rows, ys_hbm, buf.at[1 - slot], sems.at[1 - slot], inline=True,
                          both_queues=True)

    _rows_copy(ys_hbm, buf.at[slot], sems.at[slot], n_rows).wait()
    route = route_ref[...]
    y = route[:, 4:5] * buf[slot, 0:CB_TM, :] + route[:, 5:6] * buf[slot, CB_TM:n_rows, :]
    out = x_ref[...] + m_ref[0, 5:6, :] * y
    if not final:
        o_ref[...] = out
        return
    ms = jnp.mean(out * out, axis=-1, keepdims=True)
    out = out * lax.rsqrt(ms + EPS) * fg_ref[...]
    is_ctx = i < RC // CB_TM

    @pl.when(is_ctx)
    def _():
        oc_ref[...] = out

    @pl.when(jnp.logical_not(is_ctx))
    def _():
        ol_ref[...] = out


def _combine(pos, x, mods_l, route, ys, final_g=None):
    n_tiles = R // CB_TM
    n_ctx = RC // CB_TM
    final = final_g is not None
    in_specs = [pl.BlockSpec((1, 1, 2 * CB_TM), lambda i: (i, 0, 0), memory_space=pltpu.SMEM),
                pl.BlockSpec((1, 1, 2 * CB_TM), lambda i: (jnp.minimum(i + 1, n_tiles - 1), 0, 0),
                             memory_space=pltpu.SMEM),
                pl.BlockSpec((CB_TM, D), lambda i: (i, 0)),
                pl.BlockSpec((1, 6, D), lambda i: (_mod_group(i * CB_TM), 0, 0)),
                pl.BlockSpec((CB_TM, 128), lambda i: (i, 0)),
                pl.BlockSpec(memory_space=pl.ANY)]
    args = [pos, pos, x, mods_l, route, ys]
    if final:
        in_specs.append(pl.BlockSpec((1, D), lambda i: (0, 0)))
        args.append(final_g)
        out_specs = [pl.BlockSpec((CB_TM, D), lambda i: (jnp.minimum(i, n_ctx - 1), 0)),
                     pl.BlockSpec((CB_TM, D), lambda i: (jnp.maximum(i - n_ctx, 0), 0))]
        out_shape = [jax.ShapeDtypeStruct((RC, D), F32), jax.ShapeDtypeStruct((RL, D), F32)]
    else:
        out_specs = pl.BlockSpec((CB_TM, D), lambda i: (i, 0))
        out_shape = jax.ShapeDtypeStruct((R, D), F32)
    return pl.pallas_call(
        functools.partial(_combine_body, final),
        grid=(n_tiles,),
        in_specs=in_specs,
        out_specs=out_specs,
        out_shape=out_shape,
        scratch_shapes=[pltpu.VMEM((2, 2 * CB_TM, D), F32), pltpu.SemaphoreType.DMA((2,))],
        compiler_params=_cp(1),
        name="moe_combine",
    )(*args)


def _moe(x, mods_l, g, rw128, rb128, w1, w3, w2, j, final_g=None):
    hn, route, cnt = _router(x, mods_l, g, rw128, rb128)
    eid = route[:, 0:2].astype(jnp.int32)
    rank = route[:, 2:4].astype(jnp.int32)
    counts = cnt[0, :N_EXP].astype(jnp.int32)
    gsize = (counts + MOE_TG - 1) // MOE_TG * MOE_TG
    gend = jnp.cumsum(gsize)
    pos = (gend - gsize)[eid] + rank
    tile_start = jnp.arange(MOE_NT, dtype=jnp.int32) * MOE_TG
    tile_e = jnp.minimum(jnp.sum(tile_start[:, None] >= gend[None, :], axis=1), N_EXP - 1).astype(jnp.int32)
    tile_v = (tile_start < gend[-1]).astype(jnp.int32)
    tok = jnp.broadcast_to(jnp.arange(R, dtype=jnp.int32)[:, None], (R, 2))
    src = jnp.zeros((MOE_ROWS + MOE_TG,), jnp.int32).at[pos.reshape(-1)].set(
        tok.reshape(-1), unique_indices=True, mode="promise_in_bounds")
    ys = _experts(tile_e, tile_v, src.reshape(MOE_NT + 1, 1, MOE_TG), hn, w1, w3, w2, j)
    pos_t = pos.reshape(R // CB_TM, CB_TM, 2).transpose(0, 2, 1).reshape(R // CB_TM, 1, 2 * CB_TM)
    return _combine(pos_t, x, mods_l, route, ys, final_g)


def _rope_tables(n_tokens, rot_dim):
    t = np.arange(n_tokens)
    row = (t // GRID_W).astype(np.float32)
    col = (t % GRID_W).astype(np.float32)
    n_freq = rot_dim // 4
    inv = (ROPE_THETA ** (-jnp.arange(n_freq, dtype=F32) / n_freq))
    ang = jnp.concatenate([jnp.asarray(row)[:, None] * inv, jnp.asarray(col)[:, None] * inv], axis=-1)
    cos, sin = jnp.cos(ang), jnp.sin(ang)
    return jnp.concatenate([cos, cos], axis=-1), jnp.concatenate([-sin, sin], axis=-1)


def kernel(x_prompt, x_sample, c, cache_gqa_k, cache_gqa_v, cache_mla_ckv, cache_mla_krope, state_delta, c_ctx, w_mod, b_mod, norm1_g, norm2_g, w_in, dn_conv_w, dn_a_log, dn_dt_bias, dn_norm_g, gqa_q_norm, gqa_k_norm, mla_q_norm, mla_kv_norm, mla_w_uq, mla_w_ukv, w_branch, w_out, ffd_w1, ffd_w3, ffd_w2, router_w, router_b, moe_w1, moe_w3, moe_w2, final_g):
    x = (x_prompt.reshape(RC, D), x_sample.reshape(RL, D))
    cond8 = jnp.concatenate([c_ctx[None, :], c, jnp.zeros((3, D), F32)], axis=0)
    mods = _mods(cond8, w_mod, b_mod).reshape(DEPTH, 8, 6, D)

    cg, sg = _rope_tables(DEC_SEQ, GQA_HD)
    gqa_rope = (jnp.tile(cg, (1, GQA_H)), jnp.tile(sg, (1, GQA_H)),
                jnp.tile(cg, (1, GQA_KV)), jnp.tile(sg, (1, GQA_KV)))
    cm, sm_ = _rope_tables(DEC_SEQ, MLA_ROPE)
    padk = lambda t: jnp.pad(t, ((0, 0), (16, 128 - 16 - MLA_ROPE)))
    mla_rope = (jnp.tile(cm, (1, MLA_H)), jnp.tile(sm_, (1, MLA_H)),
                jnp.pad(cm, ((0, 0), (16, 128 - 16 - MLA_ROPE)), constant_values=1.0), padk(sm_))

    w_in_bf, w_branch_bf, w_out_bf = w_in.astype(BF16), w_branch.astype(BF16), w_out.astype(BF16)
    gqa_cache = (cache_gqa_k.reshape(DEC_BATCH, DEPTH, PAST, 128), cache_gqa_v.reshape(DEC_BATCH, DEPTH, PAST, 128))

    new_k, new_v, new_ckv, new_kr, new_s = [], [], [], [], []
    for l in range(DEPTH):
        pbg, proj = _inproj(x, mods[l], norm1_g[l][None, :], w_in_bf, l)

        pad128 = lambda v: jnp.pad(v.reshape(1, 8), ((0, 0), (8, 112)))
        padt = lambda v: jnp.pad(v.reshape(8, 1), ((8, 0), (0, 0)))
        sm_t = proj[:, C_SM:C_SM + 16].T
        dn_u, dn_w, dn_qg, dn_kd, dn_ai, dn_bg = _dn_prep(
            proj, sm_t, dn_conv_w[l], pad128(dn_a_log[l]), pad128(dn_dt_bias[l]),
            padt(dn_a_log[l]), padt(dn_dt_bias[l]))
        *dn_o, s_c = _dn_scan(dn_u, dn_w, dn_qg, dn_kd, dn_ai, dn_bg, state_delta, l)

        qw = jnp.tile(gqa_q_norm[l][None, :], (1, GQA_H))
        kw = jnp.tile(gqa_k_norm[l][None, :], (1, GQA_KV))
        o_g_c, kn_c = _gqa(proj, qw, kw, SEQ, SEQ, BATCH, 0, NS=CTX_NS)
        (o_g_l,) = _gqa(proj, qw, kw, DEC_SEQ, 256, DEC_BATCH, RC, rope=gqa_rope, cache=gqa_cache, layer=l)

        wq = mla_w_uq[l].reshape(MLA_QL, MLA_H, MLA_DQ)
        wq = jnp.concatenate([wq[:, :, :MLA_NOPE].reshape(MLA_QL, -1), wq[:, :, MLA_NOPE:].reshape(MLA_QL, -1)],
                             axis=1).astype(BF16)
        wkv = mla_w_ukv[l].reshape(MLA_KVL, MLA_H, MLA_NOPE + MLA_V)
        wkv = jnp.concatenate([wkv[:, :, :MLA_NOPE].reshape(MLA_KVL, -1), wkv[:, :, MLA_NOPE:].reshape(MLA_KVL, -1)],
                              axis=1).astype(BF16)
        mqw, mkvw = mla_q_norm[l][None, :], mla_kv_norm[l][None, :]
        o_m_c, ckv_c = _mla(proj, wq, wkv, mqw, mkvw, SEQ, SEQ, BATCH, 0, NS=CTX_NS)
        (o_m_l,) = _mla(proj, wq, wkv, mqw, mkvw, DEC_SEQ, 256, DEC_BATCH, RC, rope=mla_rope,
                        cache=(cache_mla_ckv, cache_mla_krope), layer=l)
        x = _merge(x, mods[l], dn_o, proj, pbg, o_g_c, o_g_l, o_m_c, o_m_l, dn_norm_g[l][None, :],
                   w_branch_bf, w_out_bf, l)

        j = l // 2
        if l % 2 == 0:
            x = _ffn(x, mods[l], norm2_g[l][None, :], ffd_w1, ffd_w3, ffd_w2, j)
        else:
            rw128 = jnp.pad(router_w[j], ((0, 0), (0, 128 - N_EXP)))
            rb128 = jnp.pad(router_b[j][None, :], ((0, 0), (0, 128 - N_EXP)), constant_values=-jnp.inf)
            x = _moe(x, mods[l], norm2_g[l][None, :], rw128, rb128, moe_w1, moe_w3, moe_w2, j,
                     final_g=final_g[None, :] if l == DEPTH - 1 else None)

        new_k.append(kn_c.reshape(BATCH, SEQ, GQA_KV, GQA_HD))
        new_v.append(proj[:RC, C_GKV + 128:C_GKV + 256].reshape(BATCH, SEQ, GQA_KV, GQA_HD))
        new_ckv.append(ckv_c.reshape(BATCH, SEQ, MLA_KVL))
        new_kr.append(proj[:RC, C_SM + 16:C_SM + 16 + MLA_ROPE].reshape(BATCH, SEQ, MLA_ROPE))
        new_s.append(s_c)

    y_c, y_l = x
    return (y_c.reshape(BATCH, SEQ, D), y_l.reshape(DEC_BATCH, DEC_SEQ, D),
            jnp.stack(new_k, axis=1), jnp.stack(new_v, axis=1), jnp.stack(new_ckv, axis=1),
            jnp.stack(new_kr, axis=1), jnp.stack(new_s, axis=1))
```
